```python
import jax, jax.numpy as jnp
from jax import lax
import numpy as np

D_MODEL = 2048
BATCH = 8
SEQ = 8192
DEPTH = 1

NORM_EPS = 1e-5
D_SSM = 2048
SSD_HEAD_DIM = 64
SSD_HEADS = D_SSM // SSD_HEAD_DIM
SSD_GROUPS = 4
SSD_HEADS_PER_GROUP = SSD_HEADS // SSD_GROUPS
SSD_STATE = 128
CONV_WIDTH = 4
CHUNK = 256
D_CONV_CH = D_SSM + 2 * SSD_GROUPS * SSD_STATE
D_POOL = 2048
POOL_WINDOWS = (2, 4, 8, 16)
POOL_GROUPS = len(POOL_WINDOWS)
POOL_GROUP_DIM = D_POOL // POOL_GROUPS
D_MIX = D_SSM + D_POOL
D_IN_PROJ = D_SSM + D_CONV_CH + SSD_HEADS + D_POOL
D_FF = -(-8 * D_MODEL // (3 * 256)) * 256

kernel_name = "hybrid_ssd_multiscale_pool_block"


def rms_norm(x, w):
    xf = x.astype(jnp.float32)
    y = xf * lax.rsqrt(jnp.mean(xf * xf, axis=-1, keepdims=True) + NORM_EPS)
    return (y * w.astype(jnp.float32)).astype(x.dtype)


def causal_depthwise_conv(u, w, b):
    ch = u.shape[-1]
    out = lax.conv_general_dilated(
        u, w.astype(u.dtype).reshape(CONV_WIDTH, 1, ch),
        window_strides=(1,), padding=[(CONV_WIDTH - 1, 0)],
        dimension_numbers=("NWC", "WIO", "NWC"), feature_group_count=ch)
    return out + b.astype(u.dtype)


def ssd_chunked_scan(xh, dt, a, b_mat, c_mat):
    bsz, seqlen = xh.shape[:2]
    pad = (-seqlen) % CHUNK
    if pad:
        padw = lambda t: jnp.pad(t, [(0, 0), (0, pad)] + [(0, 0)] * (t.ndim - 2))
        xh, dt, b_mat, c_mat = padw(xh), padw(dt), padw(b_mat), padw(c_mat)
    nc = (seqlen + pad) // CHUNK
    rs = lambda t: t.reshape((bsz, nc, CHUNK) + t.shape[2:])
    xh, dt, b_mat, c_mat = rs(xh), rs(dt), rs(b_mat), rs(c_mat)

    a_cum = jnp.cumsum(dt * a, axis=2)
    xdt = xh * dt[..., None]

    seg = a_cum[:, :, :, None] - a_cum[:, :, None, :]
    causal = jnp.tril(jnp.ones((CHUNK, CHUNK), dtype=bool))[:, :, None, None]
    decay = jnp.exp(jnp.where(causal, seg, -jnp.inf))
    cb = jnp.einsum("bclgn,bcsgn->bclsg", c_mat, b_mat)
    y_diag = jnp.einsum("bclsg,bclsgr,bcsgrp->bclgrp", cb, decay, xdt)

    decay_to_end = jnp.exp(a_cum[:, :, -1:] - a_cum)
    states = jnp.einsum("bclgn,bclgr,bclgrp->bcgrpn", b_mat, decay_to_end, xdt)
    chunk_decay = jnp.exp(a_cum[:, :, -1])

    def step(h, inp):
        s, dcy = inp
        return h * dcy[..., None, None] + s, h
    h0 = jnp.zeros(states.shape[:1] + states.shape[2:], jnp.float32)
    _, prev = lax.scan(step, h0, (jnp.moveaxis(states, 1, 0), jnp.moveaxis(chunk_decay, 1, 0)))
    prev = jnp.moveaxis(prev, 0, 1)

    y_off = jnp.einsum("bclgn,bcgrpn,bclgr->bclgrp", c_mat, prev, jnp.exp(a_cum))
    y = (y_diag + y_off).reshape((bsz, nc * CHUNK) + xh.shape[3:])
    return y[:, :seqlen]


def gated_group_rmsnorm(y, z, w):
    g = y * jax.nn.silu(z.astype(jnp.float32))
    shp = g.shape
    g = g.reshape(shp[:-1] + (SSD_GROUPS, shp[-1] // SSD_GROUPS))
    g = g * lax.rsqrt(jnp.mean(g * g, axis=-1, keepdims=True) + NORM_EPS)
    return g.reshape(shp) * w.astype(jnp.float32)


def multiscale_causal_pool(u):
    uf = u.astype(jnp.float32)
    seqlen = u.shape[1]
    cs = jnp.pad(jnp.cumsum(uf, axis=1), ((0, 0), (1, 0), (0, 0)))
    t = jnp.arange(seqlen)
    outs = []
    for gi, w in enumerate(POOL_WINDOWS):
        sl = slice(gi * POOL_GROUP_DIM, (gi + 1) * POOL_GROUP_DIM)
        csg = cs[..., sl]
        start = jnp.maximum(t + 1 - w, 0)
        win_sum = csg[:, 1:] - csg[:, start]
        count = jnp.minimum(t + 1, w).astype(jnp.float32)
        outs.append(win_sum / count[None, :, None] - uf[..., sl])
    return jnp.stack(outs, axis=2)


def hybrid_mixer(h, w_in, conv_w, conv_b, dt_bias, a_log, d_skip, ssd_norm_w,
                 pool_w, pool_scale, w_out):
    bsz, seqlen, _ = h.shape
    proj = h @ w_in.astype(h.dtype)
    z, xbc, dt_raw, u = jnp.split(
        proj, [D_SSM, D_SSM + D_CONV_CH, D_SSM + D_CONV_CH + SSD_HEADS], axis=-1)

    xbc = jax.nn.silu(causal_depthwise_conv(xbc, conv_w, conv_b)).astype(jnp.float32)
    xs, bm, cm = jnp.split(xbc, [D_SSM, D_SSM + SSD_GROUPS * SSD_STATE], axis=-1)
    dt = jax.nn.softplus(dt_raw.astype(jnp.float32) + dt_bias.astype(jnp.float32))
    a = -jnp.exp(a_log.astype(jnp.float32))
    xh = xs.reshape(bsz, seqlen, SSD_GROUPS, SSD_HEADS_PER_GROUP, SSD_HEAD_DIM)
    y = ssd_chunked_scan(
        xh,
        dt.reshape(bsz, seqlen, SSD_GROUPS, SSD_HEADS_PER_GROUP),
        a.reshape(SSD_GROUPS, SSD_HEADS_PER_GROUP),
        bm.reshape(bsz, seqlen, SSD_GROUPS, SSD_STATE),
        cm.reshape(bsz, seqlen, SSD_GROUPS, SSD_STATE))
    y = y + d_skip.astype(jnp.float32).reshape(SSD_GROUPS, SSD_HEADS_PER_GROUP)[..., None] * xh
    y_ssd = gated_group_rmsnorm(y.reshape(bsz, seqlen, D_SSM), z, ssd_norm_w)

    pooled = multiscale_causal_pool(u)
    y_pool = jnp.einsum("blgc,gcd->blgd", pooled, pool_w.astype(jnp.float32))
    y_pool = y_pool.reshape(bsz, seqlen, D_POOL) * pool_scale.astype(jnp.float32)

    mixed = jnp.concatenate([y_ssd, y_pool], axis=-1).astype(h.dtype)
    return mixed @ w_out.astype(h.dtype)


def swiglu(h, w_gate, w_up, w_down):
    return (jax.nn.silu(h @ w_gate.astype(h.dtype)) * (h @ w_up.astype(h.dtype))) @ w_down.astype(h.dtype)


def _fwd_setup_inputs(seed: int = 0) -> dict:
    key = jax.random.key(seed)
    ks = jax.random.split(key, 20)
    f32 = jnp.float32
    nrm = lambda k, shp, s: jax.random.normal(k, shp, f32) * s
    dt_init = jnp.exp(jax.random.uniform(ks[5], (DEPTH, SSD_HEADS), f32,
                                         np.log(1e-3), np.log(1e-1)))
    return {
        "x": jax.random.normal(ks[0], (BATCH, SEQ, D_MODEL), f32),
        "attn_norm_w": 1.0 + nrm(ks[1], (DEPTH, D_MODEL), 0.02),
        "w_in": nrm(ks[2], (DEPTH, D_MODEL, D_IN_PROJ), D_MODEL ** -0.5),
        "conv_w": nrm(ks[3], (DEPTH, CONV_WIDTH, D_CONV_CH), CONV_WIDTH ** -0.5),
        "conv_b": nrm(ks[4], (DEPTH, D_CONV_CH), 0.02),
        "dt_bias": dt_init + jnp.log(-jnp.expm1(-dt_init)),
        "a_log": jnp.log(jax.random.uniform(ks[6], (DEPTH, SSD_HEADS), f32, 1.0, 16.0)),
        "d_skip": 1.0 + nrm(ks[7], (DEPTH, SSD_HEADS), 0.02),
        "ssd_norm_w": 1.0 + nrm(ks[8], (DEPTH, D_SSM), 0.02),
        "pool_w": nrm(ks[9], (DEPTH, POOL_GROUPS, POOL_GROUP_DIM, POOL_GROUP_DIM), POOL_GROUP_DIM ** -0.5),
        "pool_scale": 1.0 + nrm(ks[10], (DEPTH, D_POOL), 0.02),
        "w_out": nrm(ks[11], (DEPTH, D_MIX, D_MODEL), D_MIX ** -0.5),
        "ffn_norm_w": 1.0 + nrm(ks[12], (DEPTH, D_MODEL), 0.02),
        "w_gate": nrm(ks[13], (DEPTH, D_MODEL, D_FF), D_MODEL ** -0.5),
        "w_up": nrm(ks[14], (DEPTH, D_MODEL, D_FF), D_MODEL ** -0.5),
        "w_down": nrm(ks[15], (DEPTH, D_FF, D_MODEL), D_FF ** -0.5),
        "final_norm_w": 1.0 + nrm(ks[16], (D_MODEL,), 0.02),
    }


def _fwd_reference(x, attn_norm_w, w_in, conv_w, conv_b, dt_bias, a_log, d_skip,
              ssd_norm_w, pool_w, pool_scale, w_out, ffn_norm_w, w_gate, w_up,
              w_down, final_norm_w):
    h = x
    for i in range(DEPTH):
        h = h + hybrid_mixer(rms_norm(h, attn_norm_w[i]), w_in[i], conv_w[i], conv_b[i],
                             dt_bias[i], a_log[i], d_skip[i], ssd_norm_w[i],
                             pool_w[i], pool_scale[i], w_out[i])
        h = h + swiglu(rms_norm(h, ffn_norm_w[i]), w_gate[i], w_up[i], w_down[i])
    return rms_norm(h, final_norm_w)


import jax as _jax
import jax.numpy as _jnp

TWIN_FORMAT = 'train_step'
FWD_PARAMS = ['x', 'attn_norm_w', 'w_in', 'conv_w', 'conv_b', 'dt_bias', 'a_log', 'd_skip', 'ssd_norm_w', 'pool_w', 'pool_scale', 'w_out', 'ffn_norm_w', 'w_gate', 'w_up', 'w_down', 'final_norm_w']
TWIN_WEIGHTS = ['attn_norm_w', 'w_in', 'conv_w', 'conv_b', 'dt_bias', 'a_log', 'd_skip', 'ssd_norm_w', 'pool_w', 'pool_scale', 'w_out', 'ffn_norm_w', 'w_gate', 'w_up', 'w_down', 'final_norm_w']
TWIN_DIFF_INPUT = 'x'
TWIN_INPUTS = ['x', 'attn_norm_w', 'w_in', 'conv_w', 'conv_b', 'dt_bias', 'a_log', 'd_skip', 'ssd_norm_w', 'pool_w', 'pool_scale', 'w_out', 'ffn_norm_w', 'w_gate', 'w_up', 'w_down', 'final_norm_w', 'loss_target', 'm_attn_norm_w', 'm_w_in', 'm_conv_w', 'm_conv_b', 'm_dt_bias', 'm_a_log', 'm_d_skip', 'm_ssd_norm_w', 'm_pool_w', 'm_pool_scale', 'm_w_out', 'm_ffn_norm_w', 'm_w_gate', 'm_w_up', 'm_w_down', 'm_final_norm_w', 'v_attn_norm_w', 'v_w_in', 'v_conv_w', 'v_conv_b', 'v_dt_bias', 'v_a_log', 'v_d_skip', 'v_ssd_norm_w', 'v_pool_w', 'v_pool_scale', 'v_w_out', 'v_ffn_norm_w', 'v_w_gate', 'v_w_up', 'v_w_down', 'v_final_norm_w']
TWIN_OUTPUTS = ['loss', 'grad_x', 'grad_attn_norm_w', 'grad_w_in', 'grad_conv_w', 'grad_conv_b', 'grad_dt_bias', 'grad_a_log', 'grad_d_skip', 'grad_ssd_norm_w', 'grad_pool_w', 'grad_pool_scale', 'grad_w_out', 'grad_ffn_norm_w', 'grad_w_gate', 'grad_w_up', 'grad_w_down', 'grad_final_norm_w', 'delta_attn_norm_w', 'delta_w_in', 'delta_conv_w', 'delta_conv_b', 'delta_dt_bias', 'delta_a_log', 'delta_d_skip', 'delta_ssd_norm_w', 'delta_pool_w', 'delta_pool_scale', 'delta_w_out', 'delta_ffn_norm_w', 'delta_w_gate', 'delta_w_up', 'delta_w_down', 'delta_final_norm_w', 'new_m_attn_norm_w', 'new_m_w_in', 'new_m_conv_w', 'new_m_conv_b', 'new_m_dt_bias', 'new_m_a_log', 'new_m_d_skip', 'new_m_ssd_norm_w', 'new_m_pool_w', 'new_m_pool_scale', 'new_m_w_out', 'new_m_ffn_norm_w', 'new_m_w_gate', 'new_m_w_up', 'new_m_w_down', 'new_m_final_norm_w', 'new_v_attn_norm_w', 'new_v_w_in', 'new_v_conv_w', 'new_v_conv_b', 'new_v_dt_bias', 'new_v_a_log', 'new_v_d_skip', 'new_v_ssd_norm_w', 'new_v_pool_w', 'new_v_pool_scale', 'new_v_w_out', 'new_v_ffn_norm_w', 'new_v_w_gate', 'new_v_w_up', 'new_v_w_down', 'new_v_final_norm_w']
TWIN_LEAF_KINDS = {'loss': 'loss', 'grad_x': 'grad_x', 'grad_attn_norm_w': 'grad_w', 'grad_w_in': 'grad_w', 'grad_conv_w': 'grad_w', 'grad_conv_b': 'grad_w', 'grad_dt_bias': 'grad_w', 'grad_a_log': 'grad_w', 'grad_d_skip': 'grad_w', 'grad_ssd_norm_w': 'grad_w', 'grad_pool_w': 'grad_w', 'grad_pool_scale': 'grad_w', 'grad_w_out': 'grad_w', 'grad_ffn_norm_w': 'grad_w', 'grad_w_gate': 'grad_w', 'grad_w_up': 'grad_w', 'grad_w_down': 'grad_w', 'grad_final_norm_w': 'grad_w', 'delta_attn_norm_w': 'delta_w', 'delta_w_in': 'delta_w', 'delta_conv_w': 'delta_w', 'delta_conv_b': 'delta_w', 'delta_dt_bias': 'delta_w', 'delta_a_log': 'delta_w', 'delta_d_skip': 'delta_w', 'delta_ssd_norm_w': 'delta_w', 'delta_pool_w': 'delta_w', 'delta_pool_scale': 'delta_w', 'delta_w_out': 'delta_w', 'delta_ffn_norm_w': 'delta_w', 'delta_w_gate': 'delta_w', 'delta_w_up': 'delta_w', 'delta_w_down': 'delta_w', 'delta_final_norm_w': 'delta_w', 'new_m_attn_norm_w': 'new_m', 'new_m_w_in': 'new_m', 'new_m_conv_w': 'new_m', 'new_m_conv_b': 'new_m', 'new_m_dt_bias': 'new_m', 'new_m_a_log': 'new_m', 'new_m_d_skip': 'new_m', 'new_m_ssd_norm_w': 'new_m', 'new_m_pool_w': 'new_m', 'new_m_pool_scale': 'new_m', 'new_m_w_out': 'new_m', 'new_m_ffn_norm_w': 'new_m', 'new_m_w_gate': 'new_m', 'new_m_w_up': 'new_m', 'new_m_w_down': 'new_m', 'new_m_final_norm_w': 'new_m', 'new_v_attn_norm_w': 'new_v', 'new_v_w_in': 'new_v', 'new_v_conv_w': 'new_v', 'new_v_conv_b': 'new_v', 'new_v_dt_bias': 'new_v', 'new_v_a_log': 'new_v', 'new_v_d_skip': 'new_v', 'new_v_ssd_norm_w': 'new_v', 'new_v_pool_w': 'new_v', 'new_v_pool_scale': 'new_v', 'new_v_w_out': 'new_v', 'new_v_ffn_norm_w': 'new_v', 'new_v_w_gate': 'new_v', 'new_v_w_up': 'new_v', 'new_v_w_down': 'new_v', 'new_v_final_norm_w': 'new_v'}


def _forward(args):
    return _fwd_reference(*[args[k] for k in FWD_PARAMS])


def _output_shape():
    def fwd():
        inp = _fwd_setup_inputs(0)
        return _fwd_reference(*[inp[k] for k in FWD_PARAMS])
    out = _jax.eval_shape(fwd)
    return out.shape, out.dtype

N_MICROBATCH = 1
ADAM_LR = 0.001
ADAM_B1 = 0.9
ADAM_B2 = 0.999
ADAM_EPS = 1e-08
ADAM_WD = 0.01
ADAM_STEP = 10
PER_EXAMPLE_BATCH_AXIS = {'x': 0, 'loss_target': 0}
SHARED_INPUTS = []
_WEIGHT_DTYPES = {'attn_norm_w': _jnp.float32, 'w_in': _jnp.float32, 'conv_w': _jnp.float32, 'conv_b': _jnp.float32, 'dt_bias': _jnp.float32, 'a_log': _jnp.float32, 'd_skip': _jnp.float32, 'ssd_norm_w': _jnp.float32, 'pool_w': _jnp.float32, 'pool_scale': _jnp.float32, 'w_out': _jnp.float32, 'ffn_norm_w': _jnp.float32, 'w_gate': _jnp.float32, 'w_up': _jnp.float32, 'w_down': _jnp.float32, 'final_norm_w': _jnp.float32}
MOMENT_SCALE = {'attn_norm_w': 1.250245e-01, 'w_in': 6.508515e-02, 'conv_w': 6.090982e-02, 'conv_b': 9.105735e-02, 'dt_bias': 1.790665e-01, 'a_log': 1.998958e-01, 'd_skip': 3.810444e-01, 'ssd_norm_w': 7.338259e-02, 'pool_w': 6.171850e-02, 'pool_scale': 6.318369e-02, 'w_out': 9.341076e-02, 'ffn_norm_w': 7.263289e-02, 'w_gate': 3.118440e-02, 'w_up': 3.018374e-02, 'w_down': 4.998104e-02, 'final_norm_w': 3.196880e+01}


def _to_microbatches(a, axis):
    t = _jnp.moveaxis(a, axis, 0)
    t = t.reshape((N_MICROBATCH, t.shape[0] // N_MICROBATCH) + t.shape[1:])
    return _jnp.moveaxis(t, 1, axis + 1)


def setup_inputs(seed: int = 0) -> dict:
    inp = _fwd_setup_inputs(seed)
    key = _jax.random.fold_in(_jax.random.key(seed), 7919)
    shape, _ = _output_shape()
    out = dict(inp)
    out["loss_target"] = _jax.random.normal(_jax.random.fold_in(key, 0), shape, _jnp.float32)
    for i, name in enumerate(TWIN_WEIGHTS):
        w = inp[name].astype(_jnp.float32)
        if MOMENT_SCALE is None:
            s = _jnp.sqrt(_jnp.mean(_jnp.square(w)) + 1e-30)
        else:
            s = MOMENT_SCALE[name]
        km, kv = _jax.random.split(_jax.random.fold_in(key, i + 1))
        out[name] = w
        out["m_" + name] = s * _jax.random.normal(km, w.shape, _jnp.float32)
        out["v_" + name] = (s * s) * _jax.random.uniform(kv, w.shape, _jnp.float32, 0.5, 1.5)
    if N_MICROBATCH > 1:
        for name, axis in PER_EXAMPLE_BATCH_AXIS.items():
            out[name] = _to_microbatches(out[name], axis)
    return {'x': out['x'], 'attn_norm_w': out['attn_norm_w'], 'w_in': out['w_in'], 'conv_w': out['conv_w'], 'conv_b': out['conv_b'], 'dt_bias': out['dt_bias'], 'a_log': out['a_log'], 'd_skip': out['d_skip'], 'ssd_norm_w': out['ssd_norm_w'], 'pool_w': out['pool_w'], 'pool_scale': out['pool_scale'], 'w_out': out['w_out'], 'ffn_norm_w': out['ffn_norm_w'], 'w_gate': out['w_gate'], 'w_up': out['w_up'], 'w_down': out['w_down'], 'final_norm_w': out['final_norm_w'], 'loss_target': out['loss_target'], 'm_attn_norm_w': out['m_attn_norm_w'], 'm_w_in': out['m_w_in'], 'm_conv_w': out['m_conv_w'], 'm_conv_b': out['m_conv_b'], 'm_dt_bias': out['m_dt_bias'], 'm_a_log': out['m_a_log'], 'm_d_skip': out['m_d_skip'], 'm_ssd_norm_w': out['m_ssd_norm_w'], 'm_pool_w': out['m_pool_w'], 'm_pool_scale': out['m_pool_scale'], 'm_w_out': out['m_w_out'], 'm_ffn_norm_w': out['m_ffn_norm_w'], 'm_w_gate': out['m_w_gate'], 'm_w_up': out['m_w_up'], 'm_w_down': out['m_w_down'], 'm_final_norm_w': out['m_final_norm_w'], 'v_attn_norm_w': out['v_attn_norm_w'], 'v_w_in': out['v_w_in'], 'v_conv_w': out['v_conv_w'], 'v_conv_b': out['v_conv_b'], 'v_dt_bias': out['v_dt_bias'], 'v_a_log': out['v_a_log'], 'v_d_skip': out['v_d_skip'], 'v_ssd_norm_w': out['v_ssd_norm_w'], 'v_pool_w': out['v_pool_w'], 'v_pool_scale': out['v_pool_scale'], 'v_w_out': out['v_w_out'], 'v_ffn_norm_w': out['v_ffn_norm_w'], 'v_w_gate': out['v_w_gate'], 'v_w_up': out['v_w_up'], 'v_w_down': out['v_w_down'], 'v_final_norm_w': out['v_final_norm_w']}


def _loss(weights, diff, rest, loss_target):
    with _jax.named_scope("forward"):
        args = {**rest, TWIN_DIFF_INPUT: diff, **{k: w.astype(_WEIGHT_DTYPES[k]) for k, w in weights.items()}}
        y = _forward(args)
    with _jax.named_scope("loss_head"):
        err = _jnp.square(y.astype(_jnp.float32) - loss_target)
        return 0.5 * _jnp.sum(_jnp.mean(err, axis=-1)) if err.ndim else 0.5 * err


def _adamw(w, g, m, v):
    m = ADAM_B1 * m + (1.0 - ADAM_B1) * g
    v = ADAM_B2 * v + (1.0 - ADAM_B2) * _jnp.square(g)
    m_hat = m / (1.0 - ADAM_B1 ** ADAM_STEP)
    v_hat = v / (1.0 - ADAM_B2 ** ADAM_STEP)
    delta = -ADAM_LR * (m_hat / (_jnp.sqrt(v_hat) + ADAM_EPS) + ADAM_WD * w)
    return delta, m, v


def reference(x, attn_norm_w, w_in, conv_w, conv_b, dt_bias, a_log, d_skip, ssd_norm_w, pool_w, pool_scale, w_out, ffn_norm_w, w_gate, w_up, w_down, final_norm_w, loss_target, m_attn_norm_w, m_w_in, m_conv_w, m_conv_b, m_dt_bias, m_a_log, m_d_skip, m_ssd_norm_w, m_pool_w, m_pool_scale, m_w_out, m_ffn_norm_w, m_w_gate, m_w_up, m_w_down, m_final_norm_w, v_attn_norm_w, v_w_in, v_conv_w, v_conv_b, v_dt_bias, v_a_log, v_d_skip, v_ssd_norm_w, v_pool_w, v_pool_scale, v_w_out, v_ffn_norm_w, v_w_gate, v_w_up, v_w_down, v_final_norm_w):
    given = dict(x=x, attn_norm_w=attn_norm_w, w_in=w_in, conv_w=conv_w, conv_b=conv_b, dt_bias=dt_bias, a_log=a_log, d_skip=d_skip, ssd_norm_w=ssd_norm_w, pool_w=pool_w, pool_scale=pool_scale, w_out=w_out, ffn_norm_w=ffn_norm_w, w_gate=w_gate, w_up=w_up, w_down=w_down, final_norm_w=final_norm_w, loss_target=loss_target, m_attn_norm_w=m_attn_norm_w, m_w_in=m_w_in, m_conv_w=m_conv_w, m_conv_b=m_conv_b, m_dt_bias=m_dt_bias, m_a_log=m_a_log, m_d_skip=m_d_skip, m_ssd_norm_w=m_ssd_norm_w, m_pool_w=m_pool_w, m_pool_scale=m_pool_scale, m_w_out=m_w_out, m_ffn_norm_w=m_ffn_norm_w, m_w_gate=m_w_gate, m_w_up=m_w_up, m_w_down=m_w_down, m_final_norm_w=m_final_norm_w, v_attn_norm_w=v_attn_norm_w, v_w_in=v_w_in, v_conv_w=v_conv_w, v_conv_b=v_conv_b, v_dt_bias=v_dt_bias, v_a_log=v_a_log, v_d_skip=v_d_skip, v_ssd_norm_w=v_ssd_norm_w, v_pool_w=v_pool_w, v_pool_scale=v_pool_scale, v_w_out=v_w_out, v_ffn_norm_w=v_ffn_norm_w, v_w_gate=v_w_gate, v_w_up=v_w_up, v_w_down=v_w_down, v_final_norm_w=v_final_norm_w)
    weights = {n: given[n] for n in TWIN_WEIGHTS}
    shared = {n: given[n] for n in SHARED_INPUTS}
    per_example = {n: given[n] for n in ['x']}
    grad_fn = _jax.value_and_grad(_loss, argnums=(0, 1))

    def one_microbatch(ex, loss_target):
        ex = dict(ex)
        diff = ex.pop(TWIN_DIFF_INPUT)
        return grad_fn(weights, diff, {**shared, **ex}, loss_target)

    if N_MICROBATCH == 1:
        loss, (grad_w, grad_x) = one_microbatch(per_example, given["loss_target"])
    else:
        def body(carry, xs):
            loss_sum, grad_sum = carry
            l_k, (gw_k, gx_k) = one_microbatch(xs[0], xs[1])
            with _jax.named_scope("update"):
                return (loss_sum + l_k, _jax.tree.map(_jnp.add, grad_sum, gw_k)), gx_k

        init = (_jnp.zeros((), _jnp.float32), _jax.tree.map(_jnp.zeros_like, weights))
        (loss, grad_w), grad_x = _jax.lax.scan(body, init, (per_example, given["loss_target"]))
    with _jax.named_scope("update"):
        delta_w, new_m, new_v = {}, {}, {}
        for n in TWIN_WEIGHTS:
            delta_w[n], new_m[n], new_v[n] = _adamw(weights[n], grad_w[n], given["m_" + n], given["v_" + n])
    return (loss, grad_x, *[grad_w[n] for n in TWIN_WEIGHTS], *[delta_w[n] for n in TWIN_WEIGHTS],
            *[new_m[n] for n in TWIN_WEIGHTS], *[new_v[n] for n in TWIN_WEIGHTS])
```

```python
import functools
import math

import jax
import jax.numpy as jnp
from jax import lax
from jax.experimental import pallas as pl
from jax.experimental.pallas import tpu as pltpu

F32 = jnp.float32
BF16 = jnp.bfloat16

NORM_EPS = 1e-5
HEAD_DIM = 64
SSD_GROUPS = 4
CONV_WIDTH = 4
CHUNK = 256
POOL_WINDOWS = (2, 4, 8, 16)
ADAM_LR = 0.001
ADAM_B1 = 0.9
ADAM_B2 = 0.999
ADAM_EPS = 1e-08
ADAM_WD = 0.01
ADAM_STEP = 10

N_CHIPS = 4
N_DEV = 8
LANES = 128
HALO = 16
FLAT_W = 512
VMEM_LIMIT = 52 * 1024 * 1024
MESH = pl.DeviceIdType.MESH

NN = (((1,), (0,)), ((), ()))
NT = (((1,), (1,)), ((), ()))
TN = (((0,), (0,)), ((), ()))


def _tile(n, cap, mult):
    best = None
    for t in range(mult, min(n, cap) + 1, mult):
        if n % t == 0:
            best = t
    return best if best is not None else n


def _params(sem):
    return pltpu.CompilerParams(dimension_semantics=sem, vmem_limit_bytes=VMEM_LIMIT)


def _dot(a, b, dims):
    return lax.dot_general(a, b, dims, preferred_element_type=F32)


def _sigmoid(x):
    return 1.0 / (1.0 + jnp.exp(-x))


def _silu(x):
    return x * _sigmoid(x)


def _softplus(x):
    return jnp.maximum(x, 0.0) + jnp.log(1.0 + jnp.exp(-jnp.abs(x)))


def _mm(name, mode, pairs, M, N, K, tm, tn, tk, out_dtypes, epilogue=None, extras=(), row_extras=(),
        separate=False):
    tm, tn, tk = min(tm, M), min(tn, N), min(tk, K)
    assert M % tm == 0 and N % tn == 0 and K % tk == 0, (name, M, N, K, tm, tn, tk)
    nk = K // tk
    npairs = len(pairs)
    nacc = npairs if separate else 1
    if mode == "nn":
        a_spec = pl.BlockSpec((tm, tk), lambda i, j, k: (i, k))
        b_spec = pl.BlockSpec((tk, tn), lambda i, j, k: (k, j))
        dims = NN
    elif mode == "nt":
        a_spec = pl.BlockSpec((tm, tk), lambda i, j, k: (i, k))
        b_spec = pl.BlockSpec((tn, tk), lambda i, j, k: (j, k))
        dims = NT
    else:
        a_spec = pl.BlockSpec((tk, tm), lambda i, j, k: (k, i))
        b_spec = pl.BlockSpec((tk, tn), lambda i, j, k: (k, j))
        dims = TN
    o_spec = pl.BlockSpec((tm, tn), lambda i, j, k: (i, j))
    r_spec = pl.BlockSpec((1, tn), lambda i, j, k: (0, j))
    if epilogue is None:
        epilogue = lambda accs, ex, rex: accs
    n_ex, n_rex, n_out = len(extras), len(row_extras), len(out_dtypes)

    def body(*refs):
        ab = refs[:2 * npairs]
        ex = refs[2 * npairs:2 * npairs + n_ex]
        rex = refs[2 * npairs + n_ex:2 * npairs + n_ex + n_rex]
        outs = refs[2 * npairs + n_ex + n_rex:2 * npairs + n_ex + n_rex + n_out]
        accs = refs[2 * npairs + n_ex + n_rex + n_out:]

        def products():
            res = [None] * nacc
            for p in range(npairs):
                d = _dot(ab[2 * p][...], ab[2 * p + 1][...], dims)
                q = p if separate else 0
                res[q] = d if res[q] is None else res[q] + d
            return res

        def finish(vals):
            res = epilogue(vals, [e[...] for e in ex], [r[...] for r in rex])
            for o, v in zip(outs, res):
                o[...] = v.astype(o.dtype)

        if nk == 1:
            finish(products())
        else:
            k = pl.program_id(2)
            prods = products()

            @pl.when(k == 0)
            def _():
                for q in range(nacc):
                    accs[q][...] = prods[q]

            @pl.when(k > 0)
            def _():
                for q in range(nacc):
                    accs[q][...] += prods[q]

            @pl.when(k == nk - 1)
            def _():
                finish([a[...] for a in accs])

    in_specs = [a_spec, b_spec] * npairs + [o_spec] * n_ex + [r_spec] * n_rex
    args = [t for p in pairs for t in p] + list(extras) + list(row_extras)
    outs = pl.pallas_call(
        body,
        grid=(M // tm, N // tn, nk),
        in_specs=in_specs,
        out_specs=[o_spec] * n_out,
        out_shape=[jax.ShapeDtypeStruct((M, N), d) for d in out_dtypes],
        scratch_shapes=[pltpu.VMEM((tm, tn), F32) for _ in range(nacc if nk > 1 else 0)],
        compiler_params=_params(("parallel", "parallel", "arbitrary")),
        name=name,
    )(*args)
    return outs


def _rms(xf, w):
    y = xf * lax.rsqrt(jnp.mean(xf * xf, axis=-1, keepdims=True) + NORM_EPS)
    return y * w


def _rms_fwd(name, x, w, tm):
    L, D = x.shape

    def body(x_ref, w_ref, o_ref):
        o_ref[...] = _rms(x_ref[...], w_ref[...]).astype(BF16)

    return pl.pallas_call(
        body, grid=(L // tm,),
        in_specs=[pl.BlockSpec((tm, D), lambda i: (i, 0)), pl.BlockSpec((1, D), lambda i: (0, 0))],
        out_specs=pl.BlockSpec((tm, D), lambda i: (i, 0)),
        out_shape=jax.ShapeDtypeStruct((L, D), BF16),
        compiler_params=_params(("parallel",)), name=name)(x, w)


def _rms_bwd(name, x, w, dparts, dres, tm, with_bf16):
    L, D = x.shape
    nparts = len(dparts)

    def body(*refs):
        x_ref, w_ref = refs[:2]
        p_refs = refs[2:2 + nparts]
        r_ref = refs[2 + nparts]
        outs = refs[3 + nparts:]
        dhn = p_refs[0][...]
        for p in p_refs[1:]:
            dhn = dhn + p[...]
        _, vjp = jax.vjp(_rms, x_ref[...], w_ref[...])
        dx, dw = vjp(dhn)
        dx = dx + r_ref[...]
        outs[0][...] = dx
        gw_ref = outs[1]

        @pl.when(pl.program_id(0) == 0)
        def _():
            gw_ref[...] = jnp.zeros_like(gw_ref)

        gw_ref[...] += dw
        if with_bf16:
            outs[2][...] = dx.astype(BF16)

    row = pl.BlockSpec((tm, D), lambda i: (i, 0))
    vec = pl.BlockSpec((1, D), lambda i: (0, 0))
    out_shape = [jax.ShapeDtypeStruct((L, D), F32), jax.ShapeDtypeStruct((1, D), F32)]
    out_specs = [row, vec]
    if with_bf16:
        out_shape.append(jax.ShapeDtypeStruct((L, D), BF16))
        out_specs.append(row)
    return pl.pallas_call(
        body, grid=(L // tm,),
        in_specs=[row, vec] + [row] * nparts + [row],
        out_specs=out_specs, out_shape=out_shape,
        compiler_params=_params(("arbitrary",)), name=name)(x, w, *dparts, dres)


def _final_loss(h2, wf, target, tm):
    L, D = h2.shape

    def body(h_ref, w_ref, t_ref, dh_ref, dhb_ref, loss_ref, gw_ref):
        t = t_ref[...]

        def f(h, w):
            err = jnp.square(_rms(h, w) - t)
            return 0.5 * jnp.sum(jnp.mean(err, axis=-1))

        val, vjp = jax.vjp(f, h_ref[...], w_ref[...])
        dh, dw = vjp(jnp.ones((), F32))
        dh_ref[...] = dh
        dhb_ref[...] = dh.astype(BF16)

        @pl.when(pl.program_id(0) == 0)
        def _():
            gw_ref[...] = jnp.zeros_like(gw_ref)
            loss_ref[...] = jnp.zeros_like(loss_ref)

        gw_ref[...] += dw
        loss_ref[...] += jnp.full(loss_ref.shape, val, F32)

    row = pl.BlockSpec((tm, D), lambda i: (i, 0))
    vec = pl.BlockSpec((1, D), lambda i: (0, 0))
    lspec = pl.BlockSpec((1, LANES), lambda i: (0, 0))
    return pl.pallas_call(
        body, grid=(L // tm,),
        in_specs=[row, vec, row],
        out_specs=[row, row, lspec, vec],
        out_shape=[jax.ShapeDtypeStruct((L, D), F32), jax.ShapeDtypeStruct((L, D), BF16),
                   jax.ShapeDtypeStruct((1, LANES), F32), jax.ShapeDtypeStruct((1, D), F32)],
        compiler_params=_params(("arbitrary",)), name="final_loss")(h2, wf, target)


def _gated(y, z, w):
    g = y * _silu(z)
    g = g * lax.rsqrt(jnp.mean(g * g, axis=-1, keepdims=True) + NORM_EPS)
    return g * w


def _gated_fwd(y, proj, w, DS, tm):
    L = y.shape[0]
    GW = DS // SSD_GROUPS

    def body(y_ref, z_ref, w_ref, o_ref):
        o_ref[...] = _gated(y_ref[...], z_ref[...], w_ref[...]).astype(BF16)

    blk = pl.BlockSpec((tm, GW), lambda i, g: (i, g))
    return pl.pallas_call(
        body, grid=(L // tm, SSD_GROUPS),
        in_specs=[blk, blk, pl.BlockSpec((1, GW), lambda i, g: (0, g))],
        out_specs=blk, out_shape=jax.ShapeDtypeStruct((L, DS), BF16),
        compiler_params=_params(("parallel", "parallel")), name="gated_fwd")(y, proj, w)


def _gated_bwd(y, proj, w, dout, DS, tm):
    L = y.shape[0]
    GW = DS // SSD_GROUPS

    def body(y_ref, z_ref, w_ref, d_ref, dy_ref, dz_ref, gw_ref):
        _, vjp = jax.vjp(_gated, y_ref[...], z_ref[...], w_ref[...])
        dy, dz, dw = vjp(d_ref[...])
        dy_ref[...] = dy
        dz_ref[...] = dz.astype(BF16)

        @pl.when(pl.program_id(1) == 0)
        def _():
            gw_ref[...] = jnp.zeros_like(gw_ref)

        gw_ref[...] += dw

    blk = pl.BlockSpec((tm, GW), lambda g, i: (i, g))
    vec = pl.BlockSpec((1, GW), lambda g, i: (0, g))
    return pl.pallas_call(
        body, grid=(SSD_GROUPS, L // tm),
        in_specs=[blk, blk, vec, blk],
        out_specs=[blk, blk, vec],
        out_shape=[jax.ShapeDtypeStruct((L, DS), F32), jax.ShapeDtypeStruct((L, DS), BF16),
                   jax.ShapeDtypeStruct((1, DS), F32)],
        compiler_params=_params(("parallel", "arbitrary")), name="gated_bwd")(y, proj, w, dout)


def _halo_prev(tm, cw, col0):
    return pl.BlockSpec((HALO, cw), lambda i, j: (jnp.maximum(i * (tm // HALO) - 1, 0), col0 + j))


def _halo_next(tm, cw, col0, L):
    return pl.BlockSpec((HALO, cw), lambda i, j: (jnp.minimum((i + 1) * (tm // HALO), L // HALO - 1), col0 + j))


def _conv_fwd(proj, conv_w, conv_b, DS, DCONV, tm, cw):
    L = proj.shape[0]
    col0 = DS // cw
    K = CONV_WIDTH

    def body(x_ref, p_ref, w_ref, b_ref, o_ref, ext):
        i = pl.program_id(0)
        ext[0:HALO, :] = jnp.where(i == 0, 0.0, p_ref[...])
        ext[HALO:, :] = x_ref[...]
        acc = jnp.broadcast_to(b_ref[...], (tm, cw))
        for k in range(K):
            acc = acc + w_ref[k:k + 1, :] * ext[pl.ds(HALO - (K - 1) + k, tm), :]
        o_ref[...] = _silu(acc)

    return pl.pallas_call(
        body, grid=(L // tm, DCONV // cw),
        in_specs=[pl.BlockSpec((tm, cw), lambda i, j: (i, col0 + j)), _halo_prev(tm, cw, col0),
                  pl.BlockSpec((K, cw), lambda i, j: (0, j)), pl.BlockSpec((1, cw), lambda i, j: (0, j))],
        out_specs=pl.BlockSpec((tm, cw), lambda i, j: (i, j)),
        out_shape=jax.ShapeDtypeStruct((L, DCONV), F32),
        scratch_shapes=[pltpu.VMEM((tm + HALO, cw), F32)],
        compiler_params=_params(("parallel", "parallel")), name="conv_fwd")(proj, proj, conv_w, conv_b)


def _conv_bwd(proj, dact, conv_w, conv_b, DS, DCONV, tm, cw):
    L = proj.shape[0]
    col0 = DS // cw
    K = CONV_WIDTH
    nrt = L // tm

    def body(x_ref, p_ref, n_ref, d_ref, dn_ref, w_ref, b_ref, dx_ref, dw_ref, db_ref, ext, dext):
        i = pl.program_id(1)
        last = i == nrt - 1
        ext[0:HALO, :] = jnp.where(i == 0, 0.0, p_ref[...])
        ext[HALO:HALO + tm, :] = x_ref[...]
        ext[HALO + tm:, :] = n_ref[...]
        dfull = jnp.concatenate([d_ref[...], jnp.where(last, 0.0, dn_ref[...])], axis=0)
        acc = jnp.broadcast_to(b_ref[...], (tm + HALO, cw))
        for k in range(K):
            acc = acc + w_ref[k:k + 1, :] * ext[pl.ds(HALO - (K - 1) + k, tm + HALO), :]
        sg = _sigmoid(acc)
        dconv = dfull * (sg * (1.0 + acc * (1.0 - sg)))
        dext[...] = dconv
        dx = jnp.zeros((tm, cw), F32)
        for k in range(K):
            dx = dx + w_ref[k:k + 1, :] * dext[pl.ds(K - 1 - k, tm), :]
        dx_ref[...] = dx.astype(BF16)

        @pl.when(i == 0)
        def _():
            dw_ref[...] = jnp.zeros_like(dw_ref)
            db_ref[...] = jnp.zeros_like(db_ref)

        dtile = dext[pl.ds(0, tm), :]
        db_ref[...] += jnp.sum(dtile, axis=0, keepdims=True)
        for k in range(K):
            dw_ref[k:k + 1, :] += jnp.sum(dtile * ext[pl.ds(HALO - (K - 1) + k, tm), :], axis=0, keepdims=True)

    prev = pl.BlockSpec((HALO, cw), lambda j, i: (jnp.maximum(i * (tm // HALO) - 1, 0), col0 + j))
    nxt = pl.BlockSpec((HALO, cw), lambda j, i: (jnp.minimum((i + 1) * (tm // HALO), L // HALO - 1), col0 + j))
    dnxt = pl.BlockSpec((HALO, cw), lambda j, i: (jnp.minimum((i + 1) * (tm // HALO), L // HALO - 1), j))
    return pl.pallas_call(
        body, grid=(DCONV // cw, nrt),
        in_specs=[pl.BlockSpec((tm, cw), lambda j, i: (i, col0 + j)), prev, nxt,
                  pl.BlockSpec((tm, cw), lambda j, i: (i, j)), dnxt,
                  pl.BlockSpec((K, cw), lambda j, i: (0, j)), pl.BlockSpec((1, cw), lambda j, i: (0, j))],
        out_specs=[pl.BlockSpec((tm, cw), lambda j, i: (i, j)),
                   pl.BlockSpec((K, cw), lambda j, i: (0, j)), pl.BlockSpec((1, cw), lambda j, i: (0, j))],
        out_shape=[jax.ShapeDtypeStruct((L, DCONV), BF16), jax.ShapeDtypeStruct((K, DCONV), F32),
                   jax.ShapeDtypeStruct((1, DCONV), F32)],
        scratch_shapes=[pltpu.VMEM((tm + 2 * HALO, cw), F32), pltpu.VMEM((tm + HALO, cw), F32)],
        compiler_params=_params(("parallel", "arbitrary")), name="conv_bwd",
    )(proj, proj, proj, dact, dact, conv_w, conv_b)


def _pool_fwd(proj, pool_w, pool_scale, ucol, DP, tm):
    L = proj.shape[0]
    PG = len(POOL_WINDOWS)
    PGD = DP // PG
    col0 = ucol // PGD

    def body(u_ref, p_ref, w_ref, s_ref, pooled_ref, y_ref, ext):
        i, g = pl.program_id(0), pl.program_id(1)
        ext[0:HALO, :] = jnp.where(i == 0, 0.0, p_ref[...])
        ext[HALO:, :] = u_ref[...]
        t = i * tm + lax.broadcasted_iota(jnp.int32, (tm, 1), 0)
        for gi, win in enumerate(POOL_WINDOWS):
            @pl.when(g == gi)
            def _():
                acc = ext[pl.ds(HALO, tm), :]
                for j in range(1, win):
                    acc = acc + ext[pl.ds(HALO - j, tm), :]
                count = jnp.minimum(t + 1, win).astype(F32)
                pooled = (acc / count - u_ref[...]).astype(BF16)
                pooled_ref[...] = pooled
                y_ref[...] = (_dot(pooled, w_ref[...], NN) * s_ref[...]).astype(BF16)

    blk = pl.BlockSpec((tm, PGD), lambda i, g: (i, g))
    return pl.pallas_call(
        body, grid=(L // tm, PG),
        in_specs=[pl.BlockSpec((tm, PGD), lambda i, g: (i, col0 + g)), _halo_prev(tm, PGD, col0),
                  pl.BlockSpec((None, PGD, PGD), lambda i, g: (g, 0, 0)), pl.BlockSpec((1, PGD), lambda i, g: (0, g))],
        out_specs=[blk, blk],
        out_shape=[jax.ShapeDtypeStruct((L, DP), BF16), jax.ShapeDtypeStruct((L, DP), BF16)],
        scratch_shapes=[pltpu.VMEM((tm + HALO, PGD), F32)],
        compiler_params=_params(("parallel", "parallel")), name="pool_fwd")(proj, proj, pool_w, pool_scale)


def _pool_bwd(dy, pooled, pool_w, pool_scale, tm):
    L, DP = dy.shape
    PG = len(POOL_WINDOWS)
    PGD = DP // PG
    nrt = L // tm

    def body(d_ref, dn_ref, p_ref, w_ref, s_ref, du_ref, dw_ref, ds_ref, qext):
        g, i = pl.program_id(0), pl.program_id(1)
        last = i == nrt - 1
        dfull = jnp.concatenate([d_ref[...], jnp.where(last, 0.0, dn_ref[...])], axis=0)
        dyp = (dfull * s_ref[...]).astype(BF16)
        dpooled = _dot(dyp, w_ref[...], NT)
        t = i * tm + lax.broadcasted_iota(jnp.int32, (tm + HALO, 1), 0)
        for gi, win in enumerate(POOL_WINDOWS):
            @pl.when(g == gi)
            def _():
                qext[...] = dpooled / jnp.minimum(t + 1, win).astype(F32)
                acc = qext[pl.ds(0, tm), :]
                for j in range(1, win):
                    acc = acc + qext[pl.ds(j, tm), :]
                du_ref[...] = (acc - dpooled[0:tm, :]).astype(BF16)

        @pl.when(i == 0)
        def _():
            dw_ref[...] = jnp.zeros_like(dw_ref)
            ds_ref[...] = jnp.zeros_like(ds_ref)

        pooled_t = p_ref[...]
        dw_ref[...] += _dot(pooled_t, dyp[0:tm, :], TN)
        ypre = _dot(pooled_t, w_ref[...], NN)
        ds_ref[...] += jnp.sum(d_ref[...] * ypre, axis=0, keepdims=True)

    blk = pl.BlockSpec((tm, PGD), lambda g, i: (i, g))
    nxt = pl.BlockSpec((HALO, PGD), lambda g, i: (jnp.minimum((i + 1) * (tm // HALO), L // HALO - 1), g))
    wspec = pl.BlockSpec((None, PGD, PGD), lambda g, i: (g, 0, 0))
    vec = pl.BlockSpec((1, PGD), lambda g, i: (0, g))
    return pl.pallas_call(
        body, grid=(PG, nrt),
        in_specs=[blk, nxt, blk, wspec, vec],
        out_specs=[blk, wspec, vec],
        out_shape=[jax.ShapeDtypeStruct((L, DP), BF16), jax.ShapeDtypeStruct((PG, PGD, PGD), F32),
                   jax.ShapeDtypeStruct((1, DP), F32)],
        scratch_shapes=[pltpu.VMEM((tm + HALO, PGD), F32)],
        compiler_params=_params(("parallel", "arbitrary")), name="pool_bwd")(dy, dy, pooled, pool_w, pool_scale)


def _ssd_common(dtc_raw, dtr_raw, bc, br, ac, ar):
    ch = CHUNK
    row = lax.broadcasted_iota(jnp.int32, (ch, ch), 0)
    col = lax.broadcasted_iota(jnp.int32, (ch, ch), 1)
    lower = row >= col
    dtc = _softplus(dtc_raw + bc)
    dtr = _softplus(dtr_raw + br)
    a_c = -jnp.exp(ac)
    a_r = -jnp.exp(ar)
    hi = lax.Precision.HIGHEST
    acol = jnp.dot(lower.astype(F32), dtc * a_c, preferred_element_type=F32, precision=hi)
    arow = jnp.dot(dtr * a_r, (row <= col).astype(F32), preferred_element_type=F32, precision=hi)
    return lower, row <= col, dtc, a_c, acol, arow


def _ssd_fwd(xbc, dtc_raw, dtr_raw, bias_c, bias_r, alog_c, alog_r, dskip_c, DS, N):
    L = xbc.shape[0]
    G, P, ch = SSD_GROUPS, HEAD_DIM, CHUNK
    R = dtc_raw.shape[2]
    GW = R * P
    nc = L // ch

    def body(xs_ref, b_ref, c_ref, dtc_ref, dtr_ref, bc_ref, br_ref, ac_ref, ar_ref, dk_ref,
             y_ref, st_ref, h_ref):
        @pl.when(pl.program_id(1) == 0)
        def _():
            h_ref[...] = jnp.zeros_like(h_ref)

        lower, _, dtc, _, acol_all, arow_all = _ssd_common(
            dtc_ref[...], dtr_ref[...], bc_ref[...], br_ref[...], ac_ref[...], ar_ref[...])
        bm = b_ref[...]
        cb16 = c_ref[...].astype(BF16)
        b16 = bm.astype(BF16)
        bt16 = bm.T.astype(BF16)
        cb = _dot(cb16, b16, NT)
        dk = dk_ref[...]
        st_ref[...] = h_ref[...]
        for r in range(R):
            acol = acol_all[:, r:r + 1]
            arow = arow_all[r:r + 1, :]
            alast = acol_all[ch - 1:ch, r:r + 1]
            decay = jnp.exp(jnp.where(lower, acol - arow, -1e30))
            x_h = xs_ref[:, pl.ds(r * P, P)]
            xdt = x_h * dtc[:, r:r + 1]
            m16 = (cb * decay).astype(BF16)
            h_prev = h_ref[r]
            y = _dot(m16, xdt.astype(BF16), NN)
            y = y + jnp.exp(acol) * _dot(cb16, h_prev.astype(BF16), NN)
            y = y + dk[:, r:r + 1] * x_h
            y_ref[:, pl.ds(r * P, P)] = y
            to_end = jnp.exp(alast - acol)
            h_ref[r] = jnp.exp(alast) * h_prev + _dot(bt16, (xdt * to_end).astype(BF16), NN)

    nb = DS // N
    return pl.pallas_call(
        body, grid=(G, nc),
        in_specs=[pl.BlockSpec((ch, GW), lambda g, c: (c, g)),
                  pl.BlockSpec((ch, N), lambda g, c: (c, nb + g)),
                  pl.BlockSpec((ch, N), lambda g, c: (c, nb + G + g)),
                  pl.BlockSpec((None, ch, R), lambda g, c: (g, c, 0)),
                  pl.BlockSpec((None, R, ch), lambda g, c: (g, 0, c)),
                  pl.BlockSpec((None, 1, R), lambda g, c: (g, 0, 0)),
                  pl.BlockSpec((None, R, 1), lambda g, c: (g, 0, 0)),
                  pl.BlockSpec((None, 1, R), lambda g, c: (g, 0, 0)),
                  pl.BlockSpec((None, R, 1), lambda g, c: (g, 0, 0)),
                  pl.BlockSpec((None, 1, R), lambda g, c: (g, 0, 0))],
        out_specs=[pl.BlockSpec((ch, GW), lambda g, c: (c, g)),
                   pl.BlockSpec((None, R, N, P), lambda g, c: (c, g, 0, 0))],
        out_shape=[jax.ShapeDtypeStruct((L, DS), F32), jax.ShapeDtypeStruct((nc, G * R, N, P), F32)],
        scratch_shapes=[pltpu.VMEM((R, N, P), F32)],
        compiler_params=_params(("parallel", "arbitrary")), name="ssd_fwd",
    )(xbc, xbc, xbc, dtc_raw, dtr_raw, bias_c, bias_r, alog_c, alog_r, dskip_c)


def _ssd_bwd(xbc, dtc_raw, dtr_raw, bias_c, bias_r, alog_c, alog_r, dskip_c, dy, states, DS, N):
    L = xbc.shape[0]
    G, P, ch = SSD_GROUPS, HEAD_DIM, CHUNK
    R = dtc_raw.shape[2]
    GW = R * P
    nc = L // ch

    def body(xs_ref, b_ref, c_ref, dtc_ref, dtr_ref, bc_ref, br_ref, ac_ref, ar_ref, dk_ref,
             dy_ref, stp_ref,
             dxs_ref, db_ref, dc_ref, ddt_ref, dal_ref, ddk_ref, dbias_ref, dh_ref):
        @pl.when(pl.program_id(1) == 0)
        def _():
            dh_ref[...] = jnp.zeros_like(dh_ref)
            dal_ref[...] = jnp.zeros_like(dal_ref)
            ddk_ref[...] = jnp.zeros_like(ddk_ref)
            dbias_ref[...] = jnp.zeros_like(dbias_ref)

        lower, upper, dtc, a_c, acol_all, arow_all = _ssd_common(
            dtc_ref[...], dtr_ref[...], bc_ref[...], br_ref[...], ac_ref[...], ar_ref[...])
        bm = b_ref[...]
        cm = c_ref[...]
        b16 = bm.astype(BF16)
        c16 = cm.astype(BF16)
        ct16 = cm.T.astype(BF16)
        cb = _dot(c16, b16, NT)
        cbt = _dot(b16, c16, NT)
        dk = dk_ref[...]
        lane_r = lax.broadcasted_iota(jnp.int32, (ch, R), 1)
        lane_1 = lax.broadcasted_iota(jnp.int32, (1, R), 1)
        dcb = jnp.zeros((ch, ch), F32)
        dc = jnp.zeros((ch, N), F32)
        db = jnp.zeros((ch, N), F32)
        da_all = jnp.zeros((ch, R), F32)
        q_all = jnp.zeros((ch, R), F32)
        sxd_all = jnp.zeros((ch, R), F32)
        const = jnp.zeros((1, R), F32)
        ddk = jnp.zeros((1, R), F32)
        for r in range(R):
            acol = acol_all[:, r:r + 1]
            arow = arow_all[r:r + 1, :]
            alast = acol_all[ch - 1:ch, r:r + 1]
            decay = jnp.exp(jnp.where(lower, acol - arow, -1e30))
            decay_t = jnp.exp(jnp.where(upper, arow - acol, -1e30))
            x_h = xs_ref[:, pl.ds(r * P, P)]
            dy_h = dy_ref[:, pl.ds(r * P, P)]
            dt_h = dtc[:, r:r + 1]
            dk_h = dk[:, r:r + 1]
            xdt = x_h * dt_h
            xdt16 = xdt.astype(BF16)
            dy16 = dy_h.astype(BF16)
            h_prev = stp_ref[r]
            dh_next = dh_ref[r]
            dhn16 = dh_next.astype(BF16)
            to_end = jnp.exp(alast - acol)
            e_a = jnp.exp(acol)
            m = cb * decay
            mt = cbt * decay_t
            gm = _dot(dy16, xdt16, NT)
            gmt = _dot(xdt16, dy16, NT)
            t1 = _dot(dy16, h_prev.astype(BF16), NT)
            t2 = _dot(xdt16, dhn16, NT)
            dxdt = _dot(mt.astype(BF16), dy16, NN) + to_end * _dot(b16, dhn16, NN)
            dcb = dcb + gm * decay
            dc = dc + e_a * t1
            db = db + to_end * t2
            dh_ref[r] = jnp.exp(alast) * dh_next + _dot(ct16, (dy_h * e_a).astype(BF16), NN)
            da = (jnp.sum(gm * m, axis=1, keepdims=True) - jnp.sum(gmt * mt, axis=1, keepdims=True)
                  + e_a * jnp.sum(cm * t1, axis=1, keepdims=True))
            q = to_end * jnp.sum(bm * t2, axis=1, keepdims=True)
            da_all = da_all + jnp.where(lane_r == r, da, 0.0)
            q_all = q_all + jnp.where(lane_r == r, q, 0.0)
            sxd_all = sxd_all + jnp.where(lane_r == r, jnp.sum(dxdt * x_h, axis=1, keepdims=True), 0.0)
            const = const + jnp.where(lane_1 == r, jnp.exp(alast) * jnp.sum(dh_next * h_prev), 0.0)
            ddk = ddk + jnp.where(lane_1 == r, jnp.sum(dy_h * x_h), 0.0)
            dxs_ref[:, pl.ds(r * P, P)] = dxdt * dt_h + dk_h * dy_h
        dcb16 = dcb.astype(BF16)
        dc_ref[...] = dc + _dot(dcb16, b16, NN)
        db_ref[...] = db + _dot(dcb16, c16, TN)
        hi = lax.Precision.HIGHEST
        strict_lower = jnp.logical_and(lower, jnp.logical_not(upper))
        dda = (jnp.dot(upper.astype(F32), da_all, preferred_element_type=F32, precision=hi)
               + jnp.dot(strict_lower.astype(F32), q_all, preferred_element_type=F32, precision=hi) + const)
        ddt = dda * a_c + sxd_all
        dal_ref[...] += jnp.sum(dda * dtc, axis=0, keepdims=True) * a_c
        ddk_ref[...] += ddk
        ddt_raw = ddt * _sigmoid(dtc_ref[...] + bc_ref[...])
        ddt_ref[...] = ddt_raw
        dbias_ref[...] += jnp.sum(ddt_raw, axis=0, keepdims=True)

    nb = DS // N
    rc = lambda c: nc - 1 - c
    vec_c = pl.BlockSpec((None, 1, R), lambda g, c: (g, 0, 0))
    vec_r = pl.BlockSpec((None, R, 1), lambda g, c: (g, 0, 0))
    big = pl.BlockSpec((ch, GW), lambda g, c: (rc(c), g))
    return pl.pallas_call(
        body, grid=(G, nc),
        in_specs=[big,
                  pl.BlockSpec((ch, N), lambda g, c: (rc(c), nb + g)),
                  pl.BlockSpec((ch, N), lambda g, c: (rc(c), nb + G + g)),
                  pl.BlockSpec((None, ch, R), lambda g, c: (g, rc(c), 0)),
                  pl.BlockSpec((None, R, ch), lambda g, c: (g, 0, rc(c))),
                  vec_c, vec_r, vec_c, vec_r, vec_c,
                  big,
                  pl.BlockSpec((None, R, N, P), lambda g, c: (rc(c), g, 0, 0))],
        out_specs=[big,
                   pl.BlockSpec((ch, N), lambda g, c: (rc(c), g)),
                   pl.BlockSpec((ch, N), lambda g, c: (rc(c), g)),
                   pl.BlockSpec((None, ch, R), lambda g, c: (g, rc(c), 0)),
                   vec_c, vec_c, vec_c],
        out_shape=[jax.ShapeDtypeStruct((L, DS), F32), jax.ShapeDtypeStruct((L, G * N), F32),
                   jax.ShapeDtypeStruct((L, G * N), F32), jax.ShapeDtypeStruct((G, L, R), F32),
                   jax.ShapeDtypeStruct((G, 1, R), F32), jax.ShapeDtypeStruct((G, 1, R), F32),
                   jax.ShapeDtypeStruct((G, 1, R), F32)],
        scratch_shapes=[pltpu.VMEM((R, N, P), F32)],
        compiler_params=_params(("parallel", "arbitrary")), name="ssd_bwd",
    )(xbc, xbc, xbc, dtc_raw, dtr_raw, bias_c, bias_r, alog_c, alog_r, dskip_c, dy, states)


def _adam_math(w, g, m, v):
    m = ADAM_B1 * m + (1.0 - ADAM_B1) * g
    v = ADAM_B2 * v + (1.0 - ADAM_B2) * jnp.square(g)
    m_hat = m / (1.0 - ADAM_B1 ** ADAM_STEP)
    v_hat = v / (1.0 - ADAM_B2 ** ADAM_STEP)
    delta = -ADAM_LR * (m_hat / (jnp.sqrt(v_hat) + ADAM_EPS) + ADAM_WD * w)
    return delta, m, v


def _adam(name, w, g, m, v):
    rows, cols = w.shape
    tr = _tile(rows, max(8, (1 << 18) // cols // 8 * 8), 8)

    def body(w_ref, g_ref, m_ref, v_ref, d_ref, mo_ref, vo_ref):
        d, m2, v2 = _adam_math(w_ref[...], g_ref[...], m_ref[...], v_ref[...])
        d_ref[...] = d
        mo_ref[...] = m2
        vo_ref[...] = v2

    blk = pl.BlockSpec((tr, cols), lambda i: (i, 0))
    return pl.pallas_call(
        body, grid=(rows // tr,), in_specs=[blk] * 4, out_specs=[blk] * 3,
        out_shape=[jax.ShapeDtypeStruct((rows, cols), F32)] * 3,
        compiler_params=_params(("parallel",)), name=name)(w, g, m, v)


def _small_sum_adam(gathered, w, m, v, rows):
    def body(ga_ref, w_ref, m_ref, v_ref, g_ref, d_ref, mo_ref, vo_ref):
        g = ga_ref[0:rows, :]
        for d in range(1, N_DEV):
            g = g + ga_ref[d * rows:(d + 1) * rows, :]
        g_ref[...] = g
        dl, m2, v2 = _adam_math(w_ref[...], g, m_ref[...], v_ref[...])
        d_ref[...] = dl
        mo_ref[...] = m2
        vo_ref[...] = v2

    return pl.pallas_call(
        body, out_shape=[jax.ShapeDtypeStruct((rows, LANES), F32)] * 4,
        compiler_params=pltpu.CompilerParams(vmem_limit_bytes=VMEM_LIMIT), name="small_sum_adam",
    )(gathered, w, m, v)


def _pair_sum(gflat, recv, c_idx, Rh, tr):
    nrt = Rh // tr

    def body(c_ref, a_ref, b_ref, o_ref):
        o_ref[...] = (a_ref[...] + b_ref[...]).astype(BF16)

    return pl.pallas_call(
        body,
        grid_spec=pltpu.PrefetchScalarGridSpec(
            num_scalar_prefetch=1, grid=(N_CHIPS, nrt),
            in_specs=[pl.BlockSpec((None, tr, FLAT_W), lambda k, i, c: (k, c[0] * nrt + i, 0)),
                      pl.BlockSpec((None, tr, FLAT_W), lambda k, i, c: (k, i, 0))],
            out_specs=pl.BlockSpec((None, tr, FLAT_W), lambda k, i, c: (k, i, 0))),
        out_shape=jax.ShapeDtypeStruct((N_CHIPS, Rh, FLAT_W), BF16),
        compiler_params=_params(("parallel", "parallel")), name="pair_sum")(c_idx, gflat, recv)


def _chip_sum(parts, Rh, tr):
    def body(p_ref, o_ref):
        s = p_ref[0].astype(F32)
        for k in range(1, N_CHIPS):
            s = s + p_ref[k].astype(F32)
        o_ref[...] = s

    return pl.pallas_call(
        body, grid=(Rh // tr,),
        in_specs=[pl.BlockSpec((N_CHIPS, tr, FLAT_W), lambda i: (0, i, 0))],
        out_specs=pl.BlockSpec((tr, FLAT_W), lambda i: (i, 0)),
        out_shape=jax.ShapeDtypeStruct((Rh, FLAT_W), F32),
        compiler_params=_params(("parallel",)), name="chip_sum")(parts)


_HBM = pl.BlockSpec(memory_space=pltpu.HBM)


def _chip_xy(k):
    return k // 2, k % 2


def _gather_weights(wflat, Rh):
    R2 = 2 * Rh

    def body(w_ref, o_ref, send_sems, recv_sems, fsend_sems, frecv_sems, local_sem):
        x, y, c = lax.axis_index("x"), lax.axis_index("y"), lax.axis_index("c")
        me = 2 * x + y
        own = pltpu.make_async_copy(w_ref, o_ref.at[me], local_sem)
        own.start()

        def half(ref, hc):
            return ref.at[pl.ds(pl.multiple_of(hc * Rh, 16), Rh), :]

        def ici(k):
            kx, ky = _chip_xy(k)
            return pltpu.make_async_remote_copy(
                src_ref=half(w_ref, c), dst_ref=half(o_ref.at[me], c),
                send_sem=send_sems.at[k], recv_sem=recv_sems.at[me],
                device_id=(kx, ky, c), device_id_type=MESH)

        def ici_wait(k):
            return pltpu.make_async_remote_copy(
                src_ref=half(w_ref, c), dst_ref=half(o_ref.at[k], c),
                send_sem=send_sems.at[k], recv_sem=recv_sems.at[k],
                device_id=(x, y, c), device_id_type=MESH)

        def fwd(k, hc):
            return pltpu.make_async_remote_copy(
                src_ref=half(o_ref.at[k], hc), dst_ref=half(o_ref.at[k], hc),
                send_sem=fsend_sems.at[k], recv_sem=frecv_sems.at[k],
                device_id=(x, y, 1 - c), device_id_type=MESH)

        for k in range(N_CHIPS):
            @pl.when(k != me)
            def _():
                ici(k).start()
        for k in range(N_CHIPS):
            @pl.when(k != me)
            def _():
                ici_wait(k).wait_recv()
                fwd(k, c).start()
        for k in range(N_CHIPS):
            @pl.when(k != me)
            def _():
                fwd(k, 1 - c).wait_recv()
        for k in range(N_CHIPS):
            @pl.when(k != me)
            def _():
                ici(k).wait_send()
                fwd(k, c).wait_send()
        own.wait()

    return pl.pallas_call(
        body, in_specs=[_HBM], out_specs=_HBM,
        out_shape=jax.ShapeDtypeStruct((N_CHIPS, R2, FLAT_W), BF16),
        scratch_shapes=[pltpu.SemaphoreType.DMA((N_CHIPS,))] * 4 + [pltpu.SemaphoreType.DMA],
        name="gather_weights")(wflat)


def _swap_halves(gflat, Rh):
    def body(g_ref, o_ref, send_sems, recv_sems):
        x, y, c = lax.axis_index("x"), lax.axis_index("y"), lax.axis_index("c")
        copies = []
        for k in range(N_CHIPS):
            copies.append(pltpu.make_async_remote_copy(
                src_ref=g_ref.at[k, pl.ds(pl.multiple_of((1 - c) * Rh, 8), Rh), :], dst_ref=o_ref.at[k],
                send_sem=send_sems.at[k], recv_sem=recv_sems.at[k],
                device_id=(x, y, 1 - c), device_id_type=MESH))
        for cp in copies:
            cp.start()
        for cp in copies:
            cp.wait()

    return pl.pallas_call(
        body, in_specs=[_HBM], out_specs=_HBM,
        out_shape=jax.ShapeDtypeStruct((N_CHIPS, Rh, FLAT_W), F32),
        scratch_shapes=[pltpu.SemaphoreType.DMA((N_CHIPS,))] * 2,
        name="swap_halves")(gflat)


def _scatter_partials(part):
    def body(p_ref, o_ref, send_sems, recv_sems, local_sem):
        x, y, c = lax.axis_index("x"), lax.axis_index("y"), lax.axis_index("c")
        me = 2 * x + y
        own = pltpu.make_async_copy(p_ref.at[me], o_ref.at[me], local_sem)
        own.start()

        def send(k):
            kx, ky = _chip_xy(k)
            return pltpu.make_async_remote_copy(
                src_ref=p_ref.at[k], dst_ref=o_ref.at[me],
                send_sem=send_sems.at[k], recv_sem=recv_sems.at[me],
                device_id=(kx, ky, c), device_id_type=MESH)

        def landing(k):
            return pltpu.make_async_remote_copy(
                src_ref=p_ref.at[k], dst_ref=o_ref.at[k],
                send_sem=send_sems.at[k], recv_sem=recv_sems.at[k],
                device_id=(x, y, c), device_id_type=MESH)

        for k in range(N_CHIPS):
            @pl.when(k != me)
            def _():
                send(k).start()
        for k in range(N_CHIPS):
            @pl.when(k != me)
            def _():
                landing(k).wait_recv()
        for k in range(N_CHIPS):
            @pl.when(k != me)
            def _():
                send(k).wait_send()
        own.wait()

    return pl.pallas_call(
        body, in_specs=[_HBM], out_specs=_HBM,
        out_shape=jax.ShapeDtypeStruct(part.shape, BF16),
        scratch_shapes=[pltpu.SemaphoreType.DMA((N_CHIPS,))] * 2 + [pltpu.SemaphoreType.DMA],
        name="scatter_partials")(part)


def _join_halves(half, Rh):
    def body(h_ref, o_ref, send_sem, recv_sem, local_sem):
        x, y, c = lax.axis_index("x"), lax.axis_index("y"), lax.axis_index("c")

        def rows(hc):
            return o_ref.at[pl.ds(pl.multiple_of(hc * Rh, 8), Rh), :]

        own = pltpu.make_async_copy(h_ref, rows(c), local_sem)
        own.start()
        cp = pltpu.make_async_remote_copy(
            src_ref=h_ref, dst_ref=rows(c), send_sem=send_sem, recv_sem=recv_sem,
            device_id=(x, y, 1 - c), device_id_type=MESH)
        cp.start()
        pltpu.make_async_remote_copy(
            src_ref=h_ref, dst_ref=rows(1 - c), send_sem=send_sem, recv_sem=recv_sem,
            device_id=(x, y, 1 - c), device_id_type=MESH).wait_recv()
        cp.wait_send()
        own.wait()

    return pl.pallas_call(
        body, in_specs=[_HBM], out_specs=_HBM,
        out_shape=jax.ShapeDtypeStruct((2 * Rh, FLAT_W), F32),
        scratch_shapes=[pltpu.SemaphoreType.DMA, pltpu.SemaphoreType.DMA, pltpu.SemaphoreType.DMA],
        name="join_halves")(half)


def _all_gather_small(name, blk):
    m_per, n = blk.shape

    def body(x_ref, out_ref, send_sems, recv_sems, local_sem):
        x, y, c = lax.axis_index("x"), lax.axis_index("y"), lax.axis_index("c")
        me, sibling = (x, y, c), (x, y, 1 - c)
        chips = [(1 - x, y), (x, 1 - y), (1 - x, 1 - y)]

        def rows(px, py, pc):
            return out_ref.at[pl.ds((4 * px + 2 * py + pc) * m_per, m_per), :]

        def copy(k, block, to, src=None):
            return pltpu.make_async_remote_copy(
                src_ref=rows(*block) if src is None else src, dst_ref=rows(*block),
                send_sem=send_sems.at[k], recv_sem=recv_sems.at[k],
                device_id=to, device_id_type=MESH)

        mine = pltpu.make_async_copy(x_ref, rows(*me), local_sem)
        mine.start()
        first = [copy(0, me, sibling, src=x_ref)]
        first += [copy(1 + j, me, (*chip, c), src=x_ref) for j, chip in enumerate(chips)]
        for cp in first:
            cp.start()
        passed = [copy(4 + j, (*chip, c), sibling) for j, chip in enumerate(chips)]
        for j, chip in enumerate(chips):
            copy(1 + j, (*chip, c), me).wait_recv()
            passed[j].start()
        copy(0, sibling, me).wait_recv()
        for j, chip in enumerate(chips):
            copy(4 + j, (*chip, 1 - c), me).wait_recv()
        for cp in first + passed:
            cp.wait_send()
        mine.wait()

    return pl.pallas_call(
        body, out_shape=jax.ShapeDtypeStruct((N_DEV * m_per, n), blk.dtype),
        in_specs=[pl.BlockSpec(memory_space=pltpu.VMEM)],
        out_specs=pl.BlockSpec(memory_space=pltpu.VMEM),
        scratch_shapes=[pltpu.SemaphoreType.DMA((7,)), pltpu.SemaphoreType.DMA((7,)), pltpu.SemaphoreType.DMA],
        name=name)(blk)


def _pack_rows(vecs, width):
    parts = []
    for v in vecs:
        f = v.reshape(-1)
        pad = (-f.shape[0]) % (8 * width)
        parts.append(jnp.pad(f, (0, pad)) if pad else f)
    return jnp.concatenate(parts).reshape(-1, width)


def _unpack_rows(packed, shapes, width):
    flat = packed.reshape(-1)
    out, off = [], 0
    for s in shapes:
        n = math.prod(s)
        out.append(flat[off:off + n].reshape(s))
        off += n + ((-n) % (8 * width))
    return out


def _flat_shard(pieces, R2, dtype):
    f = jnp.concatenate([p.reshape(-1).astype(dtype) for p in pieces])
    pad = R2 * FLAT_W - f.shape[0]
    if pad:
        f = jnp.pad(f, (0, pad))
    return f.reshape(R2, FLAT_W)


def kernel(x, attn_norm_w, w_in, conv_w, conv_b, dt_bias, a_log, d_skip, ssd_norm_w, pool_w, pool_scale, w_out, ffn_norm_w, w_gate, w_up, w_down, final_norm_w, loss_target, m_attn_norm_w, m_w_in, m_conv_w, m_conv_b, m_dt_bias, m_a_log, m_d_skip, m_ssd_norm_w, m_pool_w, m_pool_scale, m_w_out, m_ffn_norm_w, m_w_gate, m_w_up, m_w_down, m_final_norm_w, v_attn_norm_w, v_w_in, v_conv_w, v_conv_b, v_dt_bias, v_a_log, v_d_skip, v_ssd_norm_w, v_pool_w, v_pool_scale, v_w_out, v_ffn_norm_w, v_w_gate, v_w_up, v_w_down, v_final_norm_w):
    G, P, PG = SSD_GROUPS, HEAD_DIM, len(POOL_WINDOWS)
    _, L, D = x.shape
    H = a_log.shape[1]
    R = H // G
    DS = H * P
    DCONV = conv_b.shape[1]
    N = (DCONV - DS) // (2 * G)
    DP = pool_scale.shape[1]
    PGD = DP // PG
    DIN = N_CHIPS * w_in.shape[2]
    DFF = N_CHIPS * w_gate.shape[2]
    DMAIN = DS + DCONV + DP
    assert DIN == DMAIN + H and DS == DP and H <= LANES

    cx, cy, cc = lax.axis_index("x"), lax.axis_index("y"), lax.axis_index("c")
    chip = 2 * cx + cy

    big = [w_in[0], pool_w[0], w_out[0], w_gate[0], w_up[0], w_down[0]]
    sizes = [math.prod(p.shape) for p in big]
    Rh = -(-sum(sizes) // (2 * FLAT_W * 16)) * 16
    R2 = 2 * Rh
    gath = _gather_weights(_flat_shard(big, R2, BF16), Rh).reshape(N_CHIPS, R2 * FLAT_W)
    offs = [sum(sizes[:i]) for i in range(len(sizes))]

    def piece(i):
        return gath[:, offs[i]:offs[i] + sizes[i]].reshape((N_CHIPS,) + big[i].shape)

    def cols(p):
        return jnp.moveaxis(p, 0, -2).reshape(p.shape[1:-1] + (N_CHIPS * p.shape[-1],))

    w_in_f = cols(piece(0))
    pool_w_f = jnp.moveaxis(piece(1), 0, 1).reshape(PG, PGD, PGD)
    w_out_f = piece(2).reshape(2 * DS, D)
    w_gate_f, w_up_f = cols(piece(3)), cols(piece(4))
    w_down_f = piece(5).reshape(DFF, D)
    w_main = jnp.concatenate([w_in_f[:, :DS + DCONV], w_in_f[:, DS + DCONV + H:]], axis=1)
    w_dt = jnp.pad(w_in_f[:, DS + DCONV:DS + DCONV + H], ((0, 0), (0, LANES - H)))
    w_out_top, w_out_bot = w_out_f[:DS], w_out_f[DS:]

    cw_rows = -(-(CONV_WIDTH * DCONV // N_CHIPS) // (8 * LANES)) * 8
    cw_blk = jnp.pad(conv_w[0].reshape(-1), (0, cw_rows * LANES - CONV_WIDTH * DCONV // N_CHIPS)).reshape(cw_rows, LANES)
    cw_all = _all_gather_small("gather_conv_w", cw_blk).reshape(N_CHIPS, 2, cw_rows * LANES)[:, 0]
    conv_w_f = cols(cw_all[:, :CONV_WIDTH * DCONV // N_CHIPS].reshape(N_CHIPS, CONV_WIDTH, DCONV // N_CHIPS))

    xl, tgt = x[0], loss_target[0]
    tm_row = _tile(L, 256, HALO)
    tm_mm = _tile(L, 1024, 16)
    hn1 = _rms_fwd("rms1_fwd", xl, attn_norm_w, tm_row)
    proj, = _mm("proj_main", "nn", [(hn1, w_main)], L, DMAIN, D, tm_mm, 512, D, [F32])
    dt_raw, = _mm("proj_dt", "nn", [(hn1, w_dt)], L, LANES, D, tm_mm, LANES, D, [F32])

    cwid = _tile(math.gcd(DS, DCONV), 512, LANES)
    tm_conv = _tile(L, 512, HALO)
    xbc = _conv_fwd(proj, conv_w_f, conv_b, DS, DCONV, tm_conv, cwid)

    dt_g = dt_raw[:, :H].reshape(L, G, R)
    dtc_raw = jnp.transpose(dt_g, (1, 0, 2))
    dtr_raw = jnp.transpose(dt_g, (1, 2, 0))
    as_c = lambda v: v.reshape(G, 1, R)
    as_r = lambda v: v.reshape(G, R, 1)
    ssd_args = (xbc, dtc_raw, dtr_raw, as_c(dt_bias), as_r(dt_bias), as_c(a_log), as_r(a_log), as_c(d_skip))
    y_ssd_raw, states = _ssd_fwd(*ssd_args, DS, N)
    y_ssd = _gated_fwd(y_ssd_raw, proj, ssd_norm_w, DS, tm_row)
    pooled, y_pool = _pool_fwd(proj, pool_w_f, pool_scale, DS + DCONV, DP, tm_conv)

    add_res = lambda accs, ex, rex: [accs[0] + ex[0]]
    h1, = _mm("out_proj", "nn", [(y_ssd, w_out_top), (y_pool, w_out_bot)], L, D, DS, tm_mm, 512, 1024, [F32],
              epilogue=add_res, extras=[xl])
    hn2 = _rms_fwd("rms2_fwd", h1, ffn_norm_w, tm_row)

    def glu(accs, ex, rex):
        return [accs[0], accs[1], (_silu(accs[0]) * accs[1])]

    tn_ff = _tile(DFF, 512, LANES)
    gate, up, act = _mm("ffn_in", "nn", [(hn2, w_gate_f), (hn2, w_up_f)], L, DFF, D, tm_mm, tn_ff, D,
                        [F32, F32, BF16], epilogue=glu, separate=True)
    tk_ff = _tile(DFF, 1024, LANES)
    h2, = _mm("ffn_out", "nn", [(act, w_down_f)], L, D, DFF, tm_mm, 512, tk_ff, [F32], epilogue=add_res, extras=[h1])
    dh2, dh2_16, loss_blk, g_final = _final_loss(h2, final_norm_w.reshape(1, D), tgt, tm_row)

    def dglu(accs, ex, rex):
        gt, u = ex
        sg = _sigmoid(gt)
        return [accs[0] * u * (sg * (1.0 + gt * (1.0 - sg))), accs[0] * (gt * sg)]

    dgate, dup = _mm("ffn_out_dx", "nt", [(dh2_16, w_down_f)], L, DFF, D, tm_mm, tn_ff, D, [BF16, BF16],
                     epilogue=dglu, extras=[gate, up])
    tk_tok = _tile(L, 512, 16)
    g_w_down, = _mm("ffn_out_dw", "tn", [(act, dh2_16)], DFF, D, L, _tile(DFF, 512, LANES), 1024, tk_tok, [F32])
    dhn2, = _mm("ffn_in_dx", "nt", [(dgate, w_gate_f), (dup, w_up_f)], L, D, DFF, tm_mm, 512, tk_ff, [F32])
    g_w_gate, g_w_up = _mm("ffn_in_dw", "tn", [(hn2, dgate), (hn2, dup)], D, DFF, L, 512, tn_ff, tk_tok, [F32, F32],
                           separate=True)
    dh1, g_ffn_norm, dh1_16 = _rms_bwd("rms2_bwd", h1, ffn_norm_w, [dhn2], dh2, tm_row, True)

    dy_ssd, dy_pool = _mm("out_proj_dx", "nt", [(dh1_16, w_out_top), (dh1_16, w_out_bot)], L, DS, D, tm_mm, 512, D,
                          [F32, F32], separate=True)
    g_w_out_top, g_w_out_bot = _mm("out_proj_dw", "tn", [(y_ssd, dh1_16), (y_pool, dh1_16)], DS, D, L, 512, 1024,
                                   tk_tok, [F32, F32], separate=True)
    du, g_pool_w, g_pool_scale = _pool_bwd(dy_pool, pooled, pool_w_f, pool_scale, tm_conv)
    dy_raw, dz, g_ssd_norm = _gated_bwd(y_ssd_raw, proj, ssd_norm_w, dy_ssd, DS, tm_row)
    dxs, db, dc, ddt_raw, g_a_log, g_d_skip, g_dt_bias = _ssd_bwd(*ssd_args, dy_raw, states, DS, N)
    dxbc_act = jnp.concatenate([dxs, db, dc], axis=1)
    dxbc, g_conv_w, g_conv_b = _conv_bwd(proj, dxbc_act, conv_w_f, conv_b, DS, DCONV, tm_conv, cwid)
    dproj = jnp.concatenate([dz, dxbc, du], axis=1)
    ddt_pad = jnp.pad(jnp.transpose(ddt_raw, (1, 0, 2)).reshape(L, H), ((0, 0), (0, LANES - H))).astype(BF16)

    tk_main = _tile(DMAIN, 1024, LANES)
    dhn1a, = _mm("proj_main_dx", "nt", [(dproj, w_main)], L, D, DMAIN, tm_mm, 512, tk_main, [F32])
    dhn1b, = _mm("proj_dt_dx", "nt", [(ddt_pad, w_dt)], L, D, LANES, tm_mm, 512, LANES, [F32])
    g_w_main, = _mm("proj_main_dw", "tn", [(hn1, dproj)], D, DMAIN, L, 512, _tile(DMAIN, 1024, LANES), tk_tok, [F32])
    g_w_dt, = _mm("proj_dt_dw", "tn", [(hn1, ddt_pad)], D, LANES, L, 512, LANES, tk_tok, [F32])
    grad_x, g_attn_norm = _rms_bwd("rms1_bwd", xl, attn_norm_w, [dhn1a, dhn1b], dh1, tm_row, False)

    g_w_in = jnp.concatenate([g_w_main[:, :DS + DCONV], g_w_dt[:, :H], g_w_main[:, DS + DCONV:]], axis=1)
    g_w_out = jnp.concatenate([g_w_out_top, g_w_out_bot], axis=0)

    def shard_cols(gfull, k):
        n = gfull.shape[-1] // N_CHIPS
        return gfull[..., k * n:(k + 1) * n]

    def shard_rows(gfull, k):
        n = gfull.shape[0] // N_CHIPS
        return gfull[k * n:(k + 1) * n]

    gflat = jnp.stack([
        _flat_shard([shard_cols(g_w_in, k), g_pool_w[:, k * (PGD // N_CHIPS):(k + 1) * (PGD // N_CHIPS), :],
                     shard_rows(g_w_out, k), shard_cols(g_w_gate, k), shard_cols(g_w_up, k), shard_rows(g_w_down, k)],
                    R2, F32)
        for k in range(N_CHIPS)])
    tr = _tile(Rh, 2048, 16)
    from_sibling = _swap_halves(gflat, Rh)
    partial = _pair_sum(gflat, from_sibling, cc.reshape(1).astype(jnp.int32), Rh, tr)
    my_half = _chip_sum(_scatter_partials(partial), Rh, tr)
    gshard = _join_halves(my_half, Rh).reshape(-1)
    big_grads = [gshard[offs[i]:offs[i] + sizes[i]].reshape(big[i].shape) for i in range(len(big))]

    small_w = [attn_norm_w, conv_b, dt_bias, a_log, d_skip, ssd_norm_w, pool_scale, ffn_norm_w, final_norm_w]
    small_m = [m_attn_norm_w, m_conv_b, m_dt_bias, m_a_log, m_d_skip, m_ssd_norm_w, m_pool_scale, m_ffn_norm_w, m_final_norm_w]
    small_v = [v_attn_norm_w, v_conv_b, v_dt_bias, v_a_log, v_d_skip, v_ssd_norm_w, v_pool_scale, v_ffn_norm_w, v_final_norm_w]
    small_g = [g_attn_norm, g_conv_b, g_dt_bias.reshape(1, H), g_a_log.reshape(1, H), g_d_skip.reshape(1, H),
               g_ssd_norm, g_pool_scale, g_ffn_norm, g_final.reshape(D)]
    extra_shapes = [(CONV_WIDTH, DCONV), (1, LANES)]
    zeros_like_extra = [jnp.zeros(s, F32) for s in extra_shapes]
    g_blk = _pack_rows(small_g + [g_conv_w, loss_blk], LANES)
    rows = g_blk.shape[0]
    gathered = _all_gather_small("gather_small_grads", g_blk)
    s_g, s_d, s_m, s_v = _small_sum_adam(gathered, _pack_rows(small_w + zeros_like_extra, LANES),
                                         _pack_rows(small_m + zeros_like_extra, LANES),
                                         _pack_rows(small_v + zeros_like_extra, LANES), rows)
    shapes = [w.shape for w in small_w] + extra_shapes
    sg_list = _unpack_rows(s_g, shapes, LANES)
    sd_list = _unpack_rows(s_d, shapes, LANES)[:len(small_w)]
    sm_list = _unpack_rows(s_m, shapes, LANES)[:len(small_w)]
    sv_list = _unpack_rows(s_v, shapes, LANES)[:len(small_w)]
    loss = sg_list[-1][0, 0]
    ncw = DCONV // N_CHIPS
    grad_conv_w = lax.dynamic_slice(sg_list[-2], (0, chip * ncw), (CONV_WIDTH, ncw))

    def adam_nd(name, w, g, m, v):
        shp = w.shape
        to2 = lambda a: a.reshape(-1, shp[-1])
        d, m2, v2 = _adam(name, to2(w), to2(g), to2(m), to2(v))
        return d.reshape(shp), m2.reshape(shp), v2.reshape(shp)

    sharded = {
        "w_in": (w_in, big_grads[0][None], m_w_in, v_w_in),
        "conv_w": (conv_w, grad_conv_w[None], m_conv_w, v_conv_w),
        "pool_w": (pool_w, big_grads[1][None], m_pool_w, v_pool_w),
        "w_out": (w_out, big_grads[2][None], m_w_out, v_w_out),
        "w_gate": (w_gate, big_grads[3][None], m_w_gate, v_w_gate),
        "w_up": (w_up, big_grads[4][None], m_w_up, v_w_up),
        "w_down": (w_down, big_grads[5][None], m_w_down, v_w_down),
    }
    upd = {n: (a[1],) + adam_nd("adam_" + n, *a) for n, a in sharded.items()}
    small_names = ["attn_norm_w", "conv_b", "dt_bias", "a_log", "d_skip", "ssd_norm_w", "pool_scale", "ffn_norm_w",
                   "final_norm_w"]
    for i, n in enumerate(small_names):
        upd[n] = (sg_list[i], sd_list[i], sm_list[i], sv_list[i])

    order = ["attn_norm_w", "w_in", "conv_w", "conv_b", "dt_bias", "a_log", "d_skip", "ssd_norm_w", "pool_w",
             "pool_scale", "w_out", "ffn_norm_w", "w_gate", "w_up", "w_down", "final_norm_w"]
    outs = [loss, grad_x[None]]
    for j in range(4):
        outs += [upd[n][j] for n in order]
    return tuple(outs)
```

```python
import functools
import math

import jax
import jax.numpy as jnp
from jax import lax
from jax.experimental import pallas as pl
from jax.experimental.pallas import tpu as pltpu

F32 = jnp.float32
BF16 = jnp.bfloat16

NORM_EPS = 1e-5
HEAD_DIM = 64
SSD_GROUPS = 4
CONV_WIDTH = 4
CHUNK = 256
POOL_WINDOWS = (2, 4, 8, 16)
ADAM_LR = 0.001
ADAM_B1 = 0.9
ADAM_B2 = 0.999
ADAM_EPS = 1e-08
ADAM_WD = 0.01
ADAM_STEP = 10

N_CHIPS = 4
N_DEV = 8
LANES = 128
HALO = 16
FLAT_W = 512
VMEM_LIMIT = 52 * 1024 * 1024
MESH = pl.DeviceIdType.MESH

NN = (((1,), (0,)), ((), ()))
NT = (((1,), (1,)), ((), ()))
TN = (((0,), (0,)), ((), ()))


def _tile(n, cap, mult):
    best = None
    for t in range(mult, min(n, cap) + 1, mult):
        if n % t == 0:
            best = t
    return best if best is not None else n


def _params(sem):
    return pltpu.CompilerParams(dimension_semantics=sem, vmem_limit_bytes=VMEM_LIMIT)


def _dot(a, b, dims):
    return lax.dot_general(a, b, dims, preferred_element_type=F32)


def _sigmoid(x):
    return 1.0 / (1.0 + jnp.exp(-x))


def _silu(x):
    return x * _sigmoid(x)


def _softplus(x):
    return jnp.maximum(x, 0.0) + jnp.log(1.0 + jnp.exp(-jnp.abs(x)))


def _mm(name, mode, pairs, M, N, K, tm, tn, tk, out_dtypes, epilogue=None, extras=(), row_extras=(),
        separate=False):
    tm, tn, tk = min(tm, M), min(tn, N), min(tk, K)
    assert M % tm == 0 and N % tn == 0 and K % tk == 0, (name, M, N, K, tm, tn, tk)
    nk = K // tk
    npairs = len(pairs)
    nacc = npairs if separate else 1
    if mode == "nn":
        a_spec = pl.BlockSpec((tm, tk), lambda i, j, k: (i, k))
        b_spec = pl.BlockSpec((tk, tn), lambda i, j, k: (k, j))
        dims = NN
    elif mode == "nt":
        a_spec = pl.BlockSpec((tm, tk), lambda i, j, k: (i, k))
        b_spec = pl.BlockSpec((tn, tk), lambda i, j, k: (j, k))
        dims = NT
    else:
        a_spec = pl.BlockSpec((tk, tm), lambda i, j, k: (k, i))
        b_spec = pl.BlockSpec((tk, tn), lambda i, j, k: (k, j))
        dims = TN
    o_spec = pl.BlockSpec((tm, tn), lambda i, j, k: (i, j))
    r_spec = pl.BlockSpec((1, tn), lambda i, j, k: (0, j))
    if epilogue is None:
        epilogue = lambda accs, ex, rex: accs
    n_ex, n_rex, n_out = len(extras), len(row_extras), len(out_dtypes)

    def body(*refs):
        ab = refs[:2 * npairs]
        ex = refs[2 * npairs:2 * npairs + n_ex]
        rex = refs[2 * npairs + n_ex:2 * npairs + n_ex + n_rex]
        outs = refs[2 * npairs + n_ex + n_rex:2 * npairs + n_ex + n_rex + n_out]
        accs = refs[2 * npairs + n_ex + n_rex + n_out:]

        def products():
            res = [None] * nacc
            for p in range(npairs):
                d = _dot(ab[2 * p][...], ab[2 * p + 1][...], dims)
                q = p if separate else 0
                res[q] = d if res[q] is None else res[q] + d
            return res

        def finish(vals):
            res = epilogue(vals, [e[...] for e in ex], [r[...] for r in rex])
            for o, v in zip(outs, res):
                o[...] = v.astype(o.dtype)

        if nk == 1:
            finish(products())
        else:
            k = pl.program_id(2)

            @pl.when(k == 0)
            def _():
                for q in range(nacc):
                    accs[q][...] = jnp.zeros_like(accs[q])

            for p in range(npairs):
                accs[p if separate else 0][...] += _dot(ab[2 * p][...], ab[2 * p + 1][...], dims)

            @pl.when(k == nk - 1)
            def _():
                finish([a[...] for a in accs])

    in_specs = [a_spec, b_spec] * npairs + [o_spec] * n_ex + [r_spec] * n_rex
    args = [t for p in pairs for t in p] + list(extras) + list(row_extras)
    outs = pl.pallas_call(
        body,
        grid=(M // tm, N // tn, nk),
        in_specs=in_specs,
        out_specs=[o_spec] * n_out,
        out_shape=[jax.ShapeDtypeStruct((M, N), d) for d in out_dtypes],
        scratch_shapes=[pltpu.VMEM((tm, tn), F32) for _ in range(nacc if nk > 1 else 0)],
        compiler_params=_params(("parallel", "parallel", "arbitrary")),
        name=name,
    )(*args)
    return outs


def _rms(xf, w):
    y = xf * lax.rsqrt(jnp.mean(xf * xf, axis=-1, keepdims=True) + NORM_EPS)
    return y * w


def _rms_fwd(name, x, w, tm):
    L, D = x.shape

    def body(x_ref, w_ref, o_ref):
        o_ref[...] = _rms(x_ref[...], w_ref[...]).astype(BF16)

    return pl.pallas_call(
        body, grid=(L // tm,),
        in_specs=[pl.BlockSpec((tm, D), lambda i: (i, 0)), pl.BlockSpec((1, D), lambda i: (0, 0))],
        out_specs=pl.BlockSpec((tm, D), lambda i: (i, 0)),
        out_shape=jax.ShapeDtypeStruct((L, D), BF16),
        compiler_params=_params(("parallel",)), name=name)(x, w)


def _rms_bwd(name, x, w, dparts, dres, tm, with_bf16):
    L, D = x.shape
    nparts = len(dparts)

    def body(*refs):
        x_ref, w_ref = refs[:2]
        p_refs = refs[2:2 + nparts]
        r_ref = refs[2 + nparts]
        outs = refs[3 + nparts:]
        dhn = p_refs[0][...]
        for p in p_refs[1:]:
            dhn = dhn + p[...]
        _, vjp = jax.vjp(_rms, x_ref[...], w_ref[...])
        dx, dw = vjp(dhn)
        dx = dx + r_ref[...]
        outs[0][...] = dx
        gw_ref = outs[1]

        @pl.when(pl.program_id(0) == 0)
        def _():
            gw_ref[...] = jnp.zeros_like(gw_ref)

        gw_ref[...] += dw
        if with_bf16:
            outs[2][...] = dx.astype(BF16)

    row = pl.BlockSpec((tm, D), lambda i: (i, 0))
    vec = pl.BlockSpec((1, D), lambda i: (0, 0))
    out_shape = [jax.ShapeDtypeStruct((L, D), F32), jax.ShapeDtypeStruct((1, D), F32)]
    out_specs = [row, vec]
    if with_bf16:
        out_shape.append(jax.ShapeDtypeStruct((L, D), BF16))
        out_specs.append(row)
    return pl.pallas_call(
        body, grid=(L // tm,),
        in_specs=[row, vec] + [row] * nparts + [row],
        out_specs=out_specs, out_shape=out_shape,
        compiler_params=_params(("arbitrary",)), name=name)(x, w, *dparts, dres)


def _final_loss(h2, wf, target, tm):
    L, D = h2.shape

    def body(h_ref, w_ref, t_ref, dh_ref, dhb_ref, loss_ref, gw_ref):
        t = t_ref[...]

        def f(h, w):
            err = jnp.square(_rms(h, w) - t)
            return 0.5 * jnp.sum(jnp.mean(err, axis=-1))

        val, vjp = jax.vjp(f, h_ref[...], w_ref[...])
        dh, dw = vjp(jnp.ones((), F32))
        dh_ref[...] = dh
        dhb_ref[...] = dh.astype(BF16)

        @pl.when(pl.program_id(0) == 0)
        def _():
            gw_ref[...] = jnp.zeros_like(gw_ref)
            loss_ref[...] = jnp.zeros_like(loss_ref)

        gw_ref[...] += dw
        loss_ref[...] += jnp.full(loss_ref.shape, val, F32)

    row = pl.BlockSpec((tm, D), lambda i: (i, 0))
    vec = pl.BlockSpec((1, D), lambda i: (0, 0))
    lspec = pl.BlockSpec((1, LANES), lambda i: (0, 0))
    return pl.pallas_call(
        body, grid=(L // tm,),
        in_specs=[row, vec, row],
        out_specs=[row, row, lspec, vec],
        out_shape=[jax.ShapeDtypeStruct((L, D), F32), jax.ShapeDtypeStruct((L, D), BF16),
                   jax.ShapeDtypeStruct((1, LANES), F32), jax.ShapeDtypeStruct((1, D), F32)],
        compiler_params=_params(("arbitrary",)), name="final_loss")(h2, wf, target)


def _gated(y, z, w):
    g = y * _silu(z)
    g = g * lax.rsqrt(jnp.mean(g * g, axis=-1, keepdims=True) + NORM_EPS)
    return g * w


def _gated_fwd(y, proj, w, DS, tm):
    L = y.shape[0]
    GW = DS // SSD_GROUPS

    def body(y_ref, z_ref, w_ref, o_ref):
        o_ref[...] = _gated(y_ref[...], z_ref[...], w_ref[...]).astype(BF16)

    blk = pl.BlockSpec((tm, GW), lambda i, g: (i, g))
    return pl.pallas_call(
        body, grid=(L // tm, SSD_GROUPS),
        in_specs=[blk, blk, pl.BlockSpec((1, GW), lambda i, g: (0, g))],
        out_specs=blk, out_shape=jax.ShapeDtypeStruct((L, DS), BF16),
        compiler_params=_params(("parallel", "parallel")), name="gated_fwd")(y, proj, w)


def _gated_bwd(y, proj, w, dout, DS, tm):
    L = y.shape[0]
    GW = DS // SSD_GROUPS

    def body(y_ref, z_ref, w_ref, d_ref, dy_ref, dz_ref, gw_ref):
        _, vjp = jax.vjp(_gated, y_ref[...], z_ref[...], w_ref[...])
        dy, dz, dw = vjp(d_ref[...])
        dy_ref[...] = dy
        dz_ref[...] = dz.astype(BF16)

        @pl.when(pl.program_id(1) == 0)
        def _():
            gw_ref[...] = jnp.zeros_like(gw_ref)

        gw_ref[...] += dw

    blk = pl.BlockSpec((tm, GW), lambda g, i: (i, g))
    vec = pl.BlockSpec((1, GW), lambda g, i: (0, g))
    return pl.pallas_call(
        body, grid=(SSD_GROUPS, L // tm),
        in_specs=[blk, blk, vec, blk],
        out_specs=[blk, blk, vec],
        out_shape=[jax.ShapeDtypeStruct((L, DS), F32), jax.ShapeDtypeStruct((L, DS), BF16),
                   jax.ShapeDtypeStruct((1, DS), F32)],
        compiler_params=_params(("parallel", "arbitrary")), name="gated_bwd")(y, proj, w, dout)


def _halo_prev(tm, cw, col0):
    return pl.BlockSpec((HALO, cw), lambda i, j: (jnp.maximum(i * (tm // HALO) - 1, 0), col0 + j))


def _halo_next(tm, cw, col0, L):
    return pl.BlockSpec((HALO, cw), lambda i, j: (jnp.minimum((i + 1) * (tm // HALO), L // HALO - 1), col0 + j))


def _conv_fwd(proj, conv_w, conv_b, DS, DCONV, tm, cw):
    L = proj.shape[0]
    col0 = DS // cw
    K = CONV_WIDTH

    def body(x_ref, p_ref, w_ref, b_ref, o_ref, ext):
        i = pl.program_id(0)
        ext[0:HALO, :] = jnp.where(i == 0, 0.0, p_ref[...])
        ext[HALO:, :] = x_ref[...]
        acc = jnp.broadcast_to(b_ref[...], (tm, cw))
        for k in range(K):
            acc = acc + w_ref[k:k + 1, :] * ext[pl.ds(HALO - (K - 1) + k, tm), :]
        o_ref[...] = _silu(acc)

    return pl.pallas_call(
        body, grid=(L // tm, DCONV // cw),
        in_specs=[pl.BlockSpec((tm, cw), lambda i, j: (i, col0 + j)), _halo_prev(tm, cw, col0),
                  pl.BlockSpec((K, cw), lambda i, j: (0, j)), pl.BlockSpec((1, cw), lambda i, j: (0, j))],
        out_specs=pl.BlockSpec((tm, cw), lambda i, j: (i, j)),
        out_shape=jax.ShapeDtypeStruct((L, DCONV), F32),
        scratch_shapes=[pltpu.VMEM((tm + HALO, cw), F32)],
        compiler_params=_params(("parallel", "parallel")), name="conv_fwd")(proj, proj, conv_w, conv_b)


def _conv_bwd(proj, dact, conv_w, conv_b, DS, DCONV, tm, cw):
    L = proj.shape[0]
    col0 = DS // cw
    K = CONV_WIDTH
    nrt = L // tm

    def body(x_ref, p_ref, n_ref, d_ref, dn_ref, w_ref, b_ref, dx_ref, dw_ref, db_ref, ext, dext):
        i = pl.program_id(1)
        last = i == nrt - 1
        ext[0:HALO, :] = jnp.where(i == 0, 0.0, p_ref[...])
        ext[HALO:HALO + tm, :] = x_ref[...]
        ext[HALO + tm:, :] = n_ref[...]
        dfull = jnp.concatenate([d_ref[...], jnp.where(last, 0.0, dn_ref[...])], axis=0)
        acc = jnp.broadcast_to(b_ref[...], (tm + HALO, cw))
        for k in range(K):
            acc = acc + w_ref[k:k + 1, :] * ext[pl.ds(HALO - (K - 1) + k, tm + HALO), :]
        sg = _sigmoid(acc)
        dconv = dfull * (sg * (1.0 + acc * (1.0 - sg)))
        dext[...] = dconv
        dx = jnp.zeros((tm, cw), F32)
        for k in range(K):
            dx = dx + w_ref[k:k + 1, :] * dext[pl.ds(K - 1 - k, tm), :]
        dx_ref[...] = dx.astype(BF16)

        @pl.when(i == 0)
        def _():
            dw_ref[...] = jnp.zeros_like(dw_ref)
            db_ref[...] = jnp.zeros_like(db_ref)

        dtile = dext[pl.ds(0, tm), :]
        db_ref[...] += jnp.sum(dtile, axis=0, keepdims=True)
        for k in range(K):
            dw_ref[k:k + 1, :] += jnp.sum(dtile * ext[pl.ds(HALO - (K - 1) + k, tm), :], axis=0, keepdims=True)

    prev = pl.BlockSpec((HALO, cw), lambda j, i: (jnp.maximum(i * (tm // HALO) - 1, 0), col0 + j))
    nxt = pl.BlockSpec((HALO, cw), lambda j, i: (jnp.minimum((i + 1) * (tm // HALO), L // HALO - 1), col0 + j))
    dnxt = pl.BlockSpec((HALO, cw), lambda j, i: (jnp.minimum((i + 1) * (tm // HALO), L // HALO - 1), j))
    return pl.pallas_call(
        body, grid=(DCONV // cw, nrt),
        in_specs=[pl.BlockSpec((tm, cw), lambda j, i: (i, col0 + j)), prev, nxt,
                  pl.BlockSpec((tm, cw), lambda j, i: (i, j)), dnxt,
                  pl.BlockSpec((K, cw), lambda j, i: (0, j)), pl.BlockSpec((1, cw), lambda j, i: (0, j))],
        out_specs=[pl.BlockSpec((tm, cw), lambda j, i: (i, j)),
                   pl.BlockSpec((K, cw), lambda j, i: (0, j)), pl.BlockSpec((1, cw), lambda j, i: (0, j))],
        out_shape=[jax.ShapeDtypeStruct((L, DCONV), BF16), jax.ShapeDtypeStruct((K, DCONV), F32),
                   jax.ShapeDtypeStruct((1, DCONV), F32)],
        scratch_shapes=[pltpu.VMEM((tm + 2 * HALO, cw), F32), pltpu.VMEM((tm + HALO, cw), F32)],
        compiler_params=_params(("parallel", "arbitrary")), name="conv_bwd",
    )(proj, proj, proj, dact, dact, conv_w, conv_b)


def _pool_fwd(proj, pool_w, pool_scale, ucol, DP, tm):
    L = proj.shape[0]
    PG = len(POOL_WINDOWS)
    PGD = DP // PG
    col0 = ucol // PGD

    def body(u_ref, p_ref, w_ref, s_ref, pooled_ref, y_ref, ext):
        i, g = pl.program_id(0), pl.program_id(1)
        ext[0:HALO, :] = jnp.where(i == 0, 0.0, p_ref[...])
        ext[HALO:, :] = u_ref[...]
        t = i * tm + lax.broadcasted_iota(jnp.int32, (tm, 1), 0)
        for gi, win in enumerate(POOL_WINDOWS):
            @pl.when(g == gi)
            def _():
                acc = ext[pl.ds(HALO, tm), :]
                for j in range(1, win):
                    acc = acc + ext[pl.ds(HALO - j, tm), :]
                count = jnp.minimum(t + 1, win).astype(F32)
                pooled = (acc / count - u_ref[...]).astype(BF16)
                pooled_ref[...] = pooled
                y_ref[...] = (_dot(pooled, w_ref[...], NN) * s_ref[...]).astype(BF16)

    blk = pl.BlockSpec((tm, PGD), lambda i, g: (i, g))
    return pl.pallas_call(
        body, grid=(L // tm, PG),
        in_specs=[pl.BlockSpec((tm, PGD), lambda i, g: (i, col0 + g)), _halo_prev(tm, PGD, col0),
                  pl.BlockSpec((None, PGD, PGD), lambda i, g: (g, 0, 0)), pl.BlockSpec((1, PGD), lambda i, g: (0, g))],
        out_specs=[blk, blk],
        out_shape=[jax.ShapeDtypeStruct((L, DP), BF16), jax.ShapeDtypeStruct((L, DP), BF16)],
        scratch_shapes=[pltpu.VMEM((tm + HALO, PGD), F32)],
        compiler_params=_params(("parallel", "parallel")), name="pool_fwd")(proj, proj, pool_w, pool_scale)


def _pool_bwd(dy, pooled, pool_w, pool_scale, tm):
    L, DP = dy.shape
    PG = len(POOL_WINDOWS)
    PGD = DP // PG
    nrt = L // tm

    def body(d_ref, dn_ref, p_ref, w_ref, s_ref, du_ref, dw_ref, ds_ref, qext):
        g, i = pl.program_id(0), pl.program_id(1)
        last = i == nrt - 1
        dfull = jnp.concatenate([d_ref[...], jnp.where(last, 0.0, dn_ref[...])], axis=0)
        dyp = (dfull * s_ref[...]).astype(BF16)
        dpooled = _dot(dyp, w_ref[...], NT)
        t = i * tm + lax.broadcasted_iota(jnp.int32, (tm + HALO, 1), 0)
        for gi, win in enumerate(POOL_WINDOWS):
            @pl.when(g == gi)
            def _():
                qext[...] = dpooled / jnp.minimum(t + 1, win).astype(F32)
                acc = qext[pl.ds(0, tm), :]
                for j in range(1, win):
                    acc = acc + qext[pl.ds(j, tm), :]
                du_ref[...] = (acc - dpooled[0:tm, :]).astype(BF16)

        @pl.when(i == 0)
        def _():
            dw_ref[...] = jnp.zeros_like(dw_ref)
            ds_ref[...] = jnp.zeros_like(ds_ref)

        pooled_t = p_ref[...]
        dw_ref[...] += _dot(pooled_t, dyp[0:tm, :], TN)
        ypre = _dot(pooled_t, w_ref[...], NN)
        ds_ref[...] += jnp.sum(d_ref[...] * ypre, axis=0, keepdims=True)

    blk = pl.BlockSpec((tm, PGD), lambda g, i: (i, g))
    nxt = pl.BlockSpec((HALO, PGD), lambda g, i: (jnp.minimum((i + 1) * (tm // HALO), L // HALO - 1), g))
    wspec = pl.BlockSpec((None, PGD, PGD), lambda g, i: (g, 0, 0))
    vec = pl.BlockSpec((1, PGD), lambda g, i: (0, g))
    return pl.pallas_call(
        body, grid=(PG, nrt),
        in_specs=[blk, nxt, blk, wspec, vec],
        out_specs=[blk, wspec, vec],
        out_shape=[jax.ShapeDtypeStruct((L, DP), BF16), jax.ShapeDtypeStruct((PG, PGD, PGD), F32),
                   jax.ShapeDtypeStruct((1, DP), F32)],
        scratch_shapes=[pltpu.VMEM((tm + HALO, PGD), F32)],
        compiler_params=_params(("parallel", "arbitrary")), name="pool_bwd")(dy, dy, pooled, pool_w, pool_scale)


def _ssd_common(dtc_raw, dtr_raw, bc, br, ac, ar):
    ch = CHUNK
    row = lax.broadcasted_iota(jnp.int32, (ch, ch), 0)
    col = lax.broadcasted_iota(jnp.int32, (ch, ch), 1)
    lower = row >= col
    dtc = _softplus(dtc_raw + bc)
    dtr = _softplus(dtr_raw + br)
    a_c = -jnp.exp(ac)
    a_r = -jnp.exp(ar)
    hi = lax.Precision.HIGHEST
    acol = jnp.dot(lower.astype(F32), dtc * a_c, preferred_element_type=F32, precision=hi)
    arow = jnp.dot(dtr * a_r, (row <= col).astype(F32), preferred_element_type=F32, precision=hi)
    return lower, row <= col, dtc, a_c, acol, arow


def _ssd_fwd(xbc, dtc_raw, dtr_raw, bias_c, bias_r, alog_c, alog_r, dskip_c, DS, N):
    L = xbc.shape[0]
    G, P, ch = SSD_GROUPS, HEAD_DIM, CHUNK
    R = dtc_raw.shape[2]
    GW = R * P
    nc = L // ch

    def body(xs_ref, b_ref, c_ref, dtc_ref, dtr_ref, bc_ref, br_ref, ac_ref, ar_ref, dk_ref,
             y_ref, st_ref, h_ref):
        @pl.when(pl.program_id(1) == 0)
        def _():
            h_ref[...] = jnp.zeros_like(h_ref)

        lower, _, dtc, _, acol_all, arow_all = _ssd_common(
            dtc_ref[...], dtr_ref[...], bc_ref[...], br_ref[...], ac_ref[...], ar_ref[...])
        bm = b_ref[...]
        cb16 = c_ref[...].astype(BF16)
        b16 = bm.astype(BF16)
        bt16 = bm.T.astype(BF16)
        cb = _dot(cb16, b16, NT)
        dk = dk_ref[...]
        st_ref[...] = h_ref[...]
        for r in range(R):
            acol = acol_all[:, r:r + 1]
            arow = arow_all[r:r + 1, :]
            alast = acol_all[ch - 1:ch, r:r + 1]
            decay = jnp.exp(jnp.where(lower, acol - arow, -1e30))
            x_h = xs_ref[:, pl.ds(r * P, P)]
            xdt = x_h * dtc[:, r:r + 1]
            m16 = (cb * decay).astype(BF16)
            h_prev = h_ref[r]
            y = _dot(m16, xdt.astype(BF16), NN)
            y = y + jnp.exp(acol) * _dot(cb16, h_prev.astype(BF16), NN)
            y = y + dk[:, r:r + 1] * x_h
            y_ref[:, pl.ds(r * P, P)] = y
            to_end = jnp.exp(alast - acol)
            h_ref[r] = jnp.exp(alast) * h_prev + _dot(bt16, (xdt * to_end).astype(BF16), NN)

    nb = DS // N
    return pl.pallas_call(
        body, grid=(G, nc),
        in_specs=[pl.BlockSpec((ch, GW), lambda g, c: (c, g)),
                  pl.BlockSpec((ch, N), lambda g, c: (c, nb + g)),
                  pl.BlockSpec((ch, N), lambda g, c: (c, nb + G + g)),
                  pl.BlockSpec((None, ch, R), lambda g, c: (g, c, 0)),
                  pl.BlockSpec((None, R, ch), lambda g, c: (g, 0, c)),
                  pl.BlockSpec((None, 1, R), lambda g, c: (g, 0, 0)),
                  pl.BlockSpec((None, R, 1), lambda g, c: (g, 0, 0)),
                  pl.BlockSpec((None, 1, R), lambda g, c: (g, 0, 0)),
                  pl.BlockSpec((None, R, 1), lambda g, c: (g, 0, 0)),
                  pl.BlockSpec((None, 1, R), lambda g, c: (g, 0, 0))],
        out_specs=[pl.BlockSpec((ch, GW), lambda g, c: (c, g)),
                   pl.BlockSpec((None, R, N, P), lambda g, c: (c, g, 0, 0))],
        out_shape=[jax.ShapeDtypeStruct((L, DS), F32), jax.ShapeDtypeStruct((nc, G * R, N, P), F32)],
        scratch_shapes=[pltpu.VMEM((R, N, P), F32)],
        compiler_params=_params(("parallel", "arbitrary")), name="ssd_fwd",
    )(xbc, xbc, xbc, dtc_raw, dtr_raw, bias_c, bias_r, alog_c, alog_r, dskip_c)


def _ssd_bwd(xbc, dtc_raw, dtr_raw, bias_c, bias_r, alog_c, alog_r, dskip_c, dy, states, DS, N):
    L = xbc.shape[0]
    G, P, ch = SSD_GROUPS, HEAD_DIM, CHUNK
    R = dtc_raw.shape[2]
    GW = R * P
    nc = L // ch

    def body(xs_ref, b_ref, c_ref, dtc_ref, dtr_ref, bc_ref, br_ref, ac_ref, ar_ref, dk_ref,
             dy_ref, stp_ref,
             dxs_ref, db_ref, dc_ref, ddt_ref, dal_ref, ddk_ref, dbias_ref, dh_ref):
        @pl.when(pl.program_id(1) == 0)
        def _():
            dh_ref[...] = jnp.zeros_like(dh_ref)
            dal_ref[...] = jnp.zeros_like(dal_ref)
            ddk_ref[...] = jnp.zeros_like(ddk_ref)
            dbias_ref[...] = jnp.zeros_like(dbias_ref)

        lower, upper, dtc, a_c, acol_all, arow_all = _ssd_common(
            dtc_ref[...], dtr_ref[...], bc_ref[...], br_ref[...], ac_ref[...], ar_ref[...])
        bm = b_ref[...]
        cm = c_ref[...]
        b16 = bm.astype(BF16)
        c16 = cm.astype(BF16)
        ct16 = cm.T.astype(BF16)
        cb = _dot(c16, b16, NT)
        cbt = _dot(b16, c16, NT)
        dk = dk_ref[...]
        lane_r = lax.broadcasted_iota(jnp.int32, (ch, R), 1)
        lane_1 = lax.broadcasted_iota(jnp.int32, (1, R), 1)
        dcb = jnp.zeros((ch, ch), F32)
        dc = jnp.zeros((ch, N), F32)
        db = jnp.zeros((ch, N), F32)
        da_all = jnp.zeros((ch, R), F32)
        q_all = jnp.zeros((ch, R), F32)
        sxd_all = jnp.zeros((ch, R), F32)
        const = jnp.zeros((1, R), F32)
        ddk = jnp.zeros((1, R), F32)
        for r in range(R):
            acol = acol_all[:, r:r + 1]
            arow = arow_all[r:r + 1, :]
            alast = acol_all[ch - 1:ch, r:r + 1]
            decay = jnp.exp(jnp.where(lower, acol - arow, -1e30))
            decay_t = jnp.exp(jnp.where(upper, arow - acol, -1e30))
            x_h = xs_ref[:, pl.ds(r * P, P)]
            dy_h = dy_ref[:, pl.ds(r * P, P)]
            dt_h = dtc[:, r:r + 1]
            dk_h = dk[:, r:r + 1]
            xdt = x_h * dt_h
            xdt16 = xdt.astype(BF16)
            dy16 = dy_h.astype(BF16)
            h_prev = stp_ref[r]
            dh_next = dh_ref[r]
            dhn16 = dh_next.astype(BF16)
            to_end = jnp.exp(alast - acol)
            e_a = jnp.exp(acol)
            m = cb * decay
            mt = cbt * decay_t
            gm = _dot(dy16, xdt16, NT)
            gmt = _dot(xdt16, dy16, NT)
            t1 = _dot(dy16, h_prev.astype(BF16), NT)
            t2 = _dot(xdt16, dhn16, NT)
            dxdt = _dot(mt.astype(BF16), dy16, NN) + to_end * _dot(b16, dhn16, NN)
            dcb = dcb + gm * decay
            dc = dc + e_a * t1
            db = db + to_end * t2
            dh_ref[r] = jnp.exp(alast) * dh_next + _dot(ct16, (dy_h * e_a).astype(BF16), NN)
            da = (jnp.sum(gm * m, axis=1, keepdims=True) - jnp.sum(gmt * mt, axis=1, keepdims=True)
                  + e_a * jnp.sum(cm * t1, axis=1, keepdims=True))
            q = to_end * jnp.sum(bm * t2, axis=1, keepdims=True)
            da_all = da_all + jnp.where(lane_r == r, da, 0.0)
            q_all = q_all + jnp.where(lane_r == r, q, 0.0)
            sxd_all = sxd_all + jnp.where(lane_r == r, jnp.sum(dxdt * x_h, axis=1, keepdims=True), 0.0)
            const = const + jnp.where(lane_1 == r, jnp.exp(alast) * jnp.sum(dh_next * h_prev), 0.0)
            ddk = ddk + jnp.where(lane_1 == r, jnp.sum(dy_h * x_h), 0.0)
            dxs_ref[:, pl.ds(r * P, P)] = dxdt * dt_h + dk_h * dy_h
        dcb16 = dcb.astype(BF16)
        dc_ref[...] = dc + _dot(dcb16, b16, NN)
        db_ref[...] = db + _dot(dcb16, c16, TN)
        hi = lax.Precision.HIGHEST
        strict_lower = jnp.logical_and(lower, jnp.logical_not(upper))
        dda = (jnp.dot(upper.astype(F32), da_all, preferred_element_type=F32, precision=hi)
               + jnp.dot(strict_lower.astype(F32), q_all, preferred_element_type=F32, precision=hi) + const)
        ddt = dda * a_c + sxd_all
        dal_ref[...] += jnp.sum(dda * dtc, axis=0, keepdims=True) * a_c
        ddk_ref[...] += ddk
        ddt_raw = ddt * _sigmoid(dtc_ref[...] + bc_ref[...])
        ddt_ref[...] = ddt_raw
        dbias_ref[...] += jnp.sum(ddt_raw, axis=0, keepdims=True)

    nb = DS // N
    rc = lambda c: nc - 1 - c
    vec_c = pl.BlockSpec((None, 1, R), lambda g, c: (g, 0, 0))
    vec_r = pl.BlockSpec((None, R, 1), lambda g, c: (g, 0, 0))
    big = pl.BlockSpec((ch, GW), lambda g, c: (rc(c), g))
    return pl.pallas_call(
        body, grid=(G, nc),
        in_specs=[big,
                  pl.BlockSpec((ch, N), lambda g, c: (rc(c), nb + g)),
                  pl.BlockSpec((ch, N), lambda g, c: (rc(c), nb + G + g)),
                  pl.BlockSpec((None, ch, R), lambda g, c: (g, rc(c), 0)),
                  pl.BlockSpec((None, R, ch), lambda g, c: (g, 0, rc(c))),
                  vec_c, vec_r, vec_c, vec_r, vec_c,
                  big,
                  pl.BlockSpec((None, R, N, P), lambda g, c: (rc(c), g, 0, 0))],
        out_specs=[big,
                   pl.BlockSpec((ch, N), lambda g, c: (rc(c), g)),
                   pl.BlockSpec((ch, N), lambda g, c: (rc(c), g)),
                   pl.BlockSpec((None, ch, R), lambda g, c: (g, rc(c), 0)),
                   vec_c, vec_c, vec_c],
        out_shape=[jax.ShapeDtypeStruct((L, DS), F32), jax.ShapeDtypeStruct((L, G * N), F32),
                   jax.ShapeDtypeStruct((L, G * N), F32), jax.ShapeDtypeStruct((G, L, R), F32),
                   jax.ShapeDtypeStruct((G, 1, R), F32), jax.ShapeDtypeStruct((G, 1, R), F32),
                   jax.ShapeDtypeStruct((G, 1, R), F32)],
        scratch_shapes=[pltpu.VMEM((R, N, P), F32)],
        compiler_params=_params(("parallel", "arbitrary")), name="ssd_bwd",
    )(xbc, xbc, xbc, dtc_raw, dtr_raw, bias_c, bias_r, alog_c, alog_r, dskip_c, dy, states)


def _adam_math(w, g, m, v):
    m = ADAM_B1 * m + (1.0 - ADAM_B1) * g
    v = ADAM_B2 * v + (1.0 - ADAM_B2) * jnp.square(g)
    m_hat = m / (1.0 - ADAM_B1 ** ADAM_STEP)
    v_hat = v / (1.0 - ADAM_B2 ** ADAM_STEP)
    delta = -ADAM_LR * (m_hat / (jnp.sqrt(v_hat) + ADAM_EPS) + ADAM_WD * w)
    return delta, m, v


def _adam(name, w, g, m, v):
    rows, cols = w.shape
    tr = _tile(rows, max(8, (1 << 18) // cols // 8 * 8), 8)

    def body(w_ref, g_ref, m_ref, v_ref, d_ref, mo_ref, vo_ref):
        d, m2, v2 = _adam_math(w_ref[...], g_ref[...], m_ref[...], v_ref[...])
        d_ref[...] = d
        mo_ref[...] = m2
        vo_ref[...] = v2

    blk = pl.BlockSpec((tr, cols), lambda i: (i, 0))
    return pl.pallas_call(
        body, grid=(rows // tr,), in_specs=[blk] * 4, out_specs=[blk] * 3,
        out_shape=[jax.ShapeDtypeStruct((rows, cols), F32)] * 3,
        compiler_params=_params(("parallel",)), name=name)(w, g, m, v)


def _small_sum_adam(gathered, w, m, v, rows):
    def body(ga_ref, w_ref, m_ref, v_ref, g_ref, d_ref, mo_ref, vo_ref):
        g = ga_ref[0:rows, :]
        for d in range(1, N_DEV):
            g = g + ga_ref[d * rows:(d + 1) * rows, :]
        g_ref[...] = g
        dl, m2, v2 = _adam_math(w_ref[...], g, m_ref[...], v_ref[...])
        d_ref[...] = dl
        mo_ref[...] = m2
        vo_ref[...] = v2

    return pl.pallas_call(
        body, out_shape=[jax.ShapeDtypeStruct((rows, LANES), F32)] * 4,
        compiler_params=pltpu.CompilerParams(vmem_limit_bytes=VMEM_LIMIT), name="small_sum_adam",
    )(gathered, w, m, v)


def _row_tile(rh, cols):
    return _tile(rh, max(16, (1 << 19) // cols // 16 * 16), 16)


def _pair_sum(name, g, recv, pos):
    _, r, c = g.shape
    rh = r // 2
    tr = _row_tile(rh, c)
    nrt = rh // tr

    def body(pos_ref, a_ref, b_ref, o_ref):
        o_ref[...] = (a_ref[...] + b_ref[...]).astype(BF16)

    return pl.pallas_call(
        body,
        grid_spec=pltpu.PrefetchScalarGridSpec(
            num_scalar_prefetch=1, grid=(N_CHIPS, nrt),
            in_specs=[pl.BlockSpec((None, tr, c), lambda k, i, p: (k, p[1] * nrt + i, 0)),
                      pl.BlockSpec((None, tr, c), lambda k, i, p: (k, i, 0))],
            out_specs=pl.BlockSpec((None, tr, c), lambda k, i, p: (k, i, 0))),
        out_shape=jax.ShapeDtypeStruct((N_CHIPS, rh, c), BF16),
        compiler_params=_params(("parallel", "parallel")), name=name)(pos, g, recv)


def _chip_sum(name, parts, pos):
    _, rh, c = parts.shape
    tr = _row_tile(rh, c)
    nrt = rh // tr

    def body(pos_ref, p_ref, o_ref):
        s = p_ref[0].astype(F32)
        for k in range(1, N_CHIPS):
            s = s + p_ref[k].astype(F32)
        o_ref[...] = s

    return pl.pallas_call(
        body,
        grid_spec=pltpu.PrefetchScalarGridSpec(
            num_scalar_prefetch=1, grid=(nrt,),
            in_specs=[pl.BlockSpec((N_CHIPS, tr, c), lambda i, p: (0, i, 0))],
            out_specs=pl.BlockSpec((tr, c), lambda i, p: (p[1] * nrt + i, 0))),
        out_shape=jax.ShapeDtypeStruct((2 * rh, c), F32),
        compiler_params=_params(("parallel",)), name=name)(pos, parts)


_HBM = pl.BlockSpec(memory_space=pltpu.HBM)


def _chip_xy(k):
    return k // 2, k % 2


def _half_rows(ref, hc, rh):
    return ref.at[pl.ds(pl.multiple_of(hc * rh, 16), rh), :]


def _gather_weights(shards):
    n = len(shards)

    def body(*refs):
        w_refs, o_refs = refs[:n], refs[n:2 * n]
        send_sems, recv_sems, fsend_sems, frecv_sems = refs[2 * n:]
        x, y, c = lax.axis_index("x"), lax.axis_index("y"), lax.axis_index("c")
        me = 2 * x + y

        def ici(i, k):
            rh = shards[i].shape[0] // 2
            kx, ky = _chip_xy(k)
            return pltpu.make_async_remote_copy(
                src_ref=_half_rows(w_refs[i], c, rh), dst_ref=_half_rows(o_refs[i].at[me], c, rh),
                send_sem=send_sems.at[i, k], recv_sem=recv_sems.at[i, me],
                device_id=(kx, ky, c), device_id_type=MESH)

        def landing(i, k):
            rh = shards[i].shape[0] // 2
            return pltpu.make_async_remote_copy(
                src_ref=_half_rows(w_refs[i], c, rh), dst_ref=_half_rows(o_refs[i].at[k], c, rh),
                send_sem=send_sems.at[i, k], recv_sem=recv_sems.at[i, k],
                device_id=(x, y, c), device_id_type=MESH)

        def fwd(i, k, hc):
            rh = shards[i].shape[0] // 2
            return pltpu.make_async_remote_copy(
                src_ref=_half_rows(o_refs[i].at[k], hc, rh), dst_ref=_half_rows(o_refs[i].at[k], hc, rh),
                send_sem=fsend_sems.at[i, k], recv_sem=frecv_sems.at[i, k],
                device_id=(x, y, 1 - c), device_id_type=MESH)

        for i in range(n):
            for k in range(N_CHIPS):
                @pl.when(k != me)
                def _():
                    ici(i, k).start()
        for i in range(n):
            for k in range(N_CHIPS):
                @pl.when(k != me)
                def _():
                    landing(i, k).wait_recv()
                    fwd(i, k, c).start()
        for i in range(n):
            for k in range(N_CHIPS):
                @pl.when(k != me)
                def _():
                    fwd(i, k, 1 - c).wait_recv()
        for i in range(n):
            for k in range(N_CHIPS):
                @pl.when(k != me)
                def _():
                    ici(i, k).wait_send()
                    fwd(i, k, c).wait_send()

    return pl.pallas_call(
        body, in_specs=[_HBM] * n, out_specs=[_HBM] * n,
        out_shape=[jax.ShapeDtypeStruct((N_CHIPS,) + s.shape, s.dtype) for s in shards],
        scratch_shapes=[pltpu.SemaphoreType.DMA((n, N_CHIPS))] * 4,
        name="gather_weights")(*shards)


def _swap_halves(grads):
    n = len(grads)

    def body(*refs):
        g_refs, o_refs, send_sems, recv_sems = refs[:n], refs[n:2 * n], refs[2 * n], refs[2 * n + 1]
        x, y, c = lax.axis_index("x"), lax.axis_index("y"), lax.axis_index("c")
        copies = []
        for i in range(n):
            rh = grads[i].shape[1] // 2
            for k in range(N_CHIPS):
                copies.append(pltpu.make_async_remote_copy(
                    src_ref=_half_rows(g_refs[i].at[k], 1 - c, rh), dst_ref=o_refs[i].at[k],
                    send_sem=send_sems.at[i, k], recv_sem=recv_sems.at[i, k],
                    device_id=(x, y, 1 - c), device_id_type=MESH))
        for cp in copies:
            cp.start()
        for cp in copies:
            cp.wait()

    return pl.pallas_call(
        body, in_specs=[_HBM] * n, out_specs=[_HBM] * n,
        out_shape=[jax.ShapeDtypeStruct((N_CHIPS, g.shape[1] // 2, g.shape[2]), F32) for g in grads],
        scratch_shapes=[pltpu.SemaphoreType.DMA((n, N_CHIPS))] * 2,
        name="swap_halves")(*grads)


def _scatter_partials(parts):
    n = len(parts)

    def body(*refs):
        p_refs, o_refs, send_sems, recv_sems = refs[:n], refs[n:2 * n], refs[2 * n], refs[2 * n + 1]
        x, y, c = lax.axis_index("x"), lax.axis_index("y"), lax.axis_index("c")
        me = 2 * x + y

        def send(i, k):
            kx, ky = _chip_xy(k)
            return pltpu.make_async_remote_copy(
                src_ref=p_refs[i].at[k], dst_ref=o_refs[i].at[me],
                send_sem=send_sems.at[i, k], recv_sem=recv_sems.at[i, me],
                device_id=(kx, ky, c), device_id_type=MESH)

        def landing(i, k):
            return pltpu.make_async_remote_copy(
                src_ref=p_refs[i].at[k], dst_ref=o_refs[i].at[k],
                send_sem=send_sems.at[i, k], recv_sem=recv_sems.at[i, k],
                device_id=(x, y, c), device_id_type=MESH)

        for i in range(n):
            for k in range(N_CHIPS):
                @pl.when(k != me)
                def _():
                    send(i, k).start()
        for i in range(n):
            for k in range(N_CHIPS):
                @pl.when(k != me)
                def _():
                    landing(i, k).wait_recv()
        for i in range(n):
            for k in range(N_CHIPS):
                @pl.when(k != me)
                def _():
                    send(i, k).wait_send()

    return pl.pallas_call(
        body, in_specs=[_HBM] * n, out_specs=[_HBM] * n,
        out_shape=[jax.ShapeDtypeStruct(p.shape, BF16) for p in parts],
        scratch_shapes=[pltpu.SemaphoreType.DMA((n, N_CHIPS))] * 2,
        name="scatter_partials")(*parts)


def _join_halves(bufs):
    n = len(bufs)

    def body(*refs):
        i_refs, o_refs, send_sems, recv_sems = refs[:n], refs[n:2 * n], refs[2 * n], refs[2 * n + 1]
        x, y, c = lax.axis_index("x"), lax.axis_index("y"), lax.axis_index("c")
        copies = []
        for i in range(n):
            rh = bufs[i].shape[0] // 2
            copies.append(pltpu.make_async_remote_copy(
                src_ref=_half_rows(i_refs[i], c, rh), dst_ref=_half_rows(o_refs[i], c, rh),
                send_sem=send_sems.at[i], recv_sem=recv_sems.at[i],
                device_id=(x, y, 1 - c), device_id_type=MESH))
        for cp in copies:
            cp.start()
        for i in range(n):
            rh = bufs[i].shape[0] // 2
            pltpu.make_async_remote_copy(
                src_ref=_half_rows(i_refs[i], c, rh), dst_ref=_half_rows(o_refs[i], 1 - c, rh),
                send_sem=send_sems.at[i], recv_sem=recv_sems.at[i],
                device_id=(x, y, 1 - c), device_id_type=MESH).wait_recv()
        for cp in copies:
            cp.wait_send()

    return pl.pallas_call(
        body, in_specs=[_HBM] * n, out_specs=[_HBM] * n,
        out_shape=[jax.ShapeDtypeStruct(b.shape, F32) for b in bufs],
        input_output_aliases={i: i for i in range(n)},
        scratch_shapes=[pltpu.SemaphoreType.DMA((n,))] * 2,
        name="join_halves")(*bufs)


def _all_gather_small(name, blk):
    m_per, n = blk.shape

    def body(x_ref, out_ref, send_sems, recv_sems, local_sem):
        x, y, c = lax.axis_index("x"), lax.axis_index("y"), lax.axis_index("c")
        me, sibling = (x, y, c), (x, y, 1 - c)
        chips = [(1 - x, y), (x, 1 - y), (1 - x, 1 - y)]

        def rows(px, py, pc):
            return out_ref.at[pl.ds((4 * px + 2 * py + pc) * m_per, m_per), :]

        def copy(k, block, to, src=None):
            return pltpu.make_async_remote_copy(
                src_ref=rows(*block) if src is None else src, dst_ref=rows(*block),
                send_sem=send_sems.at[k], recv_sem=recv_sems.at[k],
                device_id=to, device_id_type=MESH)

        mine = pltpu.make_async_copy(x_ref, rows(*me), local_sem)
        mine.start()
        first = [copy(0, me, sibling, src=x_ref)]
        first += [copy(1 + j, me, (*chip, c), src=x_ref) for j, chip in enumerate(chips)]
        for cp in first:
            cp.start()
        passed = [copy(4 + j, (*chip, c), sibling) for j, chip in enumerate(chips)]
        for j, chip in enumerate(chips):
            copy(1 + j, (*chip, c), me).wait_recv()
            passed[j].start()
        copy(0, sibling, me).wait_recv()
        for j, chip in enumerate(chips):
            copy(4 + j, (*chip, 1 - c), me).wait_recv()
        for cp in first + passed:
            cp.wait_send()
        mine.wait()

    return pl.pallas_call(
        body, out_shape=jax.ShapeDtypeStruct((N_DEV * m_per, n), blk.dtype),
        in_specs=[pl.BlockSpec(memory_space=pltpu.VMEM)],
        out_specs=pl.BlockSpec(memory_space=pltpu.VMEM),
        scratch_shapes=[pltpu.SemaphoreType.DMA((7,)), pltpu.SemaphoreType.DMA((7,)), pltpu.SemaphoreType.DMA],
        name=name)(blk)


def _pack_rows(vecs, width):
    parts = []
    for v in vecs:
        f = v.reshape(-1)
        pad = (-f.shape[0]) % (8 * width)
        parts.append(jnp.pad(f, (0, pad)) if pad else f)
    return jnp.concatenate(parts).reshape(-1, width)


def _unpack_rows(packed, shapes, width):
    flat = packed.reshape(-1)
    out, off = [], 0
    for s in shapes:
        n = math.prod(s)
        out.append(flat[off:off + n].reshape(s))
        off += n + ((-n) % (8 * width))
    return out


class _WinPlan:
    def __init__(self, ncol, dt0, h, dmain):
        self.ncol, self.h = ncol, h
        self.dt_shard = dt0 // ncol
        assert (dt0 + h - 1) // ncol == self.dt_shard and dmain % LANES == 0
        self.dt_local = dt0 - self.dt_shard * ncol
        to_main = lambda g: g if g <= dt0 else g - h
        self.lo = [to_main(ncol * k) for k in range(N_CHIPS)]
        self.hi = [to_main(ncol * (k + 1)) for k in range(N_CHIPS)]
        down = lambda v: v // LANES * LANES
        self.ww = max(-(-(hi - down(lo)) // LANES) * LANES for lo, hi in zip(self.lo, self.hi))
        self.ws = [min(down(lo), dmain - self.ww) for lo in self.lo]
        self.dmain = dmain

    def to_window(self, k, shard):
        if k == self.dt_shard:
            shard = jnp.concatenate([shard[:, :self.dt_local], shard[:, self.dt_local + self.h:]], axis=1)
        left = self.lo[k] - self.ws[k]
        return jnp.pad(shard, ((0, 0), (left, self.ww - left - shard.shape[1])))

    def from_window(self, k, window, dt_cols):
        left = self.lo[k] - self.ws[k]
        body = window[:, left:left + self.hi[k] - self.lo[k]]
        if k == self.dt_shard:
            body = jnp.concatenate([body[:, :self.dt_local], dt_cols, body[:, self.dt_local:]], axis=1)
        return body

    def merge(self, windows):
        cuts = sorted({0, self.dmain} | set(self.ws) | {w + self.ww for w in self.ws})
        segs = []
        for a, b in zip(cuts[:-1], cuts[1:]):
            parts = [windows[k][:, a - self.ws[k]:b - self.ws[k]] for k in range(N_CHIPS)
                     if self.ws[k] <= a and b <= self.ws[k] + self.ww]
            segs.append(functools.reduce(jnp.add, parts))
        return jnp.concatenate(segs, axis=1)

    def split(self, g_main):
        return jnp.stack([g_main[:, w:w + self.ww] for w in self.ws])


def kernel(x, attn_norm_w, w_in, conv_w, conv_b, dt_bias, a_log, d_skip, ssd_norm_w, pool_w, pool_scale, w_out, ffn_norm_w, w_gate, w_up, w_down, final_norm_w, loss_target, m_attn_norm_w, m_w_in, m_conv_w, m_conv_b, m_dt_bias, m_a_log, m_d_skip, m_ssd_norm_w, m_pool_w, m_pool_scale, m_w_out, m_ffn_norm_w, m_w_gate, m_w_up, m_w_down, m_final_norm_w, v_attn_norm_w, v_w_in, v_conv_w, v_conv_b, v_dt_bias, v_a_log, v_d_skip, v_ssd_norm_w, v_pool_w, v_pool_scale, v_w_out, v_ffn_norm_w, v_w_gate, v_w_up, v_w_down, v_final_norm_w):
    G, P, PG = SSD_GROUPS, HEAD_DIM, len(POOL_WINDOWS)
    _, L, D = x.shape
    H = a_log.shape[1]
    R = H // G
    DS = H * P
    DCONV = conv_b.shape[1]
    N = (DCONV - DS) // (2 * G)
    DP = pool_scale.shape[1]
    PGD = DP // PG
    DIN = N_CHIPS * w_in.shape[2]
    DFF = N_CHIPS * w_gate.shape[2]
    DMAIN = DS + DCONV + DP
    assert DIN == DMAIN + H and DS == DP and H <= LANES

    cx, cy, cc = lax.axis_index("x"), lax.axis_index("y"), lax.axis_index("c")
    chip = 2 * cx + cy

    win = _WinPlan(DIN // N_CHIPS, DS + DCONV, H, DMAIN)
    my_window = lax.switch(chip, [functools.partial(win.to_window, k) for k in range(N_CHIPS)], w_in[0].astype(BF16))
    shards = [my_window, pool_w[0].reshape(PG * PGD // N_CHIPS, PGD).astype(BF16), w_out[0].astype(BF16),
              w_gate[0].astype(BF16), w_up[0].astype(BF16), w_down[0].astype(BF16)]
    gathered = _gather_weights(shards)
    gathered = [lax.dynamic_update_slice(g, s[None], (chip, 0, 0)) for g, s in zip(gathered, shards)]

    def cols(p):
        return jnp.moveaxis(p, 0, -2).reshape(p.shape[1:-1] + (N_CHIPS * p.shape[-1],))

    w_main = win.merge(gathered[0])
    pool_w_f = jnp.moveaxis(gathered[1].reshape(N_CHIPS, PG, PGD // N_CHIPS, PGD), 0, 1).reshape(PG, PGD, PGD)
    w_out_f = gathered[2].reshape(2 * DS, D)
    w_gate_f, w_up_f = cols(gathered[3]), cols(gathered[4])
    w_down_f = gathered[5].reshape(DFF, D)
    w_out_top, w_out_bot = w_out_f[:DS], w_out_f[DS:]

    ncw = CONV_WIDTH * DCONV // N_CHIPS
    dt_here = jnp.where(chip == win.dt_shard, w_in[0][:, win.dt_local:win.dt_local + H], 0.0)
    start_blk = _pack_rows([conv_w[0], dt_here], LANES)
    start_all = _all_gather_small("gather_conv_w", start_blk).reshape(N_CHIPS, 2, -1)[:, 0]
    conv_w_f = cols(start_all[:, :ncw].reshape(N_CHIPS, CONV_WIDTH, DCONV // N_CHIPS))
    dt_off = ncw + (-ncw) % (8 * LANES)
    w_dt = jnp.pad(start_all[win.dt_shard, dt_off:dt_off + D * H].reshape(D, H), ((0, 0), (0, LANES - H))).astype(BF16)

    xl, tgt = x[0], loss_target[0]
    tm_row = _tile(L, 256, HALO)
    tm_mm = _tile(L, 1024, 16)
    hn1 = _rms_fwd("rms1_fwd", xl, attn_norm_w, tm_row)
    proj, = _mm("proj_main", "nn", [(hn1, w_main)], L, DMAIN, D, tm_mm, 512, D, [F32])
    dt_raw, = _mm("proj_dt", "nn", [(hn1, w_dt)], L, LANES, D, tm_mm, LANES, D, [F32])

    cwid = _tile(math.gcd(DS, DCONV), 512, LANES)
    tm_conv = _tile(L, 512, HALO)
    xbc = _conv_fwd(proj, conv_w_f, conv_b, DS, DCONV, tm_conv, cwid)

    dt_g = dt_raw[:, :H].reshape(L, G, R)
    dtc_raw = jnp.transpose(dt_g, (1, 0, 2))
    dtr_raw = jnp.transpose(dt_g, (1, 2, 0))
    as_c = lambda v: v.reshape(G, 1, R)
    as_r = lambda v: v.reshape(G, R, 1)
    ssd_args = (xbc, dtc_raw, dtr_raw, as_c(dt_bias), as_r(dt_bias), as_c(a_log), as_r(a_log), as_c(d_skip))
    y_ssd_raw, states = _ssd_fwd(*ssd_args, DS, N)
    y_ssd = _gated_fwd(y_ssd_raw, proj, ssd_norm_w, DS, tm_row)
    pooled, y_pool = _pool_fwd(proj, pool_w_f, pool_scale, DS + DCONV, DP, tm_conv)

    add_res = lambda accs, ex, rex: [accs[0] + ex[0]]
    h1, = _mm("out_proj", "nn", [(y_ssd, w_out_top), (y_pool, w_out_bot)], L, D, DS, tm_mm, 512, DS, [F32],
              epilogue=add_res, extras=[xl])
    hn2 = _rms_fwd("rms2_fwd", h1, ffn_norm_w, tm_row)

    def glu(accs, ex, rex):
        return [accs[0], accs[1], (_silu(accs[0]) * accs[1])]

    tn_ff = _tile(DFF, 512, LANES)
    gate, up, act = _mm("ffn_in", "nn", [(hn2, w_gate_f), (hn2, w_up_f)], L, DFF, D, tm_mm, tn_ff, D,
                        [F32, F32, BF16], epilogue=glu, separate=True)
    tm_half = _tile(L, 512, 16)
    h2, = _mm("ffn_out", "nn", [(act, w_down_f)], L, D, DFF, tm_half, 512, DFF, [F32], epilogue=add_res, extras=[h1])
    dh2, dh2_16, loss_blk, g_final = _final_loss(h2, final_norm_w.reshape(1, D), tgt, tm_row)

    def dglu(accs, ex, rex):
        gt, u = ex
        sg = _sigmoid(gt)
        return [accs[0] * u * (sg * (1.0 + gt * (1.0 - sg))), accs[0] * (gt * sg)]

    dgate, dup = _mm("ffn_out_dx", "nt", [(dh2_16, w_down_f)], L, DFF, D, tm_mm, tn_ff, D, [BF16, BF16],
                     epilogue=dglu, extras=[gate, up])
    tk_tok = _tile(L, 2048, 16)
    g_w_down, = _mm("ffn_out_dw", "tn", [(act, dh2_16)], DFF, D, L, _tile(DFF, 512, LANES), 1024, tk_tok, [F32])
    dhn2, = _mm("ffn_in_dx", "nt", [(dgate, w_gate_f), (dup, w_up_f)], L, D, DFF, tm_half, 512,
                _tile(DFF, DFF // 2, LANES), [F32])
    g_w_gate, g_w_up = _mm("ffn_in_dw", "tn", [(hn2, dgate), (hn2, dup)], D, DFF, L, 512, tn_ff, tk_tok, [F32, F32],
                           separate=True)
    dh1, g_ffn_norm, dh1_16 = _rms_bwd("rms2_bwd", h1, ffn_norm_w, [dhn2], dh2, tm_row, True)

    dy_ssd, dy_pool = _mm("out_proj_dx", "nt", [(dh1_16, w_out_top), (dh1_16, w_out_bot)], L, DS, D, tm_mm, 512, D,
                          [F32, F32], separate=True)
    g_w_out_top, g_w_out_bot = _mm("out_proj_dw", "tn", [(y_ssd, dh1_16), (y_pool, dh1_16)], DS, D, L, 512, 1024,
                                   tk_tok, [F32, F32], separate=True)
    du, g_pool_w, g_pool_scale = _pool_bwd(dy_pool, pooled, pool_w_f, pool_scale, tm_conv)
    dy_raw, dz, g_ssd_norm = _gated_bwd(y_ssd_raw, proj, ssd_norm_w, dy_ssd, DS, tm_row)
    dxs, db, dc, ddt_raw, g_a_log, g_d_skip, g_dt_bias = _ssd_bwd(*ssd_args, dy_raw, states, DS, N)
    dxbc_act = jnp.concatenate([dxs, db, dc], axis=1)
    dxbc, g_conv_w, g_conv_b = _conv_bwd(proj, dxbc_act, conv_w_f, conv_b, DS, DCONV, tm_conv, cwid)
    dproj = jnp.concatenate([dz, dxbc, du], axis=1)
    ddt_pad = jnp.pad(jnp.transpose(ddt_raw, (1, 0, 2)).reshape(L, H), ((0, 0), (0, LANES - H))).astype(BF16)

    tk_main = _tile(DMAIN, DMAIN // 2, LANES)
    dhn1a, = _mm("proj_main_dx", "nt", [(dproj, w_main)], L, D, DMAIN, tm_mm, 512, tk_main, [F32])
    dhn1b, = _mm("proj_dt_dx", "nt", [(ddt_pad, w_dt)], L, D, LANES, tm_mm, 512, LANES, [F32])
    g_w_main, = _mm("proj_main_dw", "tn", [(hn1, dproj)], D, DMAIN, L, 512, _tile(DMAIN, 1024, LANES), tk_tok, [F32])
    g_w_dt, = _mm("proj_dt_dw", "tn", [(hn1, ddt_pad)], D, LANES, L, 512, LANES, tk_tok, [F32])
    grad_x, g_attn_norm = _rms_bwd("rms1_bwd", xl, attn_norm_w, [dhn1a, dhn1b], dh1, tm_row, False)

    pos = jnp.stack([chip, cc]).astype(jnp.int32)
    col_shards = lambda gfull: jnp.moveaxis(gfull.reshape(gfull.shape[0], N_CHIPS, -1), 1, 0)
    full_grads = [win.split(g_w_main),
                  jnp.moveaxis(g_pool_w.reshape(PG, N_CHIPS, PGD // N_CHIPS, PGD), 1, 0).reshape(N_CHIPS, -1, PGD),
                  jnp.stack([g_w_out_top.reshape(2, DS // 2, D), g_w_out_bot.reshape(2, DS // 2, D)]).reshape(N_CHIPS, -1, D),
                  col_shards(g_w_gate), col_shards(g_w_up), g_w_down.reshape(N_CHIPS, -1, D)]
    names = ["w_in", "pool_w", "w_out", "w_gate", "w_up", "w_down"]
    from_sibling = _swap_halves(full_grads)
    partials = [_pair_sum("pair_sum_" + n, g, r, pos) for n, g, r in zip(names, full_grads, from_sibling)]
    landed = _scatter_partials(partials)
    landed = [lax.dynamic_update_slice(l, lax.dynamic_index_in_dim(p, chip, 0), (chip, 0, 0))
              for l, p in zip(landed, partials)]
    reduced = _join_halves([_chip_sum("chip_sum_" + n, l, pos) for n, l in zip(names, landed)])

    small_w = [attn_norm_w, conv_b, dt_bias, a_log, d_skip, ssd_norm_w, pool_scale, ffn_norm_w, final_norm_w]
    small_m = [m_attn_norm_w, m_conv_b, m_dt_bias, m_a_log, m_d_skip, m_ssd_norm_w, m_pool_scale, m_ffn_norm_w, m_final_norm_w]
    small_v = [v_attn_norm_w, v_conv_b, v_dt_bias, v_a_log, v_d_skip, v_ssd_norm_w, v_pool_scale, v_ffn_norm_w, v_final_norm_w]
    small_g = [g_attn_norm, g_conv_b, g_dt_bias.reshape(1, H), g_a_log.reshape(1, H), g_d_skip.reshape(1, H),
               g_ssd_norm, g_pool_scale, g_ffn_norm, g_final.reshape(D)]
    extra_shapes = [(CONV_WIDTH, DCONV), (D, H), (1, LANES)]
    zeros_like_extra = [jnp.zeros(s, F32) for s in extra_shapes]
    g_blk = _pack_rows(small_g + [g_conv_w, g_w_dt[:, :H], loss_blk], LANES)
    rows = g_blk.shape[0]
    small_all = _all_gather_small("gather_small_grads", g_blk)
    s_g, s_d, s_m, s_v = _small_sum_adam(small_all, _pack_rows(small_w + zeros_like_extra, LANES),
                                         _pack_rows(small_m + zeros_like_extra, LANES),
                                         _pack_rows(small_v + zeros_like_extra, LANES), rows)
    shapes = [w.shape for w in small_w] + extra_shapes
    sg_list = _unpack_rows(s_g, shapes, LANES)
    sd_list = _unpack_rows(s_d, shapes, LANES)[:len(small_w)]
    sm_list = _unpack_rows(s_m, shapes, LANES)[:len(small_w)]
    sv_list = _unpack_rows(s_v, shapes, LANES)[:len(small_w)]
    loss = sg_list[-1][0, 0]
    grad_conv_w = lax.dynamic_slice(sg_list[-3], (0, chip * (DCONV // N_CHIPS)), (CONV_WIDTH, DCONV // N_CHIPS))
    grad_w_in = lax.switch(chip, [functools.partial(win.from_window, k) for k in range(N_CHIPS)], reduced[0], sg_list[-2])

    def adam_nd(name, w, g, m, v):
        shp = w.shape
        to2 = lambda a: a.reshape(-1, shp[-1])
        d, m2, v2 = _adam(name, to2(w), to2(g), to2(m), to2(v))
        return d.reshape(shp), m2.reshape(shp), v2.reshape(shp)

    sharded = {
        "w_in": (w_in, grad_w_in[None], m_w_in, v_w_in),
        "conv_w": (conv_w, grad_conv_w[None], m_conv_w, v_conv_w),
        "pool_w": (pool_w, reduced[1].reshape(pool_w.shape), m_pool_w, v_pool_w),
        "w_out": (w_out, reduced[2][None], m_w_out, v_w_out),
        "w_gate": (w_gate, reduced[3][None], m_w_gate, v_w_gate),
        "w_up": (w_up, reduced[4][None], m_w_up, v_w_up),
        "w_down": (w_down, reduced[5][None], m_w_down, v_w_down),
    }
    upd = {n: (a[1],) + adam_nd("adam_" + n, *a) for n, a in sharded.items()}
    small_names = ["attn_norm_w", "conv_b", "dt_bias", "a_log", "d_skip", "ssd_norm_w", "pool_scale", "ffn_norm_w",
                   "final_norm_w"]
    for i, n in enumerate(small_names):
        upd[n] = (sg_list[i], sd_list[i], sm_list[i], sv_list[i])

    order = ["attn_norm_w", "w_in", "conv_w", "conv_b", "dt_bias", "a_log", "d_skip", "ssd_norm_w", "pool_w",
             "pool_scale", "w_out", "ffn_norm_w", "w_gate", "w_up", "w_down", "final_norm_w"]
    outs = [loss, grad_x[None]]
    for j in range(4):
        outs += [upd[n][j] for n in order]
    return tuple(outs)
```

```python
import functools
import math

import jax
import jax.numpy as jnp
from jax import lax
from jax.experimental import pallas as pl
from jax.experimental.pallas import tpu as pltpu

F32 = jnp.float32
BF16 = jnp.bfloat16

NORM_EPS = 1e-5
HEAD_DIM = 64
SSD_GROUPS = 4
CONV_WIDTH = 4
CHUNK = 256
POOL_WINDOWS = (2, 4, 8, 16)
ADAM_LR = 0.001
ADAM_B1 = 0.9
ADAM_B2 = 0.999
ADAM_EPS = 1e-08
ADAM_WD = 0.01
ADAM_STEP = 10

N_CHIPS = 4
N_DEV = 8
LANES = 128
HALO = 16
FLAT_W = 512
VMEM_LIMIT = 52 * 1024 * 1024
MESH = pl.DeviceIdType.MESH

NN = (((1,), (0,)), ((), ()))
NT = (((1,), (1,)), ((), ()))
TN = (((0,), (0,)), ((), ()))


def _tile(n, cap, mult):
    best = None
    for t in range(mult, min(n, cap) + 1, mult):
        if n % t == 0:
            best = t
    return best if best is not None else n


def _params(sem):
    return pltpu.CompilerParams(dimension_semantics=sem, vmem_limit_bytes=VMEM_LIMIT)


def _dot(a, b, dims):
    return lax.dot_general(a, b, dims, preferred_element_type=F32)


def _sigmoid(x):
    return 1.0 / (1.0 + jnp.exp(-x))


def _silu(x):
    return x * _sigmoid(x)


def _softplus(x):
    return jnp.maximum(x, 0.0) + jnp.log(1.0 + jnp.exp(-jnp.abs(x)))


def _mm(name, mode, pairs, M, N, K, tm, tn, tk, out_dtypes, epilogue=None, extras=(), row_extras=(),
        separate=False):
    tm, tn, tk = min(tm, M), min(tn, N), min(tk, K)
    assert M % tm == 0 and N % tn == 0 and K % tk == 0, (name, M, N, K, tm, tn, tk)
    nk = K // tk
    npairs = len(pairs)
    nacc = npairs if separate else 1
    if mode == "nn":
        a_spec = pl.BlockSpec((tm, tk), lambda i, j, k: (i, k))
        b_spec = pl.BlockSpec((tk, tn), lambda i, j, k: (k, j))
        dims = NN
    elif mode == "nt":
        a_spec = pl.BlockSpec((tm, tk), lambda i, j, k: (i, k))
        b_spec = pl.BlockSpec((tn, tk), lambda i, j, k: (j, k))
        dims = NT
    else:
        a_spec = pl.BlockSpec((tk, tm), lambda i, j, k: (k, i))
        b_spec = pl.BlockSpec((tk, tn), lambda i, j, k: (k, j))
        dims = TN
    o_spec = pl.BlockSpec((tm, tn), lambda i, j, k: (i, j))
    r_spec = pl.BlockSpec((1, tn), lambda i, j, k: (0, j))
    if epilogue is None:
        epilogue = lambda accs, ex, rex: accs
    n_ex, n_rex, n_out = len(extras), len(row_extras), len(out_dtypes)

    def body(*refs):
        ab = refs[:2 * npairs]
        ex = refs[2 * npairs:2 * npairs + n_ex]
        rex = refs[2 * npairs + n_ex:2 * npairs + n_ex + n_rex]
        outs = refs[2 * npairs + n_ex + n_rex:2 * npairs + n_ex + n_rex + n_out]
        accs = refs[2 * npairs + n_ex + n_rex + n_out:]

        def products():
            res = [None] * nacc
            for p in range(npairs):
                d = _dot(ab[2 * p][...], ab[2 * p + 1][...], dims)
                q = p if separate else 0
                res[q] = d if res[q] is None else res[q] + d
            return res

        def finish(vals):
            res = epilogue(vals, [e[...] for e in ex], [r[...] for r in rex])
            for o, v in zip(outs, res):
                o[...] = v.astype(o.dtype)

        if nk == 1:
            finish(products())
        else:
            k = pl.program_id(2)

            @pl.when(k == 0)
            def _():
                for q in range(nacc):
                    accs[q][...] = jnp.zeros_like(accs[q])

            for p in range(npairs):
                accs[p if separate else 0][...] += _dot(ab[2 * p][...], ab[2 * p + 1][...], dims)

            @pl.when(k == nk - 1)
            def _():
                finish([a[...] for a in accs])

    in_specs = [a_spec, b_spec] * npairs + [o_spec] * n_ex + [r_spec] * n_rex
    args = [t for p in pairs for t in p] + list(extras) + list(row_extras)
    outs = pl.pallas_call(
        body,
        grid=(M // tm, N // tn, nk),
        in_specs=in_specs,
        out_specs=[o_spec] * n_out,
        out_shape=[jax.ShapeDtypeStruct((M, N), d) for d in out_dtypes],
        scratch_shapes=[pltpu.VMEM((tm, tn), F32) for _ in range(nacc if nk > 1 else 0)],
        compiler_params=_params(("parallel", "parallel", "arbitrary")),
        name=name,
    )(*args)
    return outs


def _rms(xf, w):
    y = xf * lax.rsqrt(jnp.mean(xf * xf, axis=-1, keepdims=True) + NORM_EPS)
    return y * w


def _rms_fwd(name, x, w, tm):
    L, D = x.shape

    def body(x_ref, w_ref, o_ref):
        o_ref[...] = _rms(x_ref[...], w_ref[...]).astype(BF16)

    return pl.pallas_call(
        body, grid=(L // tm,),
        in_specs=[pl.BlockSpec((tm, D), lambda i: (i, 0)), pl.BlockSpec((1, D), lambda i: (0, 0))],
        out_specs=pl.BlockSpec((tm, D), lambda i: (i, 0)),
        out_shape=jax.ShapeDtypeStruct((L, D), BF16),
        compiler_params=_params(("parallel",)), name=name)(x, w)


def _rms_bwd(name, x, w, dparts, dres, tm, with_bf16):
    L, D = x.shape
    nparts = len(dparts)

    def body(*refs):
        x_ref, w_ref = refs[:2]
        p_refs = refs[2:2 + nparts]
        r_ref = refs[2 + nparts]
        outs = refs[3 + nparts:]
        dhn = p_refs[0][...]
        for p in p_refs[1:]:
            dhn = dhn + p[...]
        _, vjp = jax.vjp(_rms, x_ref[...], w_ref[...])
        dx, dw = vjp(dhn)
        dx = dx + r_ref[...]
        outs[0][...] = dx
        gw_ref = outs[1]

        @pl.when(pl.program_id(0) == 0)
        def _():
            gw_ref[...] = jnp.zeros_like(gw_ref)

        gw_ref[...] += dw
        if with_bf16:
            outs[2][...] = dx.astype(BF16)

    row = pl.BlockSpec((tm, D), lambda i: (i, 0))
    vec = pl.BlockSpec((1, D), lambda i: (0, 0))
    out_shape = [jax.ShapeDtypeStruct((L, D), F32), jax.ShapeDtypeStruct((1, D), F32)]
    out_specs = [row, vec]
    if with_bf16:
        out_shape.append(jax.ShapeDtypeStruct((L, D), BF16))
        out_specs.append(row)
    return pl.pallas_call(
        body, grid=(L // tm,),
        in_specs=[row, vec] + [row] * nparts + [row],
        out_specs=out_specs, out_shape=out_shape,
        compiler_params=_params(("arbitrary",)), name=name)(x, w, *dparts, dres)


def _final_loss(h2, wf, target, tm):
    L, D = h2.shape

    def body(h_ref, w_ref, t_ref, dh_ref, dhb_ref, loss_ref, gw_ref):
        t = t_ref[...]

        def f(h, w):
            err = jnp.square(_rms(h, w) - t)
            return 0.5 * jnp.sum(jnp.mean(err, axis=-1))

        val, vjp = jax.vjp(f, h_ref[...], w_ref[...])
        dh, dw = vjp(jnp.ones((), F32))
        dh_ref[...] = dh
        dhb_ref[...] = dh.astype(BF16)

        @pl.when(pl.program_id(0) == 0)
        def _():
            gw_ref[...] = jnp.zeros_like(gw_ref)
            loss_ref[...] = jnp.zeros_like(loss_ref)

        gw_ref[...] += dw
        loss_ref[...] += jnp.full(loss_ref.shape, val, F32)

    row = pl.BlockSpec((tm, D), lambda i: (i, 0))
    vec = pl.BlockSpec((1, D), lambda i: (0, 0))
    lspec = pl.BlockSpec((1, LANES), lambda i: (0, 0))
    return pl.pallas_call(
        body, grid=(L // tm,),
        in_specs=[row, vec, row],
        out_specs=[row, row, lspec, vec],
        out_shape=[jax.ShapeDtypeStruct((L, D), F32), jax.ShapeDtypeStruct((L, D), BF16),
                   jax.ShapeDtypeStruct((1, LANES), F32), jax.ShapeDtypeStruct((1, D), F32)],
        compiler_params=_params(("arbitrary",)), name="final_loss")(h2, wf, target)


def _gated(y, z, w):
    g = y * _silu(z)
    g = g * lax.rsqrt(jnp.mean(g * g, axis=-1, keepdims=True) + NORM_EPS)
    return g * w


def _gated_fwd(y, proj, w, DS, tm):
    L = y.shape[0]
    GW = DS // SSD_GROUPS

    def body(y_ref, z_ref, w_ref, o_ref):
        o_ref[...] = _gated(y_ref[...], z_ref[...], w_ref[...]).astype(BF16)

    blk = pl.BlockSpec((tm, GW), lambda i, g: (i, g))
    return pl.pallas_call(
        body, grid=(L // tm, SSD_GROUPS),
        in_specs=[blk, blk, pl.BlockSpec((1, GW), lambda i, g: (0, g))],
        out_specs=blk, out_shape=jax.ShapeDtypeStruct((L, DS), BF16),
        compiler_params=_params(("parallel", "parallel")), name="gated_fwd")(y, proj, w)


def _gated_bwd(y, proj, w, dout, DS, tm):
    L = y.shape[0]
    GW = DS // SSD_GROUPS

    def body(y_ref, z_ref, w_ref, d_ref, dy_ref, dz_ref, gw_ref):
        _, vjp = jax.vjp(_gated, y_ref[...], z_ref[...], w_ref[...])
        dy, dz, dw = vjp(d_ref[...])
        dy_ref[...] = dy
        dz_ref[...] = dz.astype(BF16)

        @pl.when(pl.program_id(1) == 0)
        def _():
            gw_ref[...] = jnp.zeros_like(gw_ref)

        gw_ref[...] += dw

    blk = pl.BlockSpec((tm, GW), lambda g, i: (i, g))
    vec = pl.BlockSpec((1, GW), lambda g, i: (0, g))
    return pl.pallas_call(
        body, grid=(SSD_GROUPS, L // tm),
        in_specs=[blk, blk, vec, blk],
        out_specs=[blk, blk, vec],
        out_shape=[jax.ShapeDtypeStruct((L, DS), F32), jax.ShapeDtypeStruct((L, DS), BF16),
                   jax.ShapeDtypeStruct((1, DS), F32)],
        compiler_params=_params(("parallel", "arbitrary")), name="gated_bwd")(y, proj, w, dout)


def _halo_prev(tm, cw, col0):
    return pl.BlockSpec((HALO, cw), lambda i, j: (jnp.maximum(i * (tm // HALO) - 1, 0), col0 + j))


def _halo_next(tm, cw, col0, L):
    return pl.BlockSpec((HALO, cw), lambda i, j: (jnp.minimum((i + 1) * (tm // HALO), L // HALO - 1), col0 + j))


def _conv_fwd(proj, conv_w, conv_b, DS, DCONV, tm, cw):
    L = proj.shape[0]
    col0 = DS // cw
    K = CONV_WIDTH

    def body(x_ref, p_ref, w_ref, b_ref, o_ref, ext):
        i = pl.program_id(0)
        ext[0:HALO, :] = jnp.where(i == 0, 0.0, p_ref[...])
        ext[HALO:, :] = x_ref[...]
        acc = jnp.broadcast_to(b_ref[...], (tm, cw))
        for k in range(K):
            acc = acc + w_ref[k:k + 1, :] * ext[pl.ds(HALO - (K - 1) + k, tm), :]
        o_ref[...] = _silu(acc)

    return pl.pallas_call(
        body, grid=(L // tm, DCONV // cw),
        in_specs=[pl.BlockSpec((tm, cw), lambda i, j: (i, col0 + j)), _halo_prev(tm, cw, col0),
                  pl.BlockSpec((K, cw), lambda i, j: (0, j)), pl.BlockSpec((1, cw), lambda i, j: (0, j))],
        out_specs=pl.BlockSpec((tm, cw), lambda i, j: (i, j)),
        out_shape=jax.ShapeDtypeStruct((L, DCONV), F32),
        scratch_shapes=[pltpu.VMEM((tm + HALO, cw), F32)],
        compiler_params=_params(("parallel", "parallel")), name="conv_fwd")(proj, proj, conv_w, conv_b)


def _conv_bwd(proj, dact, conv_w, conv_b, DS, DCONV, tm, cw):
    L = proj.shape[0]
    col0 = DS // cw
    K = CONV_WIDTH
    nrt = L // tm

    def body(x_ref, p_ref, n_ref, d_ref, dn_ref, w_ref, b_ref, dx_ref, dw_ref, db_ref, ext, dext):
        i = pl.program_id(1)
        last = i == nrt - 1
        ext[0:HALO, :] = jnp.where(i == 0, 0.0, p_ref[...])
        ext[HALO:HALO + tm, :] = x_ref[...]
        ext[HALO + tm:, :] = n_ref[...]
        dfull = jnp.concatenate([d_ref[...], jnp.where(last, 0.0, dn_ref[...])], axis=0)
        acc = jnp.broadcast_to(b_ref[...], (tm + HALO, cw))
        for k in range(K):
            acc = acc + w_ref[k:k + 1, :] * ext[pl.ds(HALO - (K - 1) + k, tm + HALO), :]
        sg = _sigmoid(acc)
        dconv = dfull * (sg * (1.0 + acc * (1.0 - sg)))
        dext[...] = dconv
        dx = jnp.zeros((tm, cw), F32)
        for k in range(K):
            dx = dx + w_ref[k:k + 1, :] * dext[pl.ds(K - 1 - k, tm), :]
        dx_ref[...] = dx.astype(BF16)

        @pl.when(i == 0)
        def _():
            dw_ref[...] = jnp.zeros_like(dw_ref)
            db_ref[...] = jnp.zeros_like(db_ref)

        dtile = dext[pl.ds(0, tm), :]
        db_ref[...] += jnp.sum(dtile, axis=0, keepdims=True)
        for k in range(K):
            dw_ref[k:k + 1, :] += jnp.sum(dtile * ext[pl.ds(HALO - (K - 1) + k, tm), :], axis=0, keepdims=True)

    prev = pl.BlockSpec((HALO, cw), lambda j, i: (jnp.maximum(i * (tm // HALO) - 1, 0), col0 + j))
    nxt = pl.BlockSpec((HALO, cw), lambda j, i: (jnp.minimum((i + 1) * (tm // HALO), L // HALO - 1), col0 + j))
    dnxt = pl.BlockSpec((HALO, cw), lambda j, i: (jnp.minimum((i + 1) * (tm // HALO), L // HALO - 1), j))
    return pl.pallas_call(
        body, grid=(DCONV // cw, nrt),
        in_specs=[pl.BlockSpec((tm, cw), lambda j, i: (i, col0 + j)), prev, nxt,
                  pl.BlockSpec((tm, cw), lambda j, i: (i, j)), dnxt,
                  pl.BlockSpec((K, cw), lambda j, i: (0, j)), pl.BlockSpec((1, cw), lambda j, i: (0, j))],
        out_specs=[pl.BlockSpec((tm, cw), lambda j, i: (i, j)),
                   pl.BlockSpec((K, cw), lambda j, i: (0, j)), pl.BlockSpec((1, cw), lambda j, i: (0, j))],
        out_shape=[jax.ShapeDtypeStruct((L, DCONV), BF16), jax.ShapeDtypeStruct((K, DCONV), F32),
                   jax.ShapeDtypeStruct((1, DCONV), F32)],
        scratch_shapes=[pltpu.VMEM((tm + 2 * HALO, cw), F32), pltpu.VMEM((tm + HALO, cw), F32)],
        compiler_params=_params(("parallel", "arbitrary")), name="conv_bwd",
    )(proj, proj, proj, dact, dact, conv_w, conv_b)


def _pool_fwd(proj, pool_w, pool_scale, ucol, DP, tm):
    L = proj.shape[0]
    PG = len(POOL_WINDOWS)
    PGD = DP // PG
    col0 = ucol // PGD

    def body(u_ref, p_ref, w_ref, s_ref, pooled_ref, y_ref, ext):
        i, g = pl.program_id(0), pl.program_id(1)
        ext[0:HALO, :] = jnp.where(i == 0, 0.0, p_ref[...])
        ext[HALO:, :] = u_ref[...]
        t = i * tm + lax.broadcasted_iota(jnp.int32, (tm, 1), 0)
        for gi, win in enumerate(POOL_WINDOWS):
            @pl.when(g == gi)
            def _():
                acc = ext[pl.ds(HALO, tm), :]
                for j in range(1, win):
                    acc = acc + ext[pl.ds(HALO - j, tm), :]
                count = jnp.minimum(t + 1, win).astype(F32)
                pooled = (acc / count - u_ref[...]).astype(BF16)
                pooled_ref[...] = pooled
                y_ref[...] = (_dot(pooled, w_ref[...], NN) * s_ref[...]).astype(BF16)

    blk = pl.BlockSpec((tm, PGD), lambda i, g: (i, g))
    return pl.pallas_call(
        body, grid=(L // tm, PG),
        in_specs=[pl.BlockSpec((tm, PGD), lambda i, g: (i, col0 + g)), _halo_prev(tm, PGD, col0),
                  pl.BlockSpec((None, PGD, PGD), lambda i, g: (g, 0, 0)), pl.BlockSpec((1, PGD), lambda i, g: (0, g))],
        out_specs=[blk, blk],
        out_shape=[jax.ShapeDtypeStruct((L, DP), BF16), jax.ShapeDtypeStruct((L, DP), BF16)],
        scratch_shapes=[pltpu.VMEM((tm + HALO, PGD), F32)],
        compiler_params=_params(("parallel", "parallel")), name="pool_fwd")(proj, proj, pool_w, pool_scale)


def _pool_bwd(dy, pooled, pool_w, pool_scale, tm):
    L, DP = dy.shape
    PG = len(POOL_WINDOWS)
    PGD = DP // PG
    nrt = L // tm

    def body(d_ref, dn_ref, p_ref, w_ref, s_ref, du_ref, dw_ref, ds_ref, qext):
        g, i = pl.program_id(0), pl.program_id(1)
        last = i == nrt - 1
        dfull = jnp.concatenate([d_ref[...], jnp.where(last, 0.0, dn_ref[...])], axis=0)
        dyp = (dfull * s_ref[...]).astype(BF16)
        dpooled = _dot(dyp, w_ref[...], NT)
        t = i * tm + lax.broadcasted_iota(jnp.int32, (tm + HALO, 1), 0)
        for gi, win in enumerate(POOL_WINDOWS):
            @pl.when(g == gi)
            def _():
                qext[...] = dpooled / jnp.minimum(t + 1, win).astype(F32)
                acc = qext[pl.ds(0, tm), :]
                for j in range(1, win):
                    acc = acc + qext[pl.ds(j, tm), :]
                du_ref[...] = (acc - dpooled[0:tm, :]).astype(BF16)

        @pl.when(i == 0)
        def _():
            dw_ref[...] = jnp.zeros_like(dw_ref)
            ds_ref[...] = jnp.zeros_like(ds_ref)

        pooled_t = p_ref[...]
        dw_ref[...] += _dot(pooled_t, dyp[0:tm, :], TN)
        ypre = _dot(pooled_t, w_ref[...], NN)
        ds_ref[...] += jnp.sum(d_ref[...] * ypre, axis=0, keepdims=True)

    blk = pl.BlockSpec((tm, PGD), lambda g, i: (i, g))
    nxt = pl.BlockSpec((HALO, PGD), lambda g, i: (jnp.minimum((i + 1) * (tm // HALO), L // HALO - 1), g))
    wspec = pl.BlockSpec((None, PGD, PGD), lambda g, i: (g, 0, 0))
    vec = pl.BlockSpec((1, PGD), lambda g, i: (0, g))
    return pl.pallas_call(
        body, grid=(PG, nrt),
        in_specs=[blk, nxt, blk, wspec, vec],
        out_specs=[blk, wspec, vec],
        out_shape=[jax.ShapeDtypeStruct((L, DP), BF16), jax.ShapeDtypeStruct((PG, PGD, PGD), F32),
                   jax.ShapeDtypeStruct((1, DP), F32)],
        scratch_shapes=[pltpu.VMEM((tm + HALO, PGD), F32)],
        compiler_params=_params(("parallel", "arbitrary")), name="pool_bwd")(dy, dy, pooled, pool_w, pool_scale)


def _ssd_common(dtc_raw, dtr_raw, bc, br, ac, ar):
    ch = CHUNK
    row = lax.broadcasted_iota(jnp.int32, (ch, ch), 0)
    col = lax.broadcasted_iota(jnp.int32, (ch, ch), 1)
    lower = row >= col
    dtc = _softplus(dtc_raw + bc)
    dtr = _softplus(dtr_raw + br)
    a_c = -jnp.exp(ac)
    a_r = -jnp.exp(ar)
    hi = lax.Precision.HIGHEST
    acol = jnp.dot(lower.astype(F32), dtc * a_c, preferred_element_type=F32, precision=hi)
    arow = jnp.dot(dtr * a_r, (row <= col).astype(F32), preferred_element_type=F32, precision=hi)
    return lower, row <= col, dtc, a_c, acol, arow


def _ssd_fwd(xbc, dtc_raw, dtr_raw, bias_c, bias_r, alog_c, alog_r, dskip_c, DS, N):
    L = xbc.shape[0]
    G, P, ch = SSD_GROUPS, HEAD_DIM, CHUNK
    R = dtc_raw.shape[2]
    GW = R * P
    nc = L // ch

    def body(xs_ref, b_ref, c_ref, dtc_ref, dtr_ref, bc_ref, br_ref, ac_ref, ar_ref, dk_ref,
             y_ref, st_ref, h_ref):
        @pl.when(pl.program_id(1) == 0)
        def _():
            h_ref[...] = jnp.zeros_like(h_ref)

        lower, _, dtc, _, acol_all, arow_all = _ssd_common(
            dtc_ref[...], dtr_ref[...], bc_ref[...], br_ref[...], ac_ref[...], ar_ref[...])
        bm = b_ref[...]
        cb16 = c_ref[...].astype(BF16)
        b16 = bm.astype(BF16)
        bt16 = bm.T.astype(BF16)
        cb = _dot(cb16, b16, NT)
        dk = dk_ref[...]
        st_ref[...] = h_ref[...]
        for r in range(R):
            acol = acol_all[:, r:r + 1]
            arow = arow_all[r:r + 1, :]
            alast = acol_all[ch - 1:ch, r:r + 1]
            decay = jnp.exp(jnp.where(lower, acol - arow, -1e30))
            x_h = xs_ref[:, pl.ds(r * P, P)]
            xdt = x_h * dtc[:, r:r + 1]
            m16 = (cb * decay).astype(BF16)
            h_prev = h_ref[r]
            y = _dot(m16, xdt.astype(BF16), NN)
            y = y + jnp.exp(acol) * _dot(cb16, h_prev.astype(BF16), NN)
            y = y + dk[:, r:r + 1] * x_h
            y_ref[:, pl.ds(r * P, P)] = y
            to_end = jnp.exp(alast - acol)
            h_ref[r] = jnp.exp(alast) * h_prev + _dot(bt16, (xdt * to_end).astype(BF16), NN)

    nb = DS // N
    return pl.pallas_call(
        body, grid=(G, nc),
        in_specs=[pl.BlockSpec((ch, GW), lambda g, c: (c, g)),
                  pl.BlockSpec((ch, N), lambda g, c: (c, nb + g)),
                  pl.BlockSpec((ch, N), lambda g, c: (c, nb + G + g)),
                  pl.BlockSpec((None, ch, R), lambda g, c: (g, c, 0)),
                  pl.BlockSpec((None, R, ch), lambda g, c: (g, 0, c)),
                  pl.BlockSpec((None, 1, R), lambda g, c: (g, 0, 0)),
                  pl.BlockSpec((None, R, 1), lambda g, c: (g, 0, 0)),
                  pl.BlockSpec((None, 1, R), lambda g, c: (g, 0, 0)),
                  pl.BlockSpec((None, R, 1), lambda g, c: (g, 0, 0)),
                  pl.BlockSpec((None, 1, R), lambda g, c: (g, 0, 0))],
        out_specs=[pl.BlockSpec((ch, GW), lambda g, c: (c, g)),
                   pl.BlockSpec((None, R, N, P), lambda g, c: (c, g, 0, 0))],
        out_shape=[jax.ShapeDtypeStruct((L, DS), F32), jax.ShapeDtypeStruct((nc, G * R, N, P), F32)],
        scratch_shapes=[pltpu.VMEM((R, N, P), F32)],
        compiler_params=_params(("parallel", "arbitrary")), name="ssd_fwd",
    )(xbc, xbc, xbc, dtc_raw, dtr_raw, bias_c, bias_r, alog_c, alog_r, dskip_c)


def _ssd_bwd(xbc, dtc_raw, dtr_raw, bias_c, bias_r, alog_c, alog_r, dskip_c, dy, states, DS, N):
    L = xbc.shape[0]
    G, P, ch = SSD_GROUPS, HEAD_DIM, CHUNK
    R = dtc_raw.shape[2]
    GW = R * P
    nc = L // ch

    def body(xs_ref, b_ref, c_ref, dtc_ref, dtr_ref, bc_ref, br_ref, ac_ref, ar_ref, dk_ref,
             dy_ref, stp_ref,
             dxs_ref, db_ref, dc_ref, ddt_ref, dal_ref, ddk_ref, dbias_ref, dh_ref):
        @pl.when(pl.program_id(1) == 0)
        def _():
            dh_ref[...] = jnp.zeros_like(dh_ref)
            dal_ref[...] = jnp.zeros_like(dal_ref)
            ddk_ref[...] = jnp.zeros_like(ddk_ref)
            dbias_ref[...] = jnp.zeros_like(dbias_ref)

        lower, upper, dtc, a_c, acol_all, arow_all = _ssd_common(
            dtc_ref[...], dtr_ref[...], bc_ref[...], br_ref[...], ac_ref[...], ar_ref[...])
        bm = b_ref[...]
        cm = c_ref[...]
        b16 = bm.astype(BF16)
        c16 = cm.astype(BF16)
        ct16 = cm.T.astype(BF16)
        cb = _dot(c16, b16, NT)
        cbt = _dot(b16, c16, NT)
        dk = dk_ref[...]
        lane_r = lax.broadcasted_iota(jnp.int32, (ch, R), 1)
        lane_1 = lax.broadcasted_iota(jnp.int32, (1, R), 1)
        dc = jnp.zeros((ch, N), F32)
        db = jnp.zeros((ch, N), F32)
        da_all = jnp.zeros((R, ch), F32)
        q_all = jnp.zeros((R, ch), F32)
        sxd_all = jnp.zeros((ch, R), F32)
        const = jnp.zeros((1, R), F32)
        ddk = jnp.zeros((1, R), F32)
        sub_r = lax.broadcasted_iota(jnp.int32, (R, ch), 0)
        ct = cm.T
        bt = bm.T
        dcb = jnp.zeros((ch, ch), F32)
        for r in range(R):
            acol = acol_all[:, r:r + 1]
            arow = arow_all[r:r + 1, :]
            alast = acol_all[ch - 1:ch, r:r + 1]
            seg = acol - arow
            decay = jnp.exp(jnp.where(lower, seg, -1e30))
            decay_t = jnp.exp(jnp.where(upper, -seg, -1e30))
            x_h = xs_ref[:, pl.ds(r * P, P)]
            dy_h = dy_ref[:, pl.ds(r * P, P)]
            dt_h = dtc[:, r:r + 1]
            dk_h = dk[:, r:r + 1]
            xdt = x_h * dt_h
            xdt16 = xdt.astype(BF16)
            dy16 = dy_h.astype(BF16)
            h_prev = stp_ref[r]
            h16 = h_prev.astype(BF16)
            dh_next = dh_ref[r]
            dhn16 = dh_next.astype(BF16)
            to_end = jnp.exp(alast - acol)
            e_a = jnp.exp(acol)
            mt = cbt * decay_t
            pm = _dot(dy16, xdt16, NT) * decay
            wt = _dot(xdt16, dy16, NT) * mt
            dxdt = _dot(mt.astype(BF16), dy16, NN) + to_end * _dot(b16, dhn16, NN)
            dcb = dcb + pm
            dc = dc + e_a * _dot(dy16, h16, NT)
            db = db + to_end * _dot(xdt16, dhn16, NT)
            dh_ref[r] = jnp.exp(alast) * dh_next + _dot(ct16, (dy_h * e_a).astype(BF16), NN)
            da = (jnp.sum(wt, axis=0, keepdims=True) - jnp.sum(pm * cb, axis=0, keepdims=True)
                  + jnp.exp(arow) * jnp.sum(ct * _dot(h16, dy16, NT), axis=0, keepdims=True))
            q = jnp.exp(alast - arow) * jnp.sum(bt * _dot(dhn16, xdt16, NT), axis=0, keepdims=True)
            da_all = da_all + jnp.where(sub_r == r, da, 0.0)
            q_all = q_all + jnp.where(sub_r == r, q, 0.0)
            sxd_all = sxd_all + jnp.where(lane_r == r, jnp.sum(dxdt * x_h, axis=1, keepdims=True), 0.0)
            const = const + jnp.where(lane_1 == r, jnp.exp(alast) * jnp.sum(dh_next * h_prev), 0.0)
            ddk = ddk + jnp.where(lane_1 == r, jnp.sum(dy_h * x_h), 0.0)
            dxs_ref[:, pl.ds(r * P, P)] = dxdt * dt_h + dk_h * dy_h
        dcb16 = dcb.astype(BF16)
        dc_ref[...] = dc + _dot(dcb16, b16, NN)
        db_ref[...] = db + _dot(dcb16, c16, TN)
        hi = lax.Precision.HIGHEST
        strict_lower = jnp.logical_and(lower, jnp.logical_not(upper))
        dda = (lax.dot_general(upper.astype(F32), da_all, NT, preferred_element_type=F32, precision=hi)
               + lax.dot_general(strict_lower.astype(F32), q_all, NT, preferred_element_type=F32, precision=hi)
               + const)
        ddt = dda * a_c + sxd_all
        dal_ref[...] += jnp.sum(dda * dtc, axis=0, keepdims=True) * a_c
        ddk_ref[...] += ddk
        ddt_raw = ddt * _sigmoid(dtc_ref[...] + bc_ref[...])
        ddt_ref[...] = ddt_raw
        dbias_ref[...] += jnp.sum(ddt_raw, axis=0, keepdims=True)

    nb = DS // N
    rc = lambda c: nc - 1 - c
    vec_c = pl.BlockSpec((None, 1, R), lambda g, c: (g, 0, 0))
    vec_r = pl.BlockSpec((None, R, 1), lambda g, c: (g, 0, 0))
    big = pl.BlockSpec((ch, GW), lambda g, c: (rc(c), g))
    return pl.pallas_call(
        body, grid=(G, nc),
        in_specs=[big,
                  pl.BlockSpec((ch, N), lambda g, c: (rc(c), nb + g)),
                  pl.BlockSpec((ch, N), lambda g, c: (rc(c), nb + G + g)),
                  pl.BlockSpec((None, ch, R), lambda g, c: (g, rc(c), 0)),
                  pl.BlockSpec((None, R, ch), lambda g, c: (g, 0, rc(c))),
                  vec_c, vec_r, vec_c, vec_r, vec_c,
                  big,
                  pl.BlockSpec((None, R, N, P), lambda g, c: (rc(c), g, 0, 0))],
        out_specs=[big,
                   pl.BlockSpec((ch, N), lambda g, c: (rc(c), g)),
                   pl.BlockSpec((ch, N), lambda g, c: (rc(c), g)),
                   pl.BlockSpec((None, ch, R), lambda g, c: (g, rc(c), 0)),
                   vec_c, vec_c, vec_c],
        out_shape=[jax.ShapeDtypeStruct((L, DS), F32), jax.ShapeDtypeStruct((L, G * N), F32),
                   jax.ShapeDtypeStruct((L, G * N), F32), jax.ShapeDtypeStruct((G, L, R), F32),
                   jax.ShapeDtypeStruct((G, 1, R), F32), jax.ShapeDtypeStruct((G, 1, R), F32),
                   jax.ShapeDtypeStruct((G, 1, R), F32)],
        scratch_shapes=[pltpu.VMEM((R, N, P), F32)],
        compiler_params=_params(("parallel", "arbitrary")), name="ssd_bwd",
    )(xbc, xbc, xbc, dtc_raw, dtr_raw, bias_c, bias_r, alog_c, alog_r, dskip_c, dy, states)


def _adam_math(w, g, m, v):
    m = ADAM_B1 * m + (1.0 - ADAM_B1) * g
    v = ADAM_B2 * v + (1.0 - ADAM_B2) * jnp.square(g)
    m_hat = m / (1.0 - ADAM_B1 ** ADAM_STEP)
    v_hat = v / (1.0 - ADAM_B2 ** ADAM_STEP)
    delta = -ADAM_LR * (m_hat / (jnp.sqrt(v_hat) + ADAM_EPS) + ADAM_WD * w)
    return delta, m, v


def _adam(name, w, g, m, v):
    rows, cols = w.shape
    tr = _tile(rows, max(8, (1 << 18) // cols // 8 * 8), 8)

    def body(w_ref, g_ref, m_ref, v_ref, d_ref, mo_ref, vo_ref):
        d, m2, v2 = _adam_math(w_ref[...], g_ref[...], m_ref[...], v_ref[...])
        d_ref[...] = d
        mo_ref[...] = m2
        vo_ref[...] = v2

    blk = pl.BlockSpec((tr, cols), lambda i: (i, 0))
    return pl.pallas_call(
        body, grid=(rows // tr,), in_specs=[blk] * 4, out_specs=[blk] * 3,
        out_shape=[jax.ShapeDtypeStruct((rows, cols), F32)] * 3,
        compiler_params=_params(("parallel",)), name=name)(w, g, m, v)


def _small_sum_adam(gathered, w, m, v, rows):
    def body(ga_ref, w_ref, m_ref, v_ref, g_ref, d_ref, mo_ref, vo_ref):
        g = ga_ref[0:rows, :]
        for d in range(1, N_DEV):
            g = g + ga_ref[d * rows:(d + 1) * rows, :]
        g_ref[...] = g
        dl, m2, v2 = _adam_math(w_ref[...], g, m_ref[...], v_ref[...])
        d_ref[...] = dl
        mo_ref[...] = m2
        vo_ref[...] = v2

    return pl.pallas_call(
        body, out_shape=[jax.ShapeDtypeStruct((rows, LANES), F32)] * 4,
        compiler_params=pltpu.CompilerParams(vmem_limit_bytes=VMEM_LIMIT), name="small_sum_adam",
    )(gathered, w, m, v)


def _row_tile(rh, cols):
    return _tile(rh, max(16, (1 << 19) // cols // 16 * 16), 16)


def _pair_sum(name, g, recv, pos):
    _, r, c = g.shape
    rh = r // 2
    tr = _row_tile(rh, c)
    nrt = rh // tr

    def body(pos_ref, a_ref, b_ref, o_ref):
        o_ref[...] = (a_ref[...] + b_ref[...]).astype(BF16)

    return pl.pallas_call(
        body,
        grid_spec=pltpu.PrefetchScalarGridSpec(
            num_scalar_prefetch=1, grid=(N_CHIPS, nrt),
            in_specs=[pl.BlockSpec((None, tr, c), lambda k, i, p: (k, p[1] * nrt + i, 0)),
                      pl.BlockSpec((None, tr, c), lambda k, i, p: (k, i, 0))],
            out_specs=pl.BlockSpec((None, tr, c), lambda k, i, p: (k, i, 0))),
        out_shape=jax.ShapeDtypeStruct((N_CHIPS, rh, c), BF16),
        compiler_params=_params(("parallel", "parallel")), name=name)(pos, g, recv)


def _chip_sum(name, parts, pos):
    _, rh, c = parts.shape
    tr = _row_tile(rh, c)
    nrt = rh // tr

    def body(pos_ref, p_ref, o_ref):
        s = p_ref[0].astype(F32)
        for k in range(1, N_CHIPS):
            s = s + p_ref[k].astype(F32)
        o_ref[...] = s

    return pl.pallas_call(
        body,
        grid_spec=pltpu.PrefetchScalarGridSpec(
            num_scalar_prefetch=1, grid=(nrt,),
            in_specs=[pl.BlockSpec((N_CHIPS, tr, c), lambda i, p: (0, i, 0))],
            out_specs=pl.BlockSpec((tr, c), lambda i, p: (p[1] * nrt + i, 0))),
        out_shape=jax.ShapeDtypeStruct((2 * rh, c), F32),
        compiler_params=_params(("parallel",)), name=name)(pos, parts)


_HBM = pl.BlockSpec(memory_space=pltpu.HBM)


def _chip_xy(k):
    return k // 2, k % 2


def _half_rows(ref, hc, rh):
    return ref.at[pl.ds(pl.multiple_of(hc * rh, 16), rh), :]


_SEM = pl.BlockSpec(memory_space=pltpu.SEMAPHORE)
_ANY = pl.BlockSpec(memory_space=pl.ANY)
_SPLIT = pltpu.CompilerParams(has_side_effects=pltpu.SideEffectType.DATAFLOW_SIDE_EFFECTING)


def _in_hbm(a):
    return pltpu.with_memory_space_constraint(a, pltpu.HBM)


def _push_start(name, srcs, land_shapes, copies_of):
    n = len(srcs)

    def body(*refs):
        s_refs, l_refs = refs[:n], refs[n:2 * n]
        send_sems, recv_sems = refs[2 * n], refs[2 * n + 1]
        token = refs[-1]
        x, y, c = lax.axis_index("x"), lax.axis_index("y"), lax.axis_index("c")
        me = 2 * x + y
        for i in range(n):
            for k in range(N_CHIPS):
                @pl.when(k != me)
                def _():
                    src, dst, dev = copies_of(i, k, s_refs[i], l_refs[i], me, x, y, c)
                    pltpu.make_async_remote_copy(
                        src_ref=src, dst_ref=dst, send_sem=send_sems.at[N_CHIPS * i + k],
                        recv_sem=recv_sems.at[N_CHIPS * i + me], device_id=dev, device_id_type=MESH).start()
        token[...] = jnp.zeros_like(token)

    lands = [lax.empty(s, d) for s, d in land_shapes]
    outs = pl.pallas_call(
        body, name=name,
        out_shape=[pltpu.SemaphoreType.DMA((N_CHIPS * n,)), pltpu.SemaphoreType.DMA((N_CHIPS * n,))]
        + [pltpu.HBM(s.shape, s.dtype) for s in srcs] + [pltpu.HBM(s, d) for s, d in land_shapes]
        + [jax.ShapeDtypeStruct((8, LANES), F32)],
        in_specs=[_HBM] * (2 * n),
        out_specs=[_SEM, _SEM] + [_HBM] * (2 * n) + [pl.BlockSpec(memory_space=pltpu.VMEM)],
        input_output_aliases={j: 2 + j for j in range(2 * n)},
        compiler_params=_SPLIT,
    )(*[_in_hbm(s) for s in srcs], *[_in_hbm(l) for l in lands])
    return outs[0], outs[1], outs[2:2 + n], outs[2 + n:2 + 2 * n], outs[-1]


def _push_wait(name, started, after, landed_of):
    send_sems, recv_sems, srcs, lands, _ = started
    n = len(srcs)

    def body(*refs):
        s_refs, l_refs = refs[:n], refs[n:2 * n]
        send, recv = refs[2 * n], refs[2 * n + 1]
        x, y, c = lax.axis_index("x"), lax.axis_index("y"), lax.axis_index("c")
        me = 2 * x + y
        for i in range(n):
            for k in range(N_CHIPS):
                @pl.when(k != me)
                def _():
                    src, dst = landed_of(i, k, s_refs[i], l_refs[i], me, c)
                    cp = pltpu.make_async_remote_copy(
                        src_ref=src, dst_ref=dst, send_sem=send.at[N_CHIPS * i + k], recv_sem=recv.at[N_CHIPS * i + k],
                        device_id=(x, y, c), device_id_type=MESH)
                    cp.wait_send()
                    cp.wait_recv()

    outs = pl.pallas_call(
        body, name=name,
        out_shape=[pltpu.HBM(s.shape, s.dtype) for s in srcs] + [pltpu.HBM(l.shape, l.dtype) for l in lands],
        in_specs=[_HBM] * (2 * n) + [_SEM, _SEM, _ANY], out_specs=[_HBM] * (2 * n),
        input_output_aliases={j: j for j in range(2 * n)},
        compiler_params=_SPLIT,
    )(*srcs, *lands, send_sems, recv_sems, after)
    return outs[:n], outs[n:]


def _gather_start(name, shards):
    def copies_of(i, k, src, land, me, x, y, c):
        rh = shards[i].shape[0] // 2
        kx, ky = _chip_xy(k)
        return _half_rows(src, c, rh), _half_rows(land.at[me], c, rh), (kx, ky, c)

    return _push_start(name, shards, [((N_CHIPS,) + s.shape, s.dtype) for s in shards], copies_of)


def _gather_wait(name, started, after):
    shapes = [s.shape for s in started[2]]

    def landed_of(i, k, src, land, me, c):
        rh = shapes[i][0] // 2
        return _half_rows(src, c, rh), _half_rows(land.at[k], c, rh)

    return _push_wait(name, started, after, landed_of)


def _forward_halves(name, bufs):
    n = len(bufs)

    def body(*refs):
        i_refs, o_refs, send_sems, recv_sems = refs[:n], refs[n:2 * n], refs[2 * n], refs[2 * n + 1]
        x, y, c = lax.axis_index("x"), lax.axis_index("y"), lax.axis_index("c")
        me = 2 * x + y

        def fwd(i, k, hc):
            rh = bufs[i].shape[1] // 2
            return pltpu.make_async_remote_copy(
                src_ref=_half_rows(i_refs[i].at[k], hc, rh), dst_ref=_half_rows(o_refs[i].at[k], hc, rh),
                send_sem=send_sems.at[i, k], recv_sem=recv_sems.at[i, k],
                device_id=(x, y, 1 - c), device_id_type=MESH)

        for i in range(n):
            for k in range(N_CHIPS):
                @pl.when(k != me)
                def _():
                    fwd(i, k, c).start()
        for i in range(n):
            for k in range(N_CHIPS):
                @pl.when(k != me)
                def _():
                    fwd(i, k, 1 - c).wait_recv()
        for i in range(n):
            for k in range(N_CHIPS):
                @pl.when(k != me)
                def _():
                    fwd(i, k, c).wait_send()

    return pl.pallas_call(
        body, in_specs=[_HBM] * n, out_specs=[_HBM] * n,
        out_shape=[jax.ShapeDtypeStruct(b.shape, b.dtype) for b in bufs],
        input_output_aliases={i: i for i in range(n)},
        scratch_shapes=[pltpu.SemaphoreType.DMA((n, N_CHIPS))] * 2,
        name=name)(*bufs)


def _swap_halves(name, grads):
    n = len(grads)

    def body(*refs):
        g_refs, o_refs, send_sems, recv_sems = refs[:n], refs[n:2 * n], refs[2 * n], refs[2 * n + 1]
        x, y, c = lax.axis_index("x"), lax.axis_index("y"), lax.axis_index("c")
        copies = []
        for i in range(n):
            rh = grads[i].shape[1] // 2
            for k in range(N_CHIPS):
                copies.append(pltpu.make_async_remote_copy(
                    src_ref=_half_rows(g_refs[i].at[k], 1 - c, rh), dst_ref=o_refs[i].at[k],
                    send_sem=send_sems.at[i, k], recv_sem=recv_sems.at[i, k],
                    device_id=(x, y, 1 - c), device_id_type=MESH))
        for cp in copies:
            cp.start()
        for cp in copies:
            cp.wait()

    return pl.pallas_call(
        body, in_specs=[_HBM] * n, out_specs=[_HBM] * n,
        out_shape=[jax.ShapeDtypeStruct((N_CHIPS, g.shape[1] // 2, g.shape[2]), F32) for g in grads],
        scratch_shapes=[pltpu.SemaphoreType.DMA((n, N_CHIPS))] * 2,
        name=name)(*grads)


def _scatter_start(name, parts):
    def copies_of(i, k, src, land, me, x, y, c):
        kx, ky = _chip_xy(k)
        return src.at[k], land.at[me], (kx, ky, c)

    return _push_start(name, parts, [(p.shape, p.dtype) for p in parts], copies_of)


def _scatter_wait(name, started, after):
    return _push_wait(name, started, after, lambda i, k, src, land, me, c: (src.at[k], land.at[k]))


def _join_halves(bufs):
    n = len(bufs)

    def body(*refs):
        i_refs, o_refs, send_sems, recv_sems = refs[:n], refs[n:2 * n], refs[2 * n], refs[2 * n + 1]
        x, y, c = lax.axis_index("x"), lax.axis_index("y"), lax.axis_index("c")
        copies = []
        for i in range(n):
            rh = bufs[i].shape[0] // 2
            copies.append(pltpu.make_async_remote_copy(
                src_ref=_half_rows(i_refs[i], c, rh), dst_ref=_half_rows(o_refs[i], c, rh),
                send_sem=send_sems.at[i], recv_sem=recv_sems.at[i],
                device_id=(x, y, 1 - c), device_id_type=MESH))
        for cp in copies:
            cp.start()
        for i in range(n):
            rh = bufs[i].shape[0] // 2
            pltpu.make_async_remote_copy(
                src_ref=_half_rows(i_refs[i], c, rh), dst_ref=_half_rows(o_refs[i], 1 - c, rh),
                send_sem=send_sems.at[i], recv_sem=recv_sems.at[i],
                device_id=(x, y, 1 - c), device_id_type=MESH).wait_recv()
        for cp in copies:
            cp.wait_send()

    return pl.pallas_call(
        body, in_specs=[_HBM] * n, out_specs=[_HBM] * n,
        out_shape=[jax.ShapeDtypeStruct(b.shape, F32) for b in bufs],
        input_output_aliases={i: i for i in range(n)},
        scratch_shapes=[pltpu.SemaphoreType.DMA((n,))] * 2,
        name="join_halves")(*bufs)


def _all_gather_small(name, blk):
    m_per, n = blk.shape

    def body(x_ref, out_ref, send_sems, recv_sems, local_sem):
        x, y, c = lax.axis_index("x"), lax.axis_index("y"), lax.axis_index("c")
        me, sibling = (x, y, c), (x, y, 1 - c)
        chips = [(1 - x, y), (x, 1 - y), (1 - x, 1 - y)]

        def rows(px, py, pc):
            return out_ref.at[pl.ds((4 * px + 2 * py + pc) * m_per, m_per), :]

        def copy(k, block, to, src=None):
            return pltpu.make_async_remote_copy(
                src_ref=rows(*block) if src is None else src, dst_ref=rows(*block),
                send_sem=send_sems.at[k], recv_sem=recv_sems.at[k],
                device_id=to, device_id_type=MESH)

        mine = pltpu.make_async_copy(x_ref, rows(*me), local_sem)
        mine.start()
        first = [copy(0, me, sibling, src=x_ref)]
        first += [copy(1 + j, me, (*chip, c), src=x_ref) for j, chip in enumerate(chips)]
        for cp in first:
            cp.start()
        passed = [copy(4 + j, (*chip, c), sibling) for j, chip in enumerate(chips)]
        for j, chip in enumerate(chips):
            copy(1 + j, (*chip, c), me).wait_recv()
            passed[j].start()
        copy(0, sibling, me).wait_recv()
        for j, chip in enumerate(chips):
            copy(4 + j, (*chip, 1 - c), me).wait_recv()
        for cp in first + passed:
            cp.wait_send()
        mine.wait()

    return pl.pallas_call(
        body, out_shape=jax.ShapeDtypeStruct((N_DEV * m_per, n), blk.dtype),
        in_specs=[pl.BlockSpec(memory_space=pltpu.VMEM)],
        out_specs=pl.BlockSpec(memory_space=pltpu.VMEM),
        scratch_shapes=[pltpu.SemaphoreType.DMA((7,)), pltpu.SemaphoreType.DMA((7,)), pltpu.SemaphoreType.DMA],
        name=name)(blk)


def _pack_rows(vecs, width):
    parts = []
    for v in vecs:
        f = v.reshape(-1)
        pad = (-f.shape[0]) % (8 * width)
        parts.append(jnp.pad(f, (0, pad)) if pad else f)
    return jnp.concatenate(parts).reshape(-1, width)


def _unpack_rows(packed, shapes, width):
    flat = packed.reshape(-1)
    out, off = [], 0
    for s in shapes:
        n = math.prod(s)
        out.append(flat[off:off + n].reshape(s))
        off += n + ((-n) % (8 * width))
    return out


class _WinPlan:
    def __init__(self, ncol, dt0, h, dmain):
        self.ncol, self.h = ncol, h
        self.dt_shard = dt0 // ncol
        assert (dt0 + h - 1) // ncol == self.dt_shard and dmain % LANES == 0
        self.dt_local = dt0 - self.dt_shard * ncol
        to_main = lambda g: g if g <= dt0 else g - h
        self.lo = [to_main(ncol * k) for k in range(N_CHIPS)]
        self.hi = [to_main(ncol * (k + 1)) for k in range(N_CHIPS)]
        down = lambda v: v // LANES * LANES
        self.ww = max(-(-(hi - down(lo)) // LANES) * LANES for lo, hi in zip(self.lo, self.hi))
        self.ws = [min(down(lo), dmain - self.ww) for lo in self.lo]
        self.dmain = dmain

    def to_window(self, k, shard):
        if k == self.dt_shard:
            shard = jnp.concatenate([shard[:, :self.dt_local], shard[:, self.dt_local + self.h:]], axis=1)
        left = self.lo[k] - self.ws[k]
        return jnp.pad(shard, ((0, 0), (left, self.ww - left - shard.shape[1])))

    def from_window(self, k, window, dt_cols):
        left = self.lo[k] - self.ws[k]
        body = window[:, left:left + self.hi[k] - self.lo[k]]
        if k == self.dt_shard:
            body = jnp.concatenate([body[:, :self.dt_local], dt_cols, body[:, self.dt_local:]], axis=1)
        return body

    def merge(self, windows):
        cuts = sorted({0, self.dmain} | set(self.ws) | {w + self.ww for w in self.ws})
        segs = []
        for a, b in zip(cuts[:-1], cuts[1:]):
            parts = [windows[k][:, a - self.ws[k]:b - self.ws[k]] for k in range(N_CHIPS)
                     if self.ws[k] <= a and b <= self.ws[k] + self.ww]
            segs.append(functools.reduce(jnp.add, parts))
        return jnp.concatenate(segs, axis=1)

    def split(self, g_main):
        return jnp.stack([g_main[:, w:w + self.ww] for w in self.ws])


def kernel(x, attn_norm_w, w_in, conv_w, conv_b, dt_bias, a_log, d_skip, ssd_norm_w, pool_w, pool_scale, w_out, ffn_norm_w, w_gate, w_up, w_down, final_norm_w, loss_target, m_attn_norm_w, m_w_in, m_conv_w, m_conv_b, m_dt_bias, m_a_log, m_d_skip, m_ssd_norm_w, m_pool_w, m_pool_scale, m_w_out, m_ffn_norm_w, m_w_gate, m_w_up, m_w_down, m_final_norm_w, v_attn_norm_w, v_w_in, v_conv_w, v_conv_b, v_dt_bias, v_a_log, v_d_skip, v_ssd_norm_w, v_pool_w, v_pool_scale, v_w_out, v_ffn_norm_w, v_w_gate, v_w_up, v_w_down, v_final_norm_w):
    G, P, PG = SSD_GROUPS, HEAD_DIM, len(POOL_WINDOWS)
    _, L, D = x.shape
    H = a_log.shape[1]
    R = H // G
    DS = H * P
    DCONV = conv_b.shape[1]
    N = (DCONV - DS) // (2 * G)
    DP = pool_scale.shape[1]
    PGD = DP // PG
    DIN = N_CHIPS * w_in.shape[2]
    DFF = N_CHIPS * w_gate.shape[2]
    DMAIN = DS + DCONV + DP
    assert DIN == DMAIN + H and DS == DP and H <= LANES

    cx, cy, cc = lax.axis_index("x"), lax.axis_index("y"), lax.axis_index("c")
    chip = 2 * cx + cy

    win = _WinPlan(DIN // N_CHIPS, DS + DCONV, H, DMAIN)
    my_window = lax.switch(chip, [functools.partial(win.to_window, k) for k in range(N_CHIPS)], w_in[0].astype(BF16))
    shards_in = [my_window]
    shards_rest = [pool_w[0].reshape(PG * PGD // N_CHIPS, PGD).astype(BF16), w_out[0].astype(BF16),
                   w_gate[0].astype(BF16), w_up[0].astype(BF16), w_down[0].astype(BF16)]
    started_in = _gather_start("gather_start_in", shards_in)
    started_rest = _gather_start("gather_start_rest", shards_rest)

    def finish_gather(tag, started, after):
        shards, landed = _gather_wait("gather_wait_" + tag, started, after)
        landed = _forward_halves("gather_forward_" + tag, landed)
        return [lax.dynamic_update_slice(g, s[None], (chip, 0, 0)) for g, s in zip(landed, shards)]

    def cols(p):
        return jnp.moveaxis(p, 0, -2).reshape(p.shape[1:-1] + (N_CHIPS * p.shape[-1],))

    ncw = CONV_WIDTH * DCONV // N_CHIPS
    dt_here = jnp.where(chip == win.dt_shard, w_in[0][:, win.dt_local:win.dt_local + H], 0.0)
    start_blk = _pack_rows([conv_w[0], dt_here], LANES)
    start_all = _all_gather_small("gather_conv_w", start_blk).reshape(N_CHIPS, 2, -1)[:, 0]
    conv_w_f = cols(start_all[:, :ncw].reshape(N_CHIPS, CONV_WIDTH, DCONV // N_CHIPS))
    dt_off = ncw + (-ncw) % (8 * LANES)
    w_dt = jnp.pad(start_all[win.dt_shard, dt_off:dt_off + D * H].reshape(D, H), ((0, 0), (0, LANES - H))).astype(BF16)

    xl, tgt = x[0], loss_target[0]
    tm_row = _tile(L, 256, HALO)
    tm_mm = _tile(L, 1024, 16)
    hn1 = _rms_fwd("rms1_fwd", xl, attn_norm_w, tm_row)
    w_main = win.merge(finish_gather("in", started_in, hn1)[0])
    proj, = _mm("proj_main", "nn", [(hn1, w_main)], L, DMAIN, D, tm_mm, 512, D, [F32])
    dt_raw, = _mm("proj_dt", "nn", [(hn1, w_dt)], L, LANES, D, tm_mm, LANES, D, [F32])

    cwid = _tile(math.gcd(DS, DCONV), 512, LANES)
    tm_conv = _tile(L, 512, HALO)
    xbc = _conv_fwd(proj, conv_w_f, conv_b, DS, DCONV, tm_conv, cwid)

    dt_g = dt_raw[:, :H].reshape(L, G, R)
    dtc_raw = jnp.transpose(dt_g, (1, 0, 2))
    dtr_raw = jnp.transpose(dt_g, (1, 2, 0))
    as_c = lambda v: v.reshape(G, 1, R)
    as_r = lambda v: v.reshape(G, R, 1)
    ssd_args = (xbc, dtc_raw, dtr_raw, as_c(dt_bias), as_r(dt_bias), as_c(a_log), as_r(a_log), as_c(d_skip))
    y_ssd_raw, states = _ssd_fwd(*ssd_args, DS, N)
    y_ssd = _gated_fwd(y_ssd_raw, proj, ssd_norm_w, DS, tm_row)
    gathered = finish_gather("rest", started_rest, y_ssd)
    pool_w_f = jnp.moveaxis(gathered[0].reshape(N_CHIPS, PG, PGD // N_CHIPS, PGD), 0, 1).reshape(PG, PGD, PGD)
    w_out_f = gathered[1].reshape(2 * DS, D)
    w_gate_f, w_up_f = cols(gathered[2]), cols(gathered[3])
    w_down_f = gathered[4].reshape(DFF, D)
    w_out_top, w_out_bot = w_out_f[:DS], w_out_f[DS:]
    pooled, y_pool = _pool_fwd(proj, pool_w_f, pool_scale, DS + DCONV, DP, tm_conv)

    add_res = lambda accs, ex, rex: [accs[0] + ex[0]]
    h1, = _mm("out_proj", "nn", [(y_ssd, w_out_top), (y_pool, w_out_bot)], L, D, DS, tm_mm, 512, DS, [F32],
              epilogue=add_res, extras=[xl])
    hn2 = _rms_fwd("rms2_fwd", h1, ffn_norm_w, tm_row)

    def glu(accs, ex, rex):
        return [accs[0], accs[1], (_silu(accs[0]) * accs[1])]

    tn_ff = _tile(DFF, 512, LANES)
    gate, up, act = _mm("ffn_in", "nn", [(hn2, w_gate_f), (hn2, w_up_f)], L, DFF, D, tm_mm, tn_ff, D,
                        [F32, F32, BF16], epilogue=glu, separate=True)
    tm_half = _tile(L, 512, 16)
    h2, = _mm("ffn_out", "nn", [(act, w_down_f)], L, D, DFF, tm_half, 512, DFF, [F32], epilogue=add_res, extras=[h1])
    dh2, dh2_16, loss_blk, g_final = _final_loss(h2, final_norm_w.reshape(1, D), tgt, tm_row)

    def dglu(accs, ex, rex):
        gt, u = ex
        sg = _sigmoid(gt)
        return [accs[0] * u * (sg * (1.0 + gt * (1.0 - sg))), accs[0] * (gt * sg)]

    dgate, dup = _mm("ffn_out_dx", "nt", [(dh2_16, w_down_f)], L, DFF, D, tm_mm, tn_ff, D, [BF16, BF16],
                     epilogue=dglu, extras=[gate, up])
    tk_tok = _tile(L, 2048, 16)
    g_w_down, = _mm("ffn_out_dw", "tn", [(act, dh2_16)], DFF, D, L, _tile(DFF, 512, LANES), 1024, tk_tok, [F32])
    g_w_gate, g_w_up = _mm("ffn_in_dw", "tn", [(hn2, dgate), (hn2, dup)], D, DFF, L, 512, tn_ff, tk_tok, [F32, F32],
                           separate=True)

    pos = jnp.stack([chip, cc]).astype(jnp.int32)
    col_shards = lambda gfull: jnp.moveaxis(gfull.reshape(gfull.shape[0], N_CHIPS, -1), 1, 0)

    def start_reduce(tag, names, full_grads):
        from_sibling = _swap_halves("swap_halves_" + tag, full_grads)
        partials = [_pair_sum("pair_sum_" + n, g, r, pos) for n, g, r in zip(names, full_grads, from_sibling)]
        return _scatter_start("scatter_start_" + tag, partials)

    pin_row = lambda started, n: jnp.zeros((1, n), F32) + started[4][0, 0]
    add_row = lambda accs, ex, rex: [accs[0] + rex[0]]
    names_ffn = ["w_gate", "w_up", "w_down"]
    started_ffn = start_reduce("ffn", names_ffn, [col_shards(g_w_gate), col_shards(g_w_up),
                                                  g_w_down.reshape(N_CHIPS, -1, D)])
    dhn2, = _mm("ffn_in_dx", "nt", [(dgate, w_gate_f), (dup, w_up_f)], L, D, DFF, tm_half, 512,
                _tile(DFF, DFF // 2, LANES), [F32], epilogue=add_row, row_extras=[pin_row(started_ffn, D)])
    dh1, g_ffn_norm, dh1_16 = _rms_bwd("rms2_bwd", h1, ffn_norm_w, [dhn2], dh2, tm_row, True)

    dy_ssd, dy_pool = _mm("out_proj_dx", "nt", [(dh1_16, w_out_top), (dh1_16, w_out_bot)], L, DS, D, tm_mm, 512, D,
                          [F32, F32], separate=True)
    g_w_out_top, g_w_out_bot = _mm("out_proj_dw", "tn", [(y_ssd, dh1_16), (y_pool, dh1_16)], DS, D, L, 512, 1024,
                                   tk_tok, [F32, F32], separate=True)
    du, g_pool_w, g_pool_scale = _pool_bwd(dy_pool, pooled, pool_w_f, pool_scale, tm_conv)
    dy_raw, dz, g_ssd_norm = _gated_bwd(y_ssd_raw, proj, ssd_norm_w, dy_ssd, DS, tm_row)
    dxs, db, dc, ddt_raw, g_a_log, g_d_skip, g_dt_bias = _ssd_bwd(*ssd_args, dy_raw, states, DS, N)
    dxbc_act = jnp.concatenate([dxs, db, dc], axis=1)
    dxbc, g_conv_w, g_conv_b = _conv_bwd(proj, dxbc_act, conv_w_f, conv_b, DS, DCONV, tm_conv, cwid)
    dproj = jnp.concatenate([dz, dxbc, du], axis=1)
    ddt_pad = jnp.pad(jnp.transpose(ddt_raw, (1, 0, 2)).reshape(L, H), ((0, 0), (0, LANES - H))).astype(BF16)

    tk_main = _tile(DMAIN, DMAIN // 2, LANES)
    g_w_main, = _mm("proj_main_dw", "tn", [(hn1, dproj)], D, DMAIN, L, 512, _tile(DMAIN, 1024, LANES), tk_tok, [F32])
    names_mix = ["w_in", "pool_w", "w_out"]
    started_mix = start_reduce("mix", names_mix, [
        win.split(g_w_main),
        jnp.moveaxis(g_pool_w.reshape(PG, N_CHIPS, PGD // N_CHIPS, PGD), 1, 0).reshape(N_CHIPS, -1, PGD),
        jnp.stack([g_w_out_top.reshape(2, DS // 2, D), g_w_out_bot.reshape(2, DS // 2, D)]).reshape(N_CHIPS, -1, D)])
    dhn1a, = _mm("proj_main_dx", "nt", [(dproj, w_main)], L, D, DMAIN, tm_mm, 512, tk_main, [F32],
                 epilogue=add_row, row_extras=[pin_row(started_mix, D)])
    dhn1b, = _mm("proj_dt_dx", "nt", [(ddt_pad, w_dt)], L, D, LANES, tm_mm, 512, LANES, [F32])
    g_w_dt, = _mm("proj_dt_dw", "tn", [(hn1, ddt_pad)], D, LANES, L, 512, LANES, tk_tok, [F32])
    grad_x, g_attn_norm = _rms_bwd("rms1_bwd", xl, attn_norm_w, [dhn1a, dhn1b], dh1, tm_row, False)

    def finish_reduce(tag, names, started, after):
        partials, landed = _scatter_wait("scatter_wait_" + tag, started, after)
        landed = [lax.dynamic_update_slice(l, lax.dynamic_index_in_dim(p, chip, 0), (chip, 0, 0))
                  for l, p in zip(landed, partials)]
        return [_chip_sum("chip_sum_" + n, l, pos) for n, l in zip(names, landed)]

    halves = finish_reduce("ffn", names_ffn, started_ffn, grad_x) + finish_reduce("mix", names_mix, started_mix, grad_x)
    red = dict(zip(names_ffn + names_mix, _join_halves(halves)))

    small_w = [attn_norm_w, conv_b, dt_bias, a_log, d_skip, ssd_norm_w, pool_scale, ffn_norm_w, final_norm_w]
    small_m = [m_attn_norm_w, m_conv_b, m_dt_bias, m_a_log, m_d_skip, m_ssd_norm_w, m_pool_scale, m_ffn_norm_w, m_final_norm_w]
    small_v = [v_attn_norm_w, v_conv_b, v_dt_bias, v_a_log, v_d_skip, v_ssd_norm_w, v_pool_scale, v_ffn_norm_w, v_final_norm_w]
    small_g = [g_attn_norm, g_conv_b, g_dt_bias.reshape(1, H), g_a_log.reshape(1, H), g_d_skip.reshape(1, H),
               g_ssd_norm, g_pool_scale, g_ffn_norm, g_final.reshape(D)]
    extra_shapes = [(CONV_WIDTH, DCONV), (D, H), (1, LANES)]
    zeros_like_extra = [jnp.zeros(s, F32) for s in extra_shapes]
    g_blk = _pack_rows(small_g + [g_conv_w, g_w_dt[:, :H], loss_blk], LANES)
    rows = g_blk.shape[0]
    small_all = _all_gather_small("gather_small_grads", g_blk)
    s_g, s_d, s_m, s_v = _small_sum_adam(small_all, _pack_rows(small_w + zeros_like_extra, LANES),
                                         _pack_rows(small_m + zeros_like_extra, LANES),
                                         _pack_rows(small_v + zeros_like_extra, LANES), rows)
    shapes = [w.shape for w in small_w] + extra_shapes
    sg_list = _unpack_rows(s_g, shapes, LANES)
    sd_list = _unpack_rows(s_d, shapes, LANES)[:len(small_w)]
    sm_list = _unpack_rows(s_m, shapes, LANES)[:len(small_w)]
    sv_list = _unpack_rows(s_v, shapes, LANES)[:len(small_w)]
    loss = sg_list[-1][0, 0]
    grad_conv_w = lax.dynamic_slice(sg_list[-3], (0, chip * (DCONV // N_CHIPS)), (CONV_WIDTH, DCONV // N_CHIPS))
    grad_w_in = lax.switch(chip, [functools.partial(win.from_window, k) for k in range(N_CHIPS)], red["w_in"], sg_list[-2])

    def adam_nd(name, w, g, m, v):
        shp = w.shape
        to2 = lambda a: a.reshape(-1, shp[-1])
        d, m2, v2 = _adam(name, to2(w), to2(g), to2(m), to2(v))
        return d.reshape(shp), m2.reshape(shp), v2.reshape(shp)

    sharded = {
        "w_in": (w_in, grad_w_in[None], m_w_in, v_w_in),
        "conv_w": (conv_w, grad_conv_w[None], m_conv_w, v_conv_w),
        "pool_w": (pool_w, red["pool_w"].reshape(pool_w.shape), m_pool_w, v_pool_w),
        "w_out": (w_out, red["w_out"][None], m_w_out, v_w_out),
        "w_gate": (w_gate, red["w_gate"][None], m_w_gate, v_w_gate),
        "w_up": (w_up, red["w_up"][None], m_w_up, v_w_up),
        "w_down": (w_down, red["w_down"][None], m_w_down, v_w_down),
    }
    upd = {n: (a[1],) + adam_nd("adam_" + n, *a) for n, a in sharded.items()}
    small_names = ["attn_norm_w", "conv_b", "dt_bias", "a_log", "d_skip", "ssd_norm_w", "pool_scale", "ffn_norm_w",
                   "final_norm_w"]
    for i, n in enumerate(small_names):
        upd[n] = (sg_list[i], sd_list[i], sm_list[i], sv_list[i])

    order = ["attn_norm_w", "w_in", "conv_w", "conv_b", "dt_bias", "a_log", "d_skip", "ssd_norm_w", "pool_w",
             "pool_scale", "w_out", "ffn_norm_w", "w_gate", "w_up", "w_down", "final_norm_w"]
    outs = [loss, grad_x[None]]
    for j in range(4):
        outs += [upd[n][j] for n in order]
    return tuple(outs)
```

```python
import functools
import math

import jax
import jax.numpy as jnp
from jax import lax
from jax.experimental import pallas as pl
from jax.experimental.pallas import tpu as pltpu

F32 = jnp.float32
BF16 = jnp.bfloat16

NORM_EPS = 1e-5
HEAD_DIM = 64
SSD_GROUPS = 4
CONV_WIDTH = 4
CHUNK = 256
POOL_WINDOWS = (2, 4, 8, 16)
ADAM_LR = 0.001
ADAM_B1 = 0.9
ADAM_B2 = 0.999
ADAM_EPS = 1e-08
ADAM_WD = 0.01
ADAM_STEP = 10

N_CHIPS = 4
N_DEV = 8
LANES = 128
HALO = 16
FLAT_W = 512
VMEM_LIMIT = 52 * 1024 * 1024
MESH = pl.DeviceIdType.MESH

NN = (((1,), (0,)), ((), ()))
NT = (((1,), (1,)), ((), ()))
TN = (((0,), (0,)), ((), ()))


def _tile(n, cap, mult):
    best = None
    for t in range(mult, min(n, cap) + 1, mult):
        if n % t == 0:
            best = t
    return best if best is not None else n


def _params(sem):
    return pltpu.CompilerParams(dimension_semantics=sem, vmem_limit_bytes=VMEM_LIMIT)


def _dot(a, b, dims):
    return lax.dot_general(a, b, dims, preferred_element_type=F32)


def _sigmoid(x):
    return 1.0 / (1.0 + jnp.exp(-x))


def _silu(x):
    return x * _sigmoid(x)


def _softplus(x):
    return jnp.maximum(x, 0.0) + jnp.log(1.0 + jnp.exp(-jnp.abs(x)))


def _mm(name, mode, pairs, M, N, K, tm, tn, tk, out_dtypes, epilogue=None, extras=(), row_extras=(),
        separate=False):
    tm, tn, tk = min(tm, M), min(tn, N), min(tk, K)
    assert M % tm == 0 and N % tn == 0 and K % tk == 0, (name, M, N, K, tm, tn, tk)
    nk = K // tk
    npairs = len(pairs)
    nacc = npairs if separate else 1
    if mode == "nn":
        a_spec = pl.BlockSpec((tm, tk), lambda i, j, k: (i, k))
        b_spec = pl.BlockSpec((tk, tn), lambda i, j, k: (k, j))
        dims = NN
    elif mode == "nt":
        a_spec = pl.BlockSpec((tm, tk), lambda i, j, k: (i, k))
        b_spec = pl.BlockSpec((tn, tk), lambda i, j, k: (j, k))
        dims = NT
    else:
        a_spec = pl.BlockSpec((tk, tm), lambda i, j, k: (k, i))
        b_spec = pl.BlockSpec((tk, tn), lambda i, j, k: (k, j))
        dims = TN
    o_spec = pl.BlockSpec((tm, tn), lambda i, j, k: (i, j))
    r_spec = pl.BlockSpec((1, tn), lambda i, j, k: (0, j))
    if epilogue is None:
        epilogue = lambda accs, ex, rex: accs
    n_ex, n_rex, n_out = len(extras), len(row_extras), len(out_dtypes)

    def body(*refs):
        ab = refs[:2 * npairs]
        ex = refs[2 * npairs:2 * npairs + n_ex]
        rex = refs[2 * npairs + n_ex:2 * npairs + n_ex + n_rex]
        outs = refs[2 * npairs + n_ex + n_rex:2 * npairs + n_ex + n_rex + n_out]
        accs = refs[2 * npairs + n_ex + n_rex + n_out:]

        def products():
            res = [None] * nacc
            for p in range(npairs):
                d = _dot(ab[2 * p][...], ab[2 * p + 1][...], dims)
                q = p if separate else 0
                res[q] = d if res[q] is None else res[q] + d
            return res

        def finish(vals):
            res = epilogue(vals, [e[...] for e in ex], [r[...] for r in rex])
            for o, v in zip(outs, res):
                o[...] = v.astype(o.dtype)

        if nk == 1:
            finish(products())
        else:
            k = pl.program_id(2)

            @pl.when(k == 0)
            def _():
                for q in range(nacc):
                    accs[q][...] = jnp.zeros_like(accs[q])

            for p in range(npairs):
                accs[p if separate else 0][...] += _dot(ab[2 * p][...], ab[2 * p + 1][...], dims)

            @pl.when(k == nk - 1)
            def _():
                finish([a[...] for a in accs])

    in_specs = [a_spec, b_spec] * npairs + [o_spec] * n_ex + [r_spec] * n_rex
    args = [t for p in pairs for t in p] + list(extras) + list(row_extras)
    outs = pl.pallas_call(
        body,
        grid=(M // tm, N // tn, nk),
        in_specs=in_specs,
        out_specs=[o_spec] * n_out,
        out_shape=[jax.ShapeDtypeStruct((M, N), d) for d in out_dtypes],
        scratch_shapes=[pltpu.VMEM((tm, tn), F32) for _ in range(nacc if nk > 1 else 0)],
        compiler_params=_params(("parallel", "parallel", "arbitrary")),
        name=name,
    )(*args)
    return outs


def _rms(xf, w):
    y = xf * lax.rsqrt(jnp.mean(xf * xf, axis=-1, keepdims=True) + NORM_EPS)
    return y * w


def _rms_fwd(name, x, w, tm):
    L, D = x.shape

    def body(x_ref, w_ref, o_ref):
        o_ref[...] = _rms(x_ref[...], w_ref[...]).astype(BF16)

    return pl.pallas_call(
        body, grid=(L // tm,),
        in_specs=[pl.BlockSpec((tm, D), lambda i: (i, 0)), pl.BlockSpec((1, D), lambda i: (0, 0))],
        out_specs=pl.BlockSpec((tm, D), lambda i: (i, 0)),
        out_shape=jax.ShapeDtypeStruct((L, D), BF16),
        compiler_params=_params(("parallel",)), name=name)(x, w)


def _rms_bwd(name, x, w, dparts, dres, tm, with_bf16):
    L, D = x.shape
    nparts = len(dparts)

    def body(*refs):
        x_ref, w_ref = refs[:2]
        p_refs = refs[2:2 + nparts]
        r_ref = refs[2 + nparts]
        outs = refs[3 + nparts:]
        dhn = p_refs[0][...]
        for p in p_refs[1:]:
            dhn = dhn + p[...]
        _, vjp = jax.vjp(_rms, x_ref[...], w_ref[...])
        dx, dw = vjp(dhn)
        dx = dx + r_ref[...]
        outs[0][...] = dx
        gw_ref = outs[1]

        @pl.when(pl.program_id(0) == 0)
        def _():
            gw_ref[...] = jnp.zeros_like(gw_ref)

        gw_ref[...] += dw
        if with_bf16:
            outs[2][...] = dx.astype(BF16)

    row = pl.BlockSpec((tm, D), lambda i: (i, 0))
    vec = pl.BlockSpec((1, D), lambda i: (0, 0))
    out_shape = [jax.ShapeDtypeStruct((L, D), F32), jax.ShapeDtypeStruct((1, D), F32)]
    out_specs = [row, vec]
    if with_bf16:
        out_shape.append(jax.ShapeDtypeStruct((L, D), BF16))
        out_specs.append(row)
    return pl.pallas_call(
        body, grid=(L // tm,),
        in_specs=[row, vec] + [row] * nparts + [row],
        out_specs=out_specs, out_shape=out_shape,
        compiler_params=_params(("arbitrary",)), name=name)(x, w, *dparts, dres)


def _final_loss(h2, wf, target, tm):
    L, D = h2.shape

    def body(h_ref, w_ref, t_ref, dh_ref, dhb_ref, loss_ref, gw_ref):
        t = t_ref[...]

        def f(h, w):
            err = jnp.square(_rms(h, w) - t)
            return 0.5 * jnp.sum(jnp.mean(err, axis=-1))

        val, vjp = jax.vjp(f, h_ref[...], w_ref[...])
        dh, dw = vjp(jnp.ones((), F32))
        dh_ref[...] = dh
        dhb_ref[...] = dh.astype(BF16)

        @pl.when(pl.program_id(0) == 0)
        def _():
            gw_ref[...] = jnp.zeros_like(gw_ref)
            loss_ref[...] = jnp.zeros_like(loss_ref)

        gw_ref[...] += dw
        loss_ref[...] += jnp.full(loss_ref.shape, val, F32)

    row = pl.BlockSpec((tm, D), lambda i: (i, 0))
    vec = pl.BlockSpec((1, D), lambda i: (0, 0))
    lspec = pl.BlockSpec((1, LANES), lambda i: (0, 0))
    return pl.pallas_call(
        body, grid=(L // tm,),
        in_specs=[row, vec, row],
        out_specs=[row, row, lspec, vec],
        out_shape=[jax.ShapeDtypeStruct((L, D), F32), jax.ShapeDtypeStruct((L, D), BF16),
                   jax.ShapeDtypeStruct((1, LANES), F32), jax.ShapeDtypeStruct((1, D), F32)],
        compiler_params=_params(("arbitrary",)), name="final_loss")(h2, wf, target)


def _gated(y, z, w):
    g = y * _silu(z)
    g = g * lax.rsqrt(jnp.mean(g * g, axis=-1, keepdims=True) + NORM_EPS)
    return g * w


def _gated_fwd(y, proj, w, DS, tm):
    L = y.shape[0]
    GW = DS // SSD_GROUPS

    def body(y_ref, z_ref, w_ref, o_ref):
        o_ref[...] = _gated(y_ref[...], z_ref[...], w_ref[...]).astype(BF16)

    blk = pl.BlockSpec((tm, GW), lambda i, g: (i, g))
    return pl.pallas_call(
        body, grid=(L // tm, SSD_GROUPS),
        in_specs=[blk, blk, pl.BlockSpec((1, GW), lambda i, g: (0, g))],
        out_specs=blk, out_shape=jax.ShapeDtypeStruct((L, DS), BF16),
        compiler_params=_params(("parallel", "parallel")), name="gated_fwd")(y, proj, w)


def _gated_bwd(y, proj, w, dout, DS, dproj_cols, tm):
    L = y.shape[0]
    GW = DS // SSD_GROUPS

    def body(y_ref, z_ref, w_ref, d_ref, dy_ref, dz_ref, gw_ref):
        _, vjp = jax.vjp(_gated, y_ref[...], z_ref[...], w_ref[...])
        dy, dz, dw = vjp(d_ref[...])
        dy_ref[...] = dy
        dz_ref[...] = dz.astype(BF16)

        @pl.when(pl.program_id(1) == 0)
        def _():
            gw_ref[...] = jnp.zeros_like(gw_ref)

        gw_ref[...] += dw

    blk = pl.BlockSpec((tm, GW), lambda g, i: (i, g))
    vec = pl.BlockSpec((1, GW), lambda g, i: (0, g))
    return pl.pallas_call(
        body, grid=(SSD_GROUPS, L // tm),
        in_specs=[blk, blk, vec, blk],
        out_specs=[blk, blk, vec],
        out_shape=[jax.ShapeDtypeStruct((L, DS), F32), jax.ShapeDtypeStruct((L, dproj_cols), BF16),
                   jax.ShapeDtypeStruct((1, DS), F32)],
        compiler_params=_params(("parallel", "arbitrary")), name="gated_bwd")(y, proj, w, dout)


def _halo_prev(tm, cw, col0):
    return pl.BlockSpec((HALO, cw), lambda i, j: (jnp.maximum(i * (tm // HALO) - 1, 0), col0 + j))


def _halo_next(tm, cw, col0, L):
    return pl.BlockSpec((HALO, cw), lambda i, j: (jnp.minimum((i + 1) * (tm // HALO), L // HALO - 1), col0 + j))


def _conv_fwd(proj, conv_w, conv_b, DS, DCONV, tm, cw):
    L = proj.shape[0]
    col0 = DS // cw
    K = CONV_WIDTH

    def body(x_ref, p_ref, w_ref, b_ref, o_ref, ext):
        i = pl.program_id(0)
        ext[0:HALO, :] = jnp.where(i == 0, 0.0, p_ref[...])
        ext[HALO:, :] = x_ref[...]
        acc = jnp.broadcast_to(b_ref[...], (tm, cw))
        for k in range(K):
            acc = acc + w_ref[k:k + 1, :] * ext[pl.ds(HALO - (K - 1) + k, tm), :]
        o_ref[...] = _silu(acc)

    return pl.pallas_call(
        body, grid=(L // tm, DCONV // cw),
        in_specs=[pl.BlockSpec((tm, cw), lambda i, j: (i, col0 + j)), _halo_prev(tm, cw, col0),
                  pl.BlockSpec((K, cw), lambda i, j: (0, j)), pl.BlockSpec((1, cw), lambda i, j: (0, j))],
        out_specs=pl.BlockSpec((tm, cw), lambda i, j: (i, j)),
        out_shape=jax.ShapeDtypeStruct((L, DCONV), F32),
        scratch_shapes=[pltpu.VMEM((tm + HALO, cw), F32)],
        compiler_params=_params(("parallel", "parallel")), name="conv_fwd")(proj, proj, conv_w, conv_b)


def _conv_bwd(name, proj, dact, conv_w, conv_b, dproj, DS, first, tm, cw):
    L = proj.shape[0]
    ncols = dact.shape[1]
    col0 = (DS + first) // cw
    wcol0 = first // cw
    K = CONV_WIDTH
    nrt = L // tm

    def body(x_ref, p_ref, n_ref, d_ref, dn_ref, w_ref, b_ref, alias_ref, dx_ref, dw_ref, db_ref, ext, dext):
        i = pl.program_id(1)
        last = i == nrt - 1
        ext[0:HALO, :] = jnp.where(i == 0, 0.0, p_ref[...])
        ext[HALO:HALO + tm, :] = x_ref[...]
        ext[HALO + tm:, :] = n_ref[...]
        dfull = jnp.concatenate([d_ref[...], jnp.where(last, 0.0, dn_ref[...])], axis=0)
        acc = jnp.broadcast_to(b_ref[...], (tm + HALO, cw))
        for k in range(K):
            acc = acc + w_ref[k:k + 1, :] * ext[pl.ds(HALO - (K - 1) + k, tm + HALO), :]
        sg = _sigmoid(acc)
        dconv = dfull * (sg * (1.0 + acc * (1.0 - sg)))
        dext[...] = dconv
        dx = jnp.zeros((tm, cw), F32)
        for k in range(K):
            dx = dx + w_ref[k:k + 1, :] * dext[pl.ds(K - 1 - k, tm), :]
        dx_ref[...] = dx.astype(BF16)

        @pl.when(i == 0)
        def _():
            dw_ref[...] = jnp.zeros_like(dw_ref)
            db_ref[...] = jnp.zeros_like(db_ref)

        dtile = dext[pl.ds(0, tm), :]
        db_ref[...] += jnp.sum(dtile, axis=0, keepdims=True)
        for k in range(K):
            dw_ref[k:k + 1, :] += jnp.sum(dtile * ext[pl.ds(HALO - (K - 1) + k, tm), :], axis=0, keepdims=True)

    prev = pl.BlockSpec((HALO, cw), lambda j, i: (jnp.maximum(i * (tm // HALO) - 1, 0), col0 + j))
    nxt = pl.BlockSpec((HALO, cw), lambda j, i: (jnp.minimum((i + 1) * (tm // HALO), L // HALO - 1), col0 + j))
    dnxt = pl.BlockSpec((HALO, cw), lambda j, i: (jnp.minimum((i + 1) * (tm // HALO), L // HALO - 1), j))
    return pl.pallas_call(
        body, grid=(ncols // cw, nrt),
        in_specs=[pl.BlockSpec((tm, cw), lambda j, i: (i, col0 + j)), prev, nxt,
                  pl.BlockSpec((tm, cw), lambda j, i: (i, j)), dnxt,
                  pl.BlockSpec((K, cw), lambda j, i: (0, wcol0 + j)), pl.BlockSpec((1, cw), lambda j, i: (0, wcol0 + j)),
                  _ANY],
        out_specs=[pl.BlockSpec((tm, cw), lambda j, i: (i, col0 + j)),
                   pl.BlockSpec((K, cw), lambda j, i: (0, j)), pl.BlockSpec((1, cw), lambda j, i: (0, j))],
        out_shape=[jax.ShapeDtypeStruct(dproj.shape, BF16), jax.ShapeDtypeStruct((K, ncols), F32),
                   jax.ShapeDtypeStruct((1, ncols), F32)],
        input_output_aliases={7: 0},
        scratch_shapes=[pltpu.VMEM((tm + 2 * HALO, cw), F32), pltpu.VMEM((tm + HALO, cw), F32)],
        compiler_params=_params(("parallel", "arbitrary")), name=name,
    )(proj, proj, proj, dact, dact, conv_w, conv_b, dproj)


def _pool_fwd(proj, pool_w, pool_scale, ucol, DP, tm):
    L = proj.shape[0]
    PG = len(POOL_WINDOWS)
    PGD = DP // PG
    col0 = ucol // PGD

    def body(u_ref, p_ref, w_ref, s_ref, pooled_ref, y_ref, ext):
        i, g = pl.program_id(0), pl.program_id(1)
        ext[0:HALO, :] = jnp.where(i == 0, 0.0, p_ref[...])
        ext[HALO:, :] = u_ref[...]
        t = i * tm + lax.broadcasted_iota(jnp.int32, (tm, 1), 0)
        for gi, win in enumerate(POOL_WINDOWS):
            @pl.when(g == gi)
            def _():
                acc = ext[pl.ds(HALO, tm), :]
                for j in range(1, win):
                    acc = acc + ext[pl.ds(HALO - j, tm), :]
                count = jnp.minimum(t + 1, win).astype(F32)
                pooled = (acc / count - u_ref[...]).astype(BF16)
                pooled_ref[...] = pooled
                y_ref[...] = (_dot(pooled, w_ref[...], NN) * s_ref[...]).astype(BF16)

    blk = pl.BlockSpec((tm, PGD), lambda i, g: (i, g))
    return pl.pallas_call(
        body, grid=(L // tm, PG),
        in_specs=[pl.BlockSpec((tm, PGD), lambda i, g: (i, col0 + g)), _halo_prev(tm, PGD, col0),
                  pl.BlockSpec((None, PGD, PGD), lambda i, g: (g, 0, 0)), pl.BlockSpec((1, PGD), lambda i, g: (0, g))],
        out_specs=[blk, blk],
        out_shape=[jax.ShapeDtypeStruct((L, DP), BF16), jax.ShapeDtypeStruct((L, DP), BF16)],
        scratch_shapes=[pltpu.VMEM((tm + HALO, PGD), F32)],
        compiler_params=_params(("parallel", "parallel")), name="pool_fwd")(proj, proj, pool_w, pool_scale)


def _pool_bwd(dy, pooled, pool_w, pool_scale, dproj, ucol, tm):
    L, DP = dy.shape
    PG = len(POOL_WINDOWS)
    PGD = DP // PG
    nrt = L // tm
    col0 = ucol // PGD

    def body(d_ref, dn_ref, p_ref, w_ref, s_ref, alias_ref, du_ref, dw_ref, ds_ref, qext):
        g, i = pl.program_id(0), pl.program_id(1)
        last = i == nrt - 1
        dfull = jnp.concatenate([d_ref[...], jnp.where(last, 0.0, dn_ref[...])], axis=0)
        dyp = (dfull * s_ref[...]).astype(BF16)
        dpooled = _dot(dyp, w_ref[...], NT)
        t = i * tm + lax.broadcasted_iota(jnp.int32, (tm + HALO, 1), 0)
        for gi, win in enumerate(POOL_WINDOWS):
            @pl.when(g == gi)
            def _():
                qext[...] = dpooled / jnp.minimum(t + 1, win).astype(F32)
                acc = qext[pl.ds(0, tm), :]
                for j in range(1, win):
                    acc = acc + qext[pl.ds(j, tm), :]
                du_ref[...] = (acc - dpooled[0:tm, :]).astype(BF16)

        @pl.when(i == 0)
        def _():
            dw_ref[...] = jnp.zeros_like(dw_ref)
            ds_ref[...] = jnp.zeros_like(ds_ref)

        pooled_t = p_ref[...]
        dw_ref[...] += _dot(pooled_t, dyp[0:tm, :], TN)
        ypre = _dot(pooled_t, w_ref[...], NN)
        ds_ref[...] += jnp.sum(d_ref[...] * ypre, axis=0, keepdims=True)

    blk = pl.BlockSpec((tm, PGD), lambda g, i: (i, g))
    nxt = pl.BlockSpec((HALO, PGD), lambda g, i: (jnp.minimum((i + 1) * (tm // HALO), L // HALO - 1), g))
    wspec = pl.BlockSpec((None, PGD, PGD), lambda g, i: (g, 0, 0))
    vec = pl.BlockSpec((1, PGD), lambda g, i: (0, g))
    return pl.pallas_call(
        body, grid=(PG, nrt),
        in_specs=[blk, nxt, blk, wspec, vec, _ANY],
        out_specs=[pl.BlockSpec((tm, PGD), lambda g, i: (i, col0 + g)), wspec, vec],
        out_shape=[jax.ShapeDtypeStruct(dproj.shape, BF16), jax.ShapeDtypeStruct((PG, PGD, PGD), F32),
                   jax.ShapeDtypeStruct((1, DP), F32)],
        input_output_aliases={5: 0},
        scratch_shapes=[pltpu.VMEM((tm + HALO, PGD), F32)],
        compiler_params=_params(("parallel", "arbitrary")), name="pool_bwd",
    )(dy, dy, pooled, pool_w, pool_scale, dproj)


def _ssd_common(dtc_raw, dtr_raw, bc, br, ac, ar):
    ch = CHUNK
    row = lax.broadcasted_iota(jnp.int32, (ch, ch), 0)
    col = lax.broadcasted_iota(jnp.int32, (ch, ch), 1)
    lower = row >= col
    dtc = _softplus(dtc_raw + bc)
    dtr = _softplus(dtr_raw + br)
    a_c = -jnp.exp(ac)
    a_r = -jnp.exp(ar)
    hi = lax.Precision.HIGHEST
    acol = jnp.dot(lower.astype(F32), dtc * a_c, preferred_element_type=F32, precision=hi)
    arow = jnp.dot(dtr * a_r, (row <= col).astype(F32), preferred_element_type=F32, precision=hi)
    return lower, row <= col, dtc, a_c, acol, arow


def _ssd_fwd(xbc, dtc_raw, dtr_raw, bias_c, bias_r, alog_c, alog_r, dskip_c, DS, N):
    L = xbc.shape[0]
    G, P, ch = SSD_GROUPS, HEAD_DIM, CHUNK
    R = dtc_raw.shape[2]
    GW = R * P
    nc = L // ch

    def body(xs_ref, b_ref, c_ref, dtc_ref, dtr_ref, bc_ref, br_ref, ac_ref, ar_ref, dk_ref,
             y_ref, st_ref, h_ref):
        @pl.when(pl.program_id(1) == 0)
        def _():
            h_ref[...] = jnp.zeros_like(h_ref)

        lower, _, dtc, _, acol_all, arow_all = _ssd_common(
            dtc_ref[...], dtr_ref[...], bc_ref[...], br_ref[...], ac_ref[...], ar_ref[...])
        bm = b_ref[...]
        cb16 = c_ref[...].astype(BF16)
        b16 = bm.astype(BF16)
        bt16 = bm.T.astype(BF16)
        cb = _dot(cb16, b16, NT)
        dk = dk_ref[...]
        st_ref[...] = h_ref[...]
        for r in range(R):
            acol = acol_all[:, r:r + 1]
            arow = arow_all[r:r + 1, :]
            alast = acol_all[ch - 1:ch, r:r + 1]
            decay = jnp.exp(jnp.where(lower, acol - arow, -1e30))
            x_h = xs_ref[:, pl.ds(r * P, P)]
            xdt = x_h * dtc[:, r:r + 1]
            m16 = (cb * decay).astype(BF16)
            h_prev = h_ref[r]
            y = _dot(m16, xdt.astype(BF16), NN)
            y = y + jnp.exp(acol) * _dot(cb16, h_prev.astype(BF16), NN)
            y = y + dk[:, r:r + 1] * x_h
            y_ref[:, pl.ds(r * P, P)] = y
            to_end = jnp.exp(alast - acol)
            h_ref[r] = jnp.exp(alast) * h_prev + _dot(bt16, (xdt * to_end).astype(BF16), NN)

    nb = DS // N
    return pl.pallas_call(
        body, grid=(G, nc),
        in_specs=[pl.BlockSpec((ch, GW), lambda g, c: (c, g)),
                  pl.BlockSpec((ch, N), lambda g, c: (c, nb + g)),
                  pl.BlockSpec((ch, N), lambda g, c: (c, nb + G + g)),
                  pl.BlockSpec((None, ch, R), lambda g, c: (g, c, 0)),
                  pl.BlockSpec((None, R, ch), lambda g, c: (g, 0, c)),
                  pl.BlockSpec((None, 1, R), lambda g, c: (g, 0, 0)),
                  pl.BlockSpec((None, R, 1), lambda g, c: (g, 0, 0)),
                  pl.BlockSpec((None, 1, R), lambda g, c: (g, 0, 0)),
                  pl.BlockSpec((None, R, 1), lambda g, c: (g, 0, 0)),
                  pl.BlockSpec((None, 1, R), lambda g, c: (g, 0, 0))],
        out_specs=[pl.BlockSpec((ch, GW), lambda g, c: (c, g)),
                   pl.BlockSpec((None, R, N, P), lambda g, c: (c, g, 0, 0))],
        out_shape=[jax.ShapeDtypeStruct((L, DS), F32), jax.ShapeDtypeStruct((nc, G * R, N, P), F32)],
        scratch_shapes=[pltpu.VMEM((R, N, P), F32)],
        compiler_params=_params(("parallel", "arbitrary")), name="ssd_fwd",
    )(xbc, xbc, xbc, dtc_raw, dtr_raw, bias_c, bias_r, alog_c, alog_r, dskip_c)


def _ssd_bwd(xbc, dtc_raw, dtr_raw, bias_c, bias_r, alog_c, alog_r, dskip_c, dy, states, DS, N):
    L = xbc.shape[0]
    G, P, ch = SSD_GROUPS, HEAD_DIM, CHUNK
    R = dtc_raw.shape[2]
    GW = R * P
    nc = L // ch

    def body(xs_ref, b_ref, c_ref, dtc_ref, dtr_ref, bc_ref, br_ref, ac_ref, ar_ref, dk_ref,
             dy_ref, stp_ref,
             dxs_ref, db_ref, dc_ref, ddt_ref, dal_ref, ddk_ref, dbias_ref, dh_ref):
        @pl.when(pl.program_id(1) == 0)
        def _():
            dh_ref[...] = jnp.zeros_like(dh_ref)
            dal_ref[...] = jnp.zeros_like(dal_ref)
            ddk_ref[...] = jnp.zeros_like(ddk_ref)
            dbias_ref[...] = jnp.zeros_like(dbias_ref)

        lower, upper, dtc, a_c, acol_all, arow_all = _ssd_common(
            dtc_ref[...], dtr_ref[...], bc_ref[...], br_ref[...], ac_ref[...], ar_ref[...])
        bm = b_ref[...]
        cm = c_ref[...]
        b16 = bm.astype(BF16)
        c16 = cm.astype(BF16)
        ct16 = cm.T.astype(BF16)
        cb = _dot(c16, b16, NT)
        cbt = _dot(b16, c16, NT)
        dk = dk_ref[...]
        lane_r = lax.broadcasted_iota(jnp.int32, (ch, R), 1)
        lane_1 = lax.broadcasted_iota(jnp.int32, (1, R), 1)
        dc = jnp.zeros((ch, N), F32)
        db = jnp.zeros((ch, N), F32)
        da_all = jnp.zeros((R, ch), F32)
        q_all = jnp.zeros((R, ch), F32)
        sxd_all = jnp.zeros((ch, R), F32)
        const = jnp.zeros((1, R), F32)
        ddk = jnp.zeros((1, R), F32)
        sub_r = lax.broadcasted_iota(jnp.int32, (R, ch), 0)
        ct = cm.T
        bt = bm.T
        dcb = jnp.zeros((ch, ch), F32)
        for r in range(R):
            acol = acol_all[:, r:r + 1]
            arow = arow_all[r:r + 1, :]
            alast = acol_all[ch - 1:ch, r:r + 1]
            seg = acol - arow
            decay = jnp.exp(jnp.where(lower, seg, -1e30))
            decay_t = jnp.exp(jnp.where(upper, -seg, -1e30))
            x_h = xs_ref[:, pl.ds(r * P, P)]
            dy_h = dy_ref[:, pl.ds(r * P, P)]
            dt_h = dtc[:, r:r + 1]
            dk_h = dk[:, r:r + 1]
            xdt = x_h * dt_h
            xdt16 = xdt.astype(BF16)
            dy16 = dy_h.astype(BF16)
            h_prev = stp_ref[r]
            h16 = h_prev.astype(BF16)
            dh_next = dh_ref[r]
            dhn16 = dh_next.astype(BF16)
            to_end = jnp.exp(alast - acol)
            e_a = jnp.exp(acol)
            mt = cbt * decay_t
            pm = _dot(dy16, xdt16, NT) * decay
            wt = _dot(xdt16, dy16, NT) * mt
            dxdt = _dot(mt.astype(BF16), dy16, NN) + to_end * _dot(b16, dhn16, NN)
            dcb = dcb + pm
            dc = dc + e_a * _dot(dy16, h16, NT)
            db = db + to_end * _dot(xdt16, dhn16, NT)
            dh_ref[r] = jnp.exp(alast) * dh_next + _dot(ct16, (dy_h * e_a).astype(BF16), NN)
            da = (jnp.sum(wt, axis=0, keepdims=True) - jnp.sum(pm * cb, axis=0, keepdims=True)
                  + jnp.exp(arow) * jnp.sum(ct * _dot(h16, dy16, NT), axis=0, keepdims=True))
            q = jnp.exp(alast - arow) * jnp.sum(bt * _dot(dhn16, xdt16, NT), axis=0, keepdims=True)
            da_all = da_all + jnp.where(sub_r == r, da, 0.0)
            q_all = q_all + jnp.where(sub_r == r, q, 0.0)
            sxd_all = sxd_all + jnp.where(lane_r == r, jnp.sum(dxdt * x_h, axis=1, keepdims=True), 0.0)
            const = const + jnp.where(lane_1 == r, jnp.exp(alast) * jnp.sum(dh_next * h_prev), 0.0)
            ddk = ddk + jnp.where(lane_1 == r, jnp.sum(dy_h * x_h), 0.0)
            dxs_ref[:, pl.ds(r * P, P)] = dxdt * dt_h + dk_h * dy_h
        dcb16 = dcb.astype(BF16)
        dc_ref[...] = dc + _dot(dcb16, b16, NN)
        db_ref[...] = db + _dot(dcb16, c16, TN)
        hi = lax.Precision.HIGHEST
        strict_lower = jnp.logical_and(lower, jnp.logical_not(upper))
        dda = (lax.dot_general(upper.astype(F32), da_all, NT, preferred_element_type=F32, precision=hi)
               + lax.dot_general(strict_lower.astype(F32), q_all, NT, preferred_element_type=F32, precision=hi)
               + const)
        ddt = dda * a_c + sxd_all
        dal_ref[...] += jnp.sum(dda * dtc, axis=0, keepdims=True) * a_c
        ddk_ref[...] += ddk
        ddt_raw = ddt * _sigmoid(dtc_ref[...] + bc_ref[...])
        ddt_ref[...] = ddt_raw
        dbias_ref[...] += jnp.sum(ddt_raw, axis=0, keepdims=True)

    nb = DS // N
    rc = lambda c: nc - 1 - c
    vec_c = pl.BlockSpec((None, 1, R), lambda g, c: (g, 0, 0))
    vec_r = pl.BlockSpec((None, R, 1), lambda g, c: (g, 0, 0))
    big = pl.BlockSpec((ch, GW), lambda g, c: (rc(c), g))
    return pl.pallas_call(
        body, grid=(G, nc),
        in_specs=[big,
                  pl.BlockSpec((ch, N), lambda g, c: (rc(c), nb + g)),
                  pl.BlockSpec((ch, N), lambda g, c: (rc(c), nb + G + g)),
                  pl.BlockSpec((None, ch, R), lambda g, c: (g, rc(c), 0)),
                  pl.BlockSpec((None, R, ch), lambda g, c: (g, 0, rc(c))),
                  vec_c, vec_r, vec_c, vec_r, vec_c,
                  big,
                  pl.BlockSpec((None, R, N, P), lambda g, c: (rc(c), g, 0, 0))],
        out_specs=[big,
                   pl.BlockSpec((ch, N), lambda g, c: (rc(c), g)),
                   pl.BlockSpec((ch, N), lambda g, c: (rc(c), g)),
                   pl.BlockSpec((None, ch, R), lambda g, c: (g, rc(c), 0)),
                   vec_c, vec_c, vec_c],
        out_shape=[jax.ShapeDtypeStruct((L, DS), F32), jax.ShapeDtypeStruct((L, G * N), F32),
                   jax.ShapeDtypeStruct((L, G * N), F32), jax.ShapeDtypeStruct((G, L, R), F32),
                   jax.ShapeDtypeStruct((G, 1, R), F32), jax.ShapeDtypeStruct((G, 1, R), F32),
                   jax.ShapeDtypeStruct((G, 1, R), F32)],
        scratch_shapes=[pltpu.VMEM((R, N, P), F32)],
        compiler_params=_params(("parallel", "arbitrary")), name="ssd_bwd",
    )(xbc, xbc, xbc, dtc_raw, dtr_raw, bias_c, bias_r, alog_c, alog_r, dskip_c, dy, states)


def _adam_math(w, g, m, v):
    m = ADAM_B1 * m + (1.0 - ADAM_B1) * g
    v = ADAM_B2 * v + (1.0 - ADAM_B2) * jnp.square(g)
    m_hat = m / (1.0 - ADAM_B1 ** ADAM_STEP)
    v_hat = v / (1.0 - ADAM_B2 ** ADAM_STEP)
    delta = -ADAM_LR * (m_hat / (jnp.sqrt(v_hat) + ADAM_EPS) + ADAM_WD * w)
    return delta, m, v


def _adam(name, w, g, m, v):
    rows, cols = w.shape
    tr = _tile(rows, max(8, (1 << 18) // cols // 8 * 8), 8)

    def body(w_ref, g_ref, m_ref, v_ref, d_ref, mo_ref, vo_ref):
        d, m2, v2 = _adam_math(w_ref[...], g_ref[...], m_ref[...], v_ref[...])
        d_ref[...] = d
        mo_ref[...] = m2
        vo_ref[...] = v2

    blk = pl.BlockSpec((tr, cols), lambda i: (i, 0))
    return pl.pallas_call(
        body, grid=(rows // tr,), in_specs=[blk] * 4, out_specs=[blk] * 3,
        out_shape=[jax.ShapeDtypeStruct((rows, cols), F32)] * 3,
        compiler_params=_params(("parallel",)), name=name)(w, g, m, v)


def _small_sum_adam(gathered, w, m, v, rows):
    def body(ga_ref, w_ref, m_ref, v_ref, g_ref, d_ref, mo_ref, vo_ref):
        g = ga_ref[0:rows, :]
        for d in range(1, N_DEV):
            g = g + ga_ref[d * rows:(d + 1) * rows, :]
        g_ref[...] = g
        dl, m2, v2 = _adam_math(w_ref[...], g, m_ref[...], v_ref[...])
        d_ref[...] = dl
        mo_ref[...] = m2
        vo_ref[...] = v2

    return pl.pallas_call(
        body, out_shape=[jax.ShapeDtypeStruct((rows, LANES), F32)] * 4,
        compiler_params=pltpu.CompilerParams(vmem_limit_bytes=VMEM_LIMIT), name="small_sum_adam",
    )(gathered, w, m, v)


def _row_tile(rh, cols):
    return _tile(rh, max(16, (1 << 19) // cols // 16 * 16), 16)


def _shard_dims(g):
    return (g.shape[1], g.shape[2]) if g.ndim == 3 else (g.shape[0], g.shape[1] // N_CHIPS)


def _pair_sum(name, g, recv, pos):
    r, c = _shard_dims(g)
    rh = r // 2
    tr = _row_tile(rh, c)
    nrt = rh // tr

    def body(pos_ref, a_ref, b_ref, o_ref):
        o_ref[...] = (a_ref[...] + b_ref[...]).astype(BF16)

    own = (pl.BlockSpec((None, tr, c), lambda k, i, p: (k, p[1] * nrt + i, 0)) if g.ndim == 3
           else pl.BlockSpec((tr, c), lambda k, i, p: (p[1] * nrt + i, k)))
    return pl.pallas_call(
        body,
        grid_spec=pltpu.PrefetchScalarGridSpec(
            num_scalar_prefetch=1, grid=(N_CHIPS, nrt),
            in_specs=[own,
                      pl.BlockSpec((None, tr, c), lambda k, i, p: (k, i, 0))],
            out_specs=pl.BlockSpec((None, tr, c), lambda k, i, p: (k, i, 0))),
        out_shape=jax.ShapeDtypeStruct((N_CHIPS, rh, c), BF16),
        compiler_params=_params(("parallel", "parallel")), name=name)(pos, g, recv)


def _chip_sum(name, parts, pos):
    _, rh, c = parts.shape
    tr = _row_tile(rh, c)
    nrt = rh // tr

    def body(pos_ref, p_ref, o_ref):
        s = p_ref[0].astype(F32)
        for k in range(1, N_CHIPS):
            s = s + p_ref[k].astype(F32)
        o_ref[...] = s

    return pl.pallas_call(
        body,
        grid_spec=pltpu.PrefetchScalarGridSpec(
            num_scalar_prefetch=1, grid=(nrt,),
            in_specs=[pl.BlockSpec((N_CHIPS, tr, c), lambda i, p: (0, i, 0))],
            out_specs=pl.BlockSpec((tr, c), lambda i, p: (p[1] * nrt + i, 0))),
        out_shape=jax.ShapeDtypeStruct((2 * rh, c), F32),
        compiler_params=_params(("parallel",)), name=name)(pos, parts)


_HBM = pl.BlockSpec(memory_space=pltpu.HBM)


def _chip_xy(k):
    return k // 2, k % 2


def _half_rows(ref, hc, rh):
    return ref.at[pl.ds(pl.multiple_of(hc * rh, 16), rh), :]


_SEM = pl.BlockSpec(memory_space=pltpu.SEMAPHORE)
_ANY = pl.BlockSpec(memory_space=pl.ANY)
_SPLIT = pltpu.CompilerParams(has_side_effects=pltpu.SideEffectType.DATAFLOW_SIDE_EFFECTING)


def _in_hbm(a):
    return pltpu.with_memory_space_constraint(a, pltpu.HBM)


def _push_start(name, srcs, land_shapes, copies_of):
    n = len(srcs)

    def body(*refs):
        s_refs, l_refs = refs[:n], refs[n:2 * n]
        send_sems, recv_sems = refs[2 * n], refs[2 * n + 1]
        token = refs[-1]
        x, y, c = lax.axis_index("x"), lax.axis_index("y"), lax.axis_index("c")
        me = 2 * x + y
        for i in range(n):
            for k in range(N_CHIPS):
                @pl.when(k != me)
                def _():
                    src, dst, dev = copies_of(i, k, s_refs[i], l_refs[i], me, x, y, c)
                    pltpu.make_async_remote_copy(
                        src_ref=src, dst_ref=dst, send_sem=send_sems.at[N_CHIPS * i + k],
                        recv_sem=recv_sems.at[N_CHIPS * i + me], device_id=dev, device_id_type=MESH).start()
        token[...] = jnp.zeros_like(token)

    lands = [lax.empty(s, d) for s, d in land_shapes]
    outs = pl.pallas_call(
        body, name=name,
        out_shape=[pltpu.SemaphoreType.DMA((N_CHIPS * n,)), pltpu.SemaphoreType.DMA((N_CHIPS * n,))]
        + [pltpu.HBM(s.shape, s.dtype) for s in srcs] + [pltpu.HBM(s, d) for s, d in land_shapes]
        + [jax.ShapeDtypeStruct((8, LANES), F32)],
        in_specs=[_HBM] * (2 * n),
        out_specs=[_SEM, _SEM] + [_HBM] * (2 * n) + [pl.BlockSpec(memory_space=pltpu.VMEM)],
        input_output_aliases={j: 2 + j for j in range(2 * n)},
        compiler_params=_SPLIT,
    )(*[_in_hbm(s) for s in srcs], *[_in_hbm(l) for l in lands])
    return outs[0], outs[1], outs[2:2 + n], outs[2 + n:2 + 2 * n], outs[-1]


def _push_wait(name, started, after, landed_of):
    send_sems, recv_sems, srcs, lands, _ = started
    n = len(srcs)

    def body(*refs):
        s_refs, l_refs = refs[:n], refs[n:2 * n]
        send, recv = refs[2 * n], refs[2 * n + 1]
        x, y, c = lax.axis_index("x"), lax.axis_index("y"), lax.axis_index("c")
        me = 2 * x + y
        for i in range(n):
            for k in range(N_CHIPS):
                @pl.when(k != me)
                def _():
                    src, dst = landed_of(i, k, s_refs[i], l_refs[i], me, c)
                    cp = pltpu.make_async_remote_copy(
                        src_ref=src, dst_ref=dst, send_sem=send.at[N_CHIPS * i + k], recv_sem=recv.at[N_CHIPS * i + k],
                        device_id=(x, y, c), device_id_type=MESH)
                    cp.wait_send()
                    cp.wait_recv()

    outs = pl.pallas_call(
        body, name=name,
        out_shape=[pltpu.HBM(s.shape, s.dtype) for s in srcs] + [pltpu.HBM(l.shape, l.dtype) for l in lands],
        in_specs=[_HBM] * (2 * n) + [_SEM, _SEM, _ANY], out_specs=[_HBM] * (2 * n),
        input_output_aliases={j: j for j in range(2 * n)},
        compiler_params=_SPLIT,
    )(*srcs, *lands, send_sems, recv_sems, after)
    return outs[:n], outs[n:]


def _gather_start(name, shards):
    def copies_of(i, k, src, land, me, x, y, c):
        rh = shards[i].shape[0] // 2
        kx, ky = _chip_xy(k)
        return _half_rows(src, c, rh), _half_rows(land.at[me], c, rh), (kx, ky, c)

    return _push_start(name, shards, [((N_CHIPS,) + s.shape, s.dtype) for s in shards], copies_of)


def _gather_wait(name, started, after):
    shapes = [s.shape for s in started[2]]

    def landed_of(i, k, src, land, me, c):
        rh = shapes[i][0] // 2
        return _half_rows(src, c, rh), _half_rows(land.at[k], c, rh)

    return _push_wait(name, started, after, landed_of)


def _forward_halves(name, bufs):
    n = len(bufs)

    def body(*refs):
        i_refs, o_refs, send_sems, recv_sems = refs[:n], refs[n:2 * n], refs[2 * n], refs[2 * n + 1]
        x, y, c = lax.axis_index("x"), lax.axis_index("y"), lax.axis_index("c")
        me = 2 * x + y

        def fwd(i, k, hc):
            rh = bufs[i].shape[1] // 2
            return pltpu.make_async_remote_copy(
                src_ref=_half_rows(i_refs[i].at[k], hc, rh), dst_ref=_half_rows(o_refs[i].at[k], hc, rh),
                send_sem=send_sems.at[i, k], recv_sem=recv_sems.at[i, k],
                device_id=(x, y, 1 - c), device_id_type=MESH)

        for i in range(n):
            for k in range(N_CHIPS):
                @pl.when(k != me)
                def _():
                    fwd(i, k, c).start()
        for i in range(n):
            for k in range(N_CHIPS):
                @pl.when(k != me)
                def _():
                    fwd(i, k, 1 - c).wait_recv()
        for i in range(n):
            for k in range(N_CHIPS):
                @pl.when(k != me)
                def _():
                    fwd(i, k, c).wait_send()

    return pl.pallas_call(
        body, in_specs=[_HBM] * n, out_specs=[_HBM] * n,
        out_shape=[jax.ShapeDtypeStruct(b.shape, b.dtype) for b in bufs],
        input_output_aliases={i: i for i in range(n)},
        scratch_shapes=[pltpu.SemaphoreType.DMA((n, N_CHIPS))] * 2,
        name=name)(*bufs)


def _swap_halves(name, grads):
    n = len(grads)
    dims = [_shard_dims(g) for g in grads]

    def body(*refs):
        g_refs, o_refs, send_sems, recv_sems = refs[:n], refs[n:2 * n], refs[2 * n], refs[2 * n + 1]
        x, y, c = lax.axis_index("x"), lax.axis_index("y"), lax.axis_index("c")
        copies = []
        for i in range(n):
            r, cw = dims[i]
            for k in range(N_CHIPS):
                shard = g_refs[i].at[k] if grads[i].ndim == 3 else g_refs[i].at[:, pl.ds(k * cw, cw)]
                copies.append(pltpu.make_async_remote_copy(
                    src_ref=_half_rows(shard, 1 - c, r // 2), dst_ref=o_refs[i].at[k],
                    send_sem=send_sems.at[i, k], recv_sem=recv_sems.at[i, k],
                    device_id=(x, y, 1 - c), device_id_type=MESH))
        for cp in copies:
            cp.start()
        for cp in copies:
            cp.wait()

    return pl.pallas_call(
        body, in_specs=[_HBM] * n, out_specs=[_HBM] * n,
        out_shape=[jax.ShapeDtypeStruct((N_CHIPS, r // 2, cw), F32) for r, cw in dims],
        scratch_shapes=[pltpu.SemaphoreType.DMA((n, N_CHIPS))] * 2,
        name=name)(*grads)


def _scatter_start(name, parts):
    def copies_of(i, k, src, land, me, x, y, c):
        kx, ky = _chip_xy(k)
        return src.at[k], land.at[me], (kx, ky, c)

    return _push_start(name, parts, [(p.shape, p.dtype) for p in parts], copies_of)


def _scatter_wait(name, started, after):
    return _push_wait(name, started, after, lambda i, k, src, land, me, c: (src.at[k], land.at[k]))


def _join_halves(bufs):
    n = len(bufs)

    def body(*refs):
        i_refs, o_refs, send_sems, recv_sems = refs[:n], refs[n:2 * n], refs[2 * n], refs[2 * n + 1]
        x, y, c = lax.axis_index("x"), lax.axis_index("y"), lax.axis_index("c")
        copies = []
        for i in range(n):
            rh = bufs[i].shape[0] // 2
            copies.append(pltpu.make_async_remote_copy(
                src_ref=_half_rows(i_refs[i], c, rh), dst_ref=_half_rows(o_refs[i], c, rh),
                send_sem=send_sems.at[i], recv_sem=recv_sems.at[i],
                device_id=(x, y, 1 - c), device_id_type=MESH))
        for cp in copies:
            cp.start()
        for i in range(n):
            rh = bufs[i].shape[0] // 2
            pltpu.make_async_remote_copy(
                src_ref=_half_rows(i_refs[i], c, rh), dst_ref=_half_rows(o_refs[i], 1 - c, rh),
                send_sem=send_sems.at[i], recv_sem=recv_sems.at[i],
                device_id=(x, y, 1 - c), device_id_type=MESH).wait_recv()
        for cp in copies:
            cp.wait_send()

    return pl.pallas_call(
        body, in_specs=[_HBM] * n, out_specs=[_HBM] * n,
        out_shape=[jax.ShapeDtypeStruct(b.shape, F32) for b in bufs],
        input_output_aliases={i: i for i in range(n)},
        scratch_shapes=[pltpu.SemaphoreType.DMA((n,))] * 2,
        name="join_halves")(*bufs)


def _all_gather_small(name, blk):
    m_per, n = blk.shape

    def body(x_ref, out_ref, send_sems, recv_sems, local_sem):
        x, y, c = lax.axis_index("x"), lax.axis_index("y"), lax.axis_index("c")
        me, sibling = (x, y, c), (x, y, 1 - c)
        chips = [(1 - x, y), (x, 1 - y), (1 - x, 1 - y)]

        def rows(px, py, pc):
            return out_ref.at[pl.ds((4 * px + 2 * py + pc) * m_per, m_per), :]

        def copy(k, block, to, src=None):
            return pltpu.make_async_remote_copy(
                src_ref=rows(*block) if src is None else src, dst_ref=rows(*block),
                send_sem=send_sems.at[k], recv_sem=recv_sems.at[k],
                device_id=to, device_id_type=MESH)

        mine = pltpu.make_async_copy(x_ref, rows(*me), local_sem)
        mine.start()
        first = [copy(0, me, sibling, src=x_ref)]
        first += [copy(1 + j, me, (*chip, c), src=x_ref) for j, chip in enumerate(chips)]
        for cp in first:
            cp.start()
        passed = [copy(4 + j, (*chip, c), sibling) for j, chip in enumerate(chips)]
        for j, chip in enumerate(chips):
            copy(1 + j, (*chip, c), me).wait_recv()
            passed[j].start()
        copy(0, sibling, me).wait_recv()
        for j, chip in enumerate(chips):
            copy(4 + j, (*chip, 1 - c), me).wait_recv()
        for cp in first + passed:
            cp.wait_send()
        mine.wait()

    return pl.pallas_call(
        body, out_shape=jax.ShapeDtypeStruct((N_DEV * m_per, n), blk.dtype),
        in_specs=[pl.BlockSpec(memory_space=pltpu.VMEM)],
        out_specs=pl.BlockSpec(memory_space=pltpu.VMEM),
        scratch_shapes=[pltpu.SemaphoreType.DMA((7,)), pltpu.SemaphoreType.DMA((7,)), pltpu.SemaphoreType.DMA],
        name=name)(blk)


def _pack_rows(vecs, width):
    parts = []
    for v in vecs:
        f = v.reshape(-1)
        pad = (-f.shape[0]) % (8 * width)
        parts.append(jnp.pad(f, (0, pad)) if pad else f)
    return jnp.concatenate(parts).reshape(-1, width)


def _unpack_rows(packed, shapes, width):
    flat = packed.reshape(-1)
    out, off = [], 0
    for s in shapes:
        n = math.prod(s)
        out.append(flat[off:off + n].reshape(s))
        off += n + ((-n) % (8 * width))
    return out


class _WinPlan:
    def __init__(self, ncol, dt0, h, dmain):
        self.ncol, self.h = ncol, h
        self.dt_shard = dt0 // ncol
        assert (dt0 + h - 1) // ncol == self.dt_shard and dmain % LANES == 0
        self.dt_local = dt0 - self.dt_shard * ncol
        to_main = lambda g: g if g <= dt0 else g - h
        self.lo = [to_main(ncol * k) for k in range(N_CHIPS)]
        self.hi = [to_main(ncol * (k + 1)) for k in range(N_CHIPS)]
        down = lambda v: v // LANES * LANES
        self.ww = max(-(-(hi - down(lo)) // LANES) * LANES for lo, hi in zip(self.lo, self.hi))
        self.ws = [min(down(lo), dmain - self.ww) for lo in self.lo]
        self.dmain = dmain

    def to_window(self, k, shard):
        if k == self.dt_shard:
            shard = jnp.concatenate([shard[:, :self.dt_local], shard[:, self.dt_local + self.h:]], axis=1)
        left = self.lo[k] - self.ws[k]
        return jnp.pad(shard, ((0, 0), (left, self.ww - left - shard.shape[1])))

    def from_window(self, k, window, dt_cols):
        left = self.lo[k] - self.ws[k]
        body = window[:, left:left + self.hi[k] - self.lo[k]]
        if k == self.dt_shard:
            body = jnp.concatenate([body[:, :self.dt_local], dt_cols, body[:, self.dt_local:]], axis=1)
        return body

    def merge(self, windows):
        cuts = sorted({0, self.dmain} | set(self.ws) | {w + self.ww for w in self.ws})
        segs = []
        for a, b in zip(cuts[:-1], cuts[1:]):
            parts = [windows[k][:, a - self.ws[k]:b - self.ws[k]] for k in range(N_CHIPS)
                     if self.ws[k] <= a and b <= self.ws[k] + self.ww]
            segs.append(functools.reduce(jnp.add, parts))
        return jnp.concatenate(segs, axis=1)

    def split(self, g_main):
        return jnp.stack([g_main[:, w:w + self.ww] for w in self.ws])


def kernel(x, attn_norm_w, w_in, conv_w, conv_b, dt_bias, a_log, d_skip, ssd_norm_w, pool_w, pool_scale, w_out, ffn_norm_w, w_gate, w_up, w_down, final_norm_w, loss_target, m_attn_norm_w, m_w_in, m_conv_w, m_conv_b, m_dt_bias, m_a_log, m_d_skip, m_ssd_norm_w, m_pool_w, m_pool_scale, m_w_out, m_ffn_norm_w, m_w_gate, m_w_up, m_w_down, m_final_norm_w, v_attn_norm_w, v_w_in, v_conv_w, v_conv_b, v_dt_bias, v_a_log, v_d_skip, v_ssd_norm_w, v_pool_w, v_pool_scale, v_w_out, v_ffn_norm_w, v_w_gate, v_w_up, v_w_down, v_final_norm_w):
    G, P, PG = SSD_GROUPS, HEAD_DIM, len(POOL_WINDOWS)
    _, L, D = x.shape
    H = a_log.shape[1]
    R = H // G
    DS = H * P
    DCONV = conv_b.shape[1]
    N = (DCONV - DS) // (2 * G)
    DP = pool_scale.shape[1]
    PGD = DP // PG
    DIN = N_CHIPS * w_in.shape[2]
    DFF = N_CHIPS * w_gate.shape[2]
    DMAIN = DS + DCONV + DP
    assert DIN == DMAIN + H and DS == DP and H <= LANES

    cx, cy, cc = lax.axis_index("x"), lax.axis_index("y"), lax.axis_index("c")
    chip = 2 * cx + cy

    win = _WinPlan(DIN // N_CHIPS, DS + DCONV, H, DMAIN)
    my_window = lax.switch(chip, [functools.partial(win.to_window, k) for k in range(N_CHIPS)], w_in[0].astype(BF16))
    shards_in = [my_window]
    shards_rest = [pool_w[0].reshape(PG * PGD // N_CHIPS, PGD).astype(BF16), w_out[0].astype(BF16),
                   w_gate[0].astype(BF16), w_up[0].astype(BF16), w_down[0].astype(BF16)]
    started_in = _gather_start("gather_start_in", shards_in)
    started_rest = _gather_start("gather_start_rest", shards_rest)

    def finish_gather(tag, started, after):
        shards, landed = _gather_wait("gather_wait_" + tag, started, after)
        landed = _forward_halves("gather_forward_" + tag, landed)
        return [lax.dynamic_update_slice(g, s[None], (chip, 0, 0)) for g, s in zip(landed, shards)]

    def cols(p):
        return jnp.moveaxis(p, 0, -2).reshape(p.shape[1:-1] + (N_CHIPS * p.shape[-1],))

    ncw = CONV_WIDTH * DCONV // N_CHIPS
    dt_here = jnp.where(chip == win.dt_shard, w_in[0][:, win.dt_local:win.dt_local + H], 0.0)
    start_blk = _pack_rows([conv_w[0], dt_here], LANES)
    start_all = _all_gather_small("gather_conv_w", start_blk).reshape(N_CHIPS, 2, -1)[:, 0]
    conv_w_f = cols(start_all[:, :ncw].reshape(N_CHIPS, CONV_WIDTH, DCONV // N_CHIPS))
    dt_off = ncw + (-ncw) % (8 * LANES)
    w_dt = jnp.pad(start_all[win.dt_shard, dt_off:dt_off + D * H].reshape(D, H), ((0, 0), (0, LANES - H))).astype(BF16)

    xl, tgt = x[0], loss_target[0]
    tm_row = _tile(L, 256, HALO)
    tm_mm = _tile(L, 1024, 16)
    hn1 = _rms_fwd("rms1_fwd", xl, attn_norm_w + started_rest[4][0, 0], tm_row)
    w_main = win.merge(finish_gather("in", started_in, hn1)[0])
    proj, = _mm("proj_main", "nn", [(hn1, w_main)], L, DMAIN, D, tm_mm, 512, D, [F32])
    dt_raw, = _mm("proj_dt", "nn", [(hn1, w_dt)], L, LANES, D, tm_mm, LANES, D, [F32])

    cwid = _tile(math.gcd(DS, DCONV), 512, LANES)
    tm_conv = _tile(L, 512, HALO)
    xbc = _conv_fwd(proj, conv_w_f, conv_b, DS, DCONV, tm_conv, cwid)

    dt_g = dt_raw[:, :H].reshape(L, G, R)
    dtc_raw = jnp.transpose(dt_g, (1, 0, 2))
    dtr_raw = jnp.transpose(dt_g, (1, 2, 0))
    as_c = lambda v: v.reshape(G, 1, R)
    as_r = lambda v: v.reshape(G, R, 1)
    ssd_args = (xbc, dtc_raw, dtr_raw, as_c(dt_bias), as_r(dt_bias), as_c(a_log), as_r(a_log), as_c(d_skip))
    y_ssd_raw, states = _ssd_fwd(*ssd_args, DS, N)
    y_ssd = _gated_fwd(y_ssd_raw, proj, ssd_norm_w, DS, tm_row)
    gathered = finish_gather("rest", started_rest, y_ssd)
    pool_w_f = jnp.moveaxis(gathered[0].reshape(N_CHIPS, PG, PGD // N_CHIPS, PGD), 0, 1).reshape(PG, PGD, PGD)
    w_out_f = gathered[1].reshape(2 * DS, D)
    w_gate_f, w_up_f = cols(gathered[2]), cols(gathered[3])
    w_down_f = gathered[4].reshape(DFF, D)
    w_out_top, w_out_bot = w_out_f[:DS], w_out_f[DS:]
    pooled, y_pool = _pool_fwd(proj, pool_w_f, pool_scale, DS + DCONV, DP, tm_conv)

    add_res = lambda accs, ex, rex: [accs[0] + ex[0]]
    h1, = _mm("out_proj", "nn", [(y_ssd, w_out_top), (y_pool, w_out_bot)], L, D, DS, tm_mm, 512, DS, [F32],
              epilogue=add_res, extras=[xl])
    hn2 = _rms_fwd("rms2_fwd", h1, ffn_norm_w, tm_row)

    def glu(accs, ex, rex):
        return [accs[0], accs[1], (_silu(accs[0]) * accs[1])]

    tn_ff = _tile(DFF, 512, LANES)
    gate, up, act = _mm("ffn_in", "nn", [(hn2, w_gate_f), (hn2, w_up_f)], L, DFF, D, tm_mm, tn_ff, D,
                        [F32, F32, BF16], epilogue=glu, separate=True)
    tk_ff = _tile(DFF, DFF // 2, LANES)
    h2, = _mm("ffn_out", "nn", [(act, w_down_f)], L, D, DFF, tm_mm, 512, tk_ff, [F32], epilogue=add_res, extras=[h1])
    dh2, dh2_16, loss_blk, g_final = _final_loss(h2, final_norm_w.reshape(1, D), tgt, tm_row)

    def dglu(accs, ex, rex):
        gt, u = ex
        sg = _sigmoid(gt)
        return [accs[0] * u * (sg * (1.0 + gt * (1.0 - sg))), accs[0] * (gt * sg)]

    dgate, dup = _mm("ffn_out_dx", "nt", [(dh2_16, w_down_f)], L, DFF, D, tm_mm, tn_ff, D, [BF16, BF16],
                     epilogue=dglu, extras=[gate, up])
    tk_tok = _tile(L, 2048, 16)
    g_w_down, = _mm("ffn_out_dw", "tn", [(act, dh2_16)], DFF, D, L, _tile(DFF, 1536, LANES), 1024, tk_tok, [F32])
    g_w_gate, g_w_up = _mm("ffn_in_dw", "tn", [(hn2, dgate), (hn2, dup)], D, DFF, L, 1024, tn_ff, tk_tok, [F32, F32],
                           separate=True)

    pos = jnp.stack([chip, cc]).astype(jnp.int32)

    def start_reduce(tag, names, full_grads):
        from_sibling = _swap_halves("swap_halves_" + tag, full_grads)
        partials = [_pair_sum("pair_sum_" + n, g, r, pos) for n, g, r in zip(names, full_grads, from_sibling)]
        return _scatter_start("scatter_start_" + tag, partials)

    pin_row = lambda started, n: jnp.zeros((1, n), F32) + started[4][0, 0]
    add_row = lambda accs, ex, rex: [accs[0] + rex[0]]
    names_ffn = ["w_gate", "w_up", "w_down"]
    started_ffn = start_reduce("ffn", names_ffn, [g_w_gate, g_w_up, g_w_down.reshape(N_CHIPS, -1, D)])
    dhn2, = _mm("ffn_in_dx", "nt", [(dgate, w_gate_f), (dup, w_up_f)], L, D, DFF, tm_mm, 512,
                _tile(DFF, DFF // 4, LANES), [F32], epilogue=add_row, row_extras=[pin_row(started_ffn, D)])
    dh1, g_ffn_norm, dh1_16 = _rms_bwd("rms2_bwd", h1, ffn_norm_w, [dhn2], dh2, tm_row, True)

    dy_ssd, dy_pool = _mm("out_proj_dx", "nt", [(dh1_16, w_out_top), (dh1_16, w_out_bot)], L, DS, D, tm_mm, 512, D,
                          [F32, F32], separate=True)
    g_w_out_top, g_w_out_bot = _mm("out_proj_dw", "tn", [(y_ssd, dh1_16), (y_pool, dh1_16)], DS, D, L, 1024, 512,
                                   tk_tok, [F32, F32], separate=True)
    dy_raw, dproj, g_ssd_norm = _gated_bwd(y_ssd_raw, proj, ssd_norm_w, dy_ssd, DS, DMAIN, tm_row)
    dxs, db, dc, ddt_raw, g_a_log, g_d_skip, g_dt_bias = _ssd_bwd(*ssd_args, dy_raw, states, DS, N)
    g_conv_w, g_conv_b = [], []
    for tag, dact, first in (("xs", dxs, 0), ("b", db, DS), ("c", dc, DS + G * N)):
        dproj, gw, gb = _conv_bwd("conv_bwd_" + tag, proj, dact, conv_w_f, conv_b, dproj, DS, first, tm_conv, cwid)
        g_conv_w.append(gw)
        g_conv_b.append(gb)
    g_conv_w, g_conv_b = jnp.concatenate(g_conv_w, axis=1), jnp.concatenate(g_conv_b, axis=1)
    dproj, g_pool_w, g_pool_scale = _pool_bwd(dy_pool, pooled, pool_w_f, pool_scale, dproj, DS + DCONV, tm_conv)
    ddt_pad = jnp.pad(jnp.transpose(ddt_raw, (1, 0, 2)).reshape(L, H), ((0, 0), (0, LANES - H))).astype(BF16)

    tk_main = _tile(DMAIN, DMAIN // 2, LANES)
    g_w_main, = _mm("proj_main_dw", "tn", [(hn1, dproj)], D, DMAIN, L, 1024, _tile(DMAIN, 1024, LANES), tk_tok, [F32])
    names_mix = ["w_in", "pool_w", "w_out"]
    started_mix = start_reduce("mix", names_mix, [
        win.split(g_w_main),
        jnp.moveaxis(g_pool_w.reshape(PG, N_CHIPS, PGD // N_CHIPS, PGD), 1, 0).reshape(N_CHIPS, -1, PGD),
        jnp.stack([g_w_out_top.reshape(2, DS // 2, D), g_w_out_bot.reshape(2, DS // 2, D)]).reshape(N_CHIPS, -1, D)])
    dhn1a, = _mm("proj_main_dx", "nt", [(dproj, w_main)], L, D, DMAIN, tm_mm, 512, tk_main, [F32],
                 epilogue=add_row, row_extras=[pin_row(started_mix, D)])
    dhn1b, = _mm("proj_dt_dx", "nt", [(ddt_pad, w_dt)], L, D, LANES, tm_mm, 512, LANES, [F32])
    g_w_dt, = _mm("proj_dt_dw", "tn", [(hn1, ddt_pad)], D, LANES, L, 512, LANES, tk_tok, [F32])
    grad_x, g_attn_norm = _rms_bwd("rms1_bwd", xl, attn_norm_w, [dhn1a, dhn1b], dh1, tm_row, False)

    def finish_reduce(tag, names, started, after):
        partials, landed = _scatter_wait("scatter_wait_" + tag, started, after)
        landed = [lax.dynamic_update_slice(l, lax.dynamic_index_in_dim(p, chip, 0), (chip, 0, 0))
                  for l, p in zip(landed, partials)]
        return [_chip_sum("chip_sum_" + n, l, pos) for n, l in zip(names, landed)]

    halves = finish_reduce("ffn", names_ffn, started_ffn, grad_x) + finish_reduce("mix", names_mix, started_mix, grad_x)
    red = dict(zip(names_ffn + names_mix, _join_halves(halves)))

    small_w = [attn_norm_w, conv_b, dt_bias, a_log, d_skip, ssd_norm_w, pool_scale, ffn_norm_w, final_norm_w]
    small_m = [m_attn_norm_w, m_conv_b, m_dt_bias, m_a_log, m_d_skip, m_ssd_norm_w, m_pool_scale, m_ffn_norm_w, m_final_norm_w]
    small_v = [v_attn_norm_w, v_conv_b, v_dt_bias, v_a_log, v_d_skip, v_ssd_norm_w, v_pool_scale, v_ffn_norm_w, v_final_norm_w]
    small_g = [g_attn_norm, g_conv_b, g_dt_bias.reshape(1, H), g_a_log.reshape(1, H), g_d_skip.reshape(1, H),
               g_ssd_norm, g_pool_scale, g_ffn_norm, g_final.reshape(D)]
    extra_shapes = [(CONV_WIDTH, DCONV), (D, H), (1, LANES)]
    zeros_like_extra = [jnp.zeros(s, F32) for s in extra_shapes]
    g_blk = _pack_rows(small_g + [g_conv_w, g_w_dt[:, :H], loss_blk], LANES)
    rows = g_blk.shape[0]
    small_all = _all_gather_small("gather_small_grads", g_blk)
    s_g, s_d, s_m, s_v = _small_sum_adam(small_all, _pack_rows(small_w + zeros_like_extra, LANES),
                                         _pack_rows(small_m + zeros_like_extra, LANES),
                                         _pack_rows(small_v + zeros_like_extra, LANES), rows)
    shapes = [w.shape for w in small_w] + extra_shapes
    sg_list = _unpack_rows(s_g, shapes, LANES)
    sd_list = _unpack_rows(s_d, shapes, LANES)[:len(small_w)]
    sm_list = _unpack_rows(s_m, shapes, LANES)[:len(small_w)]
    sv_list = _unpack_rows(s_v, shapes, LANES)[:len(small_w)]
    loss = sg_list[-1][0, 0]
    grad_conv_w = lax.dynamic_slice(sg_list[-3], (0, chip * (DCONV // N_CHIPS)), (CONV_WIDTH, DCONV // N_CHIPS))
    grad_w_in = lax.switch(chip, [functools.partial(win.from_window, k) for k in range(N_CHIPS)], red["w_in"], sg_list[-2])

    def adam_nd(name, w, g, m, v):
        shp = w.shape
        to2 = lambda a: a.reshape(-1, shp[-1])
        d, m2, v2 = _adam(name, to2(w), to2(g), to2(m), to2(v))
        return d.reshape(shp), m2.reshape(shp), v2.reshape(shp)

    sharded = {
        "w_in": (w_in, grad_w_in[None], m_w_in, v_w_in),
        "conv_w": (conv_w, grad_conv_w[None], m_conv_w, v_conv_w),
        "pool_w": (pool_w, red["pool_w"].reshape(pool_w.shape), m_pool_w, v_pool_w),
        "w_out": (w_out, red["w_out"][None], m_w_out, v_w_out),
        "w_gate": (w_gate, red["w_gate"][None], m_w_gate, v_w_gate),
        "w_up": (w_up, red["w_up"][None], m_w_up, v_w_up),
        "w_down": (w_down, red["w_down"][None], m_w_down, v_w_down),
    }
    upd = {n: (a[1],) + adam_nd("adam_" + n, *a) for n, a in sharded.items()}
    small_names = ["attn_norm_w", "conv_b", "dt_bias", "a_log", "d_skip", "ssd_norm_w", "pool_scale", "ffn_norm_w",
                   "final_norm_w"]
    for i, n in enumerate(small_names):
        upd[n] = (sg_list[i], sd_list[i], sm_list[i], sv_list[i])

    order = ["attn_norm_w", "w_in", "conv_w", "conv_b", "dt_bias", "a_log", "d_skip", "ssd_norm_w", "pool_w",
             "pool_scale", "w_out", "ffn_norm_w", "w_gate", "w_up", "w_down", "final_norm_w"]
    outs = [loss, grad_x[None]]
    for j in range(4):
        outs += [upd[n][j] for n in order]
    return tuple(outs)
```

```python
import functools
import math

import jax
import jax.numpy as jnp
from jax import lax
from jax.experimental import pallas as pl
from jax.experimental.pallas import tpu as pltpu

F32 = jnp.float32
BF16 = jnp.bfloat16

NORM_EPS = 1e-5
HEAD_DIM = 64
SSD_GROUPS = 4
CONV_WIDTH = 4
CHUNK = 256
POOL_WINDOWS = (2, 4, 8, 16)
ADAM_LR = 0.001
ADAM_B1 = 0.9
ADAM_B2 = 0.999
ADAM_EPS = 1e-08
ADAM_WD = 0.01
ADAM_STEP = 10

N_CHIPS = 4
N_DEV = 8
LANES = 128
HALO = 16
FLAT_W = 512
VMEM_LIMIT = 52 * 1024 * 1024
MESH = pl.DeviceIdType.MESH

NN = (((1,), (0,)), ((), ()))
NT = (((1,), (1,)), ((), ()))
TN = (((0,), (0,)), ((), ()))


def _tile(n, cap, mult):
    best = None
    for t in range(mult, min(n, cap) + 1, mult):
        if n % t == 0:
            best = t
    return best if best is not None else n


def _params(sem):
    return pltpu.CompilerParams(dimension_semantics=sem, vmem_limit_bytes=VMEM_LIMIT)


def _dot(a, b, dims):
    return lax.dot_general(a, b, dims, preferred_element_type=F32)


def _sigmoid(x):
    return 1.0 / (1.0 + jnp.exp(-x))


def _silu(x):
    return x * _sigmoid(x)


def _softplus(x):
    return jnp.maximum(x, 0.0) + jnp.log(1.0 + jnp.exp(-jnp.abs(x)))


def _mm(name, mode, pairs, M, N, K, tm, tn, tk, out_dtypes, epilogue=None, extras=(), row_extras=(),
        separate=False):
    tm, tn, tk = min(tm, M), min(tn, N), min(tk, K)
    assert M % tm == 0 and N % tn == 0 and K % tk == 0, (name, M, N, K, tm, tn, tk)
    nk = K // tk
    npairs = len(pairs)
    nacc = npairs if separate else 1
    if mode == "nn":
        a_spec = pl.BlockSpec((tm, tk), lambda i, j, k: (i, k))
        b_spec = pl.BlockSpec((tk, tn), lambda i, j, k: (k, j))
        dims = NN
    elif mode == "nt":
        a_spec = pl.BlockSpec((tm, tk), lambda i, j, k: (i, k))
        b_spec = pl.BlockSpec((tn, tk), lambda i, j, k: (j, k))
        dims = NT
    else:
        a_spec = pl.BlockSpec((tk, tm), lambda i, j, k: (k, i))
        b_spec = pl.BlockSpec((tk, tn), lambda i, j, k: (k, j))
        dims = TN
    o_spec = pl.BlockSpec((tm, tn), lambda i, j, k: (i, j))
    r_spec = pl.BlockSpec((1, tn), lambda i, j, k: (0, j))
    if epilogue is None:
        epilogue = lambda accs, ex, rex: accs
    n_ex, n_rex, n_out = len(extras), len(row_extras), len(out_dtypes)

    def body(*refs):
        ab = refs[:2 * npairs]
        ex = refs[2 * npairs:2 * npairs + n_ex]
        rex = refs[2 * npairs + n_ex:2 * npairs + n_ex + n_rex]
        outs = refs[2 * npairs + n_ex + n_rex:2 * npairs + n_ex + n_rex + n_out]
        accs = refs[2 * npairs + n_ex + n_rex + n_out:]

        def products():
            res = [None] * nacc
            for p in range(npairs):
                d = _dot(ab[2 * p][...], ab[2 * p + 1][...], dims)
                q = p if separate else 0
                res[q] = d if res[q] is None else res[q] + d
            return res

        def finish(vals):
            res = epilogue(vals, [e[...] for e in ex], [r[...] for r in rex])
            for o, v in zip(outs, res):
                o[...] = v.astype(o.dtype)

        if nk == 1:
            finish(products())
        else:
            k = pl.program_id(2)

            @pl.when(k == 0)
            def _():
                for q in range(nacc):
                    accs[q][...] = jnp.zeros_like(accs[q])

            for p in range(npairs):
                accs[p if separate else 0][...] += _dot(ab[2 * p][...], ab[2 * p + 1][...], dims)

            @pl.when(k == nk - 1)
            def _():
                finish([a[...] for a in accs])

    in_specs = [a_spec, b_spec] * npairs + [o_spec] * n_ex + [r_spec] * n_rex
    args = [t for p in pairs for t in p] + list(extras) + list(row_extras)
    outs = pl.pallas_call(
        body,
        grid=(M // tm, N // tn, nk),
        in_specs=in_specs,
        out_specs=[o_spec] * n_out,
        out_shape=[jax.ShapeDtypeStruct((M, N), d) for d in out_dtypes],
        scratch_shapes=[pltpu.VMEM((tm, tn), F32) for _ in range(nacc if nk > 1 else 0)],
        compiler_params=_params(("parallel", "parallel", "arbitrary")),
        name=name,
    )(*args)
    return outs


def _rms(xf, w):
    y = xf * lax.rsqrt(jnp.mean(xf * xf, axis=-1, keepdims=True) + NORM_EPS)
    return y * w


def _rms_fwd(name, x, w, tm):
    L, D = x.shape

    def body(x_ref, w_ref, o_ref):
        o_ref[...] = _rms(x_ref[...], w_ref[...]).astype(BF16)

    return pl.pallas_call(
        body, grid=(L // tm,),
        in_specs=[pl.BlockSpec((tm, D), lambda i: (i, 0)), pl.BlockSpec((1, D), lambda i: (0, 0))],
        out_specs=pl.BlockSpec((tm, D), lambda i: (i, 0)),
        out_shape=jax.ShapeDtypeStruct((L, D), BF16),
        compiler_params=_params(("parallel",)), name=name)(x, w)


def _rms_bwd(name, x, w, dparts, dres, tm, with_bf16):
    L, D = x.shape
    nparts = len(dparts)

    def body(*refs):
        x_ref, w_ref = refs[:2]
        p_refs = refs[2:2 + nparts]
        r_ref = refs[2 + nparts]
        outs = refs[3 + nparts:]
        dhn = p_refs[0][...]
        for p in p_refs[1:]:
            dhn = dhn + p[...]
        _, vjp = jax.vjp(_rms, x_ref[...], w_ref[...])
        dx, dw = vjp(dhn)
        dx = dx + r_ref[...]
        outs[0][...] = dx
        gw_ref = outs[1]

        @pl.when(pl.program_id(0) == 0)
        def _():
            gw_ref[...] = jnp.zeros_like(gw_ref)

        gw_ref[...] += dw
        if with_bf16:
            outs[2][...] = dx.astype(BF16)

    row = pl.BlockSpec((tm, D), lambda i: (i, 0))
    vec = pl.BlockSpec((1, D), lambda i: (0, 0))
    out_shape = [jax.ShapeDtypeStruct((L, D), F32), jax.ShapeDtypeStruct((1, D), F32)]
    out_specs = [row, vec]
    if with_bf16:
        out_shape.append(jax.ShapeDtypeStruct((L, D), BF16))
        out_specs.append(row)
    return pl.pallas_call(
        body, grid=(L // tm,),
        in_specs=[row, vec] + [row] * nparts + [row],
        out_specs=out_specs, out_shape=out_shape,
        compiler_params=_params(("arbitrary",)), name=name)(x, w, *dparts, dres)


def _final_loss(h2, wf, target, tm):
    L, D = h2.shape

    def body(h_ref, w_ref, t_ref, dh_ref, dhb_ref, loss_ref, gw_ref):
        t = t_ref[...]

        def f(h, w):
            err = jnp.square(_rms(h, w) - t)
            return 0.5 * jnp.sum(jnp.mean(err, axis=-1))

        val, vjp = jax.vjp(f, h_ref[...], w_ref[...])
        dh, dw = vjp(jnp.ones((), F32))
        dh_ref[...] = dh
        dhb_ref[...] = dh.astype(BF16)

        @pl.when(pl.program_id(0) == 0)
        def _():
            gw_ref[...] = jnp.zeros_like(gw_ref)
            loss_ref[...] = jnp.zeros_like(loss_ref)

        gw_ref[...] += dw
        loss_ref[...] += jnp.full(loss_ref.shape, val, F32)

    row = pl.BlockSpec((tm, D), lambda i: (i, 0))
    vec = pl.BlockSpec((1, D), lambda i: (0, 0))
    lspec = pl.BlockSpec((1, LANES), lambda i: (0, 0))
    return pl.pallas_call(
        body, grid=(L // tm,),
        in_specs=[row, vec, row],
        out_specs=[row, row, lspec, vec],
        out_shape=[jax.ShapeDtypeStruct((L, D), F32), jax.ShapeDtypeStruct((L, D), BF16),
                   jax.ShapeDtypeStruct((1, LANES), F32), jax.ShapeDtypeStruct((1, D), F32)],
        compiler_params=_params(("arbitrary",)), name="final_loss")(h2, wf, target)


def _gated(y, z, w):
    g = y * _silu(z)
    g = g * lax.rsqrt(jnp.mean(g * g, axis=-1, keepdims=True) + NORM_EPS)
    return g * w


def _gated_fwd(y, proj, w, DS, tm):
    L = y.shape[0]
    GW = DS // SSD_GROUPS

    def body(y_ref, z_ref, w_ref, o_ref):
        o_ref[...] = _gated(y_ref[...], z_ref[...], w_ref[...]).astype(BF16)

    blk = pl.BlockSpec((tm, GW), lambda i, g: (i, g))
    return pl.pallas_call(
        body, grid=(L // tm, SSD_GROUPS),
        in_specs=[blk, blk, pl.BlockSpec((1, GW), lambda i, g: (0, g))],
        out_specs=blk, out_shape=jax.ShapeDtypeStruct((L, DS), BF16),
        compiler_params=_params(("parallel", "parallel")), name="gated_fwd")(y, proj, w)


def _gated_bwd(y, proj, w, dout, DS, dproj_cols, tm):
    L = y.shape[0]
    GW = DS // SSD_GROUPS

    def body(y_ref, z_ref, w_ref, d_ref, dy_ref, dz_ref, gw_ref):
        _, vjp = jax.vjp(_gated, y_ref[...], z_ref[...], w_ref[...])
        dy, dz, dw = vjp(d_ref[...])
        dy_ref[...] = dy
        dz_ref[...] = dz.astype(BF16)

        @pl.when(pl.program_id(1) == 0)
        def _():
            gw_ref[...] = jnp.zeros_like(gw_ref)

        gw_ref[...] += dw

    blk = pl.BlockSpec((tm, GW), lambda g, i: (i, g))
    vec = pl.BlockSpec((1, GW), lambda g, i: (0, g))
    return pl.pallas_call(
        body, grid=(SSD_GROUPS, L // tm),
        in_specs=[blk, blk, vec, blk],
        out_specs=[blk, blk, vec],
        out_shape=[jax.ShapeDtypeStruct((L, DS), F32), jax.ShapeDtypeStruct((L, dproj_cols), BF16),
                   jax.ShapeDtypeStruct((1, DS), F32)],
        compiler_params=_params(("parallel", "arbitrary")), name="gated_bwd")(y, proj, w, dout)


def _halo_prev(tm, cw, col0):
    return pl.BlockSpec((HALO, cw), lambda i, j: (jnp.maximum(i * (tm // HALO) - 1, 0), col0 + j))


def _halo_next(tm, cw, col0, L):
    return pl.BlockSpec((HALO, cw), lambda i, j: (jnp.minimum((i + 1) * (tm // HALO), L // HALO - 1), col0 + j))


def _conv_fwd(proj, conv_w, conv_b, DS, DCONV, tm, cw):
    L = proj.shape[0]
    col0 = DS // cw
    K = CONV_WIDTH

    def body(x_ref, p_ref, w_ref, b_ref, o_ref, ext):
        i = pl.program_id(0)
        ext[0:HALO, :] = jnp.where(i == 0, 0.0, p_ref[...])
        ext[HALO:, :] = x_ref[...]
        acc = jnp.broadcast_to(b_ref[...], (tm, cw))
        for k in range(K):
            acc = acc + w_ref[k:k + 1, :] * ext[pl.ds(HALO - (K - 1) + k, tm), :]
        o_ref[...] = _silu(acc)

    return pl.pallas_call(
        body, grid=(L // tm, DCONV // cw),
        in_specs=[pl.BlockSpec((tm, cw), lambda i, j: (i, col0 + j)), _halo_prev(tm, cw, col0),
                  pl.BlockSpec((K, cw), lambda i, j: (0, j)), pl.BlockSpec((1, cw), lambda i, j: (0, j))],
        out_specs=pl.BlockSpec((tm, cw), lambda i, j: (i, j)),
        out_shape=jax.ShapeDtypeStruct((L, DCONV), F32),
        scratch_shapes=[pltpu.VMEM((tm + HALO, cw), F32)],
        compiler_params=_params(("parallel", "parallel")), name="conv_fwd")(proj, proj, conv_w, conv_b)


def _conv_bwd(name, proj, dact, conv_w, conv_b, dproj, DS, first, tm, cw):
    L = proj.shape[0]
    ncols = dact.shape[1]
    col0 = (DS + first) // cw
    wcol0 = first // cw
    K = CONV_WIDTH
    nrt = L // tm

    def body(x_ref, p_ref, n_ref, d_ref, dn_ref, w_ref, b_ref, alias_ref, dx_ref, dw_ref, db_ref, ext, dext):
        i = pl.program_id(1)
        last = i == nrt - 1
        ext[0:HALO, :] = jnp.where(i == 0, 0.0, p_ref[...])
        ext[HALO:HALO + tm, :] = x_ref[...]
        ext[HALO + tm:, :] = n_ref[...]
        dfull = jnp.concatenate([d_ref[...], jnp.where(last, 0.0, dn_ref[...])], axis=0)
        acc = jnp.broadcast_to(b_ref[...], (tm + HALO, cw))
        for k in range(K):
            acc = acc + w_ref[k:k + 1, :] * ext[pl.ds(HALO - (K - 1) + k, tm + HALO), :]
        sg = _sigmoid(acc)
        dconv = dfull * (sg * (1.0 + acc * (1.0 - sg)))
        dext[...] = dconv
        dx = jnp.zeros((tm, cw), F32)
        for k in range(K):
            dx = dx + w_ref[k:k + 1, :] * dext[pl.ds(K - 1 - k, tm), :]
        dx_ref[...] = dx.astype(BF16)

        @pl.when(i == 0)
        def _():
            dw_ref[...] = jnp.zeros_like(dw_ref)
            db_ref[...] = jnp.zeros_like(db_ref)

        dtile = dext[pl.ds(0, tm), :]
        db_ref[...] += jnp.sum(dtile, axis=0, keepdims=True)
        for k in range(K):
            dw_ref[k:k + 1, :] += jnp.sum(dtile * ext[pl.ds(HALO - (K - 1) + k, tm), :], axis=0, keepdims=True)

    prev = pl.BlockSpec((HALO, cw), lambda j, i: (jnp.maximum(i * (tm // HALO) - 1, 0), col0 + j))
    nxt = pl.BlockSpec((HALO, cw), lambda j, i: (jnp.minimum((i + 1) * (tm // HALO), L // HALO - 1), col0 + j))
    dnxt = pl.BlockSpec((HALO, cw), lambda j, i: (jnp.minimum((i + 1) * (tm // HALO), L // HALO - 1), j))
    return pl.pallas_call(
        body, grid=(ncols // cw, nrt),
        in_specs=[pl.BlockSpec((tm, cw), lambda j, i: (i, col0 + j)), prev, nxt,
                  pl.BlockSpec((tm, cw), lambda j, i: (i, j)), dnxt,
                  pl.BlockSpec((K, cw), lambda j, i: (0, wcol0 + j)), pl.BlockSpec((1, cw), lambda j, i: (0, wcol0 + j)),
                  _ANY],
        out_specs=[pl.BlockSpec((tm, cw), lambda j, i: (i, col0 + j)),
                   pl.BlockSpec((K, cw), lambda j, i: (0, j)), pl.BlockSpec((1, cw), lambda j, i: (0, j))],
        out_shape=[jax.ShapeDtypeStruct(dproj.shape, BF16), jax.ShapeDtypeStruct((K, ncols), F32),
                   jax.ShapeDtypeStruct((1, ncols), F32)],
        input_output_aliases={7: 0},
        scratch_shapes=[pltpu.VMEM((tm + 2 * HALO, cw), F32), pltpu.VMEM((tm + HALO, cw), F32)],
        compiler_params=_params(("parallel", "arbitrary")), name=name,
    )(proj, proj, proj, dact, dact, conv_w, conv_b, dproj)


def _pool_fwd(proj, pool_w, pool_scale, ucol, DP, tm):
    L = proj.shape[0]
    PG = len(POOL_WINDOWS)
    PGD = DP // PG
    col0 = ucol // PGD

    def body(u_ref, p_ref, w_ref, s_ref, pooled_ref, y_ref, ext):
        i, g = pl.program_id(0), pl.program_id(1)
        ext[0:HALO, :] = jnp.where(i == 0, 0.0, p_ref[...])
        ext[HALO:, :] = u_ref[...]
        t = i * tm + lax.broadcasted_iota(jnp.int32, (tm, 1), 0)
        for gi, win in enumerate(POOL_WINDOWS):
            @pl.when(g == gi)
            def _():
                acc = ext[pl.ds(HALO, tm), :]
                for j in range(1, win):
                    acc = acc + ext[pl.ds(HALO - j, tm), :]
                count = jnp.minimum(t + 1, win).astype(F32)
                pooled = (acc / count - u_ref[...]).astype(BF16)
                pooled_ref[...] = pooled
                y_ref[...] = (_dot(pooled, w_ref[...], NN) * s_ref[...]).astype(BF16)

    blk = pl.BlockSpec((tm, PGD), lambda i, g: (i, g))
    return pl.pallas_call(
        body, grid=(L // tm, PG),
        in_specs=[pl.BlockSpec((tm, PGD), lambda i, g: (i, col0 + g)), _halo_prev(tm, PGD, col0),
                  pl.BlockSpec((None, PGD, PGD), lambda i, g: (g, 0, 0)), pl.BlockSpec((1, PGD), lambda i, g: (0, g))],
        out_specs=[blk, blk],
        out_shape=[jax.ShapeDtypeStruct((L, DP), BF16), jax.ShapeDtypeStruct((L, DP), BF16)],
        scratch_shapes=[pltpu.VMEM((tm + HALO, PGD), F32)],
        compiler_params=_params(("parallel", "parallel")), name="pool_fwd")(proj, proj, pool_w, pool_scale)


def _pool_bwd(dy, pooled, pool_w, pool_scale, dproj, ucol, tm):
    L, DP = dy.shape
    PG = len(POOL_WINDOWS)
    PGD = DP // PG
    nrt = L // tm
    col0 = ucol // PGD

    def body(d_ref, dn_ref, p_ref, w_ref, s_ref, alias_ref, du_ref, dw_ref, ds_ref, qext):
        g, i = pl.program_id(0), pl.program_id(1)
        last = i == nrt - 1
        dfull = jnp.concatenate([d_ref[...], jnp.where(last, 0.0, dn_ref[...])], axis=0)
        dyp = (dfull * s_ref[...]).astype(BF16)
        dpooled = _dot(dyp, w_ref[...], NT)
        t = i * tm + lax.broadcasted_iota(jnp.int32, (tm + HALO, 1), 0)
        for gi, win in enumerate(POOL_WINDOWS):
            @pl.when(g == gi)
            def _():
                qext[...] = dpooled / jnp.minimum(t + 1, win).astype(F32)
                acc = qext[pl.ds(0, tm), :]
                for j in range(1, win):
                    acc = acc + qext[pl.ds(j, tm), :]
                du_ref[...] = (acc - dpooled[0:tm, :]).astype(BF16)

        @pl.when(i == 0)
        def _():
            dw_ref[...] = jnp.zeros_like(dw_ref)
            ds_ref[...] = jnp.zeros_like(ds_ref)

        pooled_t = p_ref[...]
        dw_ref[...] += _dot(pooled_t, dyp[0:tm, :], TN)
        ypre = _dot(pooled_t, w_ref[...], NN)
        ds_ref[...] += jnp.sum(d_ref[...] * ypre, axis=0, keepdims=True)

    blk = pl.BlockSpec((tm, PGD), lambda g, i: (i, g))
    nxt = pl.BlockSpec((HALO, PGD), lambda g, i: (jnp.minimum((i + 1) * (tm // HALO), L // HALO - 1), g))
    wspec = pl.BlockSpec((None, PGD, PGD), lambda g, i: (g, 0, 0))
    vec = pl.BlockSpec((1, PGD), lambda g, i: (0, g))
    return pl.pallas_call(
        body, grid=(PG, nrt),
        in_specs=[blk, nxt, blk, wspec, vec, _ANY],
        out_specs=[pl.BlockSpec((tm, PGD), lambda g, i: (i, col0 + g)), wspec, vec],
        out_shape=[jax.ShapeDtypeStruct(dproj.shape, BF16), jax.ShapeDtypeStruct((PG, PGD, PGD), F32),
                   jax.ShapeDtypeStruct((1, DP), F32)],
        input_output_aliases={5: 0},
        scratch_shapes=[pltpu.VMEM((tm + HALO, PGD), F32)],
        compiler_params=_params(("parallel", "arbitrary")), name="pool_bwd",
    )(dy, dy, pooled, pool_w, pool_scale, dproj)


def _ssd_common(dtc_raw, dtr_raw, bc, br, ac, ar):
    ch = CHUNK
    row = lax.broadcasted_iota(jnp.int32, (ch, ch), 0)
    col = lax.broadcasted_iota(jnp.int32, (ch, ch), 1)
    lower = row >= col
    dtc = _softplus(dtc_raw + bc)
    dtr = _softplus(dtr_raw + br)
    a_c = -jnp.exp(ac)
    a_r = -jnp.exp(ar)
    hi = lax.Precision.HIGHEST
    acol = jnp.dot(lower.astype(F32), dtc * a_c, preferred_element_type=F32, precision=hi)
    arow = jnp.dot(dtr * a_r, (row <= col).astype(F32), preferred_element_type=F32, precision=hi)
    return lower, row <= col, dtc, a_c, acol, arow


def _ssd_fwd(xbc, dtc_raw, dtr_raw, bias_c, bias_r, alog_c, alog_r, dskip_c, DS, N):
    L = xbc.shape[0]
    G, P, ch = SSD_GROUPS, HEAD_DIM, CHUNK
    R = dtc_raw.shape[2]
    GW = R * P
    nc = L // ch

    def body(xs_ref, b_ref, c_ref, dtc_ref, dtr_ref, bc_ref, br_ref, ac_ref, ar_ref, dk_ref,
             y_ref, st_ref, h_ref):
        @pl.when(pl.program_id(1) == 0)
        def _():
            h_ref[...] = jnp.zeros_like(h_ref)

        lower, _, dtc, _, acol_all, arow_all = _ssd_common(
            dtc_ref[...], dtr_ref[...], bc_ref[...], br_ref[...], ac_ref[...], ar_ref[...])
        bm = b_ref[...]
        cb16 = c_ref[...].astype(BF16)
        b16 = bm.astype(BF16)
        bt16 = bm.T.astype(BF16)
        cb = _dot(cb16, b16, NT)
        dk = dk_ref[...]
        st_ref[...] = h_ref[...]
        for r in range(R):
            acol = acol_all[:, r:r + 1]
            arow = arow_all[r:r + 1, :]
            alast = acol_all[ch - 1:ch, r:r + 1]
            decay = jnp.exp(jnp.where(lower, acol - arow, -1e30))
            x_h = xs_ref[:, pl.ds(r * P, P)]
            xdt = x_h * dtc[:, r:r + 1]
            m16 = (cb * decay).astype(BF16)
            h_prev = h_ref[r]
            y = _dot(m16, xdt.astype(BF16), NN)
            y = y + jnp.exp(acol) * _dot(cb16, h_prev.astype(BF16), NN)
            y = y + dk[:, r:r + 1] * x_h
            y_ref[:, pl.ds(r * P, P)] = y
            to_end = jnp.exp(alast - acol)
            h_ref[r] = jnp.exp(alast) * h_prev + _dot(bt16, (xdt * to_end).astype(BF16), NN)

    nb = DS // N
    return pl.pallas_call(
        body, grid=(G, nc),
        in_specs=[pl.BlockSpec((ch, GW), lambda g, c: (c, g)),
                  pl.BlockSpec((ch, N), lambda g, c: (c, nb + g)),
                  pl.BlockSpec((ch, N), lambda g, c: (c, nb + G + g)),
                  pl.BlockSpec((None, ch, R), lambda g, c: (g, c, 0)),
                  pl.BlockSpec((None, R, ch), lambda g, c: (g, 0, c)),
                  pl.BlockSpec((None, 1, R), lambda g, c: (g, 0, 0)),
                  pl.BlockSpec((None, R, 1), lambda g, c: (g, 0, 0)),
                  pl.BlockSpec((None, 1, R), lambda g, c: (g, 0, 0)),
                  pl.BlockSpec((None, R, 1), lambda g, c: (g, 0, 0)),
                  pl.BlockSpec((None, 1, R), lambda g, c: (g, 0, 0))],
        out_specs=[pl.BlockSpec((ch, GW), lambda g, c: (c, g)),
                   pl.BlockSpec((None, R, N, P), lambda g, c: (c, g, 0, 0))],
        out_shape=[jax.ShapeDtypeStruct((L, DS), F32), jax.ShapeDtypeStruct((nc, G * R, N, P), F32)],
        scratch_shapes=[pltpu.VMEM((R, N, P), F32)],
        compiler_params=_params(("parallel", "arbitrary")), name="ssd_fwd",
    )(xbc, xbc, xbc, dtc_raw, dtr_raw, bias_c, bias_r, alog_c, alog_r, dskip_c)


def _ssd_bwd(xbc, dtc_raw, dtr_raw, bias_c, bias_r, alog_c, alog_r, dskip_c, dy, states, DS, N):
    L = xbc.shape[0]
    G, P, ch = SSD_GROUPS, HEAD_DIM, CHUNK
    R = dtc_raw.shape[2]
    GW = R * P
    nc = L // ch

    def body(xs_ref, b_ref, c_ref, dtc_ref, dtr_ref, bc_ref, br_ref, ac_ref, ar_ref, dk_ref,
             dy_ref, stp_ref,
             dxs_ref, db_ref, dc_ref, ddt_ref, dal_ref, ddk_ref, dbias_ref, dh_ref):
        @pl.when(pl.program_id(1) == 0)
        def _():
            dh_ref[...] = jnp.zeros_like(dh_ref)
            dal_ref[...] = jnp.zeros_like(dal_ref)
            ddk_ref[...] = jnp.zeros_like(ddk_ref)
            dbias_ref[...] = jnp.zeros_like(dbias_ref)

        lower, upper, dtc, a_c, acol_all, arow_all = _ssd_common(
            dtc_ref[...], dtr_ref[...], bc_ref[...], br_ref[...], ac_ref[...], ar_ref[...])
        bm = b_ref[...]
        cm = c_ref[...]
        b16 = bm.astype(BF16)
        c16 = cm.astype(BF16)
        ct16 = cm.T.astype(BF16)
        cb = _dot(c16, b16, NT)
        cbt = _dot(b16, c16, NT)
        dk = dk_ref[...]
        lane_r = lax.broadcasted_iota(jnp.int32, (ch, R), 1)
        lane_1 = lax.broadcasted_iota(jnp.int32, (1, R), 1)
        dc = jnp.zeros((ch, N), F32)
        db = jnp.zeros((ch, N), F32)
        da_all = jnp.zeros((R, ch), F32)
        q_all = jnp.zeros((R, ch), F32)
        sxd_all = jnp.zeros((ch, R), F32)
        const = jnp.zeros((1, R), F32)
        ddk = jnp.zeros((1, R), F32)
        sub_r = lax.broadcasted_iota(jnp.int32, (R, ch), 0)
        ct = cm.T
        bt = bm.T
        dcb = jnp.zeros((ch, ch), F32)
        for r in range(R):
            acol = acol_all[:, r:r + 1]
            arow = arow_all[r:r + 1, :]
            alast = acol_all[ch - 1:ch, r:r + 1]
            seg = acol - arow
            decay = jnp.exp(jnp.where(lower, seg, -1e30))
            decay_t = jnp.exp(jnp.where(upper, -seg, -1e30))
            x_h = xs_ref[:, pl.ds(r * P, P)]
            dy_h = dy_ref[:, pl.ds(r * P, P)]
            dt_h = dtc[:, r:r + 1]
            dk_h = dk[:, r:r + 1]
            xdt = x_h * dt_h
            xdt16 = xdt.astype(BF16)
            dy16 = dy_h.astype(BF16)
            h_prev = stp_ref[r]
            h16 = h_prev.astype(BF16)
            dh_next = dh_ref[r]
            dhn16 = dh_next.astype(BF16)
            to_end = jnp.exp(alast - acol)
            e_a = jnp.exp(acol)
            mt = cbt * decay_t
            pm = _dot(dy16, xdt16, NT) * decay
            wt = _dot(xdt16, dy16, NT) * mt
            dxdt = _dot(mt.astype(BF16), dy16, NN) + to_end * _dot(b16, dhn16, NN)
            dcb = dcb + pm
            dc = dc + e_a * _dot(dy16, h16, NT)
            db = db + to_end * _dot(xdt16, dhn16, NT)
            dh_ref[r] = jnp.exp(alast) * dh_next + _dot(ct16, (dy_h * e_a).astype(BF16), NN)
            da = (jnp.sum(wt, axis=0, keepdims=True) - jnp.sum(pm * cb, axis=0, keepdims=True)
                  + jnp.exp(arow) * jnp.sum(ct * _dot(h16, dy16, NT), axis=0, keepdims=True))
            q = jnp.exp(alast - arow) * jnp.sum(bt * _dot(dhn16, xdt16, NT), axis=0, keepdims=True)
            da_all = da_all + jnp.where(sub_r == r, da, 0.0)
            q_all = q_all + jnp.where(sub_r == r, q, 0.0)
            sxd_all = sxd_all + jnp.where(lane_r == r, jnp.sum(dxdt * x_h, axis=1, keepdims=True), 0.0)
            const = const + jnp.where(lane_1 == r, jnp.exp(alast) * jnp.sum(dh_next * h_prev), 0.0)
            ddk = ddk + jnp.where(lane_1 == r, jnp.sum(dy_h * x_h), 0.0)
            dxs_ref[:, pl.ds(r * P, P)] = dxdt * dt_h + dk_h * dy_h
        dcb16 = dcb.astype(BF16)
        dc_ref[...] = dc + _dot(dcb16, b16, NN)
        db_ref[...] = db + _dot(dcb16, c16, TN)
        hi = lax.Precision.HIGHEST
        strict_lower = jnp.logical_and(lower, jnp.logical_not(upper))
        dda = (lax.dot_general(upper.astype(F32), da_all, NT, preferred_element_type=F32, precision=hi)
               + lax.dot_general(strict_lower.astype(F32), q_all, NT, preferred_element_type=F32, precision=hi)
               + const)
        ddt = dda * a_c + sxd_all
        dal_ref[...] += jnp.sum(dda * dtc, axis=0, keepdims=True) * a_c
        ddk_ref[...] += ddk
        ddt_raw = ddt * _sigmoid(dtc_ref[...] + bc_ref[...])
        ddt_ref[...] = ddt_raw
        dbias_ref[...] += jnp.sum(ddt_raw, axis=0, keepdims=True)

    nb = DS // N
    rc = lambda c: nc - 1 - c
    vec_c = pl.BlockSpec((None, 1, R), lambda g, c: (g, 0, 0))
    vec_r = pl.BlockSpec((None, R, 1), lambda g, c: (g, 0, 0))
    big = pl.BlockSpec((ch, GW), lambda g, c: (rc(c), g))
    return pl.pallas_call(
        body, grid=(G, nc),
        in_specs=[big,
                  pl.BlockSpec((ch, N), lambda g, c: (rc(c), nb + g)),
                  pl.BlockSpec((ch, N), lambda g, c: (rc(c), nb + G + g)),
                  pl.BlockSpec((None, ch, R), lambda g, c: (g, rc(c), 0)),
                  pl.BlockSpec((None, R, ch), lambda g, c: (g, 0, rc(c))),
                  vec_c, vec_r, vec_c, vec_r, vec_c,
                  big,
                  pl.BlockSpec((None, R, N, P), lambda g, c: (rc(c), g, 0, 0))],
        out_specs=[big,
                   pl.BlockSpec((ch, N), lambda g, c: (rc(c), g)),
                   pl.BlockSpec((ch, N), lambda g, c: (rc(c), g)),
                   pl.BlockSpec((None, ch, R), lambda g, c: (g, rc(c), 0)),
                   vec_c, vec_c, vec_c],
        out_shape=[jax.ShapeDtypeStruct((L, DS), F32), jax.ShapeDtypeStruct((L, G * N), F32),
                   jax.ShapeDtypeStruct((L, G * N), F32), jax.ShapeDtypeStruct((G, L, R), F32),
                   jax.ShapeDtypeStruct((G, 1, R), F32), jax.ShapeDtypeStruct((G, 1, R), F32),
                   jax.ShapeDtypeStruct((G, 1, R), F32)],
        scratch_shapes=[pltpu.VMEM((R, N, P), F32)],
        compiler_params=_params(("parallel", "arbitrary")), name="ssd_bwd",
    )(xbc, xbc, xbc, dtc_raw, dtr_raw, bias_c, bias_r, alog_c, alog_r, dskip_c, dy, states)


def _adam_math(w, g, m, v):
    m = ADAM_B1 * m + (1.0 - ADAM_B1) * g
    v = ADAM_B2 * v + (1.0 - ADAM_B2) * jnp.square(g)
    m_hat = m / (1.0 - ADAM_B1 ** ADAM_STEP)
    v_hat = v / (1.0 - ADAM_B2 ** ADAM_STEP)
    delta = -ADAM_LR * (m_hat / (jnp.sqrt(v_hat) + ADAM_EPS) + ADAM_WD * w)
    return delta, m, v


def _adam(name, w, g, m, v):
    rows, cols = w.shape
    tr = _tile(rows, max(8, (1 << 18) // cols // 8 * 8), 8)

    def body(w_ref, g_ref, m_ref, v_ref, d_ref, mo_ref, vo_ref):
        d, m2, v2 = _adam_math(w_ref[...], g_ref[...], m_ref[...], v_ref[...])
        d_ref[...] = d
        mo_ref[...] = m2
        vo_ref[...] = v2

    blk = pl.BlockSpec((tr, cols), lambda i: (i, 0))
    return pl.pallas_call(
        body, grid=(rows // tr,), in_specs=[blk] * 4, out_specs=[blk] * 3,
        out_shape=[jax.ShapeDtypeStruct((rows, cols), F32)] * 3,
        compiler_params=_params(("parallel",)), name=name)(w, g, m, v)


def _small_sum_adam(gathered, w, m, v, rows):
    def body(ga_ref, w_ref, m_ref, v_ref, g_ref, d_ref, mo_ref, vo_ref):
        g = ga_ref[0:rows, :]
        for d in range(1, N_DEV):
            g = g + ga_ref[d * rows:(d + 1) * rows, :]
        g_ref[...] = g
        dl, m2, v2 = _adam_math(w_ref[...], g, m_ref[...], v_ref[...])
        d_ref[...] = dl
        mo_ref[...] = m2
        vo_ref[...] = v2

    return pl.pallas_call(
        body, out_shape=[jax.ShapeDtypeStruct((rows, LANES), F32)] * 4,
        compiler_params=pltpu.CompilerParams(vmem_limit_bytes=VMEM_LIMIT), name="small_sum_adam",
    )(gathered, w, m, v)


def _row_tile(rh, cols):
    return _tile(rh, max(16, (1 << 19) // cols // 16 * 16), 16)


def _shard_dims(g):
    return (g.shape[1], g.shape[2]) if g.ndim == 3 else (g.shape[0], g.shape[1] // N_CHIPS)


def _pair_sum(name, g, recv, pos):
    r, c = _shard_dims(g)
    rh = r // 2
    tr = _row_tile(rh, c)
    nrt = rh // tr

    def body(pos_ref, a_ref, b_ref, o_ref):
        o_ref[...] = (a_ref[...] + b_ref[...].astype(F32)).astype(BF16)

    own = (pl.BlockSpec((None, tr, c), lambda k, i, p: (k, p[1] * nrt + i, 0)) if g.ndim == 3
           else pl.BlockSpec((tr, c), lambda k, i, p: (p[1] * nrt + i, k)))
    return pl.pallas_call(
        body,
        grid_spec=pltpu.PrefetchScalarGridSpec(
            num_scalar_prefetch=1, grid=(N_CHIPS, nrt),
            in_specs=[own,
                      pl.BlockSpec((None, tr, c), lambda k, i, p: (k, i, 0))],
            out_specs=pl.BlockSpec((None, tr, c), lambda k, i, p: (k, i, 0))),
        out_shape=jax.ShapeDtypeStruct((N_CHIPS, rh, c), BF16),
        compiler_params=_params(("parallel", "parallel")), name=name)(pos, g, recv)


def _chip_sum(name, parts, pos):
    _, rh, c = parts.shape
    tr = _row_tile(rh, c)
    nrt = rh // tr

    def body(pos_ref, p_ref, o_ref):
        s = p_ref[0].astype(F32)
        for k in range(1, N_CHIPS):
            s = s + p_ref[k].astype(F32)
        o_ref[...] = s

    return pl.pallas_call(
        body,
        grid_spec=pltpu.PrefetchScalarGridSpec(
            num_scalar_prefetch=1, grid=(nrt,),
            in_specs=[pl.BlockSpec((N_CHIPS, tr, c), lambda i, p: (0, i, 0))],
            out_specs=pl.BlockSpec((tr, c), lambda i, p: (p[1] * nrt + i, 0))),
        out_shape=jax.ShapeDtypeStruct((2 * rh, c), F32),
        compiler_params=_params(("parallel",)), name=name)(pos, parts)


_HBM = pl.BlockSpec(memory_space=pltpu.HBM)


def _chip_xy(k):
    return k // 2, k % 2


def _half_rows(ref, hc, rh):
    return ref.at[pl.ds(pl.multiple_of(hc * rh, 16), rh), :]


_SEM = pl.BlockSpec(memory_space=pltpu.SEMAPHORE)
_ANY = pl.BlockSpec(memory_space=pl.ANY)
_SPLIT = pltpu.CompilerParams(has_side_effects=pltpu.SideEffectType.DATAFLOW_SIDE_EFFECTING)


def _in_hbm(a):
    return pltpu.with_memory_space_constraint(a, pltpu.HBM)


def _push_start(name, srcs, land_shapes, copies_of):
    n = len(srcs)

    def body(*refs):
        s_refs, l_refs = refs[:n], refs[n:2 * n]
        send_sems, recv_sems = refs[2 * n], refs[2 * n + 1]
        token = refs[-1]
        x, y, c = lax.axis_index("x"), lax.axis_index("y"), lax.axis_index("c")
        me = 2 * x + y
        for i in range(n):
            for k in range(N_CHIPS):
                @pl.when(k != me)
                def _():
                    src, dst, dev = copies_of(i, k, s_refs[i], l_refs[i], me, x, y, c)
                    pltpu.make_async_remote_copy(
                        src_ref=src, dst_ref=dst, send_sem=send_sems.at[N_CHIPS * i + k],
                        recv_sem=recv_sems.at[N_CHIPS * i + me], device_id=dev, device_id_type=MESH).start()
        token[...] = jnp.zeros_like(token)

    lands = [lax.empty(s, d) for s, d in land_shapes]
    outs = pl.pallas_call(
        body, name=name,
        out_shape=[pltpu.SemaphoreType.DMA((N_CHIPS * n,)), pltpu.SemaphoreType.DMA((N_CHIPS * n,))]
        + [pltpu.HBM(s.shape, s.dtype) for s in srcs] + [pltpu.HBM(s, d) for s, d in land_shapes]
        + [jax.ShapeDtypeStruct((8, LANES), F32)],
        in_specs=[_HBM] * (2 * n),
        out_specs=[_SEM, _SEM] + [_HBM] * (2 * n) + [pl.BlockSpec(memory_space=pltpu.VMEM)],
        input_output_aliases={j: 2 + j for j in range(2 * n)},
        compiler_params=_SPLIT,
    )(*[_in_hbm(s) for s in srcs], *[_in_hbm(l) for l in lands])
    return outs[0], outs[1], outs[2:2 + n], outs[2 + n:2 + 2 * n], outs[-1]


def _push_wait(name, started, after, landed_of):
    send_sems, recv_sems, srcs, lands, _ = started
    n = len(srcs)

    def body(*refs):
        s_refs, l_refs = refs[:n], refs[n:2 * n]
        send, recv = refs[2 * n], refs[2 * n + 1]
        token = refs[-1]
        token[...] = jnp.zeros_like(token)
        x, y, c = lax.axis_index("x"), lax.axis_index("y"), lax.axis_index("c")
        me = 2 * x + y
        for i in range(n):
            for k in range(N_CHIPS):
                @pl.when(k != me)
                def _():
                    src, dst = landed_of(i, k, s_refs[i], l_refs[i], me, c)
                    cp = pltpu.make_async_remote_copy(
                        src_ref=src, dst_ref=dst, send_sem=send.at[N_CHIPS * i + k], recv_sem=recv.at[N_CHIPS * i + k],
                        device_id=(x, y, c), device_id_type=MESH)
                    cp.wait_send()
                    cp.wait_recv()

    outs = pl.pallas_call(
        body, name=name,
        out_shape=[pltpu.HBM(s.shape, s.dtype) for s in srcs] + [pltpu.HBM(l.shape, l.dtype) for l in lands]
        + [jax.ShapeDtypeStruct((8, LANES), F32)],
        in_specs=[_HBM] * (2 * n) + [_SEM, _SEM, _ANY],
        out_specs=[_HBM] * (2 * n) + [pl.BlockSpec(memory_space=pltpu.VMEM)],
        input_output_aliases={j: j for j in range(2 * n)},
        compiler_params=_SPLIT,
    )(*srcs, *lands, send_sems, recv_sems, after)
    return outs[:n], outs[n:2 * n], outs[-1]


def _gather_start(name, shards):
    def copies_of(i, k, src, land, me, x, y, c):
        rh = shards[i].shape[0] // 2
        kx, ky = _chip_xy(k)
        return _half_rows(src, c, rh), _half_rows(land.at[me], c, rh), (kx, ky, c)

    return _push_start(name, shards, [((N_CHIPS,) + s.shape, s.dtype) for s in shards], copies_of)


def _gather_wait(name, started, after):
    shapes = [s.shape for s in started[2]]

    def landed_of(i, k, src, land, me, c):
        rh = shapes[i][0] // 2
        return _half_rows(src, c, rh), _half_rows(land.at[k], c, rh)

    return _push_wait(name, started, after, landed_of)


def _forward_halves(name, bufs):
    n = len(bufs)

    def body(*refs):
        i_refs, o_refs, send_sems, recv_sems = refs[:n], refs[n:2 * n], refs[2 * n], refs[2 * n + 1]
        x, y, c = lax.axis_index("x"), lax.axis_index("y"), lax.axis_index("c")
        me = 2 * x + y

        def fwd(i, k, hc):
            rh = bufs[i].shape[1] // 2
            return pltpu.make_async_remote_copy(
                src_ref=_half_rows(i_refs[i].at[k], hc, rh), dst_ref=_half_rows(o_refs[i].at[k], hc, rh),
                send_sem=send_sems.at[i, k], recv_sem=recv_sems.at[i, k],
                device_id=(x, y, 1 - c), device_id_type=MESH)

        for i in range(n):
            for k in range(N_CHIPS):
                @pl.when(k != me)
                def _():
                    fwd(i, k, c).start()
        for i in range(n):
            for k in range(N_CHIPS):
                @pl.when(k != me)
                def _():
                    fwd(i, k, 1 - c).wait_recv()
        for i in range(n):
            for k in range(N_CHIPS):
                @pl.when(k != me)
                def _():
                    fwd(i, k, c).wait_send()

    return pl.pallas_call(
        body, in_specs=[_HBM] * n, out_specs=[_HBM] * n,
        out_shape=[jax.ShapeDtypeStruct(b.shape, b.dtype) for b in bufs],
        input_output_aliases={i: i for i in range(n)},
        scratch_shapes=[pltpu.SemaphoreType.DMA((n, N_CHIPS))] * 2,
        name=name)(*bufs)


def _swap_halves(name, grads):
    n = len(grads)
    dims = [_shard_dims(g) for g in grads]

    def body(*refs):
        g_refs, o_refs, send_sems, recv_sems = refs[:n], refs[n:2 * n], refs[2 * n], refs[2 * n + 1]
        x, y, c = lax.axis_index("x"), lax.axis_index("y"), lax.axis_index("c")
        copies = []
        for i in range(n):
            r, cw = dims[i]
            for k in range(N_CHIPS):
                shard = g_refs[i].at[k] if grads[i].ndim == 3 else g_refs[i].at[:, pl.ds(k * cw, cw)]
                copies.append(pltpu.make_async_remote_copy(
                    src_ref=_half_rows(shard, 1 - c, r // 2), dst_ref=o_refs[i].at[k],
                    send_sem=send_sems.at[i, k], recv_sem=recv_sems.at[i, k],
                    device_id=(x, y, 1 - c), device_id_type=MESH))
        for cp in copies:
            cp.start()
        for cp in copies:
            cp.wait()

    return pl.pallas_call(
        body, in_specs=[_HBM] * n, out_specs=[_HBM] * n,
        out_shape=[jax.ShapeDtypeStruct((N_CHIPS, r // 2, cw), g.dtype) for (r, cw), g in zip(dims, grads)],
        scratch_shapes=[pltpu.SemaphoreType.DMA((n, N_CHIPS))] * 2,
        name=name)(*grads)


def _scatter_start(name, parts):
    def copies_of(i, k, src, land, me, x, y, c):
        kx, ky = _chip_xy(k)
        return src.at[k], land.at[me], (kx, ky, c)

    return _push_start(name, parts, [(p.shape, p.dtype) for p in parts], copies_of)


def _scatter_wait(name, started, after):
    return _push_wait(name, started, after, lambda i, k, src, land, me, c: (src.at[k], land.at[k]))


def _join_halves(bufs):
    n = len(bufs)

    def body(*refs):
        i_refs, o_refs, send_sems, recv_sems = refs[:n], refs[n:2 * n], refs[2 * n], refs[2 * n + 1]
        x, y, c = lax.axis_index("x"), lax.axis_index("y"), lax.axis_index("c")
        copies = []
        for i in range(n):
            rh = bufs[i].shape[0] // 2
            copies.append(pltpu.make_async_remote_copy(
                src_ref=_half_rows(i_refs[i], c, rh), dst_ref=_half_rows(o_refs[i], c, rh),
                send_sem=send_sems.at[i], recv_sem=recv_sems.at[i],
                device_id=(x, y, 1 - c), device_id_type=MESH))
        for cp in copies:
            cp.start()
        for i in range(n):
            rh = bufs[i].shape[0] // 2
            pltpu.make_async_remote_copy(
                src_ref=_half_rows(i_refs[i], c, rh), dst_ref=_half_rows(o_refs[i], 1 - c, rh),
                send_sem=send_sems.at[i], recv_sem=recv_sems.at[i],
                device_id=(x, y, 1 - c), device_id_type=MESH).wait_recv()
        for cp in copies:
            cp.wait_send()

    return pl.pallas_call(
        body, in_specs=[_HBM] * n, out_specs=[_HBM] * n,
        out_shape=[jax.ShapeDtypeStruct(b.shape, F32) for b in bufs],
        input_output_aliases={i: i for i in range(n)},
        scratch_shapes=[pltpu.SemaphoreType.DMA((n,))] * 2,
        name="join_halves")(*bufs)


def _all_gather_small(name, blk):
    m_per, n = blk.shape

    def body(x_ref, out_ref, send_sems, recv_sems, local_sem):
        x, y, c = lax.axis_index("x"), lax.axis_index("y"), lax.axis_index("c")
        me, sibling = (x, y, c), (x, y, 1 - c)
        chips = [(1 - x, y), (x, 1 - y), (1 - x, 1 - y)]

        def rows(px, py, pc):
            return out_ref.at[pl.ds((4 * px + 2 * py + pc) * m_per, m_per), :]

        def copy(k, block, to, src=None):
            return pltpu.make_async_remote_copy(
                src_ref=rows(*block) if src is None else src, dst_ref=rows(*block),
                send_sem=send_sems.at[k], recv_sem=recv_sems.at[k],
                device_id=to, device_id_type=MESH)

        mine = pltpu.make_async_copy(x_ref, rows(*me), local_sem)
        mine.start()
        first = [copy(0, me, sibling, src=x_ref)]
        first += [copy(1 + j, me, (*chip, c), src=x_ref) for j, chip in enumerate(chips)]
        for cp in first:
            cp.start()
        passed = [copy(4 + j, (*chip, c), sibling) for j, chip in enumerate(chips)]
        for j, chip in enumerate(chips):
            copy(1 + j, (*chip, c), me).wait_recv()
            passed[j].start()
        copy(0, sibling, me).wait_recv()
        for j, chip in enumerate(chips):
            copy(4 + j, (*chip, 1 - c), me).wait_recv()
        for cp in first + passed:
            cp.wait_send()
        mine.wait()

    return pl.pallas_call(
        body, out_shape=jax.ShapeDtypeStruct((N_DEV * m_per, n), blk.dtype),
        in_specs=[pl.BlockSpec(memory_space=pltpu.VMEM)],
        out_specs=pl.BlockSpec(memory_space=pltpu.VMEM),
        scratch_shapes=[pltpu.SemaphoreType.DMA((7,)), pltpu.SemaphoreType.DMA((7,)), pltpu.SemaphoreType.DMA],
        name=name)(blk)


def _pack_rows(vecs, width):
    parts = []
    for v in vecs:
        f = v.reshape(-1)
        pad = (-f.shape[0]) % (8 * width)
        parts.append(jnp.pad(f, (0, pad)) if pad else f)
    return jnp.concatenate(parts).reshape(-1, width)


def _unpack_rows(packed, shapes, width):
    flat = packed.reshape(-1)
    out, off = [], 0
    for s in shapes:
        n = math.prod(s)
        out.append(flat[off:off + n].reshape(s))
        off += n + ((-n) % (8 * width))
    return out


class _WinPlan:
    def __init__(self, ncol, dt0, h, dmain):
        self.ncol, self.h = ncol, h
        self.dt_shard = dt0 // ncol
        assert (dt0 + h - 1) // ncol == self.dt_shard and dmain % LANES == 0
        self.dt_local = dt0 - self.dt_shard * ncol
        to_main = lambda g: g if g <= dt0 else g - h
        self.lo = [to_main(ncol * k) for k in range(N_CHIPS)]
        self.hi = [to_main(ncol * (k + 1)) for k in range(N_CHIPS)]
        down = lambda v: v // LANES * LANES
        self.ww = max(-(-(hi - down(lo)) // LANES) * LANES for lo, hi in zip(self.lo, self.hi))
        self.ws = [min(down(lo), dmain - self.ww) for lo in self.lo]
        self.dmain = dmain

    def to_window(self, k, shard):
        if k == self.dt_shard:
            shard = jnp.concatenate([shard[:, :self.dt_local], shard[:, self.dt_local + self.h:]], axis=1)
        left = self.lo[k] - self.ws[k]
        return jnp.pad(shard, ((0, 0), (left, self.ww - left - shard.shape[1])))

    def from_window(self, k, window, dt_cols):
        left = self.lo[k] - self.ws[k]
        body = window[:, left:left + self.hi[k] - self.lo[k]]
        if k == self.dt_shard:
            body = jnp.concatenate([body[:, :self.dt_local], dt_cols, body[:, self.dt_local:]], axis=1)
        return body

    def merge(self, windows):
        cuts = sorted({0, self.dmain} | set(self.ws) | {w + self.ww for w in self.ws})
        segs = []
        for a, b in zip(cuts[:-1], cuts[1:]):
            parts = [windows[k][:, a - self.ws[k]:b - self.ws[k]] for k in range(N_CHIPS)
                     if self.ws[k] <= a and b <= self.ws[k] + self.ww]
            segs.append(functools.reduce(jnp.add, parts))
        return jnp.concatenate(segs, axis=1)

    def split(self, g_main):
        return jnp.stack([g_main[:, w:w + self.ww] for w in self.ws])


def kernel(x, attn_norm_w, w_in, conv_w, conv_b, dt_bias, a_log, d_skip, ssd_norm_w, pool_w, pool_scale, w_out, ffn_norm_w, w_gate, w_up, w_down, final_norm_w, loss_target, m_attn_norm_w, m_w_in, m_conv_w, m_conv_b, m_dt_bias, m_a_log, m_d_skip, m_ssd_norm_w, m_pool_w, m_pool_scale, m_w_out, m_ffn_norm_w, m_w_gate, m_w_up, m_w_down, m_final_norm_w, v_attn_norm_w, v_w_in, v_conv_w, v_conv_b, v_dt_bias, v_a_log, v_d_skip, v_ssd_norm_w, v_pool_w, v_pool_scale, v_w_out, v_ffn_norm_w, v_w_gate, v_w_up, v_w_down, v_final_norm_w):
    G, P, PG = SSD_GROUPS, HEAD_DIM, len(POOL_WINDOWS)
    _, L, D = x.shape
    H = a_log.shape[1]
    R = H // G
    DS = H * P
    DCONV = conv_b.shape[1]
    N = (DCONV - DS) // (2 * G)
    DP = pool_scale.shape[1]
    PGD = DP // PG
    DIN = N_CHIPS * w_in.shape[2]
    DFF = N_CHIPS * w_gate.shape[2]
    DMAIN = DS + DCONV + DP
    assert DIN == DMAIN + H and DS == DP and H <= LANES

    cx, cy, cc = lax.axis_index("x"), lax.axis_index("y"), lax.axis_index("c")
    chip = 2 * cx + cy

    win = _WinPlan(DIN // N_CHIPS, DS + DCONV, H, DMAIN)
    my_window = lax.switch(chip, [functools.partial(win.to_window, k) for k in range(N_CHIPS)], w_in[0].astype(BF16))
    started_in = _gather_start("gather_start_in", [my_window])

    def forward_gathered(tag, shards, landed):
        landed = _forward_halves("gather_forward_" + tag, landed)
        return [lax.dynamic_update_slice(g, s[None], (chip, 0, 0)) for g, s in zip(landed, shards)]

    def cols(p):
        return jnp.moveaxis(p, 0, -2).reshape(p.shape[1:-1] + (N_CHIPS * p.shape[-1],))

    ncw = CONV_WIDTH * DCONV // N_CHIPS
    dt_here = jnp.where(chip == win.dt_shard, w_in[0][:, win.dt_local:win.dt_local + H], 0.0)
    start_blk = _pack_rows([conv_w[0], dt_here], LANES)
    start_all = _all_gather_small("gather_conv_w", start_blk).reshape(N_CHIPS, 2, -1)[:, 0]
    conv_w_f = cols(start_all[:, :ncw].reshape(N_CHIPS, CONV_WIDTH, DCONV // N_CHIPS))
    dt_off = ncw + (-ncw) % (8 * LANES)
    w_dt = jnp.pad(start_all[win.dt_shard, dt_off:dt_off + D * H].reshape(D, H), ((0, 0), (0, LANES - H))).astype(BF16)

    xl, tgt = x[0], loss_target[0]
    tm_row = _tile(L, 256, HALO)
    tm_mm = _tile(L, 1024, 16)
    hn1 = _rms_fwd("rms1_fwd", xl, attn_norm_w, tm_row)
    shards_in, landed_in, landed_token = _gather_wait("gather_wait_in", started_in, hn1)
    shards_rest = [(pool_w[0].reshape(PG * PGD // N_CHIPS, PGD) + landed_token[0, 0]).astype(BF16),
                   w_out[0].astype(BF16), w_gate[0].astype(BF16), w_up[0].astype(BF16), w_down[0].astype(BF16)]
    started_rest = _gather_start("gather_start_rest", shards_rest)
    pin_row = lambda started, n: jnp.zeros((1, n), F32) + started[4][0, 0]
    add_row = lambda accs, ex, rex: [accs[0] + rex[0]]
    w_main = win.merge(forward_gathered("in", shards_in, landed_in)[0])
    proj, = _mm("proj_main", "nn", [(hn1, w_main)], L, DMAIN, D, tm_mm, 512, D, [F32],
                epilogue=add_row, row_extras=[pin_row(started_rest, DMAIN)])
    dt_raw, = _mm("proj_dt", "nn", [(hn1, w_dt)], L, LANES, D, tm_mm, LANES, D, [F32])

    cwid = _tile(math.gcd(DS, DCONV), 512, LANES)
    tm_conv = _tile(L, 1024, HALO)
    xbc = _conv_fwd(proj, conv_w_f, conv_b, DS, DCONV, tm_conv, cwid)

    dt_g = dt_raw[:, :H].reshape(L, G, R)
    dtc_raw = jnp.transpose(dt_g, (1, 0, 2))
    dtr_raw = jnp.transpose(dt_g, (1, 2, 0))
    as_c = lambda v: v.reshape(G, 1, R)
    as_r = lambda v: v.reshape(G, R, 1)
    ssd_args = (xbc, dtc_raw, dtr_raw, as_c(dt_bias), as_r(dt_bias), as_c(a_log), as_r(a_log), as_c(d_skip))
    y_ssd_raw, states = _ssd_fwd(*ssd_args, DS, N)
    y_ssd = _gated_fwd(y_ssd_raw, proj, ssd_norm_w, DS, tm_conv)
    gathered = forward_gathered("rest", *_gather_wait("gather_wait_rest", started_rest, y_ssd)[:2])
    pool_w_f = jnp.moveaxis(gathered[0].reshape(N_CHIPS, PG, PGD // N_CHIPS, PGD), 0, 1).reshape(PG, PGD, PGD)
    w_out_f = gathered[1].reshape(2 * DS, D)
    w_gate_f, w_up_f = cols(gathered[2]), cols(gathered[3])
    w_down_f = gathered[4].reshape(DFF, D)
    w_out_top, w_out_bot = w_out_f[:DS], w_out_f[DS:]
    pooled, y_pool = _pool_fwd(proj, pool_w_f, pool_scale, DS + DCONV, DP, tm_conv)

    add_res = lambda accs, ex, rex: [accs[0] + ex[0]]
    h1, = _mm("out_proj", "nn", [(y_ssd, w_out_top), (y_pool, w_out_bot)], L, D, DS, tm_mm, 512, DS, [F32],
              epilogue=add_res, extras=[xl])
    hn2 = _rms_fwd("rms2_fwd", h1, ffn_norm_w, tm_row)

    def glu(accs, ex, rex):
        return [accs[0], accs[1], (_silu(accs[0]) * accs[1])]

    tn_ff = _tile(DFF, 512, LANES)
    gate, up, act = _mm("ffn_in", "nn", [(hn2, w_gate_f), (hn2, w_up_f)], L, DFF, D, tm_mm, tn_ff, D,
                        [F32, F32, BF16], epilogue=glu, separate=True)
    tk_ff = _tile(DFF, DFF // 2, LANES)
    h2, = _mm("ffn_out", "nn", [(act, w_down_f)], L, D, DFF, tm_mm, 512, tk_ff, [F32], epilogue=add_res, extras=[h1])
    dh2, dh2_16, loss_blk, g_final = _final_loss(h2, final_norm_w.reshape(1, D), tgt, tm_row)

    def dglu(accs, ex, rex):
        gt, u = ex
        sg = _sigmoid(gt)
        return [accs[0] * u * (sg * (1.0 + gt * (1.0 - sg))), accs[0] * (gt * sg)]

    dgate, dup = _mm("ffn_out_dx", "nt", [(dh2_16, w_down_f)], L, DFF, D, tm_mm, tn_ff, D, [BF16, BF16],
                     epilogue=dglu, extras=[gate, up])
    tk_tok = _tile(L, 2048, 16)
    twice = lambda accs, ex, rex: list(accs) + list(accs)
    g_w_down, g_w_down16 = _mm("ffn_out_dw", "tn", [(act, dh2_16)], DFF, D, L, _tile(DFF, 1536, LANES), 1024, tk_tok,
                               [F32, BF16], epilogue=twice)
    g_w_gate, g_w_up, g_w_gate16, g_w_up16 = _mm("ffn_in_dw", "tn", [(hn2, dgate), (hn2, dup)], D, DFF, L, 1024, tn_ff,
                                                 tk_tok, [F32, F32, BF16, BF16], epilogue=twice, separate=True)

    pos = jnp.stack([chip, cc]).astype(jnp.int32)

    def start_reduce(tag, names, full_grads, full_grads16):
        from_sibling = _swap_halves("swap_halves_" + tag, full_grads16)
        partials = [_pair_sum("pair_sum_" + n, g, r, pos) for n, g, r in zip(names, full_grads, from_sibling)]
        return _scatter_start("scatter_start_" + tag, partials)

    names_ffn = ["w_gate", "w_up", "w_down"]
    started_ffn = start_reduce("ffn", names_ffn, [g_w_gate, g_w_up, g_w_down.reshape(N_CHIPS, -1, D)],
                               [g_w_gate16, g_w_up16, g_w_down16.reshape(N_CHIPS, -1, D)])
    dhn2, = _mm("ffn_in_dx", "nt", [(dgate, w_gate_f), (dup, w_up_f)], L, D, DFF, tm_mm, 512,
                _tile(DFF, DFF // 4, LANES), [F32], epilogue=add_row, row_extras=[pin_row(started_ffn, D)])
    dh1, g_ffn_norm, dh1_16 = _rms_bwd("rms2_bwd", h1, ffn_norm_w, [dhn2], dh2, tm_row, True)

    dy_ssd, dy_pool = _mm("out_proj_dx", "nt", [(dh1_16, w_out_top), (dh1_16, w_out_bot)], L, DS, D, tm_mm, 512, D,
                          [F32, F32], separate=True)
    g_w_out_top, g_w_out_bot, g_w_out_top16, g_w_out_bot16 = _mm(
        "out_proj_dw", "tn", [(y_ssd, dh1_16), (y_pool, dh1_16)], DS, D, L, 1024, 512, tk_tok, [F32, F32, BF16, BF16],
        epilogue=twice, separate=True)
    dy_raw, dproj, g_ssd_norm = _gated_bwd(y_ssd_raw, proj, ssd_norm_w, dy_ssd, DS, DMAIN, tm_conv)
    dxs, db, dc, ddt_raw, g_a_log, g_d_skip, g_dt_bias = _ssd_bwd(*ssd_args, dy_raw, states, DS, N)
    g_conv_w, g_conv_b = [], []
    for tag, dact, first in (("xs", dxs, 0), ("b", db, DS), ("c", dc, DS + G * N)):
        dproj, gw, gb = _conv_bwd("conv_bwd_" + tag, proj, dact, conv_w_f, conv_b, dproj, DS, first, tm_conv, cwid)
        g_conv_w.append(gw)
        g_conv_b.append(gb)
    g_conv_w, g_conv_b = jnp.concatenate(g_conv_w, axis=1), jnp.concatenate(g_conv_b, axis=1)
    dproj, g_pool_w, g_pool_scale = _pool_bwd(dy_pool, pooled, pool_w_f, pool_scale, dproj, DS + DCONV, tm_conv)
    ddt_pad = jnp.pad(jnp.transpose(ddt_raw, (1, 0, 2)).reshape(L, H), ((0, 0), (0, LANES - H))).astype(BF16)

    tk_main = _tile(DMAIN, DMAIN // 2, LANES)
    g_w_main, g_w_main16 = _mm("proj_main_dw", "tn", [(hn1, dproj)], D, DMAIN, L, 1024, _tile(DMAIN, 1024, LANES),
                               tk_tok, [F32, BF16], epilogue=twice)
    names_mix = ["w_in", "pool_w", "w_out"]
    pool_shards = jnp.moveaxis(g_pool_w.reshape(PG, N_CHIPS, PGD // N_CHIPS, PGD), 1, 0).reshape(N_CHIPS, -1, PGD)
    out_shards = lambda top, bot: jnp.stack([top.reshape(2, DS // 2, D), bot.reshape(2, DS // 2, D)]).reshape(N_CHIPS, -1, D)
    started_mix = start_reduce(
        "mix", names_mix, [win.split(g_w_main), pool_shards, out_shards(g_w_out_top, g_w_out_bot)],
        [win.split(g_w_main16), pool_shards.astype(BF16), out_shards(g_w_out_top16, g_w_out_bot16)])
    dhn1a, = _mm("proj_main_dx", "nt", [(dproj, w_main)], L, D, DMAIN, tm_mm, 512, tk_main, [F32],
                 epilogue=add_row, row_extras=[pin_row(started_mix, D)])
    dhn1b, = _mm("proj_dt_dx", "nt", [(ddt_pad, w_dt)], L, D, LANES, tm_mm, 512, LANES, [F32])
    g_w_dt, = _mm("proj_dt_dw", "tn", [(hn1, ddt_pad)], D, LANES, L, 512, LANES, tk_tok, [F32])
    grad_x, g_attn_norm = _rms_bwd("rms1_bwd", xl, attn_norm_w, [dhn1a, dhn1b], dh1, tm_row, False)

    def finish_reduce(tag, names, started, after):
        partials, landed, _ = _scatter_wait("scatter_wait_" + tag, started, after)
        landed = [lax.dynamic_update_slice(l, lax.dynamic_index_in_dim(p, chip, 0), (chip, 0, 0))
                  for l, p in zip(landed, partials)]
        return [_chip_sum("chip_sum_" + n, l, pos) for n, l in zip(names, landed)]

    halves = finish_reduce("ffn", names_ffn, started_ffn, grad_x) + finish_reduce("mix", names_mix, started_mix, grad_x)
    red = dict(zip(names_ffn + names_mix, _join_halves(halves)))

    small_w = [attn_norm_w, conv_b, dt_bias, a_log, d_skip, ssd_norm_w, pool_scale, ffn_norm_w, final_norm_w]
    small_m = [m_attn_norm_w, m_conv_b, m_dt_bias, m_a_log, m_d_skip, m_ssd_norm_w, m_pool_scale, m_ffn_norm_w, m_final_norm_w]
    small_v = [v_attn_norm_w, v_conv_b, v_dt_bias, v_a_log, v_d_skip, v_ssd_norm_w, v_pool_scale, v_ffn_norm_w, v_final_norm_w]
    small_g = [g_attn_norm, g_conv_b, g_dt_bias.reshape(1, H), g_a_log.reshape(1, H), g_d_skip.reshape(1, H),
               g_ssd_norm, g_pool_scale, g_ffn_norm, g_final.reshape(D)]
    extra_shapes = [(CONV_WIDTH, DCONV), (D, H), (1, LANES)]
    zeros_like_extra = [jnp.zeros(s, F32) for s in extra_shapes]
    g_blk = _pack_rows(small_g + [g_conv_w, g_w_dt[:, :H], loss_blk], LANES)
    rows = g_blk.shape[0]
    small_all = _all_gather_small("gather_small_grads", g_blk)
    s_g, s_d, s_m, s_v = _small_sum_adam(small_all, _pack_rows(small_w + zeros_like_extra, LANES),
                                         _pack_rows(small_m + zeros_like_extra, LANES),
                                         _pack_rows(small_v + zeros_like_extra, LANES), rows)
    shapes = [w.shape for w in small_w] + extra_shapes
    sg_list = _unpack_rows(s_g, shapes, LANES)
    sd_list = _unpack_rows(s_d, shapes, LANES)[:len(small_w)]
    sm_list = _unpack_rows(s_m, shapes, LANES)[:len(small_w)]
    sv_list = _unpack_rows(s_v, shapes, LANES)[:len(small_w)]
    loss = sg_list[-1][0, 0]
    grad_conv_w = lax.dynamic_slice(sg_list[-3], (0, chip * (DCONV // N_CHIPS)), (CONV_WIDTH, DCONV // N_CHIPS))
    grad_w_in = lax.switch(chip, [functools.partial(win.from_window, k) for k in range(N_CHIPS)], red["w_in"], sg_list[-2])

    def adam_nd(name, w, g, m, v):
        shp = w.shape
        to2 = lambda a: a.reshape(-1, shp[-1])
        d, m2, v2 = _adam(name, to2(w), to2(g), to2(m), to2(v))
        return d.reshape(shp), m2.reshape(shp), v2.reshape(shp)

    sharded = {
        "w_in": (w_in, grad_w_in[None], m_w_in, v_w_in),
        "conv_w": (conv_w, grad_conv_w[None], m_conv_w, v_conv_w),
        "pool_w": (pool_w, red["pool_w"].reshape(pool_w.shape), m_pool_w, v_pool_w),
        "w_out": (w_out, red["w_out"][None], m_w_out, v_w_out),
        "w_gate": (w_gate, red["w_gate"][None], m_w_gate, v_w_gate),
        "w_up": (w_up, red["w_up"][None], m_w_up, v_w_up),
        "w_down": (w_down, red["w_down"][None], m_w_down, v_w_down),
    }
    upd = {n: (a[1],) + adam_nd("adam_" + n, *a) for n, a in sharded.items()}
    small_names = ["attn_norm_w", "conv_b", "dt_bias", "a_log", "d_skip", "ssd_norm_w", "pool_scale", "ffn_norm_w",
                   "final_norm_w"]
    for i, n in enumerate(small_names):
        upd[n] = (sg_list[i], sd_list[i], sm_list[i], sv_list[i])

    order = ["attn_norm_w", "w_in", "conv_w", "conv_b", "dt_bias", "a_log", "d_skip", "ssd_norm_w", "pool_w",
             "pool_scale", "w_out", "ffn_norm_w", "w_gate", "w_up", "w_down", "final_norm_w"]
    outs = [loss, grad_x[None]]
    for j in range(4):
        outs += [upd[n][j] for n in order]
    return tuple(outs)
```

```python
import functools
import math

import jax
import jax.numpy as jnp
from jax import lax
from jax.experimental import pallas as pl
from jax.experimental.pallas import tpu as pltpu

F32 = jnp.float32
BF16 = jnp.bfloat16

NORM_EPS = 1e-5
HEAD_DIM = 64
SSD_GROUPS = 4
CONV_WIDTH = 4
CHUNK = 256
POOL_WINDOWS = (2, 4, 8, 16)
ADAM_LR = 0.001
ADAM_B1 = 0.9
ADAM_B2 = 0.999
ADAM_EPS = 1e-08
ADAM_WD = 0.01
ADAM_STEP = 10

N_CHIPS = 4
N_DEV = 8
LANES = 128
HALO = 16
FLAT_W = 512
VMEM_LIMIT = 52 * 1024 * 1024
MESH = pl.DeviceIdType.MESH

NN = (((1,), (0,)), ((), ()))
NT = (((1,), (1,)), ((), ()))
TN = (((0,), (0,)), ((), ()))


def _tile(n, cap, mult):
    best = None
    for t in range(mult, min(n, cap) + 1, mult):
        if n % t == 0:
            best = t
    return best if best is not None else n


def _params(sem):
    return pltpu.CompilerParams(dimension_semantics=sem, vmem_limit_bytes=VMEM_LIMIT)


def _dot(a, b, dims):
    return lax.dot_general(a, b, dims, preferred_element_type=F32)


def _sigmoid(x):
    return 1.0 / (1.0 + jnp.exp(-x))


def _silu(x):
    return x * _sigmoid(x)


def _softplus(x):
    return jnp.maximum(x, 0.0) + jnp.log(1.0 + jnp.exp(-jnp.abs(x)))


def _mm(name, mode, pairs, M, N, K, tm, tn, tk, out_dtypes, epilogue=None, extras=(), row_extras=(),
        separate=False):
    tm, tn, tk = min(tm, M), min(tn, N), min(tk, K)
    assert M % tm == 0 and N % tn == 0 and K % tk == 0, (name, M, N, K, tm, tn, tk)
    nk = K // tk
    npairs = len(pairs)
    nacc = npairs if separate else 1
    if mode == "nn":
        a_spec = pl.BlockSpec((tm, tk), lambda i, j, k: (i, k))
        b_spec = pl.BlockSpec((tk, tn), lambda i, j, k: (k, j))
        dims = NN
    elif mode == "nt":
        a_spec = pl.BlockSpec((tm, tk), lambda i, j, k: (i, k))
        b_spec = pl.BlockSpec((tn, tk), lambda i, j, k: (j, k))
        dims = NT
    else:
        a_spec = pl.BlockSpec((tk, tm), lambda i, j, k: (k, i))
        b_spec = pl.BlockSpec((tk, tn), lambda i, j, k: (k, j))
        dims = TN
    o_spec = pl.BlockSpec((tm, tn), lambda i, j, k: (i, j))
    r_spec = pl.BlockSpec((1, tn), lambda i, j, k: (0, j))
    if epilogue is None:
        epilogue = lambda accs, ex, rex: accs
    n_ex, n_rex, n_out = len(extras), len(row_extras), len(out_dtypes)

    def body(*refs):
        ab = refs[:2 * npairs]
        ex = refs[2 * npairs:2 * npairs + n_ex]
        rex = refs[2 * npairs + n_ex:2 * npairs + n_ex + n_rex]
        outs = refs[2 * npairs + n_ex + n_rex:2 * npairs + n_ex + n_rex + n_out]
        accs = refs[2 * npairs + n_ex + n_rex + n_out:]

        def products():
            res = [None] * nacc
            for p in range(npairs):
                d = _dot(ab[2 * p][...], ab[2 * p + 1][...], dims)
                q = p if separate else 0
                res[q] = d if res[q] is None else res[q] + d
            return res

        def finish(vals):
            res = epilogue(vals, [e[...] for e in ex], [r[...] for r in rex])
            for o, v in zip(outs, res):
                o[...] = v.astype(o.dtype)

        if nk == 1:
            finish(products())
        else:
            k = pl.program_id(2)

            @pl.when(k == 0)
            def _():
                for q in range(nacc):
                    accs[q][...] = jnp.zeros_like(accs[q])

            for p in range(npairs):
                accs[p if separate else 0][...] += _dot(ab[2 * p][...], ab[2 * p + 1][...], dims)

            @pl.when(k == nk - 1)
            def _():
                finish([a[...] for a in accs])

    in_specs = [a_spec, b_spec] * npairs + [o_spec] * n_ex + [r_spec] * n_rex
    args = [t for p in pairs for t in p] + list(extras) + list(row_extras)
    outs = pl.pallas_call(
        body,
        grid=(M // tm, N // tn, nk),
        in_specs=in_specs,
        out_specs=[o_spec] * n_out,
        out_shape=[jax.ShapeDtypeStruct((M, N), d) for d in out_dtypes],
        scratch_shapes=[pltpu.VMEM((tm, tn), F32) for _ in range(nacc if nk > 1 else 0)],
        compiler_params=_params(("parallel", "parallel", "arbitrary")),
        name=name,
    )(*args)
    return outs


def _rms(xf, w):
    y = xf * lax.rsqrt(jnp.mean(xf * xf, axis=-1, keepdims=True) + NORM_EPS)
    return y * w


def _rms_fwd(name, x, w, tm):
    L, D = x.shape

    def body(x_ref, w_ref, o_ref):
        o_ref[...] = _rms(x_ref[...], w_ref[...]).astype(BF16)

    return pl.pallas_call(
        body, grid=(L // tm,),
        in_specs=[pl.BlockSpec((tm, D), lambda i: (i, 0)), pl.BlockSpec((1, D), lambda i: (0, 0))],
        out_specs=pl.BlockSpec((tm, D), lambda i: (i, 0)),
        out_shape=jax.ShapeDtypeStruct((L, D), BF16),
        compiler_params=_params(("parallel",)), name=name)(x, w)


def _rms_bwd(name, x, w, dparts, dres, tm, with_bf16):
    L, D = x.shape
    nparts = len(dparts)

    def body(*refs):
        x_ref, w_ref = refs[:2]
        p_refs = refs[2:2 + nparts]
        r_ref = refs[2 + nparts]
        outs = refs[3 + nparts:]
        dhn = p_refs[0][...]
        for p in p_refs[1:]:
            dhn = dhn + p[...]
        _, vjp = jax.vjp(_rms, x_ref[...], w_ref[...])
        dx, dw = vjp(dhn)
        dx = dx + r_ref[...]
        outs[0][...] = dx
        gw_ref = outs[1]

        @pl.when(pl.program_id(0) == 0)
        def _():
            gw_ref[...] = jnp.zeros_like(gw_ref)

        gw_ref[...] += dw
        if with_bf16:
            outs[2][...] = dx.astype(BF16)

    row = pl.BlockSpec((tm, D), lambda i: (i, 0))
    vec = pl.BlockSpec((1, D), lambda i: (0, 0))
    out_shape = [jax.ShapeDtypeStruct((L, D), F32), jax.ShapeDtypeStruct((1, D), F32)]
    out_specs = [row, vec]
    if with_bf16:
        out_shape.append(jax.ShapeDtypeStruct((L, D), BF16))
        out_specs.append(row)
    return pl.pallas_call(
        body, grid=(L // tm,),
        in_specs=[row, vec] + [row] * nparts + [row],
        out_specs=out_specs, out_shape=out_shape,
        compiler_params=_params(("arbitrary",)), name=name)(x, w, *dparts, dres)


def _final_loss(h2, wf, target, tm):
    L, D = h2.shape

    def body(h_ref, w_ref, t_ref, dh_ref, dhb_ref, loss_ref, gw_ref):
        t = t_ref[...]

        def f(h, w):
            err = jnp.square(_rms(h, w) - t)
            return 0.5 * jnp.sum(jnp.mean(err, axis=-1))

        val, vjp = jax.vjp(f, h_ref[...], w_ref[...])
        dh, dw = vjp(jnp.ones((), F32))
        dh_ref[...] = dh
        dhb_ref[...] = dh.astype(BF16)

        @pl.when(pl.program_id(0) == 0)
        def _():
            gw_ref[...] = jnp.zeros_like(gw_ref)
            loss_ref[...] = jnp.zeros_like(loss_ref)

        gw_ref[...] += dw
        loss_ref[...] += jnp.full(loss_ref.shape, val, F32)

    row = pl.BlockSpec((tm, D), lambda i: (i, 0))
    vec = pl.BlockSpec((1, D), lambda i: (0, 0))
    lspec = pl.BlockSpec((1, LANES), lambda i: (0, 0))
    return pl.pallas_call(
        body, grid=(L // tm,),
        in_specs=[row, vec, row],
        out_specs=[row, row, lspec, vec],
        out_shape=[jax.ShapeDtypeStruct((L, D), F32), jax.ShapeDtypeStruct((L, D), BF16),
                   jax.ShapeDtypeStruct((1, LANES), F32), jax.ShapeDtypeStruct((1, D), F32)],
        compiler_params=_params(("arbitrary",)), name="final_loss")(h2, wf, target)


def _gated(y, z, w):
    g = y * _silu(z)
    g = g * lax.rsqrt(jnp.mean(g * g, axis=-1, keepdims=True) + NORM_EPS)
    return g * w


def _gated_fwd(y, proj, w, DS, tm):
    L = y.shape[0]
    GW = DS // SSD_GROUPS

    def body(y_ref, z_ref, w_ref, o_ref):
        o_ref[...] = _gated(y_ref[...], z_ref[...], w_ref[...]).astype(BF16)

    blk = pl.BlockSpec((tm, GW), lambda i, g: (i, g))
    return pl.pallas_call(
        body, grid=(L // tm, SSD_GROUPS),
        in_specs=[blk, blk, pl.BlockSpec((1, GW), lambda i, g: (0, g))],
        out_specs=blk, out_shape=jax.ShapeDtypeStruct((L, DS), BF16),
        compiler_params=_params(("parallel", "parallel")), name="gated_fwd")(y, proj, w)


def _gated_bwd(y, proj, w, dout, DS, dproj_cols, tm):
    L = y.shape[0]
    GW = DS // SSD_GROUPS

    def body(y_ref, z_ref, w_ref, d_ref, dy_ref, dz_ref, gw_ref):
        _, vjp = jax.vjp(_gated, y_ref[...], z_ref[...], w_ref[...])
        dy, dz, dw = vjp(d_ref[...])
        dy_ref[...] = dy
        dz_ref[...] = dz.astype(BF16)

        @pl.when(pl.program_id(1) == 0)
        def _():
            gw_ref[...] = jnp.zeros_like(gw_ref)

        gw_ref[...] += dw

    blk = pl.BlockSpec((tm, GW), lambda g, i: (i, g))
    vec = pl.BlockSpec((1, GW), lambda g, i: (0, g))
    return pl.pallas_call(
        body, grid=(SSD_GROUPS, L // tm),
        in_specs=[blk, blk, vec, blk],
        out_specs=[blk, blk, vec],
        out_shape=[jax.ShapeDtypeStruct((L, DS), F32), jax.ShapeDtypeStruct((L, dproj_cols), BF16),
                   jax.ShapeDtypeStruct((1, DS), F32)],
        compiler_params=_params(("parallel", "arbitrary")), name="gated_bwd")(y, proj, w, dout)


def _halo_prev(tm, cw, col0):
    return pl.BlockSpec((HALO, cw), lambda i, j: (jnp.maximum(i * (tm // HALO) - 1, 0), col0 + j))


def _halo_next(tm, cw, col0, L):
    return pl.BlockSpec((HALO, cw), lambda i, j: (jnp.minimum((i + 1) * (tm // HALO), L // HALO - 1), col0 + j))


def _conv_fwd(proj, conv_w, conv_b, DS, DCONV, tm, cw):
    L = proj.shape[0]
    col0 = DS // cw
    K = CONV_WIDTH

    def body(x_ref, p_ref, w_ref, b_ref, o_ref, ext):
        i = pl.program_id(0)
        ext[0:HALO, :] = jnp.where(i == 0, 0.0, p_ref[...])
        ext[HALO:, :] = x_ref[...]
        acc = jnp.broadcast_to(b_ref[...], (tm, cw))
        for k in range(K):
            acc = acc + w_ref[k:k + 1, :] * ext[pl.ds(HALO - (K - 1) + k, tm), :]
        o_ref[...] = _silu(acc)

    return pl.pallas_call(
        body, grid=(L // tm, DCONV // cw),
        in_specs=[pl.BlockSpec((tm, cw), lambda i, j: (i, col0 + j)), _halo_prev(tm, cw, col0),
                  pl.BlockSpec((K, cw), lambda i, j: (0, j)), pl.BlockSpec((1, cw), lambda i, j: (0, j))],
        out_specs=pl.BlockSpec((tm, cw), lambda i, j: (i, j)),
        out_shape=jax.ShapeDtypeStruct((L, DCONV), F32),
        scratch_shapes=[pltpu.VMEM((tm + HALO, cw), F32)],
        compiler_params=_params(("parallel", "parallel")), name="conv_fwd")(proj, proj, conv_w, conv_b)


def _conv_bwd(name, proj, dact, conv_w, conv_b, dproj, DS, first, tm, cw):
    L = proj.shape[0]
    ncols = dact.shape[1]
    col0 = (DS + first) // cw
    wcol0 = first // cw
    K = CONV_WIDTH
    nrt = L // tm

    def body(x_ref, p_ref, n_ref, d_ref, dn_ref, w_ref, b_ref, alias_ref, dx_ref, dw_ref, db_ref, ext, dext):
        i = pl.program_id(1)
        last = i == nrt - 1
        ext[0:HALO, :] = jnp.where(i == 0, 0.0, p_ref[...])
        ext[HALO:HALO + tm, :] = x_ref[...]
        ext[HALO + tm:, :] = n_ref[...]
        dfull = jnp.concatenate([d_ref[...], jnp.where(last, 0.0, dn_ref[...])], axis=0)
        acc = jnp.broadcast_to(b_ref[...], (tm + HALO, cw))
        for k in range(K):
            acc = acc + w_ref[k:k + 1, :] * ext[pl.ds(HALO - (K - 1) + k, tm + HALO), :]
        sg = _sigmoid(acc)
        dconv = dfull * (sg * (1.0 + acc * (1.0 - sg)))
        dext[...] = dconv

        @pl.when(i == 0)
        def _():
            dw_ref[...] = jnp.zeros_like(dw_ref)
            db_ref[...] = jnp.zeros_like(db_ref)

        x_tile = x_ref[...]
        dx = jnp.zeros((tm, cw), F32)
        for k in range(K):
            shifted = dext[pl.ds(K - 1 - k, tm), :]
            dx = dx + w_ref[k:k + 1, :] * shifted
            dw_ref[k:k + 1, :] += jnp.sum(shifted * x_tile, axis=0, keepdims=True)
        dx_ref[...] = dx.astype(BF16)
        db_ref[...] += jnp.sum(dext[pl.ds(0, tm), :], axis=0, keepdims=True)

    prev = pl.BlockSpec((HALO, cw), lambda j, i: (jnp.maximum(i * (tm // HALO) - 1, 0), col0 + j))
    nxt = pl.BlockSpec((HALO, cw), lambda j, i: (jnp.minimum((i + 1) * (tm // HALO), L // HALO - 1), col0 + j))
    dnxt = pl.BlockSpec((HALO, cw), lambda j, i: (jnp.minimum((i + 1) * (tm // HALO), L // HALO - 1), j))
    return pl.pallas_call(
        body, grid=(ncols // cw, nrt),
        in_specs=[pl.BlockSpec((tm, cw), lambda j, i: (i, col0 + j)), prev, nxt,
                  pl.BlockSpec((tm, cw), lambda j, i: (i, j)), dnxt,
                  pl.BlockSpec((K, cw), lambda j, i: (0, wcol0 + j)), pl.BlockSpec((1, cw), lambda j, i: (0, wcol0 + j)),
                  _ANY],
        out_specs=[pl.BlockSpec((tm, cw), lambda j, i: (i, col0 + j)),
                   pl.BlockSpec((K, cw), lambda j, i: (0, j)), pl.BlockSpec((1, cw), lambda j, i: (0, j))],
        out_shape=[jax.ShapeDtypeStruct(dproj.shape, BF16), jax.ShapeDtypeStruct((K, ncols), F32),
                   jax.ShapeDtypeStruct((1, ncols), F32)],
        input_output_aliases={7: 0},
        scratch_shapes=[pltpu.VMEM((tm + 2 * HALO, cw), F32), pltpu.VMEM((tm + HALO, cw), F32)],
        compiler_params=_params(("parallel", "arbitrary")), name=name,
    )(proj, proj, proj, dact, dact, conv_w, conv_b, dproj)


def _pool_fwd(proj, pool_w, pool_scale, ucol, DP, tm):
    L = proj.shape[0]
    PG = len(POOL_WINDOWS)
    PGD = DP // PG
    col0 = ucol // PGD

    def body(u_ref, p_ref, w_ref, s_ref, pooled_ref, y_ref, ext):
        i, g = pl.program_id(0), pl.program_id(1)
        ext[0:HALO, :] = jnp.where(i == 0, 0.0, p_ref[...])
        ext[HALO:, :] = u_ref[...]
        t = i * tm + lax.broadcasted_iota(jnp.int32, (tm, 1), 0)
        for gi, win in enumerate(POOL_WINDOWS):
            @pl.when(g == gi)
            def _():
                acc = ext[pl.ds(HALO, tm), :]
                for j in range(1, win):
                    acc = acc + ext[pl.ds(HALO - j, tm), :]
                count = jnp.minimum(t + 1, win).astype(F32)
                pooled = (acc / count - u_ref[...]).astype(BF16)
                pooled_ref[...] = pooled
                y_ref[...] = (_dot(pooled, w_ref[...], NN) * s_ref[...]).astype(BF16)

    blk = pl.BlockSpec((tm, PGD), lambda i, g: (i, g))
    return pl.pallas_call(
        body, grid=(L // tm, PG),
        in_specs=[pl.BlockSpec((tm, PGD), lambda i, g: (i, col0 + g)), _halo_prev(tm, PGD, col0),
                  pl.BlockSpec((None, PGD, PGD), lambda i, g: (g, 0, 0)), pl.BlockSpec((1, PGD), lambda i, g: (0, g))],
        out_specs=[blk, blk],
        out_shape=[jax.ShapeDtypeStruct((L, DP), BF16), jax.ShapeDtypeStruct((L, DP), BF16)],
        scratch_shapes=[pltpu.VMEM((tm + HALO, PGD), F32)],
        compiler_params=_params(("parallel", "parallel")), name="pool_fwd")(proj, proj, pool_w, pool_scale)


def _pool_bwd(dy, pooled, pool_w, pool_scale, dproj, ucol, tm):
    L, DP = dy.shape
    PG = len(POOL_WINDOWS)
    PGD = DP // PG
    nrt = L // tm
    col0 = ucol // PGD

    def body(d_ref, dn_ref, p_ref, w_ref, s_ref, alias_ref, du_ref, dw_ref, ds_ref, qext):
        g, i = pl.program_id(0), pl.program_id(1)
        last = i == nrt - 1
        dfull = jnp.concatenate([d_ref[...], jnp.where(last, 0.0, dn_ref[...])], axis=0)
        dyp = (dfull * s_ref[...]).astype(BF16)
        dpooled = _dot(dyp, w_ref[...], NT)
        t = i * tm + lax.broadcasted_iota(jnp.int32, (tm + HALO, 1), 0)
        for gi, win in enumerate(POOL_WINDOWS):
            @pl.when(g == gi)
            def _():
                qext[...] = dpooled / jnp.minimum(t + 1, win).astype(F32)
                acc = qext[pl.ds(0, tm), :]
                for j in range(1, win):
                    acc = acc + qext[pl.ds(j, tm), :]
                du_ref[...] = (acc - dpooled[0:tm, :]).astype(BF16)

        @pl.when(i == 0)
        def _():
            dw_ref[...] = jnp.zeros_like(dw_ref)
            ds_ref[...] = jnp.zeros_like(ds_ref)

        pooled_t = p_ref[...]
        dw_ref[...] += _dot(pooled_t, dyp[0:tm, :], TN)
        ypre = _dot(pooled_t, w_ref[...], NN)
        ds_ref[...] += jnp.sum(d_ref[...] * ypre, axis=0, keepdims=True)

    blk = pl.BlockSpec((tm, PGD), lambda g, i: (i, g))
    nxt = pl.BlockSpec((HALO, PGD), lambda g, i: (jnp.minimum((i + 1) * (tm // HALO), L // HALO - 1), g))
    wspec = pl.BlockSpec((None, PGD, PGD), lambda g, i: (g, 0, 0))
    vec = pl.BlockSpec((1, PGD), lambda g, i: (0, g))
    return pl.pallas_call(
        body, grid=(PG, nrt),
        in_specs=[blk, nxt, blk, wspec, vec, _ANY],
        out_specs=[pl.BlockSpec((tm, PGD), lambda g, i: (i, col0 + g)), wspec, vec],
        out_shape=[jax.ShapeDtypeStruct(dproj.shape, BF16), jax.ShapeDtypeStruct((PG, PGD, PGD), F32),
                   jax.ShapeDtypeStruct((1, DP), F32)],
        input_output_aliases={5: 0},
        scratch_shapes=[pltpu.VMEM((tm + HALO, PGD), F32)],
        compiler_params=_params(("parallel", "arbitrary")), name="pool_bwd",
    )(dy, dy, pooled, pool_w, pool_scale, dproj)


def _ssd_common(dtc_raw, dtr_raw, bc, br, ac, ar):
    ch = CHUNK
    row = lax.broadcasted_iota(jnp.int32, (ch, ch), 0)
    col = lax.broadcasted_iota(jnp.int32, (ch, ch), 1)
    lower = row >= col
    dtc = _softplus(dtc_raw + bc)
    dtr = _softplus(dtr_raw + br)
    a_c = -jnp.exp(ac)
    a_r = -jnp.exp(ar)
    hi = lax.Precision.HIGHEST
    acol = jnp.dot(lower.astype(F32), dtc * a_c, preferred_element_type=F32, precision=hi)
    arow = jnp.dot(dtr * a_r, (row <= col).astype(F32), preferred_element_type=F32, precision=hi)
    return lower, row <= col, dtc, a_c, acol, arow, dtr


def _spread_heads(cols):
    R = cols.shape[1]
    shape = (R, R * LANES)
    spread = (lax.broadcasted_iota(jnp.int32, shape, 0) == lax.broadcasted_iota(jnp.int32, shape, 1) // LANES).astype(BF16)
    out, rest = None, cols
    for _ in range(3):
        part = rest.astype(BF16)
        rest = rest - part.astype(F32)
        term = _dot(part, spread, NN)
        out = term if out is None else out + term
    return out


def _ssd_fwd(xbc, dtc_raw, dtr_raw, bias_c, bias_r, alog_c, alog_r, dskip_c, DS, N):
    L = xbc.shape[0]
    G, P, ch = SSD_GROUPS, HEAD_DIM, CHUNK
    R = dtc_raw.shape[2]
    GW = R * P
    nc = L // ch

    def body(xs_ref, b_ref, c_ref, dtc_ref, dtr_ref, bc_ref, br_ref, ac_ref, ar_ref, dk_ref,
             y_ref, st_ref, h_ref):
        @pl.when(pl.program_id(1) == 0)
        def _():
            h_ref[...] = jnp.zeros_like(h_ref)

        lower, _, _, _, acol_all, arow_all, dtr = _ssd_common(
            dtc_ref[...], dtr_ref[...], bc_ref[...], br_ref[...], ac_ref[...], ar_ref[...])
        bm = b_ref[...]
        cb16 = c_ref[...].astype(BF16)
        b16 = bm.astype(BF16)
        bt = bm.T
        cb = _dot(cb16, b16, NT)
        dk = dk_ref[...]
        st_ref[...] = h_ref[...]
        for r in range(R):
            acol = acol_all[:, r:r + 1]
            arow = arow_all[r:r + 1, :]
            alast = acol_all[ch - 1:ch, r:r + 1]
            dt_row = dtr[r:r + 1, :]
            decay = jnp.exp(jnp.where(lower, acol - arow, -1e30))
            x_h = xs_ref[:, pl.ds(r * P, P)]
            x16 = x_h.astype(BF16)
            m16 = (cb * decay * dt_row).astype(BF16)
            h_prev = h_ref[r]
            y = _dot(m16, x16, NN)
            y = y + jnp.exp(acol) * _dot(cb16, h_prev.astype(BF16), NN)
            y = y + dk[:, r:r + 1] * x_h
            y_ref[:, pl.ds(r * P, P)] = y
            to_end_dt = jnp.exp(alast - arow) * dt_row
            h_ref[r] = jnp.exp(alast) * h_prev + _dot((bt * to_end_dt).astype(BF16), x16, NN)

    nb = DS // N
    return pl.pallas_call(
        body, grid=(G, nc),
        in_specs=[pl.BlockSpec((ch, GW), lambda g, c: (c, g)),
                  pl.BlockSpec((ch, N), lambda g, c: (c, nb + g)),
                  pl.BlockSpec((ch, N), lambda g, c: (c, nb + G + g)),
                  pl.BlockSpec((None, ch, R), lambda g, c: (g, c, 0)),
                  pl.BlockSpec((None, R, ch), lambda g, c: (g, 0, c)),
                  pl.BlockSpec((None, 1, R), lambda g, c: (g, 0, 0)),
                  pl.BlockSpec((None, R, 1), lambda g, c: (g, 0, 0)),
                  pl.BlockSpec((None, 1, R), lambda g, c: (g, 0, 0)),
                  pl.BlockSpec((None, R, 1), lambda g, c: (g, 0, 0)),
                  pl.BlockSpec((None, 1, R), lambda g, c: (g, 0, 0))],
        out_specs=[pl.BlockSpec((ch, GW), lambda g, c: (c, g)),
                   pl.BlockSpec((None, R, N, P), lambda g, c: (c, g, 0, 0))],
        out_shape=[jax.ShapeDtypeStruct((L, DS), F32), jax.ShapeDtypeStruct((nc, G * R, N, P), F32)],
        scratch_shapes=[pltpu.VMEM((R, N, P), F32)],
        compiler_params=_params(("parallel", "arbitrary")), name="ssd_fwd",
    )(xbc, xbc, xbc, dtc_raw, dtr_raw, bias_c, bias_r, alog_c, alog_r, dskip_c)


def _ssd_bwd(xbc, dtc_raw, dtr_raw, bias_c, bias_r, alog_c, alog_r, dskip_c, dy, states, DS, N):
    L = xbc.shape[0]
    G, P, ch = SSD_GROUPS, HEAD_DIM, CHUNK
    R = dtc_raw.shape[2]
    GW = R * P
    nc = L // ch
    assert N == LANES and P <= LANES and ch % LANES == 0

    def body(xs_ref, b_ref, c_ref, dtc_ref, dtr_ref, bc_ref, br_ref, ac_ref, ar_ref, dk_ref,
             dy_ref, stp_ref,
             dxs_ref, db_ref, dc_ref, ddt_ref, dal_ref, ddk_ref, dbias_ref, dh_ref):
        @pl.when(pl.program_id(1) == 0)
        def _():
            dh_ref[...] = jnp.zeros_like(dh_ref)
            dal_ref[...] = jnp.zeros_like(dal_ref)
            ddk_ref[...] = jnp.zeros_like(ddk_ref)
            dbias_ref[...] = jnp.zeros_like(dbias_ref)

        lower, upper, dtc, a_c, acol_all, arow_all, _ = _ssd_common(
            dtc_ref[...], dtr_ref[...], bc_ref[...], br_ref[...], ac_ref[...], ar_ref[...])
        bm = b_ref[...]
        cm = c_ref[...]
        b16 = bm.astype(BF16)
        c16 = cm.astype(BF16)
        ct16 = cm.T.astype(BF16)
        cb = _dot(c16, b16, NT)
        cbt = _dot(b16, c16, NT)
        dk = dk_ref[...]
        lane_r = lax.broadcasted_iota(jnp.int32, (ch, R), 1)
        lane_1 = lax.broadcasted_iota(jnp.int32, (1, R), 1)
        dc = jnp.zeros((ch, N), F32)
        db = jnp.zeros((ch, N), F32)
        da_all = jnp.zeros((R, ch), F32)
        q_all = jnp.zeros((R, ch), F32)
        sxd_all = jnp.zeros((ch, R), F32)
        const = jnp.zeros((1, R), F32)
        ddk = jnp.zeros((1, R), F32)
        sub_r = lax.broadcasted_iota(jnp.int32, (R, ch), 0)
        ct = cm.T
        bt = bm.T
        dcb = jnp.zeros((ch, ch), F32)
        acol_lanes = _spread_heads(acol_all)
        dt_lanes = _spread_heads(dtc)
        for r in range(R):
            a128 = acol_lanes[:, r * LANES:(r + 1) * LANES]
            arow = arow_all[r:r + 1, :]
            alast = acol_all[ch - 1:ch, r:r + 1]
            seg = jnp.tile(a128, (1, ch // LANES)) - arow
            decay = jnp.exp(jnp.where(lower, seg, -1e30))
            decay_t = jnp.exp(jnp.where(upper, -seg, -1e30))
            x_h = xs_ref[:, pl.ds(r * P, P)]
            dy_h = dy_ref[:, pl.ds(r * P, P)]
            dt_h = dt_lanes[:, r * LANES:r * LANES + P]
            dk_h = dk[:, r:r + 1]
            xdt = x_h * dt_h
            xdt16 = xdt.astype(BF16)
            dy16 = dy_h.astype(BF16)
            h_prev = stp_ref[r]
            h16 = h_prev.astype(BF16)
            dh_next = dh_ref[r]
            dhn16 = dh_next.astype(BF16)
            to_end_n = jnp.exp(alast - a128)
            e_a_n = jnp.exp(a128)
            to_end, e_a = to_end_n[:, :P], e_a_n[:, :P]
            mt = cbt * decay_t
            pm = _dot(dy16, xdt16, NT) * decay
            wt = _dot(xdt16, dy16, NT) * mt
            dxdt = _dot(mt.astype(BF16), dy16, NN) + to_end * _dot(b16, dhn16, NN)
            dcb = dcb + pm
            dc = dc + e_a_n * _dot(dy16, h16, NT)
            db = db + to_end_n * _dot(xdt16, dhn16, NT)
            dh_ref[r] = jnp.exp(alast) * dh_next + _dot(ct16, (dy_h * e_a).astype(BF16), NN)
            da = (jnp.sum(wt, axis=0, keepdims=True) - jnp.sum(pm * cb, axis=0, keepdims=True)
                  + jnp.exp(arow) * jnp.sum(ct * _dot(h16, dy16, NT), axis=0, keepdims=True))
            q = jnp.exp(alast - arow) * jnp.sum(bt * _dot(dhn16, xdt16, NT), axis=0, keepdims=True)
            da_all = da_all + jnp.where(sub_r == r, da, 0.0)
            q_all = q_all + jnp.where(sub_r == r, q, 0.0)
            sxd_all = sxd_all + jnp.where(lane_r == r, jnp.sum(dxdt * x_h, axis=1, keepdims=True), 0.0)
            const = const + jnp.where(lane_1 == r, jnp.exp(alast) * jnp.sum(dh_next * h_prev), 0.0)
            ddk = ddk + jnp.where(lane_1 == r, jnp.sum(dy_h * x_h), 0.0)
            dxs_ref[:, pl.ds(r * P, P)] = dxdt * dt_h + dk_h * dy_h
        dcb16 = dcb.astype(BF16)
        dc_ref[...] = dc + _dot(dcb16, b16, NN)
        db_ref[...] = db + _dot(dcb16, c16, TN)
        hi = lax.Precision.HIGHEST
        strict_lower = jnp.logical_and(lower, jnp.logical_not(upper))
        dda = (lax.dot_general(upper.astype(F32), da_all, NT, preferred_element_type=F32, precision=hi)
               + lax.dot_general(strict_lower.astype(F32), q_all, NT, preferred_element_type=F32, precision=hi)
               + const)
        ddt = dda * a_c + sxd_all
        dal_ref[...] += jnp.sum(dda * dtc, axis=0, keepdims=True) * a_c
        ddk_ref[...] += ddk
        ddt_raw = ddt * _sigmoid(dtc_ref[...] + bc_ref[...])
        ddt_ref[...] = ddt_raw
        dbias_ref[...] += jnp.sum(ddt_raw, axis=0, keepdims=True)

    nb = DS // N
    rc = lambda c: nc - 1 - c
    vec_c = pl.BlockSpec((None, 1, R), lambda g, c: (g, 0, 0))
    vec_r = pl.BlockSpec((None, R, 1), lambda g, c: (g, 0, 0))
    big = pl.BlockSpec((ch, GW), lambda g, c: (rc(c), g))
    return pl.pallas_call(
        body, grid=(G, nc),
        in_specs=[big,
                  pl.BlockSpec((ch, N), lambda g, c: (rc(c), nb + g)),
                  pl.BlockSpec((ch, N), lambda g, c: (rc(c), nb + G + g)),
                  pl.BlockSpec((None, ch, R), lambda g, c: (g, rc(c), 0)),
                  pl.BlockSpec((None, R, ch), lambda g, c: (g, 0, rc(c))),
                  vec_c, vec_r, vec_c, vec_r, vec_c,
                  big,
                  pl.BlockSpec((None, R, N, P), lambda g, c: (rc(c), g, 0, 0))],
        out_specs=[big,
                   pl.BlockSpec((ch, N), lambda g, c: (rc(c), g)),
                   pl.BlockSpec((ch, N), lambda g, c: (rc(c), g)),
                   pl.BlockSpec((None, ch, R), lambda g, c: (g, rc(c), 0)),
                   vec_c, vec_c, vec_c],
        out_shape=[jax.ShapeDtypeStruct((L, DS), F32), jax.ShapeDtypeStruct((L, G * N), F32),
                   jax.ShapeDtypeStruct((L, G * N), F32), jax.ShapeDtypeStruct((G, L, R), F32),
                   jax.ShapeDtypeStruct((G, 1, R), F32), jax.ShapeDtypeStruct((G, 1, R), F32),
                   jax.ShapeDtypeStruct((G, 1, R), F32)],
        scratch_shapes=[pltpu.VMEM((R, N, P), F32)],
        compiler_params=_params(("parallel", "arbitrary")), name="ssd_bwd",
    )(xbc, xbc, xbc, dtc_raw, dtr_raw, bias_c, bias_r, alog_c, alog_r, dskip_c, dy, states)


def _adam_math(w, g, m, v):
    m = ADAM_B1 * m + (1.0 - ADAM_B1) * g
    v = ADAM_B2 * v + (1.0 - ADAM_B2) * jnp.square(g)
    m_hat = m / (1.0 - ADAM_B1 ** ADAM_STEP)
    v_hat = v / (1.0 - ADAM_B2 ** ADAM_STEP)
    delta = -ADAM_LR * (m_hat / (jnp.sqrt(v_hat) + ADAM_EPS) + ADAM_WD * w)
    return delta, m, v


def _adam(name, w, g, m, v):
    rows, cols = w.shape
    tr = _tile(rows, max(8, (1 << 19) // cols // 8 * 8), 8)

    def body(w_ref, g_ref, m_ref, v_ref, d_ref, mo_ref, vo_ref):
        d, m2, v2 = _adam_math(w_ref[...], g_ref[...], m_ref[...], v_ref[...])
        d_ref[...] = d
        mo_ref[...] = m2
        vo_ref[...] = v2

    blk = pl.BlockSpec((tr, cols), lambda i: (i, 0))
    return pl.pallas_call(
        body, grid=(rows // tr,), in_specs=[blk] * 4, out_specs=[blk] * 3,
        out_shape=[jax.ShapeDtypeStruct((rows, cols), F32)] * 3,
        compiler_params=_params(("parallel",)), name=name)(w, g, m, v)


def _small_sum_adam(gathered, w, m, v, rows):
    def body(ga_ref, w_ref, m_ref, v_ref, g_ref, d_ref, mo_ref, vo_ref):
        g = ga_ref[0:rows, :]
        for d in range(1, N_DEV):
            g = g + ga_ref[d * rows:(d + 1) * rows, :]
        g_ref[...] = g
        dl, m2, v2 = _adam_math(w_ref[...], g, m_ref[...], v_ref[...])
        d_ref[...] = dl
        mo_ref[...] = m2
        vo_ref[...] = v2

    return pl.pallas_call(
        body, out_shape=[jax.ShapeDtypeStruct((rows, LANES), F32)] * 4,
        compiler_params=pltpu.CompilerParams(vmem_limit_bytes=VMEM_LIMIT), name="small_sum_adam",
    )(gathered, w, m, v)


def _row_tile(rh, cols):
    return _tile(rh, max(16, (1 << 19) // cols // 16 * 16), 16)


def _shard_dims(g):
    return (g.shape[1], g.shape[2]) if g.ndim == 3 else (g.shape[0], g.shape[1] // N_CHIPS)


def _pair_sum(name, g, recv, pos):
    r, c = _shard_dims(g)
    rh = r // 2
    tr = _row_tile(rh, c)
    nrt = rh // tr

    def body(pos_ref, a_ref, b_ref, o_ref):
        o_ref[...] = (a_ref[...] + b_ref[...].astype(F32)).astype(BF16)

    own = (pl.BlockSpec((None, tr, c), lambda k, i, p: (k, p[1] * nrt + i, 0)) if g.ndim == 3
           else pl.BlockSpec((tr, c), lambda k, i, p: (p[1] * nrt + i, k)))
    return pl.pallas_call(
        body,
        grid_spec=pltpu.PrefetchScalarGridSpec(
            num_scalar_prefetch=1, grid=(N_CHIPS, nrt),
            in_specs=[own,
                      pl.BlockSpec((None, tr, c), lambda k, i, p: (k, i, 0))],
            out_specs=pl.BlockSpec((None, tr, c), lambda k, i, p: (k, i, 0))),
        out_shape=jax.ShapeDtypeStruct((N_CHIPS, rh, c), BF16),
        compiler_params=_params(("parallel", "parallel")), name=name)(pos, g, recv)


def _chip_sum(name, parts, pos):
    _, rh, c = parts.shape
    tr = _row_tile(rh, c)
    nrt = rh // tr

    def body(pos_ref, p_ref, o_ref):
        s = p_ref[0].astype(F32)
        for k in range(1, N_CHIPS):
            s = s + p_ref[k].astype(F32)
        o_ref[...] = s

    return pl.pallas_call(
        body,
        grid_spec=pltpu.PrefetchScalarGridSpec(
            num_scalar_prefetch=1, grid=(nrt,),
            in_specs=[pl.BlockSpec((N_CHIPS, tr, c), lambda i, p: (0, i, 0))],
            out_specs=pl.BlockSpec((tr, c), lambda i, p: (p[1] * nrt + i, 0))),
        out_shape=jax.ShapeDtypeStruct((2 * rh, c), F32),
        compiler_params=_params(("parallel",)), name=name)(pos, parts)


_HBM = pl.BlockSpec(memory_space=pltpu.HBM)


def _chip_xy(k):
    return k // 2, k % 2


def _half_rows(ref, hc, rh):
    return ref.at[pl.ds(pl.multiple_of(hc * rh, 16), rh), :]


_SEM = pl.BlockSpec(memory_space=pltpu.SEMAPHORE)
_ANY = pl.BlockSpec(memory_space=pl.ANY)
_SPLIT = pltpu.CompilerParams(has_side_effects=pltpu.SideEffectType.DATAFLOW_SIDE_EFFECTING)


def _in_hbm(a):
    return pltpu.with_memory_space_constraint(a, pltpu.HBM)


def _push_start(name, srcs, land_shapes, copies_of):
    n = len(srcs)

    def body(*refs):
        s_refs, l_refs = refs[:n], refs[n:2 * n]
        send_sems, recv_sems = refs[2 * n], refs[2 * n + 1]
        token = refs[-1]
        x, y, c = lax.axis_index("x"), lax.axis_index("y"), lax.axis_index("c")
        me = 2 * x + y
        for i in range(n):
            for k in range(N_CHIPS):
                @pl.when(k != me)
                def _():
                    src, dst, dev = copies_of(i, k, s_refs[i], l_refs[i], me, x, y, c)
                    pltpu.make_async_remote_copy(
                        src_ref=src, dst_ref=dst, send_sem=send_sems.at[N_CHIPS * i + k],
                        recv_sem=recv_sems.at[N_CHIPS * i + me], device_id=dev, device_id_type=MESH).start()
        token[...] = jnp.zeros_like(token)

    lands = [lax.empty(s, d) for s, d in land_shapes]
    outs = pl.pallas_call(
        body, name=name,
        out_shape=[pltpu.SemaphoreType.DMA((N_CHIPS * n,)), pltpu.SemaphoreType.DMA((N_CHIPS * n,))]
        + [pltpu.HBM(s.shape, s.dtype) for s in srcs] + [pltpu.HBM(s, d) for s, d in land_shapes]
        + [jax.ShapeDtypeStruct((8, LANES), F32)],
        in_specs=[_HBM] * (2 * n),
        out_specs=[_SEM, _SEM] + [_HBM] * (2 * n) + [pl.BlockSpec(memory_space=pltpu.VMEM)],
        input_output_aliases={j: 2 + j for j in range(2 * n)},
        compiler_params=_SPLIT,
    )(*[_in_hbm(s) for s in srcs], *[_in_hbm(l) for l in lands])
    return outs[0], outs[1], outs[2:2 + n], outs[2 + n:2 + 2 * n], outs[-1]


def _push_wait(name, started, after, landed_of):
    send_sems, recv_sems, srcs, lands, _ = started
    n = len(srcs)

    def body(*refs):
        s_refs, l_refs = refs[:n], refs[n:2 * n]
        send, recv = refs[2 * n], refs[2 * n + 1]
        token = refs[-1]
        token[...] = jnp.zeros_like(token)
        x, y, c = lax.axis_index("x"), lax.axis_index("y"), lax.axis_index("c")
        me = 2 * x + y
        for i in range(n):
            for k in range(N_CHIPS):
                @pl.when(k != me)
                def _():
                    src, dst = landed_of(i, k, s_refs[i], l_refs[i], me, c)
                    cp = pltpu.make_async_remote_copy(
                        src_ref=src, dst_ref=dst, send_sem=send.at[N_CHIPS * i + k], recv_sem=recv.at[N_CHIPS * i + k],
                        device_id=(x, y, c), device_id_type=MESH)
                    cp.wait_send()
                    cp.wait_recv()

    outs = pl.pallas_call(
        body, name=name,
        out_shape=[pltpu.HBM(s.shape, s.dtype) for s in srcs] + [pltpu.HBM(l.shape, l.dtype) for l in lands]
        + [jax.ShapeDtypeStruct((8, LANES), F32)],
        in_specs=[_HBM] * (2 * n) + [_SEM, _SEM, _ANY],
        out_specs=[_HBM] * (2 * n) + [pl.BlockSpec(memory_space=pltpu.VMEM)],
        input_output_aliases={j: j for j in range(2 * n)},
        compiler_params=_SPLIT,
    )(*srcs, *lands, send_sems, recv_sems, after)
    return outs[:n], outs[n:2 * n], outs[-1]


def _gather_start(name, shards):
    def copies_of(i, k, src, land, me, x, y, c):
        rh = shards[i].shape[0] // 2
        kx, ky = _chip_xy(k)
        return _half_rows(src, c, rh), _half_rows(land.at[me], c, rh), (kx, ky, c)

    return _push_start(name, shards, [((N_CHIPS,) + s.shape, s.dtype) for s in shards], copies_of)


def _gather_wait(name, started, after):
    shapes = [s.shape for s in started[2]]

    def landed_of(i, k, src, land, me, c):
        rh = shapes[i][0] // 2
        return _half_rows(src, c, rh), _half_rows(land.at[k], c, rh)

    return _push_wait(name, started, after, landed_of)


def _forward_halves(name, bufs):
    n = len(bufs)

    def body(*refs):
        i_refs, o_refs, send_sems, recv_sems = refs[:n], refs[n:2 * n], refs[2 * n], refs[2 * n + 1]
        x, y, c = lax.axis_index("x"), lax.axis_index("y"), lax.axis_index("c")
        me = 2 * x + y

        def fwd(i, k, hc):
            rh = bufs[i].shape[1] // 2
            return pltpu.make_async_remote_copy(
                src_ref=_half_rows(i_refs[i].at[k], hc, rh), dst_ref=_half_rows(o_refs[i].at[k], hc, rh),
                send_sem=send_sems.at[i, k], recv_sem=recv_sems.at[i, k],
                device_id=(x, y, 1 - c), device_id_type=MESH)

        for i in range(n):
            for k in range(N_CHIPS):
                @pl.when(k != me)
                def _():
                    fwd(i, k, c).start()
        for i in range(n):
            for k in range(N_CHIPS):
                @pl.when(k != me)
                def _():
                    fwd(i, k, 1 - c).wait_recv()
        for i in range(n):
            for k in range(N_CHIPS):
                @pl.when(k != me)
                def _():
                    fwd(i, k, c).wait_send()

    return pl.pallas_call(
        body, in_specs=[_HBM] * n, out_specs=[_HBM] * n,
        out_shape=[jax.ShapeDtypeStruct(b.shape, b.dtype) for b in bufs],
        input_output_aliases={i: i for i in range(n)},
        scratch_shapes=[pltpu.SemaphoreType.DMA((n, N_CHIPS))] * 2,
        name=name)(*bufs)


def _swap_halves(name, grads):
    n = len(grads)
    dims = [_shard_dims(g) for g in grads]

    def body(*refs):
        g_refs, o_refs, send_sems, recv_sems = refs[:n], refs[n:2 * n], refs[2 * n], refs[2 * n + 1]
        x, y, c = lax.axis_index("x"), lax.axis_index("y"), lax.axis_index("c")
        copies = []
        for i in range(n):
            r, cw = dims[i]
            for k in range(N_CHIPS):
                shard = g_refs[i].at[k] if grads[i].ndim == 3 else g_refs[i].at[:, pl.ds(k * cw, cw)]
                copies.append(pltpu.make_async_remote_copy(
                    src_ref=_half_rows(shard, 1 - c, r // 2), dst_ref=o_refs[i].at[k],
                    send_sem=send_sems.at[i, k], recv_sem=recv_sems.at[i, k],
                    device_id=(x, y, 1 - c), device_id_type=MESH))
        for cp in copies:
            cp.start()
        for cp in copies:
            cp.wait()

    return pl.pallas_call(
        body, in_specs=[_HBM] * n, out_specs=[_HBM] * n,
        out_shape=[jax.ShapeDtypeStruct((N_CHIPS, r // 2, cw), g.dtype) for (r, cw), g in zip(dims, grads)],
        scratch_shapes=[pltpu.SemaphoreType.DMA((n, N_CHIPS))] * 2,
        name=name)(*grads)


def _scatter_start(name, parts):
    def copies_of(i, k, src, land, me, x, y, c):
        kx, ky = _chip_xy(k)
        return src.at[k], land.at[me], (kx, ky, c)

    return _push_start(name, parts, [(p.shape, p.dtype) for p in parts], copies_of)


def _scatter_wait(name, started, after):
    return _push_wait(name, started, after, lambda i, k, src, land, me, c: (src.at[k], land.at[k]))


def _join_halves(bufs):
    n = len(bufs)

    def body(*refs):
        i_refs, o_refs, send_sems, recv_sems = refs[:n], refs[n:2 * n], refs[2 * n], refs[2 * n + 1]
        x, y, c = lax.axis_index("x"), lax.axis_index("y"), lax.axis_index("c")
        copies = []
        for i in range(n):
            rh = bufs[i].shape[0] // 2
            copies.append(pltpu.make_async_remote_copy(
                src_ref=_half_rows(i_refs[i], c, rh), dst_ref=_half_rows(o_refs[i], c, rh),
                send_sem=send_sems.at[i], recv_sem=recv_sems.at[i],
                device_id=(x, y, 1 - c), device_id_type=MESH))
        for cp in copies:
            cp.start()
        for i in range(n):
            rh = bufs[i].shape[0] // 2
            pltpu.make_async_remote_copy(
                src_ref=_half_rows(i_refs[i], c, rh), dst_ref=_half_rows(o_refs[i], 1 - c, rh),
                send_sem=send_sems.at[i], recv_sem=recv_sems.at[i],
                device_id=(x, y, 1 - c), device_id_type=MESH).wait_recv()
        for cp in copies:
            cp.wait_send()

    return pl.pallas_call(
        body, in_specs=[_HBM] * n, out_specs=[_HBM] * n,
        out_shape=[jax.ShapeDtypeStruct(b.shape, F32) for b in bufs],
        input_output_aliases={i: i for i in range(n)},
        scratch_shapes=[pltpu.SemaphoreType.DMA((n,))] * 2,
        name="join_halves")(*bufs)


def _all_gather_small(name, blk):
    m_per, n = blk.shape

    def body(x_ref, out_ref, send_sems, recv_sems, local_sem):
        x, y, c = lax.axis_index("x"), lax.axis_index("y"), lax.axis_index("c")
        me, sibling = (x, y, c), (x, y, 1 - c)
        chips = [(1 - x, y), (x, 1 - y), (1 - x, 1 - y)]

        def rows(px, py, pc):
            return out_ref.at[pl.ds((4 * px + 2 * py + pc) * m_per, m_per), :]

        def copy(k, block, to, src=None):
            return pltpu.make_async_remote_copy(
                src_ref=rows(*block) if src is None else src, dst_ref=rows(*block),
                send_sem=send_sems.at[k], recv_sem=recv_sems.at[k],
                device_id=to, device_id_type=MESH)

        mine = pltpu.make_async_copy(x_ref, rows(*me), local_sem)
        mine.start()
        first = [copy(0, me, sibling, src=x_ref)]
        first += [copy(1 + j, me, (*chip, c), src=x_ref) for j, chip in enumerate(chips)]
        for cp in first:
            cp.start()
        passed = [copy(4 + j, (*chip, c), sibling) for j, chip in enumerate(chips)]
        for j, chip in enumerate(chips):
            copy(1 + j, (*chip, c), me).wait_recv()
            passed[j].start()
        copy(0, sibling, me).wait_recv()
        for j, chip in enumerate(chips):
            copy(4 + j, (*chip, 1 - c), me).wait_recv()
        for cp in first + passed:
            cp.wait_send()
        mine.wait()

    return pl.pallas_call(
        body, out_shape=jax.ShapeDtypeStruct((N_DEV * m_per, n), blk.dtype),
        in_specs=[pl.BlockSpec(memory_space=pltpu.VMEM)],
        out_specs=pl.BlockSpec(memory_space=pltpu.VMEM),
        scratch_shapes=[pltpu.SemaphoreType.DMA((7,)), pltpu.SemaphoreType.DMA((7,)), pltpu.SemaphoreType.DMA],
        name=name)(blk)


def _pack_rows(vecs, width):
    parts = []
    for v in vecs:
        f = v.reshape(-1)
        pad = (-f.shape[0]) % (8 * width)
        parts.append(jnp.pad(f, (0, pad)) if pad else f)
    return jnp.concatenate(parts).reshape(-1, width)


def _unpack_rows(packed, shapes, width):
    flat = packed.reshape(-1)
    out, off = [], 0
    for s in shapes:
        n = math.prod(s)
        out.append(flat[off:off + n].reshape(s))
        off += n + ((-n) % (8 * width))
    return out


class _WinPlan:
    def __init__(self, ncol, dt0, h, dmain):
        self.ncol, self.h = ncol, h
        self.dt_shard = dt0 // ncol
        assert (dt0 + h - 1) // ncol == self.dt_shard and dmain % LANES == 0
        self.dt_local = dt0 - self.dt_shard * ncol
        to_main = lambda g: g if g <= dt0 else g - h
        self.lo = [to_main(ncol * k) for k in range(N_CHIPS)]
        self.hi = [to_main(ncol * (k + 1)) for k in range(N_CHIPS)]
        down = lambda v: v // LANES * LANES
        self.ww = max(-(-(hi - down(lo)) // LANES) * LANES for lo, hi in zip(self.lo, self.hi))
        self.ws = [min(down(lo), dmain - self.ww) for lo in self.lo]
        self.dmain = dmain

    def to_window(self, k, shard):
        if k == self.dt_shard:
            shard = jnp.concatenate([shard[:, :self.dt_local], shard[:, self.dt_local + self.h:]], axis=1)
        left = self.lo[k] - self.ws[k]
        return jnp.pad(shard, ((0, 0), (left, self.ww - left - shard.shape[1])))

    def from_window(self, k, window, dt_cols):
        left = self.lo[k] - self.ws[k]
        body = window[:, left:left + self.hi[k] - self.lo[k]]
        if k == self.dt_shard:
            body = jnp.concatenate([body[:, :self.dt_local], dt_cols, body[:, self.dt_local:]], axis=1)
        return body

    def merge(self, windows):
        cuts = sorted({0, self.dmain} | set(self.ws) | {w + self.ww for w in self.ws})
        segs = []
        for a, b in zip(cuts[:-1], cuts[1:]):
            parts = [windows[k][:, a - self.ws[k]:b - self.ws[k]] for k in range(N_CHIPS)
                     if self.ws[k] <= a and b <= self.ws[k] + self.ww]
            segs.append(functools.reduce(jnp.add, parts))
        return jnp.concatenate(segs, axis=1)

    def split(self, g_main):
        return jnp.stack([g_main[:, w:w + self.ww] for w in self.ws])


def kernel(x, attn_norm_w, w_in, conv_w, conv_b, dt_bias, a_log, d_skip, ssd_norm_w, pool_w, pool_scale, w_out, ffn_norm_w, w_gate, w_up, w_down, final_norm_w, loss_target, m_attn_norm_w, m_w_in, m_conv_w, m_conv_b, m_dt_bias, m_a_log, m_d_skip, m_ssd_norm_w, m_pool_w, m_pool_scale, m_w_out, m_ffn_norm_w, m_w_gate, m_w_up, m_w_down, m_final_norm_w, v_attn_norm_w, v_w_in, v_conv_w, v_conv_b, v_dt_bias, v_a_log, v_d_skip, v_ssd_norm_w, v_pool_w, v_pool_scale, v_w_out, v_ffn_norm_w, v_w_gate, v_w_up, v_w_down, v_final_norm_w):
    G, P, PG = SSD_GROUPS, HEAD_DIM, len(POOL_WINDOWS)
    _, L, D = x.shape
    H = a_log.shape[1]
    R = H // G
    DS = H * P
    DCONV = conv_b.shape[1]
    N = (DCONV - DS) // (2 * G)
    DP = pool_scale.shape[1]
    PGD = DP // PG
    DIN = N_CHIPS * w_in.shape[2]
    DFF = N_CHIPS * w_gate.shape[2]
    DMAIN = DS + DCONV + DP
    assert DIN == DMAIN + H and DS == DP and H <= LANES

    cx, cy, cc = lax.axis_index("x"), lax.axis_index("y"), lax.axis_index("c")
    chip = 2 * cx + cy

    win = _WinPlan(DIN // N_CHIPS, DS + DCONV, H, DMAIN)
    my_window = lax.switch(chip, [functools.partial(win.to_window, k) for k in range(N_CHIPS)], w_in[0].astype(BF16))
    started_in = _gather_start("gather_start_in", [my_window])

    def forward_gathered(tag, shards, landed):
        landed = _forward_halves("gather_forward_" + tag, landed)
        return [lax.dynamic_update_slice(g, s[None], (chip, 0, 0)) for g, s in zip(landed, shards)]

    def cols(p):
        return jnp.moveaxis(p, 0, -2).reshape(p.shape[1:-1] + (N_CHIPS * p.shape[-1],))

    ncw = CONV_WIDTH * DCONV // N_CHIPS
    dt_here = jnp.where(chip == win.dt_shard, w_in[0][:, win.dt_local:win.dt_local + H], 0.0)
    start_blk = _pack_rows([conv_w[0], dt_here], LANES)
    start_all = _all_gather_small("gather_conv_w", start_blk).reshape(N_CHIPS, 2, -1)[:, 0]
    conv_w_f = cols(start_all[:, :ncw].reshape(N_CHIPS, CONV_WIDTH, DCONV // N_CHIPS))
    dt_off = ncw + (-ncw) % (8 * LANES)
    w_dt = jnp.pad(start_all[win.dt_shard, dt_off:dt_off + D * H].reshape(D, H), ((0, 0), (0, LANES - H))).astype(BF16)

    xl, tgt = x[0], loss_target[0]
    tm_row = _tile(L, 256, HALO)
    tm_mm = _tile(L, 1024, 16)
    hn1 = _rms_fwd("rms1_fwd", xl, attn_norm_w, tm_row)
    shards_in, landed_in, landed_token = _gather_wait("gather_wait_in", started_in, hn1)
    shards_rest = [(pool_w[0].reshape(PG * PGD // N_CHIPS, PGD) + landed_token[0, 0]).astype(BF16),
                   w_out[0].astype(BF16), w_gate[0].astype(BF16), w_up[0].astype(BF16), w_down[0].astype(BF16)]
    started_rest = _gather_start("gather_start_rest", shards_rest)
    pin_row = lambda started, n: jnp.zeros((1, n), F32) + started[4][0, 0]
    add_row = lambda accs, ex, rex: [accs[0] + rex[0]]
    w_main = win.merge(forward_gathered("in", shards_in, landed_in)[0])
    proj, = _mm("proj_main", "nn", [(hn1, w_main)], L, DMAIN, D, tm_mm, 512, D, [F32],
                epilogue=add_row, row_extras=[pin_row(started_rest, DMAIN)])
    dt_raw, = _mm("proj_dt", "nn", [(hn1, w_dt)], L, LANES, D, tm_mm, LANES, D, [F32])

    cwid = _tile(math.gcd(DS, DCONV), 512, LANES)
    tm_conv = _tile(L, 1024, HALO)
    xbc = _conv_fwd(proj, conv_w_f, conv_b, DS, DCONV, tm_conv, cwid)

    dt_g = dt_raw[:, :H].reshape(L, G, R)
    dtc_raw = jnp.transpose(dt_g, (1, 0, 2))
    dtr_raw = jnp.transpose(dt_g, (1, 2, 0))
    as_c = lambda v: v.reshape(G, 1, R)
    as_r = lambda v: v.reshape(G, R, 1)
    ssd_args = (xbc, dtc_raw, dtr_raw, as_c(dt_bias), as_r(dt_bias), as_c(a_log), as_r(a_log), as_c(d_skip))
    y_ssd_raw, states = _ssd_fwd(*ssd_args, DS, N)
    y_ssd = _gated_fwd(y_ssd_raw, proj, ssd_norm_w, DS, tm_conv)
    gathered = forward_gathered("rest", *_gather_wait("gather_wait_rest", started_rest, y_ssd)[:2])
    pool_w_f = jnp.moveaxis(gathered[0].reshape(N_CHIPS, PG, PGD // N_CHIPS, PGD), 0, 1).reshape(PG, PGD, PGD)
    w_out_f = gathered[1].reshape(2 * DS, D)
    w_gate_f, w_up_f = cols(gathered[2]), cols(gathered[3])
    w_down_f = gathered[4].reshape(DFF, D)
    w_out_top, w_out_bot = w_out_f[:DS], w_out_f[DS:]
    pooled, y_pool = _pool_fwd(proj, pool_w_f, pool_scale, DS + DCONV, DP, tm_conv)

    add_res = lambda accs, ex, rex: [accs[0] + ex[0]]
    h1, = _mm("out_proj", "nn", [(y_ssd, w_out_top), (y_pool, w_out_bot)], L, D, DS, tm_mm, 512, DS, [F32],
              epilogue=add_res, extras=[xl])
    hn2 = _rms_fwd("rms2_fwd", h1, ffn_norm_w, tm_row)

    def glu(accs, ex, rex):
        return [accs[0], accs[1], (_silu(accs[0]) * accs[1])]

    tn_ff = _tile(DFF, 512, LANES)
    gate, up, act = _mm("ffn_in", "nn", [(hn2, w_gate_f), (hn2, w_up_f)], L, DFF, D, tm_mm, tn_ff, D,
                        [F32, F32, BF16], epilogue=glu, separate=True)
    tk_ff = _tile(DFF, DFF // 2, LANES)
    h2, = _mm("ffn_out", "nn", [(act, w_down_f)], L, D, DFF, tm_mm, 512, tk_ff, [F32], epilogue=add_res, extras=[h1])
    dh2, dh2_16, loss_blk, g_final = _final_loss(h2, final_norm_w.reshape(1, D), tgt, tm_row)

    def dglu(accs, ex, rex):
        gt, u = ex
        sg = _sigmoid(gt)
        return [accs[0] * u * (sg * (1.0 + gt * (1.0 - sg))), accs[0] * (gt * sg)]

    dgate, dup = _mm("ffn_out_dx", "nt", [(dh2_16, w_down_f)], L, DFF, D, tm_mm, tn_ff, D, [BF16, BF16],
                     epilogue=dglu, extras=[gate, up])
    tk_tok = _tile(L, 2048, 16)
    twice = lambda accs, ex, rex: list(accs) + list(accs)
    g_w_down, g_w_down16 = _mm("ffn_out_dw", "tn", [(act, dh2_16)], DFF, D, L, _tile(DFF, 1536, LANES), 1024, tk_tok,
                               [F32, BF16], epilogue=twice)
    g_w_gate, g_w_up, g_w_gate16, g_w_up16 = _mm("ffn_in_dw", "tn", [(hn2, dgate), (hn2, dup)], D, DFF, L, 1024, tn_ff,
                                                 tk_tok, [F32, F32, BF16, BF16], epilogue=twice, separate=True)

    pos = jnp.stack([chip, cc]).astype(jnp.int32)

    def start_reduce(tag, names, full_grads, full_grads16):
        from_sibling = _swap_halves("swap_halves_" + tag, full_grads16)
        partials = [_pair_sum("pair_sum_" + n, g, r, pos) for n, g, r in zip(names, full_grads, from_sibling)]
        return _scatter_start("scatter_start_" + tag, partials)

    names_ffn = ["w_gate", "w_up", "w_down"]
    started_ffn = start_reduce("ffn", names_ffn, [g_w_gate, g_w_up, g_w_down.reshape(N_CHIPS, -1, D)],
                               [g_w_gate16, g_w_up16, g_w_down16.reshape(N_CHIPS, -1, D)])
    dhn2, = _mm("ffn_in_dx", "nt", [(dgate, w_gate_f), (dup, w_up_f)], L, D, DFF, tm_mm, 512,
                _tile(DFF, DFF // 4, LANES), [F32], epilogue=add_row, row_extras=[pin_row(started_ffn, D)])
    dh1, g_ffn_norm, dh1_16 = _rms_bwd("rms2_bwd", h1, ffn_norm_w, [dhn2], dh2, tm_row, True)

    dy_ssd, dy_pool = _mm("out_proj_dx", "nt", [(dh1_16, w_out_top), (dh1_16, w_out_bot)], L, DS, D, tm_mm, 512, D,
                          [F32, F32], separate=True)
    g_w_out_top, g_w_out_bot, g_w_out_top16, g_w_out_bot16 = _mm(
        "out_proj_dw", "tn", [(y_ssd, dh1_16), (y_pool, dh1_16)], DS, D, L, 1024, 512, tk_tok, [F32, F32, BF16, BF16],
        epilogue=twice, separate=True)
    dy_raw, dproj, g_ssd_norm = _gated_bwd(y_ssd_raw, proj, ssd_norm_w, dy_ssd, DS, DMAIN, tm_conv)
    dxs, db, dc, ddt_raw, g_a_log, g_d_skip, g_dt_bias = _ssd_bwd(*ssd_args, dy_raw, states, DS, N)
    g_conv_w, g_conv_b = [], []
    for tag, dact, first in (("xs", dxs, 0), ("b", db, DS), ("c", dc, DS + G * N)):
        dproj, gw, gb = _conv_bwd("conv_bwd_" + tag, proj, dact, conv_w_f, conv_b, dproj, DS, first, tm_conv, cwid)
        g_conv_w.append(gw)
        g_conv_b.append(gb)
    g_conv_w, g_conv_b = jnp.concatenate(g_conv_w, axis=1), jnp.concatenate(g_conv_b, axis=1)
    dproj, g_pool_w, g_pool_scale = _pool_bwd(dy_pool, pooled, pool_w_f, pool_scale, dproj, DS + DCONV, tm_conv)
    ddt_pad = jnp.pad(jnp.transpose(ddt_raw, (1, 0, 2)).reshape(L, H), ((0, 0), (0, LANES - H))).astype(BF16)

    tk_main = _tile(DMAIN, DMAIN // 2, LANES)
    g_w_main, g_w_main16 = _mm("proj_main_dw", "tn", [(hn1, dproj)], D, DMAIN, L, 1024, _tile(DMAIN, 1024, LANES),
                               tk_tok, [F32, BF16], epilogue=twice)
    names_mix = ["w_in", "pool_w", "w_out"]
    pool_shards = jnp.moveaxis(g_pool_w.reshape(PG, N_CHIPS, PGD // N_CHIPS, PGD), 1, 0).reshape(N_CHIPS, -1, PGD)
    out_shards = lambda top, bot: jnp.stack([top.reshape(2, DS // 2, D), bot.reshape(2, DS // 2, D)]).reshape(N_CHIPS, -1, D)
    started_mix = start_reduce(
        "mix", names_mix, [win.split(g_w_main), pool_shards, out_shards(g_w_out_top, g_w_out_bot)],
        [win.split(g_w_main16), pool_shards.astype(BF16), out_shards(g_w_out_top16, g_w_out_bot16)])
    dhn1a, = _mm("proj_main_dx", "nt", [(dproj, w_main)], L, D, DMAIN, tm_mm, 512, tk_main, [F32],
                 epilogue=add_row, row_extras=[pin_row(started_mix, D)])
    dhn1b, = _mm("proj_dt_dx", "nt", [(ddt_pad, w_dt)], L, D, LANES, tm_mm, 512, LANES, [F32])
    g_w_dt, = _mm("proj_dt_dw", "tn", [(hn1, ddt_pad)], D, LANES, L, 512, LANES, tk_tok, [F32])
    grad_x, g_attn_norm = _rms_bwd("rms1_bwd", xl, attn_norm_w, [dhn1a, dhn1b], dh1, tm_row, False)

    def finish_reduce(tag, names, started, after):
        partials, landed, _ = _scatter_wait("scatter_wait_" + tag, started, after)
        landed = [lax.dynamic_update_slice(l, lax.dynamic_index_in_dim(p, chip, 0), (chip, 0, 0))
                  for l, p in zip(landed, partials)]
        return [_chip_sum("chip_sum_" + n, l, pos) for n, l in zip(names, landed)]

    halves = finish_reduce("ffn", names_ffn, started_ffn, grad_x) + finish_reduce("mix", names_mix, started_mix, grad_x)
    red = dict(zip(names_ffn + names_mix, _join_halves(halves)))

    small_w = [attn_norm_w, conv_b, dt_bias, a_log, d_skip, ssd_norm_w, pool_scale, ffn_norm_w, final_norm_w]
    small_m = [m_attn_norm_w, m_conv_b, m_dt_bias, m_a_log, m_d_skip, m_ssd_norm_w, m_pool_scale, m_ffn_norm_w, m_final_norm_w]
    small_v = [v_attn_norm_w, v_conv_b, v_dt_bias, v_a_log, v_d_skip, v_ssd_norm_w, v_pool_scale, v_ffn_norm_w, v_final_norm_w]
    small_g = [g_attn_norm, g_conv_b, g_dt_bias.reshape(1, H), g_a_log.reshape(1, H), g_d_skip.reshape(1, H),
               g_ssd_norm, g_pool_scale, g_ffn_norm, g_final.reshape(D)]
    extra_shapes = [(CONV_WIDTH, DCONV), (D, H), (1, LANES)]
    zeros_like_extra = [jnp.zeros(s, F32) for s in extra_shapes]
    g_blk = _pack_rows(small_g + [g_conv_w, g_w_dt[:, :H], loss_blk], LANES)
    rows = g_blk.shape[0]
    small_all = _all_gather_small("gather_small_grads", g_blk)
    s_g, s_d, s_m, s_v = _small_sum_adam(small_all, _pack_rows(small_w + zeros_like_extra, LANES),
                                         _pack_rows(small_m + zeros_like_extra, LANES),
                                         _pack_rows(small_v + zeros_like_extra, LANES), rows)
    shapes = [w.shape for w in small_w] + extra_shapes
    sg_list = _unpack_rows(s_g, shapes, LANES)
    sd_list = _unpack_rows(s_d, shapes, LANES)[:len(small_w)]
    sm_list = _unpack_rows(s_m, shapes, LANES)[:len(small_w)]
    sv_list = _unpack_rows(s_v, shapes, LANES)[:len(small_w)]
    loss = sg_list[-1][0, 0]
    grad_conv_w = lax.dynamic_slice(sg_list[-3], (0, chip * (DCONV // N_CHIPS)), (CONV_WIDTH, DCONV // N_CHIPS))
    grad_w_in = lax.switch(chip, [functools.partial(win.from_window, k) for k in range(N_CHIPS)], red["w_in"], sg_list[-2])

    def adam_nd(name, w, g, m, v):
        shp = w.shape
        to2 = lambda a: a.reshape(-1, shp[-1])
        d, m2, v2 = _adam(name, to2(w), to2(g), to2(m), to2(v))
        return d.reshape(shp), m2.reshape(shp), v2.reshape(shp)

    sharded = {
        "w_in": (w_in, grad_w_in[None], m_w_in, v_w_in),
        "conv_w": (conv_w, grad_conv_w[None], m_conv_w, v_conv_w),
        "pool_w": (pool_w, red["pool_w"].reshape(pool_w.shape), m_pool_w, v_pool_w),
        "w_out": (w_out, red["w_out"][None], m_w_out, v_w_out),
        "w_gate": (w_gate, red["w_gate"][None], m_w_gate, v_w_gate),
        "w_up": (w_up, red["w_up"][None], m_w_up, v_w_up),
        "w_down": (w_down, red["w_down"][None], m_w_down, v_w_down),
    }
    upd = {n: (a[1],) + adam_nd("adam_" + n, *a) for n, a in sharded.items()}
    small_names = ["attn_norm_w", "conv_b", "dt_bias", "a_log", "d_skip", "ssd_norm_w", "pool_scale", "ffn_norm_w",
                   "final_norm_w"]
    for i, n in enumerate(small_names):
        upd[n] = (sg_list[i], sd_list[i], sm_list[i], sv_list[i])

    order = ["attn_norm_w", "w_in", "conv_w", "conv_b", "dt_bias", "a_log", "d_skip", "ssd_norm_w", "pool_w",
             "pool_scale", "w_out", "ffn_norm_w", "w_gate", "w_up", "w_down", "final_norm_w"]
    outs = [loss, grad_x[None]]
    for j in range(4):
        outs += [upd[n][j] for n in order]
    return tuple(outs)
```

```python
import functools
import math

import jax
import jax.numpy as jnp
from jax import lax
from jax.experimental import pallas as pl
from jax.experimental.pallas import tpu as pltpu

F32 = jnp.float32
BF16 = jnp.bfloat16

NORM_EPS = 1e-5
HEAD_DIM = 64
SSD_GROUPS = 4
CONV_WIDTH = 4
CHUNK = 256
POOL_WINDOWS = (2, 4, 8, 16)
ADAM_LR = 0.001
ADAM_B1 = 0.9
ADAM_B2 = 0.999
ADAM_EPS = 1e-08
ADAM_WD = 0.01
ADAM_STEP = 10

N_CHIPS = 4
N_DEV = 8
LANES = 128
HALO = 16
FLAT_W = 512
VMEM_LIMIT = 52 * 1024 * 1024
MESH = pl.DeviceIdType.MESH

NN = (((1,), (0,)), ((), ()))
NT = (((1,), (1,)), ((), ()))
TN = (((0,), (0,)), ((), ()))


def _tile(n, cap, mult):
    best = None
    for t in range(mult, min(n, cap) + 1, mult):
        if n % t == 0:
            best = t
    return best if best is not None else n


def _params(sem):
    return pltpu.CompilerParams(dimension_semantics=sem, vmem_limit_bytes=VMEM_LIMIT)


def _dot(a, b, dims):
    return lax.dot_general(a, b, dims, preferred_element_type=F32)


def _sigmoid(x):
    return 1.0 / (1.0 + jnp.exp(-x))


def _silu(x):
    return x * _sigmoid(x)


def _softplus(x):
    return jnp.maximum(x, 0.0) + jnp.log(1.0 + jnp.exp(-jnp.abs(x)))


def _mm(name, mode, pairs, M, N, K, tm, tn, tk, out_dtypes, epilogue=None, extras=(), row_extras=(),
        separate=False):
    tm, tn, tk = min(tm, M), min(tn, N), min(tk, K)
    assert M % tm == 0 and N % tn == 0 and K % tk == 0, (name, M, N, K, tm, tn, tk)
    nk = K // tk
    npairs = len(pairs)
    nacc = npairs if separate else 1
    if mode == "nn":
        a_spec = pl.BlockSpec((tm, tk), lambda i, j, k: (i, k))
        b_spec = pl.BlockSpec((tk, tn), lambda i, j, k: (k, j))
        dims = NN
    elif mode == "nt":
        a_spec = pl.BlockSpec((tm, tk), lambda i, j, k: (i, k))
        b_spec = pl.BlockSpec((tn, tk), lambda i, j, k: (j, k))
        dims = NT
    else:
        a_spec = pl.BlockSpec((tk, tm), lambda i, j, k: (k, i))
        b_spec = pl.BlockSpec((tk, tn), lambda i, j, k: (k, j))
        dims = TN
    o_spec = pl.BlockSpec((tm, tn), lambda i, j, k: (i, j))
    r_spec = pl.BlockSpec((1, tn), lambda i, j, k: (0, j))
    if epilogue is None:
        epilogue = lambda accs, ex, rex: accs
    n_ex, n_rex, n_out = len(extras), len(row_extras), len(out_dtypes)

    def body(*refs):
        ab = refs[:2 * npairs]
        ex = refs[2 * npairs:2 * npairs + n_ex]
        rex = refs[2 * npairs + n_ex:2 * npairs + n_ex + n_rex]
        outs = refs[2 * npairs + n_ex + n_rex:2 * npairs + n_ex + n_rex + n_out]
        accs = refs[2 * npairs + n_ex + n_rex + n_out:]

        def products():
            res = [None] * nacc
            for p in range(npairs):
                d = _dot(ab[2 * p][...], ab[2 * p + 1][...], dims)
                q = p if separate else 0
                res[q] = d if res[q] is None else res[q] + d
            return res

        def finish(vals):
            res = epilogue(vals, [e[...] for e in ex], [r[...] for r in rex])
            for o, v in zip(outs, res):
                o[...] = v.astype(o.dtype)

        if nk == 1:
            finish(products())
        else:
            k = pl.program_id(2)

            @pl.when(k == 0)
            def _():
                for q in range(nacc):
                    accs[q][...] = jnp.zeros_like(accs[q])

            for p in range(npairs):
                accs[p if separate else 0][...] += _dot(ab[2 * p][...], ab[2 * p + 1][...], dims)

            @pl.when(k == nk - 1)
            def _():
                finish([a[...] for a in accs])

    in_specs = [a_spec, b_spec] * npairs + [o_spec] * n_ex + [r_spec] * n_rex
    args = [t for p in pairs for t in p] + list(extras) + list(row_extras)
    outs = pl.pallas_call(
        body,
        grid=(M // tm, N // tn, nk),
        in_specs=in_specs,
        out_specs=[o_spec] * n_out,
        out_shape=[jax.ShapeDtypeStruct((M, N), d) for d in out_dtypes],
        scratch_shapes=[pltpu.VMEM((tm, tn), F32) for _ in range(nacc if nk > 1 else 0)],
        compiler_params=_params(("parallel", "parallel", "arbitrary")),
        name=name,
    )(*args)
    return outs


def _rms(xf, w):
    y = xf * lax.rsqrt(jnp.mean(xf * xf, axis=-1, keepdims=True) + NORM_EPS)
    return y * w


def _rms_fwd(name, x, w, tm):
    L, D = x.shape

    def body(x_ref, w_ref, o_ref):
        o_ref[...] = _rms(x_ref[...], w_ref[...]).astype(BF16)

    return pl.pallas_call(
        body, grid=(L // tm,),
        in_specs=[pl.BlockSpec((tm, D), lambda i: (i, 0)), pl.BlockSpec((1, D), lambda i: (0, 0))],
        out_specs=pl.BlockSpec((tm, D), lambda i: (i, 0)),
        out_shape=jax.ShapeDtypeStruct((L, D), BF16),
        compiler_params=_params(("parallel",)), name=name)(x, w)


def _rms_bwd(name, x, w, dparts, dres, tm, with_bf16):
    L, D = x.shape
    nparts = len(dparts)

    def body(*refs):
        x_ref, w_ref = refs[:2]
        p_refs = refs[2:2 + nparts]
        r_ref = refs[2 + nparts]
        outs = refs[3 + nparts:]
        dhn = p_refs[0][...]
        for p in p_refs[1:]:
            dhn = dhn + p[...]
        _, vjp = jax.vjp(_rms, x_ref[...], w_ref[...])
        dx, dw = vjp(dhn)
        dx = dx + r_ref[...]
        outs[0][...] = dx
        gw_ref = outs[1]

        @pl.when(pl.program_id(0) == 0)
        def _():
            gw_ref[...] = jnp.zeros_like(gw_ref)

        gw_ref[...] += dw
        if with_bf16:
            outs[2][...] = dx.astype(BF16)

    row = pl.BlockSpec((tm, D), lambda i: (i, 0))
    vec = pl.BlockSpec((1, D), lambda i: (0, 0))
    out_shape = [jax.ShapeDtypeStruct((L, D), F32), jax.ShapeDtypeStruct((1, D), F32)]
    out_specs = [row, vec]
    if with_bf16:
        out_shape.append(jax.ShapeDtypeStruct((L, D), BF16))
        out_specs.append(row)
    return pl.pallas_call(
        body, grid=(L // tm,),
        in_specs=[row, vec] + [row] * nparts + [row],
        out_specs=out_specs, out_shape=out_shape,
        compiler_params=_params(("arbitrary",)), name=name)(x, w, *dparts, dres)


def _final_loss(h2, wf, target, tm):
    L, D = h2.shape

    def body(h_ref, w_ref, t_ref, dh_ref, dhb_ref, loss_ref, gw_ref):
        t = t_ref[...]

        def f(h, w):
            err = jnp.square(_rms(h, w) - t)
            return 0.5 * jnp.sum(jnp.mean(err, axis=-1))

        val, vjp = jax.vjp(f, h_ref[...], w_ref[...])
        dh, dw = vjp(jnp.ones((), F32))
        dh_ref[...] = dh
        dhb_ref[...] = dh.astype(BF16)

        @pl.when(pl.program_id(0) == 0)
        def _():
            gw_ref[...] = jnp.zeros_like(gw_ref)
            loss_ref[...] = jnp.zeros_like(loss_ref)

        gw_ref[...] += dw
        loss_ref[...] += jnp.full(loss_ref.shape, val, F32)

    row = pl.BlockSpec((tm, D), lambda i: (i, 0))
    vec = pl.BlockSpec((1, D), lambda i: (0, 0))
    lspec = pl.BlockSpec((1, LANES), lambda i: (0, 0))
    return pl.pallas_call(
        body, grid=(L // tm,),
        in_specs=[row, vec, row],
        out_specs=[row, row, lspec, vec],
        out_shape=[jax.ShapeDtypeStruct((L, D), F32), jax.ShapeDtypeStruct((L, D), BF16),
                   jax.ShapeDtypeStruct((1, LANES), F32), jax.ShapeDtypeStruct((1, D), F32)],
        compiler_params=_params(("arbitrary",)), name="final_loss")(h2, wf, target)


def _gated(y, z, w):
    g = y * _silu(z)
    g = g * lax.rsqrt(jnp.mean(g * g, axis=-1, keepdims=True) + NORM_EPS)
    return g * w


def _gated_fwd(y, proj, w, DS, tm):
    L = y.shape[0]
    GW = DS // SSD_GROUPS

    def body(y_ref, z_ref, w_ref, o_ref):
        o_ref[...] = _gated(y_ref[...], z_ref[...], w_ref[...]).astype(BF16)

    blk = pl.BlockSpec((tm, GW), lambda i, g: (i, g))
    return pl.pallas_call(
        body, grid=(L // tm, SSD_GROUPS),
        in_specs=[blk, blk, pl.BlockSpec((1, GW), lambda i, g: (0, g))],
        out_specs=blk, out_shape=jax.ShapeDtypeStruct((L, DS), BF16),
        compiler_params=_params(("parallel", "parallel")), name="gated_fwd")(y, proj, w)


def _gated_bwd(y, proj, w, dout, DS, dproj_cols, tm):
    L = y.shape[0]
    GW = DS // SSD_GROUPS

    def body(y_ref, z_ref, w_ref, d_ref, dy_ref, dz_ref, gw_ref):
        _, vjp = jax.vjp(_gated, y_ref[...], z_ref[...], w_ref[...])
        dy, dz, dw = vjp(d_ref[...])
        dy_ref[...] = dy
        dz_ref[...] = dz.astype(BF16)

        @pl.when(pl.program_id(1) == 0)
        def _():
            gw_ref[...] = jnp.zeros_like(gw_ref)

        gw_ref[...] += dw

    blk = pl.BlockSpec((tm, GW), lambda g, i: (i, g))
    vec = pl.BlockSpec((1, GW), lambda g, i: (0, g))
    return pl.pallas_call(
        body, grid=(SSD_GROUPS, L // tm),
        in_specs=[blk, blk, vec, blk],
        out_specs=[blk, blk, vec],
        out_shape=[jax.ShapeDtypeStruct((L, DS), F32), jax.ShapeDtypeStruct((L, dproj_cols), BF16),
                   jax.ShapeDtypeStruct((1, DS), F32)],
        compiler_params=_params(("parallel", "arbitrary")), name="gated_bwd")(y, proj, w, dout)


def _halo_prev(tm, cw, col0):
    return pl.BlockSpec((HALO, cw), lambda i, j: (jnp.maximum(i * (tm // HALO) - 1, 0), col0 + j))


def _halo_next(tm, cw, col0, L):
    return pl.BlockSpec((HALO, cw), lambda i, j: (jnp.minimum((i + 1) * (tm // HALO), L // HALO - 1), col0 + j))


def _conv_fwd(proj, conv_w, conv_b, DS, DCONV, tm, cw):
    L = proj.shape[0]
    col0 = DS // cw
    K = CONV_WIDTH

    def body(x_ref, p_ref, w_ref, b_ref, o_ref, ext):
        i = pl.program_id(0)
        ext[0:HALO, :] = jnp.where(i == 0, 0.0, p_ref[...])
        ext[HALO:, :] = x_ref[...]
        acc = jnp.broadcast_to(b_ref[...], (tm, cw))
        for k in range(K):
            acc = acc + w_ref[k:k + 1, :] * ext[pl.ds(HALO - (K - 1) + k, tm), :]
        o_ref[...] = _silu(acc)

    return pl.pallas_call(
        body, grid=(L // tm, DCONV // cw),
        in_specs=[pl.BlockSpec((tm, cw), lambda i, j: (i, col0 + j)), _halo_prev(tm, cw, col0),
                  pl.BlockSpec((K, cw), lambda i, j: (0, j)), pl.BlockSpec((1, cw), lambda i, j: (0, j))],
        out_specs=pl.BlockSpec((tm, cw), lambda i, j: (i, j)),
        out_shape=jax.ShapeDtypeStruct((L, DCONV), F32),
        scratch_shapes=[pltpu.VMEM((tm + HALO, cw), F32)],
        compiler_params=_params(("parallel", "parallel")), name="conv_fwd")(proj, proj, conv_w, conv_b)


def _conv_bwd(name, proj, dact, conv_w, conv_b, dproj, DS, first, tm, cw):
    L = proj.shape[0]
    ncols = dact.shape[1]
    col0 = (DS + first) // cw
    wcol0 = first // cw
    K = CONV_WIDTH
    nrt = L // tm

    def body(x_ref, p_ref, n_ref, d_ref, dn_ref, w_ref, b_ref, alias_ref, dx_ref, dw_ref, db_ref, ext, dext):
        i = pl.program_id(1)
        last = i == nrt - 1
        ext[0:HALO, :] = jnp.where(i == 0, 0.0, p_ref[...])
        ext[HALO:HALO + tm, :] = x_ref[...]
        ext[HALO + tm:, :] = n_ref[...]
        dfull = jnp.concatenate([d_ref[...], jnp.where(last, 0.0, dn_ref[...])], axis=0)
        acc = jnp.broadcast_to(b_ref[...], (tm + HALO, cw))
        for k in range(K):
            acc = acc + w_ref[k:k + 1, :] * ext[pl.ds(HALO - (K - 1) + k, tm + HALO), :]
        sg = _sigmoid(acc)
        dconv = dfull * (sg * (1.0 + acc * (1.0 - sg)))
        dext[...] = dconv

        @pl.when(i == 0)
        def _():
            dw_ref[...] = jnp.zeros_like(dw_ref)
            db_ref[...] = jnp.zeros_like(db_ref)

        dx = jnp.zeros((tm, cw), F32)
        for k in range(K):
            dx = dx + w_ref[k:k + 1, :] * dext[pl.ds(K - 1 - k, tm), :]
        dx_ref[...] = dx.astype(BF16)
        dtile = dext[pl.ds(0, tm), :]
        db_ref[...] += jnp.sum(dtile, axis=0, keepdims=True)
        for k in range(K):
            dw_ref[k:k + 1, :] += jnp.sum(dtile * ext[pl.ds(HALO - (K - 1) + k, tm), :], axis=0, keepdims=True)

    prev = pl.BlockSpec((HALO, cw), lambda j, i: (jnp.maximum(i * (tm // HALO) - 1, 0), col0 + j))
    nxt = pl.BlockSpec((HALO, cw), lambda j, i: (jnp.minimum((i + 1) * (tm // HALO), L // HALO - 1), col0 + j))
    dnxt = pl.BlockSpec((HALO, cw), lambda j, i: (jnp.minimum((i + 1) * (tm // HALO), L // HALO - 1), j))
    return pl.pallas_call(
        body, grid=(ncols // cw, nrt),
        in_specs=[pl.BlockSpec((tm, cw), lambda j, i: (i, col0 + j)), prev, nxt,
                  pl.BlockSpec((tm, cw), lambda j, i: (i, j)), dnxt,
                  pl.BlockSpec((K, cw), lambda j, i: (0, wcol0 + j)), pl.BlockSpec((1, cw), lambda j, i: (0, wcol0 + j)),
                  _ANY],
        out_specs=[pl.BlockSpec((tm, cw), lambda j, i: (i, col0 + j)),
                   pl.BlockSpec((K, cw), lambda j, i: (0, j)), pl.BlockSpec((1, cw), lambda j, i: (0, j))],
        out_shape=[jax.ShapeDtypeStruct(dproj.shape, BF16), jax.ShapeDtypeStruct((K, ncols), F32),
                   jax.ShapeDtypeStruct((1, ncols), F32)],
        input_output_aliases={7: 0},
        scratch_shapes=[pltpu.VMEM((tm + 2 * HALO, cw), F32), pltpu.VMEM((tm + HALO, cw), F32)],
        compiler_params=_params(("parallel", "arbitrary")), name=name,
    )(proj, proj, proj, dact, dact, conv_w, conv_b, dproj)


def _pool_fwd(proj, pool_w, pool_scale, ucol, DP, tm):
    L = proj.shape[0]
    PG = len(POOL_WINDOWS)
    PGD = DP // PG
    col0 = ucol // PGD

    def body(u_ref, p_ref, w_ref, s_ref, pooled_ref, y_ref, ext):
        i, g = pl.program_id(0), pl.program_id(1)
        ext[0:HALO, :] = jnp.where(i == 0, 0.0, p_ref[...])
        ext[HALO:, :] = u_ref[...]
        t = i * tm + lax.broadcasted_iota(jnp.int32, (tm, 1), 0)
        for gi, win in enumerate(POOL_WINDOWS):
            @pl.when(g == gi)
            def _():
                acc = ext[pl.ds(HALO, tm), :]
                for j in range(1, win):
                    acc = acc + ext[pl.ds(HALO - j, tm), :]
                count = jnp.minimum(t + 1, win).astype(F32)
                pooled = (acc / count - u_ref[...]).astype(BF16)
                pooled_ref[...] = pooled
                y_ref[...] = (_dot(pooled, w_ref[...], NN) * s_ref[...]).astype(BF16)

    blk = pl.BlockSpec((tm, PGD), lambda i, g: (i, g))
    return pl.pallas_call(
        body, grid=(L // tm, PG),
        in_specs=[pl.BlockSpec((tm, PGD), lambda i, g: (i, col0 + g)), _halo_prev(tm, PGD, col0),
                  pl.BlockSpec((None, PGD, PGD), lambda i, g: (g, 0, 0)), pl.BlockSpec((1, PGD), lambda i, g: (0, g))],
        out_specs=[blk, blk],
        out_shape=[jax.ShapeDtypeStruct((L, DP), BF16), jax.ShapeDtypeStruct((L, DP), BF16)],
        scratch_shapes=[pltpu.VMEM((tm + HALO, PGD), F32)],
        compiler_params=_params(("parallel", "parallel")), name="pool_fwd")(proj, proj, pool_w, pool_scale)


def _pool_bwd(dy, pooled, pool_w, pool_scale, dproj, ucol, tm):
    L, DP = dy.shape
    PG = len(POOL_WINDOWS)
    PGD = DP // PG
    nrt = L // tm
    col0 = ucol // PGD

    def body(d_ref, dn_ref, p_ref, w_ref, s_ref, alias_ref, du_ref, dw_ref, ds_ref, qext):
        g, i = pl.program_id(0), pl.program_id(1)
        last = i == nrt - 1
        dfull = jnp.concatenate([d_ref[...], jnp.where(last, 0.0, dn_ref[...])], axis=0)
        dyp = (dfull * s_ref[...]).astype(BF16)
        dpooled = _dot(dyp, w_ref[...], NT)
        t = i * tm + lax.broadcasted_iota(jnp.int32, (tm + HALO, 1), 0)
        for gi, win in enumerate(POOL_WINDOWS):
            @pl.when(g == gi)
            def _():
                qext[...] = dpooled / jnp.minimum(t + 1, win).astype(F32)
                acc = qext[pl.ds(0, tm), :]
                for j in range(1, win):
                    acc = acc + qext[pl.ds(j, tm), :]
                du_ref[...] = (acc - dpooled[0:tm, :]).astype(BF16)

        @pl.when(i == 0)
        def _():
            dw_ref[...] = jnp.zeros_like(dw_ref)
            ds_ref[...] = jnp.zeros_like(ds_ref)

        pooled_t = p_ref[...]
        dw_ref[...] += _dot(pooled_t, dyp[0:tm, :], TN)
        ypre = _dot(pooled_t, w_ref[...], NN)
        ds_ref[...] += jnp.sum(d_ref[...] * ypre, axis=0, keepdims=True)

    blk = pl.BlockSpec((tm, PGD), lambda g, i: (i, g))
    nxt = pl.BlockSpec((HALO, PGD), lambda g, i: (jnp.minimum((i + 1) * (tm // HALO), L // HALO - 1), g))
    wspec = pl.BlockSpec((None, PGD, PGD), lambda g, i: (g, 0, 0))
    vec = pl.BlockSpec((1, PGD), lambda g, i: (0, g))
    return pl.pallas_call(
        body, grid=(PG, nrt),
        in_specs=[blk, nxt, blk, wspec, vec, _ANY],
        out_specs=[pl.BlockSpec((tm, PGD), lambda g, i: (i, col0 + g)), wspec, vec],
        out_shape=[jax.ShapeDtypeStruct(dproj.shape, BF16), jax.ShapeDtypeStruct((PG, PGD, PGD), F32),
                   jax.ShapeDtypeStruct((1, DP), F32)],
        input_output_aliases={5: 0},
        scratch_shapes=[pltpu.VMEM((tm + HALO, PGD), F32)],
        compiler_params=_params(("parallel", "arbitrary")), name="pool_bwd",
    )(dy, dy, pooled, pool_w, pool_scale, dproj)


def _ssd_common(dtc_raw, dtr_raw, bc, br, ac, ar):
    ch = CHUNK
    row = lax.broadcasted_iota(jnp.int32, (ch, ch), 0)
    col = lax.broadcasted_iota(jnp.int32, (ch, ch), 1)
    lower = row >= col
    dtc = _softplus(dtc_raw + bc)
    dtr = _softplus(dtr_raw + br)
    a_c = -jnp.exp(ac)
    a_r = -jnp.exp(ar)
    hi = lax.Precision.HIGHEST
    acol = jnp.dot(lower.astype(F32), dtc * a_c, preferred_element_type=F32, precision=hi)
    arow = jnp.dot(dtr * a_r, (row <= col).astype(F32), preferred_element_type=F32, precision=hi)
    return lower, row <= col, dtc, a_c, acol, arow, dtr


def _spread_heads(cols, passes=3):
    R = cols.shape[1]
    shape = (R, R * LANES)
    spread = (lax.broadcasted_iota(jnp.int32, shape, 0) == lax.broadcasted_iota(jnp.int32, shape, 1) // LANES).astype(BF16)
    out, rest = None, cols
    for _ in range(passes):
        part = rest.astype(BF16)
        rest = rest - part.astype(F32)
        term = _dot(part, spread, NN)
        out = term if out is None else out + term
    return out


def _ssd_fwd(xbc, dtc_raw, dtr_raw, bias_c, bias_r, alog_c, alog_r, dskip_c, DS, N):
    L = xbc.shape[0]
    G, P, ch = SSD_GROUPS, HEAD_DIM, CHUNK
    R = dtc_raw.shape[2]
    GW = R * P
    nc = L // ch

    def body(xs_ref, b_ref, c_ref, dtc_ref, dtr_ref, bc_ref, br_ref, ac_ref, ar_ref, dk_ref,
             y_ref, st_ref, h_ref):
        @pl.when(pl.program_id(1) == 0)
        def _():
            h_ref[...] = jnp.zeros_like(h_ref)

        lower, _, _, _, acol_all, arow_all, dtr = _ssd_common(
            dtc_ref[...], dtr_ref[...], bc_ref[...], br_ref[...], ac_ref[...], ar_ref[...])
        bm = b_ref[...]
        cb16 = c_ref[...].astype(BF16)
        b16 = bm.astype(BF16)
        bt = bm.T
        cb = _dot(cb16, b16, NT)
        dk = dk_ref[...]
        st_ref[...] = h_ref[...]
        for r in range(R):
            acol = acol_all[:, r:r + 1]
            arow = arow_all[r:r + 1, :]
            alast = acol_all[ch - 1:ch, r:r + 1]
            dt_row = dtr[r:r + 1, :]
            decay = jnp.exp(jnp.where(lower, acol - arow, -1e30))
            x_h = xs_ref[:, pl.ds(r * P, P)]
            x16 = x_h.astype(BF16)
            m16 = (cb * decay * dt_row).astype(BF16)
            h_prev = h_ref[r]
            y = _dot(m16, x16, NN)
            y = y + jnp.exp(acol) * _dot(cb16, h_prev.astype(BF16), NN)
            y = y + dk[:, r:r + 1] * x_h
            y_ref[:, pl.ds(r * P, P)] = y
            to_end_dt = jnp.exp(alast - arow) * dt_row
            h_ref[r] = jnp.exp(alast) * h_prev + _dot((bt * to_end_dt).astype(BF16), x16, NN)

    nb = DS // N
    return pl.pallas_call(
        body, grid=(G, nc),
        in_specs=[pl.BlockSpec((ch, GW), lambda g, c: (c, g)),
                  pl.BlockSpec((ch, N), lambda g, c: (c, nb + g)),
                  pl.BlockSpec((ch, N), lambda g, c: (c, nb + G + g)),
                  pl.BlockSpec((None, ch, R), lambda g, c: (g, c, 0)),
                  pl.BlockSpec((None, R, ch), lambda g, c: (g, 0, c)),
                  pl.BlockSpec((None, 1, R), lambda g, c: (g, 0, 0)),
                  pl.BlockSpec((None, R, 1), lambda g, c: (g, 0, 0)),
                  pl.BlockSpec((None, 1, R), lambda g, c: (g, 0, 0)),
                  pl.BlockSpec((None, R, 1), lambda g, c: (g, 0, 0)),
                  pl.BlockSpec((None, 1, R), lambda g, c: (g, 0, 0))],
        out_specs=[pl.BlockSpec((ch, GW), lambda g, c: (c, g)),
                   pl.BlockSpec((None, R, N, P), lambda g, c: (c, g, 0, 0))],
        out_shape=[jax.ShapeDtypeStruct((L, DS), F32), jax.ShapeDtypeStruct((nc, G * R, N, P), F32)],
        scratch_shapes=[pltpu.VMEM((R, N, P), F32)],
        compiler_params=_params(("parallel", "arbitrary")), name="ssd_fwd",
    )(xbc, xbc, xbc, dtc_raw, dtr_raw, bias_c, bias_r, alog_c, alog_r, dskip_c)


def _ssd_bwd(xbc, dtc_raw, dtr_raw, bias_c, bias_r, alog_c, alog_r, dskip_c, dy, states, DS, N):
    L = xbc.shape[0]
    G, P, ch = SSD_GROUPS, HEAD_DIM, CHUNK
    R = dtc_raw.shape[2]
    GW = R * P
    nc = L // ch
    assert N == LANES and P <= LANES and ch % LANES == 0

    def body(xs_ref, b_ref, c_ref, dtc_ref, dtr_ref, bc_ref, br_ref, ac_ref, ar_ref, dk_ref,
             dy_ref, stp_ref,
             dxs_ref, db_ref, dc_ref, ddt_ref, dal_ref, ddk_ref, dbias_ref, dh_ref):
        @pl.when(pl.program_id(1) == 0)
        def _():
            dh_ref[...] = jnp.zeros_like(dh_ref)
            dal_ref[...] = jnp.zeros_like(dal_ref)
            ddk_ref[...] = jnp.zeros_like(ddk_ref)
            dbias_ref[...] = jnp.zeros_like(dbias_ref)

        lower, upper, dtc, a_c, acol_all, arow_all, _ = _ssd_common(
            dtc_ref[...], dtr_ref[...], bc_ref[...], br_ref[...], ac_ref[...], ar_ref[...])
        bm = b_ref[...]
        cm = c_ref[...]
        b16 = bm.astype(BF16)
        c16 = cm.astype(BF16)
        ct16 = cm.T.astype(BF16)
        cb = _dot(c16, b16, NT)
        cbt = _dot(b16, c16, NT)
        dk = dk_ref[...]
        lane_r = lax.broadcasted_iota(jnp.int32, (ch, R), 1)
        lane_1 = lax.broadcasted_iota(jnp.int32, (1, R), 1)
        dc = jnp.zeros((ch, N), F32)
        db = jnp.zeros((ch, N), F32)
        da_all = jnp.zeros((R, ch), F32)
        q_all = jnp.zeros((R, ch), F32)
        sxd_all = jnp.zeros((ch, R), F32)
        const = jnp.zeros((1, R), F32)
        ddk = jnp.zeros((1, R), F32)
        sub_r = lax.broadcasted_iota(jnp.int32, (R, ch), 0)
        ct = cm.T
        bt = bm.T
        dcb = jnp.zeros((ch, ch), F32)
        acol_lanes = _spread_heads(acol_all)
        dt_lanes = _spread_heads(dtc, passes=2)
        for r in range(R):
            a128 = acol_lanes[:, r * LANES:(r + 1) * LANES]
            arow = arow_all[r:r + 1, :]
            alast = acol_all[ch - 1:ch, r:r + 1]
            seg = jnp.tile(a128, (1, ch // LANES)) - arow
            decay = jnp.exp(jnp.where(lower, seg, -1e30))
            decay_t = jnp.exp(jnp.where(upper, -seg, -1e30))
            x_h = xs_ref[:, pl.ds(r * P, P)]
            dy_h = dy_ref[:, pl.ds(r * P, P)]
            dt_h = dt_lanes[:, r * LANES:r * LANES + P]
            dk_h = dk[:, r:r + 1]
            xdt = x_h * dt_h
            xdt16 = xdt.astype(BF16)
            dy16 = dy_h.astype(BF16)
            h_prev = stp_ref[r]
            h16 = h_prev.astype(BF16)
            dh_next = dh_ref[r]
            dhn16 = dh_next.astype(BF16)
            to_end_n = jnp.exp(alast - a128)
            e_a_n = jnp.exp(a128)
            to_end, e_a = to_end_n[:, :P], e_a_n[:, :P]
            mt = cbt * decay_t
            pm = _dot(dy16, xdt16, NT) * decay
            wt = _dot(xdt16, dy16, NT) * mt
            dxdt = _dot(mt.astype(BF16), dy16, NN) + to_end * _dot(b16, dhn16, NN)
            dcb = dcb + pm
            dc = dc + e_a_n * _dot(dy16, h16, NT)
            db = db + to_end_n * _dot(xdt16, dhn16, NT)
            dh_ref[r] = jnp.exp(alast) * dh_next + _dot(ct16, (dy_h * e_a).astype(BF16), NN)
            da = (jnp.sum(wt, axis=0, keepdims=True) - jnp.sum(pm * cb, axis=0, keepdims=True)
                  + jnp.exp(arow) * jnp.sum(ct * _dot(h16, dy16, NT), axis=0, keepdims=True))
            q = jnp.exp(alast - arow) * jnp.sum(bt * _dot(dhn16, xdt16, NT), axis=0, keepdims=True)
            da_all = da_all + jnp.where(sub_r == r, da, 0.0)
            q_all = q_all + jnp.where(sub_r == r, q, 0.0)
            sxd_all = sxd_all + jnp.where(lane_r == r, jnp.sum(dxdt * x_h, axis=1, keepdims=True), 0.0)
            const = const + jnp.where(lane_1 == r, jnp.exp(alast) * jnp.sum(dh_next * h_prev), 0.0)
            ddk = ddk + jnp.where(lane_1 == r, jnp.sum(dy_h * x_h), 0.0)
            dxs_ref[:, pl.ds(r * P, P)] = dxdt * dt_h + dk_h * dy_h
        dcb16 = dcb.astype(BF16)
        dc_ref[...] = dc + _dot(dcb16, b16, NN)
        db_ref[...] = db + _dot(dcb16, c16, TN)
        hi = lax.Precision.HIGHEST
        strict_lower = jnp.logical_and(lower, jnp.logical_not(upper))
        dda = (lax.dot_general(upper.astype(F32), da_all, NT, preferred_element_type=F32, precision=hi)
               + lax.dot_general(strict_lower.astype(F32), q_all, NT, preferred_element_type=F32, precision=hi)
               + const)
        ddt = dda * a_c + sxd_all
        dal_ref[...] += jnp.sum(dda * dtc, axis=0, keepdims=True) * a_c
        ddk_ref[...] += ddk
        ddt_raw = ddt * _sigmoid(dtc_ref[...] + bc_ref[...])
        ddt_ref[...] = ddt_raw
        dbias_ref[...] += jnp.sum(ddt_raw, axis=0, keepdims=True)

    nb = DS // N
    rc = lambda c: nc - 1 - c
    vec_c = pl.BlockSpec((None, 1, R), lambda g, c: (g, 0, 0))
    vec_r = pl.BlockSpec((None, R, 1), lambda g, c: (g, 0, 0))
    big = pl.BlockSpec((ch, GW), lambda g, c: (rc(c), g))
    return pl.pallas_call(
        body, grid=(G, nc),
        in_specs=[big,
                  pl.BlockSpec((ch, N), lambda g, c: (rc(c), nb + g)),
                  pl.BlockSpec((ch, N), lambda g, c: (rc(c), nb + G + g)),
                  pl.BlockSpec((None, ch, R), lambda g, c: (g, rc(c), 0)),
                  pl.BlockSpec((None, R, ch), lambda g, c: (g, 0, rc(c))),
                  vec_c, vec_r, vec_c, vec_r, vec_c,
                  big,
                  pl.BlockSpec((None, R, N, P), lambda g, c: (rc(c), g, 0, 0))],
        out_specs=[big,
                   pl.BlockSpec((ch, N), lambda g, c: (rc(c), g)),
                   pl.BlockSpec((ch, N), lambda g, c: (rc(c), g)),
                   pl.BlockSpec((None, ch, R), lambda g, c: (g, rc(c), 0)),
                   vec_c, vec_c, vec_c],
        out_shape=[jax.ShapeDtypeStruct((L, DS), F32), jax.ShapeDtypeStruct((L, G * N), F32),
                   jax.ShapeDtypeStruct((L, G * N), F32), jax.ShapeDtypeStruct((G, L, R), F32),
                   jax.ShapeDtypeStruct((G, 1, R), F32), jax.ShapeDtypeStruct((G, 1, R), F32),
                   jax.ShapeDtypeStruct((G, 1, R), F32)],
        scratch_shapes=[pltpu.VMEM((R, N, P), F32)],
        compiler_params=_params(("parallel", "arbitrary")), name="ssd_bwd",
    )(xbc, xbc, xbc, dtc_raw, dtr_raw, bias_c, bias_r, alog_c, alog_r, dskip_c, dy, states)


def _adam_math(w, g, m, v):
    m = ADAM_B1 * m + (1.0 - ADAM_B1) * g
    v = ADAM_B2 * v + (1.0 - ADAM_B2) * jnp.square(g)
    m_hat = m / (1.0 - ADAM_B1 ** ADAM_STEP)
    v_hat = v / (1.0 - ADAM_B2 ** ADAM_STEP)
    delta = -ADAM_LR * (m_hat / (jnp.sqrt(v_hat) + ADAM_EPS) + ADAM_WD * w)
    return delta, m, v


def _adam(name, w, g, m, v):
    rows, cols = w.shape
    tr = _tile(rows, max(8, (1 << 19) // cols // 8 * 8), 8)

    def body(w_ref, g_ref, m_ref, v_ref, go_ref, d_ref, mo_ref, vo_ref):
        g = g_ref[...]
        d, m2, v2 = _adam_math(w_ref[...], g, m_ref[...], v_ref[...])
        go_ref[...] = g
        d_ref[...] = d
        mo_ref[...] = m2
        vo_ref[...] = v2

    blk = pl.BlockSpec((tr, cols), lambda i: (i, 0))
    return pl.pallas_call(
        body, grid=(rows // tr,), in_specs=[blk] * 4, out_specs=[blk] * 4,
        out_shape=[jax.ShapeDtypeStruct((rows, cols), F32)] * 4,
        compiler_params=_params(("parallel",)), name=name)(w, g, m, v)


def _small_sum_adam(gathered, w, m, v, rows):
    def body(ga_ref, w_ref, m_ref, v_ref, g_ref, d_ref, mo_ref, vo_ref):
        g = ga_ref[0:rows, :]
        for d in range(1, N_DEV):
            g = g + ga_ref[d * rows:(d + 1) * rows, :]
        g_ref[...] = g
        dl, m2, v2 = _adam_math(w_ref[...], g, m_ref[...], v_ref[...])
        d_ref[...] = dl
        mo_ref[...] = m2
        vo_ref[...] = v2

    return pl.pallas_call(
        body, out_shape=[jax.ShapeDtypeStruct((rows, LANES), F32)] * 4,
        compiler_params=pltpu.CompilerParams(vmem_limit_bytes=VMEM_LIMIT), name="small_sum_adam",
    )(gathered, w, m, v)


def _row_tile(rh, cols):
    return _tile(rh, max(16, (1 << 19) // cols // 16 * 16), 16)


def _shard_dims(g, window=None):
    if g.ndim == 3:
        return g.shape[1], g.shape[2]
    return g.shape[0], (window[1] if window else g.shape[1] // N_CHIPS)


def _pair_sum(name, g, recv, pos, window=None):
    r, c = _shard_dims(g, window)
    rh = r // 2
    cb = LANES if window else c
    tr = _row_tile(rh, cb)
    nrt = rh // tr

    def body(pos_ref, a_ref, b_ref, o_ref):
        o_ref[...] = (a_ref[...] + b_ref[...].astype(F32)).astype(BF16)

    if g.ndim == 3:
        own = pl.BlockSpec((None, tr, c), lambda k, i, j, p: (k, p[1] * nrt + i, 0))
    elif window:
        pos = jnp.concatenate([pos, jnp.asarray([s // LANES for s in window[0]], jnp.int32)])
        own = pl.BlockSpec((tr, cb), lambda k, i, j, p: (p[1] * nrt + i, p[2 + k] + j))
    else:
        own = pl.BlockSpec((tr, c), lambda k, i, j, p: (p[1] * nrt + i, k))
    part = pl.BlockSpec((None, tr, cb), lambda k, i, j, p: (k, i, j))
    return pl.pallas_call(
        body,
        grid_spec=pltpu.PrefetchScalarGridSpec(
            num_scalar_prefetch=1, grid=(N_CHIPS, nrt, c // cb), in_specs=[own, part], out_specs=part),
        out_shape=jax.ShapeDtypeStruct((N_CHIPS, rh, c), BF16),
        compiler_params=_params(("parallel", "parallel", "parallel")), name=name)(pos, g, recv)


def _chip_sum(name, parts, pos):
    _, rh, c = parts.shape
    tr = _row_tile(rh, c)
    nrt = rh // tr

    def body(pos_ref, p_ref, o_ref):
        s = p_ref[0].astype(F32)
        for k in range(1, N_CHIPS):
            s = s + p_ref[k].astype(F32)
        o_ref[...] = s

    return pl.pallas_call(
        body,
        grid_spec=pltpu.PrefetchScalarGridSpec(
            num_scalar_prefetch=1, grid=(nrt,),
            in_specs=[pl.BlockSpec((N_CHIPS, tr, c), lambda i, p: (0, i, 0))],
            out_specs=pl.BlockSpec((tr, c), lambda i, p: (p[1] * nrt + i, 0))),
        out_shape=jax.ShapeDtypeStruct((2 * rh, c), F32),
        compiler_params=_params(("parallel",)), name=name)(pos, parts)


_HBM = pl.BlockSpec(memory_space=pltpu.HBM)


def _chip_xy(k):
    return k // 2, k % 2


def _half_rows(ref, hc, rh):
    return ref.at[pl.ds(pl.multiple_of(hc * rh, 16), rh), :]


_SEM = pl.BlockSpec(memory_space=pltpu.SEMAPHORE)
_ANY = pl.BlockSpec(memory_space=pl.ANY)
_SPLIT = pltpu.CompilerParams(has_side_effects=pltpu.SideEffectType.DATAFLOW_SIDE_EFFECTING)


def _in_hbm(a):
    return pltpu.with_memory_space_constraint(a, pltpu.HBM)


def _push_start(name, srcs, land_shapes, copies_of):
    n = len(srcs)

    def body(*refs):
        s_refs, l_refs = refs[:n], refs[n:2 * n]
        send_sems, recv_sems = refs[2 * n], refs[2 * n + 1]
        token = refs[-1]
        x, y, c = lax.axis_index("x"), lax.axis_index("y"), lax.axis_index("c")
        me = 2 * x + y
        for i in range(n):
            for k in range(N_CHIPS):
                @pl.when(k != me)
                def _():
                    src, dst, dev = copies_of(i, k, s_refs[i], l_refs[i], me, x, y, c)
                    pltpu.make_async_remote_copy(
                        src_ref=src, dst_ref=dst, send_sem=send_sems.at[N_CHIPS * i + k],
                        recv_sem=recv_sems.at[N_CHIPS * i + me], device_id=dev, device_id_type=MESH).start()
        token[...] = jnp.zeros_like(token)

    lands = [lax.empty(s, d) for s, d in land_shapes]
    outs = pl.pallas_call(
        body, name=name,
        out_shape=[pltpu.SemaphoreType.DMA((N_CHIPS * n,)), pltpu.SemaphoreType.DMA((N_CHIPS * n,))]
        + [pltpu.HBM(s.shape, s.dtype) for s in srcs] + [pltpu.HBM(s, d) for s, d in land_shapes]
        + [jax.ShapeDtypeStruct((8, LANES), F32)],
        in_specs=[_HBM] * (2 * n),
        out_specs=[_SEM, _SEM] + [_HBM] * (2 * n) + [pl.BlockSpec(memory_space=pltpu.VMEM)],
        input_output_aliases={j: 2 + j for j in range(2 * n)},
        compiler_params=_SPLIT,
    )(*[_in_hbm(s) for s in srcs], *[_in_hbm(l) for l in lands])
    return outs[0], outs[1], outs[2:2 + n], outs[2 + n:2 + 2 * n], outs[-1]


def _push_wait(name, started, after, landed_of):
    send_sems, recv_sems, srcs, lands, _ = started
    n = len(srcs)

    def body(*refs):
        s_refs, l_refs = refs[:n], refs[n:2 * n]
        send, recv = refs[2 * n], refs[2 * n + 1]
        token = refs[-1]
        token[...] = jnp.zeros_like(token)
        x, y, c = lax.axis_index("x"), lax.axis_index("y"), lax.axis_index("c")
        me = 2 * x + y
        for i in range(n):
            for k in range(N_CHIPS):
                @pl.when(k != me)
                def _():
                    src, dst = landed_of(i, k, s_refs[i], l_refs[i], me, c)
                    cp = pltpu.make_async_remote_copy(
                        src_ref=src, dst_ref=dst, send_sem=send.at[N_CHIPS * i + k], recv_sem=recv.at[N_CHIPS * i + k],
                        device_id=(x, y, c), device_id_type=MESH)
                    cp.wait_send()
                    cp.wait_recv()

    outs = pl.pallas_call(
        body, name=name,
        out_shape=[pltpu.HBM(s.shape, s.dtype) for s in srcs] + [pltpu.HBM(l.shape, l.dtype) for l in lands]
        + [jax.ShapeDtypeStruct((8, LANES), F32)],
        in_specs=[_HBM] * (2 * n) + [_SEM, _SEM, _ANY],
        out_specs=[_HBM] * (2 * n) + [pl.BlockSpec(memory_space=pltpu.VMEM)],
        input_output_aliases={j: j for j in range(2 * n)},
        compiler_params=_SPLIT,
    )(*srcs, *lands, send_sems, recv_sems, after)
    return outs[:n], outs[n:2 * n], outs[-1]


def _gather_start(name, shards):
    def copies_of(i, k, src, land, me, x, y, c):
        rh = shards[i].shape[0] // 2
        kx, ky = _chip_xy(k)
        return _half_rows(src, c, rh), _half_rows(land.at[me], c, rh), (kx, ky, c)

    return _push_start(name, shards, [((N_CHIPS,) + s.shape, s.dtype) for s in shards], copies_of)


def _gather_wait(name, started, after):
    shapes = [s.shape for s in started[2]]

    def landed_of(i, k, src, land, me, c):
        rh = shapes[i][0] // 2
        return _half_rows(src, c, rh), _half_rows(land.at[k], c, rh)

    return _push_wait(name, started, after, landed_of)


def _forward_halves(name, bufs):
    n = len(bufs)

    def body(*refs):
        i_refs, o_refs, send_sems, recv_sems = refs[:n], refs[n:2 * n], refs[2 * n], refs[2 * n + 1]
        x, y, c = lax.axis_index("x"), lax.axis_index("y"), lax.axis_index("c")
        me = 2 * x + y

        def fwd(i, k, hc):
            rh = bufs[i].shape[1] // 2
            return pltpu.make_async_remote_copy(
                src_ref=_half_rows(i_refs[i].at[k], hc, rh), dst_ref=_half_rows(o_refs[i].at[k], hc, rh),
                send_sem=send_sems.at[i, k], recv_sem=recv_sems.at[i, k],
                device_id=(x, y, 1 - c), device_id_type=MESH)

        for i in range(n):
            for k in range(N_CHIPS):
                @pl.when(k != me)
                def _():
                    fwd(i, k, c).start()
        for i in range(n):
            for k in range(N_CHIPS):
                @pl.when(k != me)
                def _():
                    fwd(i, k, 1 - c).wait_recv()
        for i in range(n):
            for k in range(N_CHIPS):
                @pl.when(k != me)
                def _():
                    fwd(i, k, c).wait_send()

    return pl.pallas_call(
        body, in_specs=[_HBM] * n, out_specs=[_HBM] * n,
        out_shape=[jax.ShapeDtypeStruct(b.shape, b.dtype) for b in bufs],
        input_output_aliases={i: i for i in range(n)},
        scratch_shapes=[pltpu.SemaphoreType.DMA((n, N_CHIPS))] * 2,
        name=name)(*bufs)


def _swap_halves(name, grads, windows):
    n = len(grads)
    dims = [_shard_dims(g, w) for g, w in zip(grads, windows)]

    def body(*refs):
        g_refs, o_refs, send_sems, recv_sems = refs[:n], refs[n:2 * n], refs[2 * n], refs[2 * n + 1]
        x, y, c = lax.axis_index("x"), lax.axis_index("y"), lax.axis_index("c")
        copies = []
        for i in range(n):
            r, cw = dims[i]
            for k in range(N_CHIPS):
                if grads[i].ndim == 3:
                    shard = g_refs[i].at[k]
                else:
                    shard = g_refs[i].at[:, pl.ds(windows[i][0][k] if windows[i] else k * cw, cw)]
                copies.append(pltpu.make_async_remote_copy(
                    src_ref=_half_rows(shard, 1 - c, r // 2), dst_ref=o_refs[i].at[k],
                    send_sem=send_sems.at[i, k], recv_sem=recv_sems.at[i, k],
                    device_id=(x, y, 1 - c), device_id_type=MESH))
        for cp in copies:
            cp.start()
        for cp in copies:
            cp.wait()

    return pl.pallas_call(
        body, in_specs=[_HBM] * n, out_specs=[_HBM] * n,
        out_shape=[jax.ShapeDtypeStruct((N_CHIPS, r // 2, cw), g.dtype) for (r, cw), g in zip(dims, grads)],
        scratch_shapes=[pltpu.SemaphoreType.DMA((n, N_CHIPS))] * 2,
        name=name)(*grads)


def _scatter_start(name, parts):
    def copies_of(i, k, src, land, me, x, y, c):
        kx, ky = _chip_xy(k)
        return src.at[k], land.at[me], (kx, ky, c)

    return _push_start(name, parts, [(p.shape, p.dtype) for p in parts], copies_of)


def _scatter_wait(name, started, after):
    return _push_wait(name, started, after, lambda i, k, src, land, me, c: (src.at[k], land.at[k]))


def _join_halves(bufs):
    n = len(bufs)

    def body(*refs):
        i_refs, o_refs, send_sems, recv_sems = refs[:n], refs[n:2 * n], refs[2 * n], refs[2 * n + 1]
        x, y, c = lax.axis_index("x"), lax.axis_index("y"), lax.axis_index("c")
        copies = []
        for i in range(n):
            rh = bufs[i].shape[0] // 2
            copies.append(pltpu.make_async_remote_copy(
                src_ref=_half_rows(i_refs[i], c, rh), dst_ref=_half_rows(o_refs[i], c, rh),
                send_sem=send_sems.at[i], recv_sem=recv_sems.at[i],
                device_id=(x, y, 1 - c), device_id_type=MESH))
        for cp in copies:
            cp.start()
        for i in range(n):
            rh = bufs[i].shape[0] // 2
            pltpu.make_async_remote_copy(
                src_ref=_half_rows(i_refs[i], c, rh), dst_ref=_half_rows(o_refs[i], 1 - c, rh),
                send_sem=send_sems.at[i], recv_sem=recv_sems.at[i],
                device_id=(x, y, 1 - c), device_id_type=MESH).wait_recv()
        for cp in copies:
            cp.wait_send()

    return pl.pallas_call(
        body, in_specs=[_HBM] * n, out_specs=[_HBM] * n,
        out_shape=[jax.ShapeDtypeStruct(b.shape, F32) for b in bufs],
        input_output_aliases={i: i for i in range(n)},
        scratch_shapes=[pltpu.SemaphoreType.DMA((n,))] * 2,
        name="join_halves")(*bufs)


def _all_gather_small(name, blk):
    m_per, n = blk.shape

    def body(x_ref, out_ref, send_sems, recv_sems, local_sem):
        x, y, c = lax.axis_index("x"), lax.axis_index("y"), lax.axis_index("c")
        me, sibling = (x, y, c), (x, y, 1 - c)
        chips = [(1 - x, y), (x, 1 - y), (1 - x, 1 - y)]

        def rows(px, py, pc):
            return out_ref.at[pl.ds((4 * px + 2 * py + pc) * m_per, m_per), :]

        def copy(k, block, to, src=None):
            return pltpu.make_async_remote_copy(
                src_ref=rows(*block) if src is None else src, dst_ref=rows(*block),
                send_sem=send_sems.at[k], recv_sem=recv_sems.at[k],
                device_id=to, device_id_type=MESH)

        mine = pltpu.make_async_copy(x_ref, rows(*me), local_sem)
        mine.start()
        first = [copy(0, me, sibling, src=x_ref)]
        first += [copy(1 + j, me, (*chip, c), src=x_ref) for j, chip in enumerate(chips)]
        for cp in first:
            cp.start()
        passed = [copy(4 + j, (*chip, c), sibling) for j, chip in enumerate(chips)]
        for j, chip in enumerate(chips):
            copy(1 + j, (*chip, c), me).wait_recv()
            passed[j].start()
        copy(0, sibling, me).wait_recv()
        for j, chip in enumerate(chips):
            copy(4 + j, (*chip, 1 - c), me).wait_recv()
        for cp in first + passed:
            cp.wait_send()
        mine.wait()

    return pl.pallas_call(
        body, out_shape=jax.ShapeDtypeStruct((N_DEV * m_per, n), blk.dtype),
        in_specs=[pl.BlockSpec(memory_space=pltpu.VMEM)],
        out_specs=pl.BlockSpec(memory_space=pltpu.VMEM),
        scratch_shapes=[pltpu.SemaphoreType.DMA((7,)), pltpu.SemaphoreType.DMA((7,)), pltpu.SemaphoreType.DMA],
        name=name)(blk)


def _pack_rows(vecs, width):
    parts = []
    for v in vecs:
        f = v.reshape(-1)
        pad = (-f.shape[0]) % (8 * width)
        parts.append(jnp.pad(f, (0, pad)) if pad else f)
    return jnp.concatenate(parts).reshape(-1, width)


def _unpack_rows(packed, shapes, width):
    flat = packed.reshape(-1)
    out, off = [], 0
    for s in shapes:
        n = math.prod(s)
        out.append(flat[off:off + n].reshape(s))
        off += n + ((-n) % (8 * width))
    return out


class _WinPlan:
    def __init__(self, ncol, dt0, h, dmain):
        self.ncol, self.h = ncol, h
        self.dt_shard = dt0 // ncol
        assert (dt0 + h - 1) // ncol == self.dt_shard and dmain % LANES == 0
        self.dt_local = dt0 - self.dt_shard * ncol
        to_main = lambda g: g if g <= dt0 else g - h
        self.lo = [to_main(ncol * k) for k in range(N_CHIPS)]
        self.hi = [to_main(ncol * (k + 1)) for k in range(N_CHIPS)]
        down = lambda v: v // LANES * LANES
        self.ww = max(-(-(hi - down(lo)) // LANES) * LANES for lo, hi in zip(self.lo, self.hi))
        self.ws = [min(down(lo), dmain - self.ww) for lo in self.lo]
        self.dmain = dmain

    def to_window(self, k, shard):
        if k == self.dt_shard:
            shard = jnp.concatenate([shard[:, :self.dt_local], shard[:, self.dt_local + self.h:]], axis=1)
        left = self.lo[k] - self.ws[k]
        return jnp.pad(shard, ((0, 0), (left, self.ww - left - shard.shape[1])))

    def from_window(self, k, window, dt_cols):
        left = self.lo[k] - self.ws[k]
        body = window[:, left:left + self.hi[k] - self.lo[k]]
        if k == self.dt_shard:
            body = jnp.concatenate([body[:, :self.dt_local], dt_cols, body[:, self.dt_local:]], axis=1)
        return body

    def merge(self, windows):
        cuts = sorted({0, self.dmain} | set(self.ws) | {w + self.ww for w in self.ws})
        segs = []
        for a, b in zip(cuts[:-1], cuts[1:]):
            parts = [windows[k][:, a - self.ws[k]:b - self.ws[k]] for k in range(N_CHIPS)
                     if self.ws[k] <= a and b <= self.ws[k] + self.ww]
            segs.append(functools.reduce(jnp.add, parts))
        return jnp.concatenate(segs, axis=1)


def kernel(x, attn_norm_w, w_in, conv_w, conv_b, dt_bias, a_log, d_skip, ssd_norm_w, pool_w, pool_scale, w_out, ffn_norm_w, w_gate, w_up, w_down, final_norm_w, loss_target, m_attn_norm_w, m_w_in, m_conv_w, m_conv_b, m_dt_bias, m_a_log, m_d_skip, m_ssd_norm_w, m_pool_w, m_pool_scale, m_w_out, m_ffn_norm_w, m_w_gate, m_w_up, m_w_down, m_final_norm_w, v_attn_norm_w, v_w_in, v_conv_w, v_conv_b, v_dt_bias, v_a_log, v_d_skip, v_ssd_norm_w, v_pool_w, v_pool_scale, v_w_out, v_ffn_norm_w, v_w_gate, v_w_up, v_w_down, v_final_norm_w):
    G, P, PG = SSD_GROUPS, HEAD_DIM, len(POOL_WINDOWS)
    _, L, D = x.shape
    H = a_log.shape[1]
    R = H // G
    DS = H * P
    DCONV = conv_b.shape[1]
    N = (DCONV - DS) // (2 * G)
    DP = pool_scale.shape[1]
    PGD = DP // PG
    DIN = N_CHIPS * w_in.shape[2]
    DFF = N_CHIPS * w_gate.shape[2]
    DMAIN = DS + DCONV + DP
    assert DIN == DMAIN + H and DS == DP and H <= LANES

    cx, cy, cc = lax.axis_index("x"), lax.axis_index("y"), lax.axis_index("c")
    chip = 2 * cx + cy

    win = _WinPlan(DIN // N_CHIPS, DS + DCONV, H, DMAIN)
    my_window = lax.switch(chip, [functools.partial(win.to_window, k) for k in range(N_CHIPS)], w_in[0].astype(BF16))
    started_in = _gather_start("gather_start_in", [my_window])

    def forward_gathered(tag, shards, landed):
        landed = _forward_halves("gather_forward_" + tag, landed)
        return [lax.dynamic_update_slice(g, s[None], (chip, 0, 0)) for g, s in zip(landed, shards)]

    def cols(p):
        return jnp.moveaxis(p, 0, -2).reshape(p.shape[1:-1] + (N_CHIPS * p.shape[-1],))

    ncw = CONV_WIDTH * DCONV // N_CHIPS
    dt_here = jnp.where(chip == win.dt_shard, w_in[0][:, win.dt_local:win.dt_local + H], 0.0)
    start_blk = _pack_rows([conv_w[0], dt_here], LANES)
    start_all = _all_gather_small("gather_conv_w", start_blk).reshape(N_CHIPS, 2, -1)[:, 0]
    conv_w_f = cols(start_all[:, :ncw].reshape(N_CHIPS, CONV_WIDTH, DCONV // N_CHIPS))
    dt_off = ncw + (-ncw) % (8 * LANES)
    w_dt = jnp.pad(start_all[win.dt_shard, dt_off:dt_off + D * H].reshape(D, H), ((0, 0), (0, LANES - H))).astype(BF16)

    xl, tgt = x[0], loss_target[0]
    tm_row = _tile(L, 256, HALO)
    tm_mm = _tile(L, 1024, 16)
    hn1 = _rms_fwd("rms1_fwd", xl, attn_norm_w, tm_row)
    shards_in, landed_in, landed_token = _gather_wait("gather_wait_in", started_in, hn1)
    shards_rest = [(pool_w[0].reshape(PG * PGD // N_CHIPS, PGD) + landed_token[0, 0]).astype(BF16),
                   w_out[0].astype(BF16), w_gate[0].astype(BF16), w_up[0].astype(BF16), w_down[0].astype(BF16)]
    started_rest = _gather_start("gather_start_rest", shards_rest)
    pin_row = lambda started, n: jnp.zeros((1, n), F32) + started[4][0, 0]
    add_row = lambda accs, ex, rex: [accs[0] + rex[0]]
    w_main = win.merge(forward_gathered("in", shards_in, landed_in)[0])
    proj, = _mm("proj_main", "nn", [(hn1, w_main)], L, DMAIN, D, tm_mm, 512, D, [F32],
                epilogue=add_row, row_extras=[pin_row(started_rest, DMAIN)])
    dt_raw, = _mm("proj_dt", "nn", [(hn1, w_dt)], L, LANES, D, tm_mm, LANES, D, [F32])

    cwid = _tile(math.gcd(DS, DCONV), 512, LANES)
    tm_conv = _tile(L, 1024, HALO)
    xbc = _conv_fwd(proj, conv_w_f, conv_b, DS, DCONV, tm_conv, cwid)

    dt_g = dt_raw[:, :H].reshape(L, G, R)
    dtc_raw = jnp.transpose(dt_g, (1, 0, 2))
    dtr_raw = jnp.transpose(dt_g, (1, 2, 0))
    as_c = lambda v: v.reshape(G, 1, R)
    as_r = lambda v: v.reshape(G, R, 1)
    ssd_args = (xbc, dtc_raw, dtr_raw, as_c(dt_bias), as_r(dt_bias), as_c(a_log), as_r(a_log), as_c(d_skip))
    y_ssd_raw, states = _ssd_fwd(*ssd_args, DS, N)
    y_ssd = _gated_fwd(y_ssd_raw, proj, ssd_norm_w, DS, tm_conv)
    gathered = forward_gathered("rest", *_gather_wait("gather_wait_rest", started_rest, y_ssd)[:2])
    pool_w_f = jnp.moveaxis(gathered[0].reshape(N_CHIPS, PG, PGD // N_CHIPS, PGD), 0, 1).reshape(PG, PGD, PGD)
    w_out_f = gathered[1].reshape(2 * DS, D)
    w_gate_f, w_up_f = cols(gathered[2]), cols(gathered[3])
    w_down_f = gathered[4].reshape(DFF, D)
    w_out_top, w_out_bot = w_out_f[:DS], w_out_f[DS:]
    pooled, y_pool = _pool_fwd(proj, pool_w_f, pool_scale, DS + DCONV, DP, tm_conv)

    add_res = lambda accs, ex, rex: [accs[0] + ex[0]]
    h1, = _mm("out_proj", "nn", [(y_ssd, w_out_top), (y_pool, w_out_bot)], L, D, DS, tm_mm, 512, DS, [F32],
              epilogue=add_res, extras=[xl])
    hn2 = _rms_fwd("rms2_fwd", h1, ffn_norm_w, tm_row)

    def glu(accs, ex, rex):
        return [accs[0], accs[1], (_silu(accs[0]) * accs[1])]

    tn_ff = _tile(DFF, 512, LANES)
    gate, up, act = _mm("ffn_in", "nn", [(hn2, w_gate_f), (hn2, w_up_f)], L, DFF, D, tm_mm, tn_ff, D,
                        [F32, F32, BF16], epilogue=glu, separate=True)
    tk_ff = _tile(DFF, DFF // 2, LANES)
    h2, = _mm("ffn_out", "nn", [(act, w_down_f)], L, D, DFF, tm_mm, 512, tk_ff, [F32], epilogue=add_res, extras=[h1])
    dh2, dh2_16, loss_blk, g_final = _final_loss(h2, final_norm_w.reshape(1, D), tgt, tm_row)

    def dglu(accs, ex, rex):
        gt, u = ex
        sg = _sigmoid(gt)
        return [accs[0] * u * (sg * (1.0 + gt * (1.0 - sg))), accs[0] * (gt * sg)]

    dgate, dup = _mm("ffn_out_dx", "nt", [(dh2_16, w_down_f)], L, DFF, D, tm_mm, tn_ff, D, [BF16, BF16],
                     epilogue=dglu, extras=[gate, up])
    tk_tok = _tile(L, 2048, 16)
    twice = lambda accs, ex, rex: list(accs) + list(accs)
    g_w_down, g_w_down16 = _mm("ffn_out_dw", "tn", [(act, dh2_16)], DFF, D, L, _tile(DFF, 1536, LANES), 1024, tk_tok,
                               [F32, BF16], epilogue=twice)
    g_w_gate, g_w_up, g_w_gate16, g_w_up16 = _mm("ffn_in_dw", "tn", [(hn2, dgate), (hn2, dup)], D, DFF, L, 1024, tn_ff,
                                                 tk_tok, [F32, F32, BF16, BF16], epilogue=twice, separate=True)

    pos = jnp.stack([chip, cc]).astype(jnp.int32)

    def start_reduce(tag, names, full_grads, full_grads16, windows):
        from_sibling = _swap_halves("swap_halves_" + tag, full_grads16, windows)
        partials = [_pair_sum("pair_sum_" + n, g, r, pos, w)
                    for n, g, r, w in zip(names, full_grads, from_sibling, windows)]
        return _scatter_start("scatter_start_" + tag, partials)

    names_ffn = ["w_gate", "w_up", "w_down"]
    started_ffn = start_reduce("ffn", names_ffn, [g_w_gate, g_w_up, g_w_down.reshape(N_CHIPS, -1, D)],
                               [g_w_gate16, g_w_up16, g_w_down16.reshape(N_CHIPS, -1, D)], [None] * 3)
    dhn2, = _mm("ffn_in_dx", "nt", [(dgate, w_gate_f), (dup, w_up_f)], L, D, DFF, tm_mm, 512,
                _tile(DFF, DFF // 4, LANES), [F32], epilogue=add_row, row_extras=[pin_row(started_ffn, D)])
    dh1, g_ffn_norm, dh1_16 = _rms_bwd("rms2_bwd", h1, ffn_norm_w, [dhn2], dh2, tm_row, True)

    dy_ssd, dy_pool = _mm("out_proj_dx", "nt", [(dh1_16, w_out_top), (dh1_16, w_out_bot)], L, DS, D, tm_mm, 512, D,
                          [F32, F32], separate=True)
    g_w_out_top, g_w_out_bot, g_w_out_top16, g_w_out_bot16 = _mm(
        "out_proj_dw", "tn", [(y_ssd, dh1_16), (y_pool, dh1_16)], DS, D, L, 1024, 512, tk_tok, [F32, F32, BF16, BF16],
        epilogue=twice, separate=True)
    dy_raw, dproj, g_ssd_norm = _gated_bwd(y_ssd_raw, proj, ssd_norm_w, dy_ssd, DS, DMAIN, tm_conv)
    dxs, db, dc, ddt_raw, g_a_log, g_d_skip, g_dt_bias = _ssd_bwd(*ssd_args, dy_raw, states, DS, N)
    g_conv_w, g_conv_b = [], []
    for tag, dact, first in (("xs", dxs, 0), ("b", db, DS), ("c", dc, DS + G * N)):
        dproj, gw, gb = _conv_bwd("conv_bwd_" + tag, proj, dact, conv_w_f, conv_b, dproj, DS, first, tm_conv, cwid)
        g_conv_w.append(gw)
        g_conv_b.append(gb)
    g_conv_w, g_conv_b = jnp.concatenate(g_conv_w, axis=1), jnp.concatenate(g_conv_b, axis=1)
    dproj, g_pool_w, g_pool_scale = _pool_bwd(dy_pool, pooled, pool_w_f, pool_scale, dproj, DS + DCONV, tm_conv)
    ddt_pad = jnp.pad(jnp.transpose(ddt_raw, (1, 0, 2)).reshape(L, H), ((0, 0), (0, LANES - H))).astype(BF16)

    tk_main = _tile(DMAIN, DMAIN // 2, LANES)
    g_w_main, g_w_main16 = _mm("proj_main_dw", "tn", [(hn1, dproj)], D, DMAIN, L, 1024, _tile(DMAIN, 1024, LANES),
                               tk_tok, [F32, BF16], epilogue=twice)
    names_mix = ["w_in", "pool_w", "w_out"]
    pool_shards = jnp.moveaxis(g_pool_w.reshape(PG, N_CHIPS, PGD // N_CHIPS, PGD), 1, 0).reshape(N_CHIPS, -1, PGD)
    out_shards = lambda top, bot: jnp.stack([top.reshape(2, DS // 2, D), bot.reshape(2, DS // 2, D)]).reshape(N_CHIPS, -1, D)
    started_mix = start_reduce(
        "mix", names_mix, [g_w_main, pool_shards, out_shards(g_w_out_top, g_w_out_bot)],
        [g_w_main16, pool_shards.astype(BF16), out_shards(g_w_out_top16, g_w_out_bot16)],
        [(win.ws, win.ww), None, None])
    dhn1a, = _mm("proj_main_dx", "nt", [(dproj, w_main)], L, D, DMAIN, tm_mm, 512, tk_main, [F32],
                 epilogue=add_row, row_extras=[pin_row(started_mix, D)])
    dhn1b, = _mm("proj_dt_dx", "nt", [(ddt_pad, w_dt)], L, D, LANES, tm_mm, 512, LANES, [F32])
    g_w_dt, = _mm("proj_dt_dw", "tn", [(hn1, ddt_pad)], D, LANES, L, 512, LANES, tk_tok, [F32])
    grad_x, g_attn_norm = _rms_bwd("rms1_bwd", xl, attn_norm_w, [dhn1a, dhn1b], dh1, tm_row, False)

    def finish_reduce(tag, names, started, after):
        partials, landed, _ = _scatter_wait("scatter_wait_" + tag, started, after)
        landed = [lax.dynamic_update_slice(l, lax.dynamic_index_in_dim(p, chip, 0), (chip, 0, 0))
                  for l, p in zip(landed, partials)]
        return [_chip_sum("chip_sum_" + n, l, pos) for n, l in zip(names, landed)]

    halves = finish_reduce("ffn", names_ffn, started_ffn, grad_x) + finish_reduce("mix", names_mix, started_mix, grad_x)
    red = dict(zip(names_ffn + names_mix, _join_halves(halves)))

    small_w = [attn_norm_w, conv_b, dt_bias, a_log, d_skip, ssd_norm_w, pool_scale, ffn_norm_w, final_norm_w]
    small_m = [m_attn_norm_w, m_conv_b, m_dt_bias, m_a_log, m_d_skip, m_ssd_norm_w, m_pool_scale, m_ffn_norm_w, m_final_norm_w]
    small_v = [v_attn_norm_w, v_conv_b, v_dt_bias, v_a_log, v_d_skip, v_ssd_norm_w, v_pool_scale, v_ffn_norm_w, v_final_norm_w]
    small_g = [g_attn_norm, g_conv_b, g_dt_bias.reshape(1, H), g_a_log.reshape(1, H), g_d_skip.reshape(1, H),
               g_ssd_norm, g_pool_scale, g_ffn_norm, g_final.reshape(D)]
    extra_shapes = [(CONV_WIDTH, DCONV), (D, H), (1, LANES)]
    zeros_like_extra = [jnp.zeros(s, F32) for s in extra_shapes]
    g_blk = _pack_rows(small_g + [g_conv_w, g_w_dt[:, :H], loss_blk], LANES)
    rows = g_blk.shape[0]
    small_all = _all_gather_small("gather_small_grads", g_blk)
    s_g, s_d, s_m, s_v = _small_sum_adam(small_all, _pack_rows(small_w + zeros_like_extra, LANES),
                                         _pack_rows(small_m + zeros_like_extra, LANES),
                                         _pack_rows(small_v + zeros_like_extra, LANES), rows)
    shapes = [w.shape for w in small_w] + extra_shapes
    sg_list = _unpack_rows(s_g, shapes, LANES)
    sd_list = _unpack_rows(s_d, shapes, LANES)[:len(small_w)]
    sm_list = _unpack_rows(s_m, shapes, LANES)[:len(small_w)]
    sv_list = _unpack_rows(s_v, shapes, LANES)[:len(small_w)]
    loss = sg_list[-1][0, 0]
    grad_conv_w = lax.dynamic_slice(sg_list[-3], (0, chip * (DCONV // N_CHIPS)), (CONV_WIDTH, DCONV // N_CHIPS))
    grad_w_in = lax.switch(chip, [functools.partial(win.from_window, k) for k in range(N_CHIPS)], red["w_in"], sg_list[-2])

    def adam_nd(name, w, g, m, v):
        shp = w.shape
        to2 = lambda a: a.reshape(-1, shp[-1])
        return tuple(o.reshape(shp) for o in _adam(name, to2(w), to2(g), to2(m), to2(v)))

    sharded = {
        "w_in": (w_in, grad_w_in[None], m_w_in, v_w_in),
        "conv_w": (conv_w, grad_conv_w[None], m_conv_w, v_conv_w),
        "pool_w": (pool_w, red["pool_w"].reshape(pool_w.shape), m_pool_w, v_pool_w),
        "w_out": (w_out, red["w_out"][None], m_w_out, v_w_out),
        "w_gate": (w_gate, red["w_gate"][None], m_w_gate, v_w_gate),
        "w_up": (w_up, red["w_up"][None], m_w_up, v_w_up),
        "w_down": (w_down, red["w_down"][None], m_w_down, v_w_down),
    }
    upd = {n: adam_nd("adam_" + n, *a) for n, a in sharded.items()}
    small_names = ["attn_norm_w", "conv_b", "dt_bias", "a_log", "d_skip", "ssd_norm_w", "pool_scale", "ffn_norm_w",
                   "final_norm_w"]
    for i, n in enumerate(small_names):
        upd[n] = (sg_list[i], sd_list[i], sm_list[i], sv_list[i])

    order = ["attn_norm_w", "w_in", "conv_w", "conv_b", "dt_bias", "a_log", "d_skip", "ssd_norm_w", "pool_w",
             "pool_scale", "w_out", "ffn_norm_w", "w_gate", "w_up", "w_down", "final_norm_w"]
    outs = [loss, grad_x[None]]
    for j in range(4):
        outs += [upd[n][j] for n in order]
    return tuple(outs)
```

```python
import functools
import math

import jax
import jax.numpy as jnp
from jax import lax
from jax.experimental import pallas as pl
from jax.experimental.pallas import tpu as pltpu

F32 = jnp.float32
BF16 = jnp.bfloat16

NORM_EPS = 1e-5
HEAD_DIM = 64
SSD_GROUPS = 4
CONV_WIDTH = 4
CHUNK = 256
POOL_WINDOWS = (2, 4, 8, 16)
ADAM_LR = 0.001
ADAM_B1 = 0.9
ADAM_B2 = 0.999
ADAM_EPS = 1e-08
ADAM_WD = 0.01
ADAM_STEP = 10

N_CHIPS = 4
N_DEV = 8
LANES = 128
HALO = 16
FLAT_W = 512
VMEM_LIMIT = 52 * 1024 * 1024
MESH = pl.DeviceIdType.MESH

NN = (((1,), (0,)), ((), ()))
NT = (((1,), (1,)), ((), ()))
TN = (((0,), (0,)), ((), ()))


def _tile(n, cap, mult):
    best = None
    for t in range(mult, min(n, cap) + 1, mult):
        if n % t == 0:
            best = t
    return best if best is not None else n


def _params(sem):
    return pltpu.CompilerParams(dimension_semantics=sem, vmem_limit_bytes=VMEM_LIMIT)


def _dot(a, b, dims):
    return lax.dot_general(a, b, dims, preferred_element_type=F32)


def _sigmoid(x):
    return 1.0 / (1.0 + jnp.exp(-x))


def _silu(x):
    return x * _sigmoid(x)


def _softplus(x):
    return jnp.maximum(x, 0.0) + jnp.log(1.0 + jnp.exp(-jnp.abs(x)))


def _mm(name, mode, pairs, M, N, K, tm, tn, tk, out_dtypes, epilogue=None, extras=(), row_extras=(),
        separate=False):
    tm, tn, tk = min(tm, M), min(tn, N), min(tk, K)
    assert M % tm == 0 and N % tn == 0 and K % tk == 0, (name, M, N, K, tm, tn, tk)
    nk = K // tk
    npairs = len(pairs)
    nacc = npairs if separate else 1
    if mode == "nn":
        a_spec = pl.BlockSpec((tm, tk), lambda i, j, k: (i, k))
        b_spec = pl.BlockSpec((tk, tn), lambda i, j, k: (k, j))
        dims = NN
    elif mode == "nt":
        a_spec = pl.BlockSpec((tm, tk), lambda i, j, k: (i, k))
        b_spec = pl.BlockSpec((tn, tk), lambda i, j, k: (j, k))
        dims = NT
    else:
        a_spec = pl.BlockSpec((tk, tm), lambda i, j, k: (k, i))
        b_spec = pl.BlockSpec((tk, tn), lambda i, j, k: (k, j))
        dims = TN
    o_spec = pl.BlockSpec((tm, tn), lambda i, j, k: (i, j))
    r_spec = pl.BlockSpec((1, tn), lambda i, j, k: (0, j))
    if epilogue is None:
        epilogue = lambda accs, ex, rex: accs
    n_ex, n_rex, n_out = len(extras), len(row_extras), len(out_dtypes)

    def body(*refs):
        ab = refs[:2 * npairs]
        ex = refs[2 * npairs:2 * npairs + n_ex]
        rex = refs[2 * npairs + n_ex:2 * npairs + n_ex + n_rex]
        outs = refs[2 * npairs + n_ex + n_rex:2 * npairs + n_ex + n_rex + n_out]
        accs = refs[2 * npairs + n_ex + n_rex + n_out:]

        def products():
            res = [None] * nacc
            for p in range(npairs):
                d = _dot(ab[2 * p][...], ab[2 * p + 1][...], dims)
                q = p if separate else 0
                res[q] = d if res[q] is None else res[q] + d
            return res

        def finish(vals):
            res = epilogue(vals, [e[...] for e in ex], [r[...] for r in rex])
            for o, v in zip(outs, res):
                o[...] = v.astype(o.dtype)

        if nk == 1:
            finish(products())
        else:
            k = pl.program_id(2)

            @pl.when(k == 0)
            def _():
                for q in range(nacc):
                    accs[q][...] = jnp.zeros_like(accs[q])

            for p in range(npairs):
                accs[p if separate else 0][...] += _dot(ab[2 * p][...], ab[2 * p + 1][...], dims)

            @pl.when(k == nk - 1)
            def _():
                finish([a[...] for a in accs])

    in_specs = [a_spec, b_spec] * npairs + [o_spec] * n_ex + [r_spec] * n_rex
    args = [t for p in pairs for t in p] + list(extras) + list(row_extras)
    outs = pl.pallas_call(
        body,
        grid=(M // tm, N // tn, nk),
        in_specs=in_specs,
        out_specs=[o_spec] * n_out,
        out_shape=[jax.ShapeDtypeStruct((M, N), d) for d in out_dtypes],
        scratch_shapes=[pltpu.VMEM((tm, tn), F32) for _ in range(nacc if nk > 1 else 0)],
        compiler_params=_params(("parallel", "parallel", "arbitrary")),
        name=name,
    )(*args)
    return outs


def _rms(xf, w):
    y = xf * lax.rsqrt(jnp.mean(xf * xf, axis=-1, keepdims=True) + NORM_EPS)
    return y * w


def _rms_fwd(name, x, w, tm):
    L, D = x.shape

    def body(x_ref, w_ref, o_ref):
        o_ref[...] = _rms(x_ref[...], w_ref[...]).astype(BF16)

    return pl.pallas_call(
        body, grid=(L // tm,),
        in_specs=[pl.BlockSpec((tm, D), lambda i: (i, 0)), pl.BlockSpec((1, D), lambda i: (0, 0))],
        out_specs=pl.BlockSpec((tm, D), lambda i: (i, 0)),
        out_shape=jax.ShapeDtypeStruct((L, D), BF16),
        compiler_params=_params(("parallel",)), name=name)(x, w)


def _rms_bwd(name, x, w, dparts, dres, tm, with_bf16):
    L, D = x.shape
    nparts = len(dparts)

    def body(*refs):
        x_ref, w_ref = refs[:2]
        p_refs = refs[2:2 + nparts]
        r_ref = refs[2 + nparts]
        outs = refs[3 + nparts:]
        dhn = p_refs[0][...]
        for p in p_refs[1:]:
            dhn = dhn + p[...]
        _, vjp = jax.vjp(_rms, x_ref[...], w_ref[...])
        dx, dw = vjp(dhn)
        dx = dx + r_ref[...]
        outs[0][...] = dx
        gw_ref = outs[1]

        @pl.when(pl.program_id(0) == 0)
        def _():
            gw_ref[...] = jnp.zeros_like(gw_ref)

        gw_ref[...] += dw
        if with_bf16:
            outs[2][...] = dx.astype(BF16)

    row = pl.BlockSpec((tm, D), lambda i: (i, 0))
    vec = pl.BlockSpec((1, D), lambda i: (0, 0))
    out_shape = [jax.ShapeDtypeStruct((L, D), F32), jax.ShapeDtypeStruct((1, D), F32)]
    out_specs = [row, vec]
    if with_bf16:
        out_shape.append(jax.ShapeDtypeStruct((L, D), BF16))
        out_specs.append(row)
    return pl.pallas_call(
        body, grid=(L // tm,),
        in_specs=[row, vec] + [row] * nparts + [row],
        out_specs=out_specs, out_shape=out_shape,
        compiler_params=_params(("arbitrary",)), name=name)(x, w, *dparts, dres)


def _final_loss(h2, wf, target, tm):
    L, D = h2.shape

    def body(h_ref, w_ref, t_ref, dh_ref, dhb_ref, loss_ref, gw_ref):
        t = t_ref[...]

        def f(h, w):
            err = jnp.square(_rms(h, w) - t)
            return 0.5 * jnp.sum(jnp.mean(err, axis=-1))

        val, vjp = jax.vjp(f, h_ref[...], w_ref[...])
        dh, dw = vjp(jnp.ones((), F32))
        dh_ref[...] = dh
        dhb_ref[...] = dh.astype(BF16)

        @pl.when(pl.program_id(0) == 0)
        def _():
            gw_ref[...] = jnp.zeros_like(gw_ref)
            loss_ref[...] = jnp.zeros_like(loss_ref)

        gw_ref[...] += dw
        loss_ref[...] += jnp.full(loss_ref.shape, val, F32)

    row = pl.BlockSpec((tm, D), lambda i: (i, 0))
    vec = pl.BlockSpec((1, D), lambda i: (0, 0))
    lspec = pl.BlockSpec((1, LANES), lambda i: (0, 0))
    return pl.pallas_call(
        body, grid=(L // tm,),
        in_specs=[row, vec, row],
        out_specs=[row, row, lspec, vec],
        out_shape=[jax.ShapeDtypeStruct((L, D), F32), jax.ShapeDtypeStruct((L, D), BF16),
                   jax.ShapeDtypeStruct((1, LANES), F32), jax.ShapeDtypeStruct((1, D), F32)],
        compiler_params=_params(("arbitrary",)), name="final_loss")(h2, wf, target)


def _gated(y, z, w):
    g = y * _silu(z)
    g = g * lax.rsqrt(jnp.mean(g * g, axis=-1, keepdims=True) + NORM_EPS)
    return g * w


def _gated_fwd(y, proj, w, DS, tm):
    L = y.shape[0]
    GW = DS // SSD_GROUPS

    def body(y_ref, z_ref, w_ref, o_ref):
        o_ref[...] = _gated(y_ref[...], z_ref[...], w_ref[...]).astype(BF16)

    blk = pl.BlockSpec((tm, GW), lambda i, g: (i, g))
    return pl.pallas_call(
        body, grid=(L // tm, SSD_GROUPS),
        in_specs=[blk, blk, pl.BlockSpec((1, GW), lambda i, g: (0, g))],
        out_specs=blk, out_shape=jax.ShapeDtypeStruct((L, DS), BF16),
        compiler_params=_params(("parallel", "parallel")), name="gated_fwd")(y, proj, w)


def _gated_bwd(y, proj, w, dout, DS, dproj_cols, tm):
    L = y.shape[0]
    GW = DS // SSD_GROUPS

    def body(y_ref, z_ref, w_ref, d_ref, dy_ref, dz_ref, gw_ref):
        _, vjp = jax.vjp(_gated, y_ref[...], z_ref[...], w_ref[...])
        dy, dz, dw = vjp(d_ref[...])
        dy_ref[...] = dy
        dz_ref[...] = dz.astype(BF16)

        @pl.when(pl.program_id(1) == 0)
        def _():
            gw_ref[...] = jnp.zeros_like(gw_ref)

        gw_ref[...] += dw

    blk = pl.BlockSpec((tm, GW), lambda g, i: (i, g))
    vec = pl.BlockSpec((1, GW), lambda g, i: (0, g))
    return pl.pallas_call(
        body, grid=(SSD_GROUPS, L // tm),
        in_specs=[blk, blk, vec, blk],
        out_specs=[blk, blk, vec],
        out_shape=[jax.ShapeDtypeStruct((L, DS), F32), jax.ShapeDtypeStruct((L, dproj_cols), BF16),
                   jax.ShapeDtypeStruct((1, DS), F32)],
        compiler_params=_params(("parallel", "arbitrary")), name="gated_bwd")(y, proj, w, dout)


def _halo_prev(tm, cw, col0):
    return pl.BlockSpec((HALO, cw), lambda i, j: (jnp.maximum(i * (tm // HALO) - 1, 0), col0 + j))


def _halo_next(tm, cw, col0, L):
    return pl.BlockSpec((HALO, cw), lambda i, j: (jnp.minimum((i + 1) * (tm // HALO), L // HALO - 1), col0 + j))


def _conv_fwd(proj, conv_w, conv_b, DS, DCONV, tm, cw):
    L = proj.shape[0]
    col0 = DS // cw
    K = CONV_WIDTH

    def body(x_ref, p_ref, w_ref, b_ref, o_ref, ext):
        i = pl.program_id(0)
        ext[0:HALO, :] = jnp.where(i == 0, 0.0, p_ref[...])
        ext[HALO:, :] = x_ref[...]
        acc = jnp.broadcast_to(b_ref[...], (tm, cw))
        for k in range(K):
            acc = acc + w_ref[k:k + 1, :] * ext[pl.ds(HALO - (K - 1) + k, tm), :]
        o_ref[...] = _silu(acc)

    return pl.pallas_call(
        body, grid=(L // tm, DCONV // cw),
        in_specs=[pl.BlockSpec((tm, cw), lambda i, j: (i, col0 + j)), _halo_prev(tm, cw, col0),
                  pl.BlockSpec((K, cw), lambda i, j: (0, j)), pl.BlockSpec((1, cw), lambda i, j: (0, j))],
        out_specs=pl.BlockSpec((tm, cw), lambda i, j: (i, j)),
        out_shape=jax.ShapeDtypeStruct((L, DCONV), F32),
        scratch_shapes=[pltpu.VMEM((tm + HALO, cw), F32)],
        compiler_params=_params(("parallel", "parallel")), name="conv_fwd")(proj, proj, conv_w, conv_b)


def _conv_bwd(name, proj, dact, conv_w, conv_b, dproj, DS, first, tm, cw):
    L = proj.shape[0]
    ncols = dact.shape[1]
    col0 = (DS + first) // cw
    wcol0 = first // cw
    K = CONV_WIDTH
    nrt = L // tm

    def body(x_ref, p_ref, n_ref, d_ref, dn_ref, w_ref, b_ref, alias_ref, dx_ref, dw_ref, db_ref, ext, dext):
        i = pl.program_id(1)
        last = i == nrt - 1
        ext[0:HALO, :] = jnp.where(i == 0, 0.0, p_ref[...])
        ext[HALO:HALO + tm, :] = x_ref[...]
        ext[HALO + tm:, :] = n_ref[...]
        dfull = jnp.concatenate([d_ref[...], jnp.where(last, 0.0, dn_ref[...])], axis=0)
        acc = jnp.broadcast_to(b_ref[...], (tm + HALO, cw))
        for k in range(K):
            acc = acc + w_ref[k:k + 1, :] * ext[pl.ds(HALO - (K - 1) + k, tm + HALO), :]
        sg = _sigmoid(acc)
        dconv = dfull * (sg * (1.0 + acc * (1.0 - sg)))
        dext[...] = dconv

        @pl.when(i == 0)
        def _():
            dw_ref[...] = jnp.zeros_like(dw_ref)
            db_ref[...] = jnp.zeros_like(db_ref)

        dx = jnp.zeros((tm, cw), F32)
        for k in range(K):
            dx = dx + w_ref[k:k + 1, :] * dext[pl.ds(K - 1 - k, tm), :]
        dx_ref[...] = dx.astype(BF16)
        dtile = dext[pl.ds(0, tm), :]
        db_ref[...] += jnp.sum(dtile, axis=0, keepdims=True)
        for k in range(K):
            dw_ref[k:k + 1, :] += jnp.sum(dtile * ext[pl.ds(HALO - (K - 1) + k, tm), :], axis=0, keepdims=True)

    prev = pl.BlockSpec((HALO, cw), lambda j, i: (jnp.maximum(i * (tm // HALO) - 1, 0), col0 + j))
    nxt = pl.BlockSpec((HALO, cw), lambda j, i: (jnp.minimum((i + 1) * (tm // HALO), L // HALO - 1), col0 + j))
    dnxt = pl.BlockSpec((HALO, cw), lambda j, i: (jnp.minimum((i + 1) * (tm // HALO), L // HALO - 1), j))
    return pl.pallas_call(
        body, grid=(ncols // cw, nrt),
        in_specs=[pl.BlockSpec((tm, cw), lambda j, i: (i, col0 + j)), prev, nxt,
                  pl.BlockSpec((tm, cw), lambda j, i: (i, j)), dnxt,
                  pl.BlockSpec((K, cw), lambda j, i: (0, wcol0 + j)), pl.BlockSpec((1, cw), lambda j, i: (0, wcol0 + j)),
                  _ANY],
        out_specs=[pl.BlockSpec((tm, cw), lambda j, i: (i, col0 + j)),
                   pl.BlockSpec((K, cw), lambda j, i: (0, j)), pl.BlockSpec((1, cw), lambda j, i: (0, j))],
        out_shape=[jax.ShapeDtypeStruct(dproj.shape, BF16), jax.ShapeDtypeStruct((K, ncols), F32),
                   jax.ShapeDtypeStruct((1, ncols), F32)],
        input_output_aliases={7: 0},
        scratch_shapes=[pltpu.VMEM((tm + 2 * HALO, cw), F32), pltpu.VMEM((tm + HALO, cw), F32)],
        compiler_params=_params(("parallel", "arbitrary")), name=name,
    )(proj, proj, proj, dact, dact, conv_w, conv_b, dproj)


def _pool_fwd(proj, pool_w, pool_scale, ucol, DP, tm):
    L = proj.shape[0]
    PG = len(POOL_WINDOWS)
    PGD = DP // PG
    col0 = ucol // PGD

    def body(u_ref, p_ref, w_ref, s_ref, pooled_ref, y_ref, ext):
        i, g = pl.program_id(0), pl.program_id(1)
        ext[0:HALO, :] = jnp.where(i == 0, 0.0, p_ref[...])
        ext[HALO:, :] = u_ref[...]
        t = i * tm + lax.broadcasted_iota(jnp.int32, (tm, 1), 0)
        for gi, win in enumerate(POOL_WINDOWS):
            @pl.when(g == gi)
            def _():
                acc = ext[pl.ds(HALO, tm), :]
                for j in range(1, win):
                    acc = acc + ext[pl.ds(HALO - j, tm), :]
                count = jnp.minimum(t + 1, win).astype(F32)
                pooled = (acc / count - u_ref[...]).astype(BF16)
                pooled_ref[...] = pooled
                y_ref[...] = (_dot(pooled, w_ref[...], NN) * s_ref[...]).astype(BF16)

    blk = pl.BlockSpec((tm, PGD), lambda i, g: (i, g))
    return pl.pallas_call(
        body, grid=(L // tm, PG),
        in_specs=[pl.BlockSpec((tm, PGD), lambda i, g: (i, col0 + g)), _halo_prev(tm, PGD, col0),
                  pl.BlockSpec((None, PGD, PGD), lambda i, g: (g, 0, 0)), pl.BlockSpec((1, PGD), lambda i, g: (0, g))],
        out_specs=[blk, blk],
        out_shape=[jax.ShapeDtypeStruct((L, DP), BF16), jax.ShapeDtypeStruct((L, DP), BF16)],
        scratch_shapes=[pltpu.VMEM((tm + HALO, PGD), F32)],
        compiler_params=_params(("parallel", "parallel")), name="pool_fwd")(proj, proj, pool_w, pool_scale)


def _pool_bwd(dy, pooled, pool_w, pool_scale, dproj, ucol, tm):
    L, DP = dy.shape
    PG = len(POOL_WINDOWS)
    PGD = DP // PG
    nrt = L // tm
    col0 = ucol // PGD

    def body(d_ref, dn_ref, p_ref, w_ref, s_ref, alias_ref, du_ref, dw_ref, ds_ref, qext):
        g, i = pl.program_id(0), pl.program_id(1)
        last = i == nrt - 1
        dfull = jnp.concatenate([d_ref[...], jnp.where(last, 0.0, dn_ref[...])], axis=0)
        dyp = (dfull * s_ref[...]).astype(BF16)
        dpooled = _dot(dyp, w_ref[...], NT)
        t = i * tm + lax.broadcasted_iota(jnp.int32, (tm + HALO, 1), 0)
        for gi, win in enumerate(POOL_WINDOWS):
            @pl.when(g == gi)
            def _():
                qext[...] = dpooled / jnp.minimum(t + 1, win).astype(F32)
                acc = qext[pl.ds(0, tm), :]
                for j in range(1, win):
                    acc = acc + qext[pl.ds(j, tm), :]
                du_ref[...] = (acc - dpooled[0:tm, :]).astype(BF16)

        @pl.when(i == 0)
        def _():
            dw_ref[...] = jnp.zeros_like(dw_ref)
            ds_ref[...] = jnp.zeros_like(ds_ref)

        pooled_t = p_ref[...]
        dw_ref[...] += _dot(pooled_t, dyp[0:tm, :], TN)
        ypre = _dot(pooled_t, w_ref[...], NN)
        ds_ref[...] += jnp.sum(d_ref[...] * ypre, axis=0, keepdims=True)

    blk = pl.BlockSpec((tm, PGD), lambda g, i: (i, g))
    nxt = pl.BlockSpec((HALO, PGD), lambda g, i: (jnp.minimum((i + 1) * (tm // HALO), L // HALO - 1), g))
    wspec = pl.BlockSpec((None, PGD, PGD), lambda g, i: (g, 0, 0))
    vec = pl.BlockSpec((1, PGD), lambda g, i: (0, g))
    return pl.pallas_call(
        body, grid=(PG, nrt),
        in_specs=[blk, nxt, blk, wspec, vec, _ANY],
        out_specs=[pl.BlockSpec((tm, PGD), lambda g, i: (i, col0 + g)), wspec, vec],
        out_shape=[jax.ShapeDtypeStruct(dproj.shape, BF16), jax.ShapeDtypeStruct((PG, PGD, PGD), F32),
                   jax.ShapeDtypeStruct((1, DP), F32)],
        input_output_aliases={5: 0},
        scratch_shapes=[pltpu.VMEM((tm + HALO, PGD), F32)],
        compiler_params=_params(("parallel", "arbitrary")), name="pool_bwd",
    )(dy, dy, pooled, pool_w, pool_scale, dproj)


def _ssd_common(dtc_raw, dtr_raw, bc, br, ac, ar):
    ch = CHUNK
    row = lax.broadcasted_iota(jnp.int32, (ch, ch), 0)
    col = lax.broadcasted_iota(jnp.int32, (ch, ch), 1)
    lower = row >= col
    dtc = _softplus(dtc_raw + bc)
    dtr = _softplus(dtr_raw + br)
    a_c = -jnp.exp(ac)
    a_r = -jnp.exp(ar)
    hi = lax.Precision.HIGHEST
    acol = jnp.dot(lower.astype(F32), dtc * a_c, preferred_element_type=F32, precision=hi)
    arow = jnp.dot(dtr * a_r, (row <= col).astype(F32), preferred_element_type=F32, precision=hi)
    return lower, row <= col, dtc, a_c, acol, arow, dtr


def _spread_heads(cols, passes=3):
    R = cols.shape[1]
    shape = (R, R * LANES)
    spread = (lax.broadcasted_iota(jnp.int32, shape, 0) == lax.broadcasted_iota(jnp.int32, shape, 1) // LANES).astype(BF16)
    out, rest = None, cols
    for _ in range(passes):
        part = rest.astype(BF16)
        rest = rest - part.astype(F32)
        term = _dot(part, spread, NN)
        out = term if out is None else out + term
    return out


def _ssd_fwd(xbc, dtc_raw, dtr_raw, bias_c, bias_r, alog_c, alog_r, dskip_c, DS, N):
    L = xbc.shape[0]
    G, P, ch = SSD_GROUPS, HEAD_DIM, CHUNK
    R = dtc_raw.shape[2]
    GW = R * P
    nc = L // ch

    def body(xs_ref, b_ref, c_ref, dtc_ref, dtr_ref, bc_ref, br_ref, ac_ref, ar_ref, dk_ref,
             y_ref, st_ref, h_ref):
        @pl.when(pl.program_id(1) == 0)
        def _():
            h_ref[...] = jnp.zeros_like(h_ref)

        lower, _, _, _, acol_all, arow_all, dtr = _ssd_common(
            dtc_ref[...], dtr_ref[...], bc_ref[...], br_ref[...], ac_ref[...], ar_ref[...])
        bm = b_ref[...]
        cb16 = c_ref[...].astype(BF16)
        b16 = bm.astype(BF16)
        bt = bm.T
        cb = _dot(cb16, b16, NT)
        dk = dk_ref[...]
        st_ref[...] = h_ref[...]
        for r in range(R):
            acol = acol_all[:, r:r + 1]
            arow = arow_all[r:r + 1, :]
            alast = acol_all[ch - 1:ch, r:r + 1]
            dt_row = dtr[r:r + 1, :]
            decay = jnp.exp(jnp.where(lower, acol - arow, -1e30))
            x_h = xs_ref[:, pl.ds(r * P, P)]
            x16 = x_h.astype(BF16)
            m16 = (cb * decay * dt_row).astype(BF16)
            h_prev = h_ref[r]
            y = _dot(m16, x16, NN)
            y = y + jnp.exp(acol) * _dot(cb16, h_prev.astype(BF16), NN)
            y = y + dk[:, r:r + 1] * x_h
            y_ref[:, pl.ds(r * P, P)] = y
            to_end_dt = jnp.exp(alast - arow) * dt_row
            h_ref[r] = jnp.exp(alast) * h_prev + _dot((bt * to_end_dt).astype(BF16), x16, NN)

    nb = DS // N
    return pl.pallas_call(
        body, grid=(G, nc),
        in_specs=[pl.BlockSpec((ch, GW), lambda g, c: (c, g)),
                  pl.BlockSpec((ch, N), lambda g, c: (c, nb + g)),
                  pl.BlockSpec((ch, N), lambda g, c: (c, nb + G + g)),
                  pl.BlockSpec((None, ch, R), lambda g, c: (g, c, 0)),
                  pl.BlockSpec((None, R, ch), lambda g, c: (g, 0, c)),
                  pl.BlockSpec((None, 1, R), lambda g, c: (g, 0, 0)),
                  pl.BlockSpec((None, R, 1), lambda g, c: (g, 0, 0)),
                  pl.BlockSpec((None, 1, R), lambda g, c: (g, 0, 0)),
                  pl.BlockSpec((None, R, 1), lambda g, c: (g, 0, 0)),
                  pl.BlockSpec((None, 1, R), lambda g, c: (g, 0, 0))],
        out_specs=[pl.BlockSpec((ch, GW), lambda g, c: (c, g)),
                   pl.BlockSpec((None, R, N, P), lambda g, c: (c, g, 0, 0))],
        out_shape=[jax.ShapeDtypeStruct((L, DS), F32), jax.ShapeDtypeStruct((nc, G * R, N, P), F32)],
        scratch_shapes=[pltpu.VMEM((R, N, P), F32)],
        compiler_params=_params(("parallel", "arbitrary")), name="ssd_fwd",
    )(xbc, xbc, xbc, dtc_raw, dtr_raw, bias_c, bias_r, alog_c, alog_r, dskip_c)


def _ssd_bwd(xbc, dtc_raw, dtr_raw, bias_c, bias_r, alog_c, alog_r, dskip_c, dy, states, DS, N):
    L = xbc.shape[0]
    G, P, ch = SSD_GROUPS, HEAD_DIM, CHUNK
    R = dtc_raw.shape[2]
    GW = R * P
    nc = L // ch
    assert N == LANES and P <= LANES and ch % LANES == 0

    def body(xs_ref, b_ref, c_ref, dtc_ref, dtr_ref, bc_ref, br_ref, ac_ref, ar_ref, dk_ref,
             dy_ref, stp_ref,
             dxs_ref, db_ref, dc_ref, ddt_ref, dal_ref, ddk_ref, dbias_ref, dh_ref):
        @pl.when(pl.program_id(1) == 0)
        def _():
            dh_ref[...] = jnp.zeros_like(dh_ref)
            dal_ref[...] = jnp.zeros_like(dal_ref)
            ddk_ref[...] = jnp.zeros_like(ddk_ref)
            dbias_ref[...] = jnp.zeros_like(dbias_ref)

        lower, upper, dtc, a_c, acol_all, arow_all, _ = _ssd_common(
            dtc_ref[...], dtr_ref[...], bc_ref[...], br_ref[...], ac_ref[...], ar_ref[...])
        bm = b_ref[...]
        cm = c_ref[...]
        b16 = bm.astype(BF16)
        c16 = cm.astype(BF16)
        ct16 = cm.T.astype(BF16)
        cb = _dot(c16, b16, NT)
        cbt = _dot(b16, c16, NT)
        dk = dk_ref[...]
        lane_r = lax.broadcasted_iota(jnp.int32, (ch, R), 1)
        lane_1 = lax.broadcasted_iota(jnp.int32, (1, R), 1)
        dc = jnp.zeros((ch, N), F32)
        db = jnp.zeros((ch, N), F32)
        da_all = jnp.zeros((R, ch), F32)
        q_all = jnp.zeros((R, ch), F32)
        sxd_all = jnp.zeros((ch, R), F32)
        const = jnp.zeros((1, R), F32)
        ddk = jnp.zeros((1, R), F32)
        sub_r = lax.broadcasted_iota(jnp.int32, (R, ch), 0)
        ct = cm.T
        bt = bm.T
        dcb = jnp.zeros((ch, ch), F32)
        acol_lanes = _spread_heads(acol_all)
        dt_lanes = _spread_heads(dtc, passes=2)
        for r in range(R):
            a128 = acol_lanes[:, r * LANES:(r + 1) * LANES]
            arow = arow_all[r:r + 1, :]
            alast = acol_all[ch - 1:ch, r:r + 1]
            seg = jnp.tile(a128, (1, ch // LANES)) - arow
            decay = jnp.exp(jnp.where(lower, seg, -1e30))
            decay_t = jnp.exp(jnp.where(upper, -seg, -1e30))
            x_h = xs_ref[:, pl.ds(r * P, P)]
            dy_h = dy_ref[:, pl.ds(r * P, P)]
            dt_h = dt_lanes[:, r * LANES:r * LANES + P]
            dk_h = dk[:, r:r + 1]
            xdt = x_h * dt_h
            xdt16 = xdt.astype(BF16)
            dy16 = dy_h.astype(BF16)
            h_prev = stp_ref[r]
            h16 = h_prev.astype(BF16)
            dh_next = dh_ref[r]
            dhn16 = dh_next.astype(BF16)
            to_end_n = jnp.exp(alast - a128)
            e_a_n = jnp.exp(a128)
            to_end, e_a = to_end_n[:, :P], e_a_n[:, :P]
            mt = cbt * decay_t
            pm = _dot(dy16, xdt16, NT) * decay
            wt = _dot(xdt16, dy16, NT) * mt
            dxdt = _dot(mt.astype(BF16), dy16, NN) + to_end * _dot(b16, dhn16, NN)
            dcb = dcb + pm
            dc = dc + e_a_n * _dot(dy16, h16, NT)
            db = db + to_end_n * _dot(xdt16, dhn16, NT)
            dh_ref[r] = jnp.exp(alast) * dh_next + _dot(ct16, (dy_h * e_a).astype(BF16), NN)
            da = (jnp.sum(wt, axis=0, keepdims=True) - jnp.sum(pm * cb, axis=0, keepdims=True)
                  + jnp.exp(arow) * jnp.sum(ct * _dot(h16, dy16, NT), axis=0, keepdims=True))
            q = jnp.exp(alast - arow) * jnp.sum(bt * _dot(dhn16, xdt16, NT), axis=0, keepdims=True)
            da_all = da_all + jnp.where(sub_r == r, da, 0.0)
            q_all = q_all + jnp.where(sub_r == r, q, 0.0)
            sxd_all = sxd_all + jnp.where(lane_r == r, jnp.sum(dxdt * x_h, axis=1, keepdims=True), 0.0)
            const = const + jnp.where(lane_1 == r, jnp.exp(alast) * jnp.sum(dh_next * h_prev), 0.0)
            ddk = ddk + jnp.where(lane_1 == r, jnp.sum(dy_h * x_h), 0.0)
            dxs_ref[:, pl.ds(r * P, P)] = dxdt * dt_h + dk_h * dy_h
        dcb16 = dcb.astype(BF16)
        dc_ref[...] = dc + _dot(dcb16, b16, NN)
        db_ref[...] = db + _dot(dcb16, c16, TN)
        hi = lax.Precision.HIGHEST
        strict_lower = jnp.logical_and(lower, jnp.logical_not(upper))
        dda = (lax.dot_general(upper.astype(F32), da_all, NT, preferred_element_type=F32, precision=hi)
               + lax.dot_general(strict_lower.astype(F32), q_all, NT, preferred_element_type=F32, precision=hi)
               + const)
        ddt = dda * a_c + sxd_all
        dal_ref[...] += jnp.sum(dda * dtc, axis=0, keepdims=True) * a_c
        ddk_ref[...] += ddk
        ddt_raw = ddt * _sigmoid(dtc_ref[...] + bc_ref[...])
        ddt_ref[...] = ddt_raw
        dbias_ref[...] += jnp.sum(ddt_raw, axis=0, keepdims=True)

    nb = DS // N
    rc = lambda c: nc - 1 - c
    vec_c = pl.BlockSpec((None, 1, R), lambda g, c: (g, 0, 0))
    vec_r = pl.BlockSpec((None, R, 1), lambda g, c: (g, 0, 0))
    big = pl.BlockSpec((ch, GW), lambda g, c: (rc(c), g))
    return pl.pallas_call(
        body, grid=(G, nc),
        in_specs=[big,
                  pl.BlockSpec((ch, N), lambda g, c: (rc(c), nb + g)),
                  pl.BlockSpec((ch, N), lambda g, c: (rc(c), nb + G + g)),
                  pl.BlockSpec((None, ch, R), lambda g, c: (g, rc(c), 0)),
                  pl.BlockSpec((None, R, ch), lambda g, c: (g, 0, rc(c))),
                  vec_c, vec_r, vec_c, vec_r, vec_c,
                  big,
                  pl.BlockSpec((None, R, N, P), lambda g, c: (rc(c), g, 0, 0))],
        out_specs=[big,
                   pl.BlockSpec((ch, N), lambda g, c: (rc(c), g)),
                   pl.BlockSpec((ch, N), lambda g, c: (rc(c), g)),
                   pl.BlockSpec((None, ch, R), lambda g, c: (g, rc(c), 0)),
                   vec_c, vec_c, vec_c],
        out_shape=[jax.ShapeDtypeStruct((L, DS), F32), jax.ShapeDtypeStruct((L, G * N), F32),
                   jax.ShapeDtypeStruct((L, G * N), F32), jax.ShapeDtypeStruct((G, L, R), F32),
                   jax.ShapeDtypeStruct((G, 1, R), F32), jax.ShapeDtypeStruct((G, 1, R), F32),
                   jax.ShapeDtypeStruct((G, 1, R), F32)],
        scratch_shapes=[pltpu.VMEM((R, N, P), F32)],
        compiler_params=_params(("parallel", "arbitrary")), name="ssd_bwd",
    )(xbc, xbc, xbc, dtc_raw, dtr_raw, bias_c, bias_r, alog_c, alog_r, dskip_c, dy, states)


def _adam_math(w, g, m, v):
    m = ADAM_B1 * m + (1.0 - ADAM_B1) * g
    v = ADAM_B2 * v + (1.0 - ADAM_B2) * jnp.square(g)
    m_hat = m / (1.0 - ADAM_B1 ** ADAM_STEP)
    v_hat = v / (1.0 - ADAM_B2 ** ADAM_STEP)
    delta = -ADAM_LR * (m_hat / (jnp.sqrt(v_hat) + ADAM_EPS) + ADAM_WD * w)
    return delta, m, v


def _adam(name, w, g, m, v):
    rows, cols = w.shape
    tr = _tile(rows, max(8, (1 << 19) // cols // 8 * 8), 8)

    def body(w_ref, g_ref, m_ref, v_ref, go_ref, d_ref, mo_ref, vo_ref):
        g = g_ref[...]
        d, m2, v2 = _adam_math(w_ref[...], g, m_ref[...], v_ref[...])
        go_ref[...] = g
        d_ref[...] = d
        mo_ref[...] = m2
        vo_ref[...] = v2

    blk = pl.BlockSpec((tr, cols), lambda i: (i, 0))
    return pl.pallas_call(
        body, grid=(rows // tr,), in_specs=[blk] * 4, out_specs=[blk] * 4,
        out_shape=[jax.ShapeDtypeStruct((rows, cols), F32)] * 4,
        compiler_params=_params(("parallel",)), name=name)(w, g, m, v)


def _small_sum_adam(gathered, w, m, v, rows):
    def body(ga_ref, w_ref, m_ref, v_ref, g_ref, d_ref, mo_ref, vo_ref):
        g = ga_ref[0:rows, :]
        for d in range(1, N_DEV):
            g = g + ga_ref[d * rows:(d + 1) * rows, :]
        g_ref[...] = g
        dl, m2, v2 = _adam_math(w_ref[...], g, m_ref[...], v_ref[...])
        d_ref[...] = dl
        mo_ref[...] = m2
        vo_ref[...] = v2

    return pl.pallas_call(
        body, out_shape=[jax.ShapeDtypeStruct((rows, LANES), F32)] * 4,
        compiler_params=pltpu.CompilerParams(vmem_limit_bytes=VMEM_LIMIT), name="small_sum_adam",
    )(gathered, w, m, v)


def _row_tile(rh, cols):
    return _tile(rh, max(16, (1 << 19) // cols // 16 * 16), 16)


def _shard_dims(g, window=None):
    if g.ndim == 3:
        return g.shape[1], g.shape[2]
    return g.shape[0], (window[1] if window else g.shape[1] // N_CHIPS)


def _pair_sum(name, g, recv, pos, window=None):
    r, c = _shard_dims(g, window)
    rh = r // 2
    cb = LANES if window else c
    tr = _row_tile(rh, cb)
    nrt = rh // tr

    def body(pos_ref, a_ref, b_ref, o_ref):
        o_ref[...] = (a_ref[...] + b_ref[...].astype(F32)).astype(BF16)

    if g.ndim == 3:
        own = pl.BlockSpec((None, tr, c), lambda k, i, j, p: (k, p[1] * nrt + i, 0))
    elif window:
        pos = jnp.concatenate([pos, jnp.asarray([s // LANES for s in window[0]], jnp.int32)])
        own = pl.BlockSpec((tr, cb), lambda k, i, j, p: (p[1] * nrt + i, p[2 + k] + j))
    else:
        own = pl.BlockSpec((tr, c), lambda k, i, j, p: (p[1] * nrt + i, k))
    part = pl.BlockSpec((None, tr, cb), lambda k, i, j, p: (k, i, j))
    return pl.pallas_call(
        body,
        grid_spec=pltpu.PrefetchScalarGridSpec(
            num_scalar_prefetch=1, grid=(N_CHIPS, nrt, c // cb), in_specs=[own, part], out_specs=part),
        out_shape=jax.ShapeDtypeStruct((N_CHIPS, rh, c), BF16),
        compiler_params=_params(("parallel", "parallel", "parallel")), name=name)(pos, g, recv)


def _chip_sum(name, parts, pos):
    _, rh, c = parts.shape
    tr = _row_tile(rh, c)
    nrt = rh // tr

    def body(pos_ref, p_ref, o_ref):
        s = p_ref[0].astype(F32)
        for k in range(1, N_CHIPS):
            s = s + p_ref[k].astype(F32)
        o_ref[...] = s

    return pl.pallas_call(
        body,
        grid_spec=pltpu.PrefetchScalarGridSpec(
            num_scalar_prefetch=1, grid=(nrt,),
            in_specs=[pl.BlockSpec((N_CHIPS, tr, c), lambda i, p: (0, i, 0))],
            out_specs=pl.BlockSpec((tr, c), lambda i, p: (p[1] * nrt + i, 0))),
        out_shape=jax.ShapeDtypeStruct((2 * rh, c), F32),
        compiler_params=_params(("parallel",)), name=name)(pos, parts)


_HBM = pl.BlockSpec(memory_space=pltpu.HBM)


def _chip_xy(k):
    return k // 2, k % 2


def _half_rows(ref, hc, rh):
    return ref.at[pl.ds(pl.multiple_of(hc * rh, 16), rh), :]


_SEM = pl.BlockSpec(memory_space=pltpu.SEMAPHORE)
_ANY = pl.BlockSpec(memory_space=pl.ANY)
_SPLIT = pltpu.CompilerParams(has_side_effects=pltpu.SideEffectType.DATAFLOW_SIDE_EFFECTING)


def _in_hbm(a):
    return pltpu.with_memory_space_constraint(a, pltpu.HBM)


def _push_start(name, srcs, land_shapes, copies_of):
    n = len(srcs)

    def body(*refs):
        s_refs, l_refs = refs[:n], refs[n:2 * n]
        send_sems, recv_sems = refs[2 * n], refs[2 * n + 1]
        token = refs[-1]
        x, y, c = lax.axis_index("x"), lax.axis_index("y"), lax.axis_index("c")
        me = 2 * x + y
        for i in range(n):
            for k in range(N_CHIPS):
                @pl.when(k != me)
                def _():
                    src, dst, dev = copies_of(i, k, s_refs[i], l_refs[i], me, x, y, c)
                    pltpu.make_async_remote_copy(
                        src_ref=src, dst_ref=dst, send_sem=send_sems.at[N_CHIPS * i + k],
                        recv_sem=recv_sems.at[N_CHIPS * i + me], device_id=dev, device_id_type=MESH).start()
        token[...] = jnp.zeros_like(token)

    lands = [lax.empty(s, d) for s, d in land_shapes]
    outs = pl.pallas_call(
        body, name=name,
        out_shape=[pltpu.SemaphoreType.DMA((N_CHIPS * n,)), pltpu.SemaphoreType.DMA((N_CHIPS * n,))]
        + [pltpu.HBM(s.shape, s.dtype) for s in srcs] + [pltpu.HBM(s, d) for s, d in land_shapes]
        + [jax.ShapeDtypeStruct((8, LANES), F32)],
        in_specs=[_HBM] * (2 * n),
        out_specs=[_SEM, _SEM] + [_HBM] * (2 * n) + [pl.BlockSpec(memory_space=pltpu.VMEM)],
        input_output_aliases={j: 2 + j for j in range(2 * n)},
        compiler_params=_SPLIT,
    )(*[_in_hbm(s) for s in srcs], *[_in_hbm(l) for l in lands])
    return outs[0], outs[1], outs[2:2 + n], outs[2 + n:2 + 2 * n], outs[-1]


def _push_wait(name, started, after, landed_of):
    send_sems, recv_sems, srcs, lands, _ = started
    n = len(srcs)
    after = list(after) if isinstance(after, (list, tuple)) else [after]

    def body(*refs):
        s_refs, l_refs = refs[:n], refs[n:2 * n]
        send, recv = refs[2 * n], refs[2 * n + 1]
        token = refs[-1]
        token[...] = jnp.zeros_like(token)
        x, y, c = lax.axis_index("x"), lax.axis_index("y"), lax.axis_index("c")
        me = 2 * x + y
        for i in range(n):
            for k in range(N_CHIPS):
                @pl.when(k != me)
                def _():
                    src, dst = landed_of(i, k, s_refs[i], l_refs[i], me, c)
                    cp = pltpu.make_async_remote_copy(
                        src_ref=src, dst_ref=dst, send_sem=send.at[N_CHIPS * i + k], recv_sem=recv.at[N_CHIPS * i + k],
                        device_id=(x, y, c), device_id_type=MESH)
                    cp.wait_send()
                    cp.wait_recv()

    outs = pl.pallas_call(
        body, name=name,
        out_shape=[pltpu.HBM(s.shape, s.dtype) for s in srcs] + [pltpu.HBM(l.shape, l.dtype) for l in lands]
        + [jax.ShapeDtypeStruct((8, LANES), F32)],
        in_specs=[_HBM] * (2 * n) + [_SEM, _SEM] + [_ANY] * len(after),
        out_specs=[_HBM] * (2 * n) + [pl.BlockSpec(memory_space=pltpu.VMEM)],
        input_output_aliases={j: j for j in range(2 * n)},
        compiler_params=_SPLIT,
    )(*srcs, *lands, send_sems, recv_sems, *after)
    return outs[:n], outs[n:2 * n], outs[-1]


def _gather_start(name, shards):
    def copies_of(i, k, src, land, me, x, y, c):
        rh = shards[i].shape[0] // 2
        kx, ky = _chip_xy(k)
        return _half_rows(src, c, rh), _half_rows(land.at[me], c, rh), (kx, ky, c)

    return _push_start(name, shards, [((N_CHIPS,) + s.shape, s.dtype) for s in shards], copies_of)


def _gather_wait(name, started, after):
    shapes = [s.shape for s in started[2]]

    def landed_of(i, k, src, land, me, c):
        rh = shapes[i][0] // 2
        return _half_rows(src, c, rh), _half_rows(land.at[k], c, rh)

    return _push_wait(name, started, after, landed_of)


def _forward_halves(name, bufs):
    n = len(bufs)

    def body(*refs):
        i_refs, o_refs, send_sems, recv_sems = refs[:n], refs[n:2 * n], refs[2 * n], refs[2 * n + 1]
        x, y, c = lax.axis_index("x"), lax.axis_index("y"), lax.axis_index("c")
        me = 2 * x + y

        def fwd(i, k, hc):
            rh = bufs[i].shape[1] // 2
            return pltpu.make_async_remote_copy(
                src_ref=_half_rows(i_refs[i].at[k], hc, rh), dst_ref=_half_rows(o_refs[i].at[k], hc, rh),
                send_sem=send_sems.at[i, k], recv_sem=recv_sems.at[i, k],
                device_id=(x, y, 1 - c), device_id_type=MESH)

        for i in range(n):
            for k in range(N_CHIPS):
                @pl.when(k != me)
                def _():
                    fwd(i, k, c).start()
        for i in range(n):
            for k in range(N_CHIPS):
                @pl.when(k != me)
                def _():
                    fwd(i, k, 1 - c).wait_recv()
        for i in range(n):
            for k in range(N_CHIPS):
                @pl.when(k != me)
                def _():
                    fwd(i, k, c).wait_send()

    return pl.pallas_call(
        body, in_specs=[_HBM] * n, out_specs=[_HBM] * n,
        out_shape=[jax.ShapeDtypeStruct(b.shape, b.dtype) for b in bufs],
        input_output_aliases={i: i for i in range(n)},
        scratch_shapes=[pltpu.SemaphoreType.DMA((n, N_CHIPS))] * 2,
        name=name)(*bufs)


def _swap_halves(name, grads, windows):
    n = len(grads)
    dims = [_shard_dims(g, w) for g, w in zip(grads, windows)]

    def body(*refs):
        g_refs, o_refs, send_sems, recv_sems = refs[:n], refs[n:2 * n], refs[2 * n], refs[2 * n + 1]
        x, y, c = lax.axis_index("x"), lax.axis_index("y"), lax.axis_index("c")
        copies = []
        for i in range(n):
            r, cw = dims[i]
            for k in range(N_CHIPS):
                if grads[i].ndim == 3:
                    shard = g_refs[i].at[k]
                else:
                    shard = g_refs[i].at[:, pl.ds(windows[i][0][k] if windows[i] else k * cw, cw)]
                copies.append(pltpu.make_async_remote_copy(
                    src_ref=_half_rows(shard, 1 - c, r // 2), dst_ref=o_refs[i].at[k],
                    send_sem=send_sems.at[i, k], recv_sem=recv_sems.at[i, k],
                    device_id=(x, y, 1 - c), device_id_type=MESH))
        for cp in copies:
            cp.start()
        for cp in copies:
            cp.wait()

    return pl.pallas_call(
        body, in_specs=[_HBM] * n, out_specs=[_HBM] * n,
        out_shape=[jax.ShapeDtypeStruct((N_CHIPS, r // 2, cw), g.dtype) for (r, cw), g in zip(dims, grads)],
        scratch_shapes=[pltpu.SemaphoreType.DMA((n, N_CHIPS))] * 2,
        name=name)(*grads)


def _scatter_start(name, parts):
    def copies_of(i, k, src, land, me, x, y, c):
        kx, ky = _chip_xy(k)
        return src.at[k], land.at[me], (kx, ky, c)

    return _push_start(name, parts, [(p.shape, p.dtype) for p in parts], copies_of)


def _scatter_wait(name, started, after):
    return _push_wait(name, started, after, lambda i, k, src, land, me, c: (src.at[k], land.at[k]))


def _join_halves(bufs):
    n = len(bufs)

    def body(*refs):
        i_refs, o_refs, send_sems, recv_sems = refs[:n], refs[n:2 * n], refs[2 * n], refs[2 * n + 1]
        x, y, c = lax.axis_index("x"), lax.axis_index("y"), lax.axis_index("c")
        copies = []
        for i in range(n):
            rh = bufs[i].shape[0] // 2
            copies.append(pltpu.make_async_remote_copy(
                src_ref=_half_rows(i_refs[i], c, rh), dst_ref=_half_rows(o_refs[i], c, rh),
                send_sem=send_sems.at[i], recv_sem=recv_sems.at[i],
                device_id=(x, y, 1 - c), device_id_type=MESH))
        for cp in copies:
            cp.start()
        for i in range(n):
            rh = bufs[i].shape[0] // 2
            pltpu.make_async_remote_copy(
                src_ref=_half_rows(i_refs[i], c, rh), dst_ref=_half_rows(o_refs[i], 1 - c, rh),
                send_sem=send_sems.at[i], recv_sem=recv_sems.at[i],
                device_id=(x, y, 1 - c), device_id_type=MESH).wait_recv()
        for cp in copies:
            cp.wait_send()

    return pl.pallas_call(
        body, in_specs=[_HBM] * n, out_specs=[_HBM] * n,
        out_shape=[jax.ShapeDtypeStruct(b.shape, F32) for b in bufs],
        input_output_aliases={i: i for i in range(n)},
        scratch_shapes=[pltpu.SemaphoreType.DMA((n,))] * 2,
        name="join_halves")(*bufs)


def _all_gather_small(name, blk):
    m_per, n = blk.shape

    def body(x_ref, out_ref, send_sems, recv_sems, local_sem):
        x, y, c = lax.axis_index("x"), lax.axis_index("y"), lax.axis_index("c")
        me, sibling = (x, y, c), (x, y, 1 - c)
        chips = [(1 - x, y), (x, 1 - y), (1 - x, 1 - y)]

        def rows(px, py, pc):
            return out_ref.at[pl.ds((4 * px + 2 * py + pc) * m_per, m_per), :]

        def copy(k, block, to, src=None):
            return pltpu.make_async_remote_copy(
                src_ref=rows(*block) if src is None else src, dst_ref=rows(*block),
                send_sem=send_sems.at[k], recv_sem=recv_sems.at[k],
                device_id=to, device_id_type=MESH)

        mine = pltpu.make_async_copy(x_ref, rows(*me), local_sem)
        mine.start()
        first = [copy(0, me, sibling, src=x_ref)]
        first += [copy(1 + j, me, (*chip, c), src=x_ref) for j, chip in enumerate(chips)]
        for cp in first:
            cp.start()
        passed = [copy(4 + j, (*chip, c), sibling) for j, chip in enumerate(chips)]
        for j, chip in enumerate(chips):
            copy(1 + j, (*chip, c), me).wait_recv()
            passed[j].start()
        copy(0, sibling, me).wait_recv()
        for j, chip in enumerate(chips):
            copy(4 + j, (*chip, 1 - c), me).wait_recv()
        for cp in first + passed:
            cp.wait_send()
        mine.wait()

    return pl.pallas_call(
        body, out_shape=jax.ShapeDtypeStruct((N_DEV * m_per, n), blk.dtype),
        in_specs=[pl.BlockSpec(memory_space=pltpu.VMEM)],
        out_specs=pl.BlockSpec(memory_space=pltpu.VMEM),
        scratch_shapes=[pltpu.SemaphoreType.DMA((7,)), pltpu.SemaphoreType.DMA((7,)), pltpu.SemaphoreType.DMA],
        name=name)(blk)


def _pack_rows(vecs, width):
    parts = []
    for v in vecs:
        f = v.reshape(-1)
        pad = (-f.shape[0]) % (8 * width)
        parts.append(jnp.pad(f, (0, pad)) if pad else f)
    return jnp.concatenate(parts).reshape(-1, width)


def _unpack_rows(packed, shapes, width):
    flat = packed.reshape(-1)
    out, off = [], 0
    for s in shapes:
        n = math.prod(s)
        out.append(flat[off:off + n].reshape(s))
        off += n + ((-n) % (8 * width))
    return out


class _WinPlan:
    def __init__(self, ncol, dt0, h, dmain):
        self.ncol, self.h = ncol, h
        self.dt_shard = dt0 // ncol
        assert (dt0 + h - 1) // ncol == self.dt_shard and dmain % LANES == 0
        self.dt_local = dt0 - self.dt_shard * ncol
        to_main = lambda g: g if g <= dt0 else g - h
        self.lo = [to_main(ncol * k) for k in range(N_CHIPS)]
        self.hi = [to_main(ncol * (k + 1)) for k in range(N_CHIPS)]
        down = lambda v: v // LANES * LANES
        self.ww = max(-(-(hi - down(lo)) // LANES) * LANES for lo, hi in zip(self.lo, self.hi))
        self.ws = [min(down(lo), dmain - self.ww) for lo in self.lo]
        self.dmain = dmain

    def to_window(self, k, shard):
        if k == self.dt_shard:
            shard = jnp.concatenate([shard[:, :self.dt_local], shard[:, self.dt_local + self.h:]], axis=1)
        left = self.lo[k] - self.ws[k]
        return jnp.pad(shard, ((0, 0), (left, self.ww - left - shard.shape[1])))

    def from_window(self, k, window, dt_cols):
        left = self.lo[k] - self.ws[k]
        body = window[:, left:left + self.hi[k] - self.lo[k]]
        if k == self.dt_shard:
            body = jnp.concatenate([body[:, :self.dt_local], dt_cols, body[:, self.dt_local:]], axis=1)
        return body

    def merge(self, windows):
        cuts = sorted({0, self.dmain} | set(self.ws) | {w + self.ww for w in self.ws})
        segs = []
        for a, b in zip(cuts[:-1], cuts[1:]):
            parts = [windows[k][:, a - self.ws[k]:b - self.ws[k]] for k in range(N_CHIPS)
                     if self.ws[k] <= a and b <= self.ws[k] + self.ww]
            segs.append(functools.reduce(jnp.add, parts))
        return jnp.concatenate(segs, axis=1)


def kernel(x, attn_norm_w, w_in, conv_w, conv_b, dt_bias, a_log, d_skip, ssd_norm_w, pool_w, pool_scale, w_out, ffn_norm_w, w_gate, w_up, w_down, final_norm_w, loss_target, m_attn_norm_w, m_w_in, m_conv_w, m_conv_b, m_dt_bias, m_a_log, m_d_skip, m_ssd_norm_w, m_pool_w, m_pool_scale, m_w_out, m_ffn_norm_w, m_w_gate, m_w_up, m_w_down, m_final_norm_w, v_attn_norm_w, v_w_in, v_conv_w, v_conv_b, v_dt_bias, v_a_log, v_d_skip, v_ssd_norm_w, v_pool_w, v_pool_scale, v_w_out, v_ffn_norm_w, v_w_gate, v_w_up, v_w_down, v_final_norm_w):
    G, P, PG = SSD_GROUPS, HEAD_DIM, len(POOL_WINDOWS)
    _, L, D = x.shape
    H = a_log.shape[1]
    R = H // G
    DS = H * P
    DCONV = conv_b.shape[1]
    N = (DCONV - DS) // (2 * G)
    DP = pool_scale.shape[1]
    PGD = DP // PG
    DIN = N_CHIPS * w_in.shape[2]
    DFF = N_CHIPS * w_gate.shape[2]
    DMAIN = DS + DCONV + DP
    assert DIN == DMAIN + H and DS == DP and H <= LANES

    cx, cy, cc = lax.axis_index("x"), lax.axis_index("y"), lax.axis_index("c")
    chip = 2 * cx + cy

    win = _WinPlan(DIN // N_CHIPS, DS + DCONV, H, DMAIN)
    my_window = lax.switch(chip, [functools.partial(win.to_window, k) for k in range(N_CHIPS)], w_in[0].astype(BF16))
    started_in = _gather_start("gather_start_in", [my_window])

    def forward_gathered(tag, shards, landed):
        landed = _forward_halves("gather_forward_" + tag, landed)
        return [lax.dynamic_update_slice(g, s[None], (chip, 0, 0)) for g, s in zip(landed, shards)]

    def cols(p):
        return jnp.moveaxis(p, 0, -2).reshape(p.shape[1:-1] + (N_CHIPS * p.shape[-1],))

    ncw = CONV_WIDTH * DCONV // N_CHIPS
    dt_here = jnp.where(chip == win.dt_shard, w_in[0][:, win.dt_local:win.dt_local + H], 0.0)
    start_blk = _pack_rows([conv_w[0] + started_in[4][0, 0], dt_here], LANES)
    start_all = _all_gather_small("gather_conv_w", start_blk).reshape(N_CHIPS, 2, -1)[:, 0]
    conv_w_f = cols(start_all[:, :ncw].reshape(N_CHIPS, CONV_WIDTH, DCONV // N_CHIPS))
    dt_off = ncw + (-ncw) % (8 * LANES)
    w_dt = jnp.pad(start_all[win.dt_shard, dt_off:dt_off + D * H].reshape(D, H), ((0, 0), (0, LANES - H))).astype(BF16)

    xl, tgt = x[0], loss_target[0]
    tm_row = _tile(L, 256, HALO)
    tm_mm = _tile(L, 1024, 16)
    hn1 = _rms_fwd("rms1_fwd", xl, attn_norm_w, tm_row)
    rest16 = [w_out[0].astype(BF16), w_gate[0].astype(BF16), w_up[0].astype(BF16), w_down[0].astype(BF16)]
    shards_in, landed_in, landed_token = _gather_wait("gather_wait_in", started_in, [hn1, conv_w_f, w_dt] + rest16)
    shards_rest = [(pool_w[0].reshape(PG * PGD // N_CHIPS, PGD) + landed_token[0, 0]).astype(BF16)] + rest16
    started_rest = _gather_start("gather_start_rest", shards_rest)
    pin_row = lambda started, n: jnp.zeros((1, n), F32) + started[4][0, 0]
    add_row = lambda accs, ex, rex: [accs[0] + rex[0]]
    w_main = win.merge(forward_gathered("in", shards_in, landed_in)[0])
    proj, = _mm("proj_main", "nn", [(hn1, w_main)], L, DMAIN, D, tm_mm, 512, D, [F32],
                epilogue=add_row, row_extras=[pin_row(started_rest, DMAIN)])
    dt_raw, = _mm("proj_dt", "nn", [(hn1, w_dt)], L, LANES, D, tm_mm, LANES, D, [F32])

    cwid = _tile(math.gcd(DS, DCONV), 512, LANES)
    tm_conv = _tile(L, 1024, HALO)
    xbc = _conv_fwd(proj, conv_w_f, conv_b, DS, DCONV, tm_conv, cwid)

    dt_g = dt_raw[:, :H].reshape(L, G, R)
    dtc_raw = jnp.transpose(dt_g, (1, 0, 2))
    dtr_raw = jnp.transpose(dt_g, (1, 2, 0))
    as_c = lambda v: v.reshape(G, 1, R)
    as_r = lambda v: v.reshape(G, R, 1)
    ssd_args = (xbc, dtc_raw, dtr_raw, as_c(dt_bias), as_r(dt_bias), as_c(a_log), as_r(a_log), as_c(d_skip))
    y_ssd_raw, states = _ssd_fwd(*ssd_args, DS, N)
    y_ssd = _gated_fwd(y_ssd_raw, proj, ssd_norm_w, DS, tm_conv)
    gathered = forward_gathered("rest", *_gather_wait("gather_wait_rest", started_rest, y_ssd)[:2])
    pool_w_f = jnp.moveaxis(gathered[0].reshape(N_CHIPS, PG, PGD // N_CHIPS, PGD), 0, 1).reshape(PG, PGD, PGD)
    w_out_f = gathered[1].reshape(2 * DS, D)
    w_gate_f, w_up_f = cols(gathered[2]), cols(gathered[3])
    w_down_f = gathered[4].reshape(DFF, D)
    w_out_top, w_out_bot = w_out_f[:DS], w_out_f[DS:]
    pooled, y_pool = _pool_fwd(proj, pool_w_f, pool_scale, DS + DCONV, DP, tm_conv)

    add_res = lambda accs, ex, rex: [accs[0] + ex[0]]
    h1, = _mm("out_proj", "nn", [(y_ssd, w_out_top), (y_pool, w_out_bot)], L, D, DS, tm_mm, 512, DS, [F32],
              epilogue=add_res, extras=[xl])
    hn2 = _rms_fwd("rms2_fwd", h1, ffn_norm_w, tm_row)

    def glu(accs, ex, rex):
        return [accs[0], accs[1], (_silu(accs[0]) * accs[1])]

    tn_ff = _tile(DFF, 512, LANES)
    gate, up, act = _mm("ffn_in", "nn", [(hn2, w_gate_f), (hn2, w_up_f)], L, DFF, D, tm_mm, tn_ff, D,
                        [F32, F32, BF16], epilogue=glu, separate=True)
    tk_ff = _tile(DFF, DFF // 2, LANES)
    h2, = _mm("ffn_out", "nn", [(act, w_down_f)], L, D, DFF, tm_mm, 512, tk_ff, [F32], epilogue=add_res, extras=[h1])
    dh2, dh2_16, loss_blk, g_final = _final_loss(h2, final_norm_w.reshape(1, D), tgt, tm_row)

    def dglu(accs, ex, rex):
        gt, u = ex
        sg = _sigmoid(gt)
        return [accs[0] * u * (sg * (1.0 + gt * (1.0 - sg))), accs[0] * (gt * sg)]

    dgate, dup = _mm("ffn_out_dx", "nt", [(dh2_16, w_down_f)], L, DFF, D, tm_mm, tn_ff, D, [BF16, BF16],
                     epilogue=dglu, extras=[gate, up])
    tk_tok = _tile(L, 2048, 16)
    twice = lambda accs, ex, rex: list(accs) + list(accs)
    g_w_down, g_w_down16 = _mm("ffn_out_dw", "tn", [(act, dh2_16)], DFF, D, L, _tile(DFF, 1536, LANES), 1024, tk_tok,
                               [F32, BF16], epilogue=twice)
    g_w_gate, g_w_up, g_w_gate16, g_w_up16 = _mm("ffn_in_dw", "tn", [(hn2, dgate), (hn2, dup)], D, DFF, L, 1024, tn_ff,
                                                 tk_tok, [F32, F32, BF16, BF16], epilogue=twice, separate=True)

    pos = jnp.stack([chip, cc]).astype(jnp.int32)

    def start_reduce(tag, names, full_grads, full_grads16, windows):
        from_sibling = _swap_halves("swap_halves_" + tag, full_grads16, windows)
        partials = [_pair_sum("pair_sum_" + n, g, r, pos, w)
                    for n, g, r, w in zip(names, full_grads, from_sibling, windows)]
        return _scatter_start("scatter_start_" + tag, partials)

    names_ffn = ["w_gate", "w_up", "w_down"]
    started_ffn = start_reduce("ffn", names_ffn, [g_w_gate, g_w_up, g_w_down.reshape(N_CHIPS, -1, D)],
                               [g_w_gate16, g_w_up16, g_w_down16.reshape(N_CHIPS, -1, D)], [None] * 3)
    dhn2, = _mm("ffn_in_dx", "nt", [(dgate, w_gate_f), (dup, w_up_f)], L, D, DFF, tm_mm, 512,
                _tile(DFF, DFF // 4, LANES), [F32], epilogue=add_row, row_extras=[pin_row(started_ffn, D)])
    dh1, g_ffn_norm, dh1_16 = _rms_bwd("rms2_bwd", h1, ffn_norm_w, [dhn2], dh2, tm_row, True)

    dy_ssd, dy_pool = _mm("out_proj_dx", "nt", [(dh1_16, w_out_top), (dh1_16, w_out_bot)], L, DS, D, tm_mm, 512, D,
                          [F32, F32], separate=True)
    g_w_out_top, g_w_out_bot, g_w_out_top16, g_w_out_bot16 = _mm(
        "out_proj_dw", "tn", [(y_ssd, dh1_16), (y_pool, dh1_16)], DS, D, L, 1024, 512, tk_tok, [F32, F32, BF16, BF16],
        epilogue=twice, separate=True)
    dy_raw, dproj, g_ssd_norm = _gated_bwd(y_ssd_raw, proj, ssd_norm_w, dy_ssd, DS, DMAIN, tm_conv)
    dxs, db, dc, ddt_raw, g_a_log, g_d_skip, g_dt_bias = _ssd_bwd(*ssd_args, dy_raw, states, DS, N)
    g_conv_w, g_conv_b = [], []
    for tag, dact, first in (("xs", dxs, 0), ("b", db, DS), ("c", dc, DS + G * N)):
        dproj, gw, gb = _conv_bwd("conv_bwd_" + tag, proj, dact, conv_w_f, conv_b, dproj, DS, first, tm_conv, cwid)
        g_conv_w.append(gw)
        g_conv_b.append(gb)
    g_conv_w, g_conv_b = jnp.concatenate(g_conv_w, axis=1), jnp.concatenate(g_conv_b, axis=1)
    dproj, g_pool_w, g_pool_scale = _pool_bwd(dy_pool, pooled, pool_w_f, pool_scale, dproj, DS + DCONV, tm_conv)
    ddt_pad = jnp.pad(jnp.transpose(ddt_raw, (1, 0, 2)).reshape(L, H), ((0, 0), (0, LANES - H))).astype(BF16)

    tk_main = _tile(DMAIN, DMAIN // 2, LANES)
    g_w_main, g_w_main16 = _mm("proj_main_dw", "tn", [(hn1, dproj)], D, DMAIN, L, 1024, _tile(DMAIN, 1024, LANES),
                               tk_tok, [F32, BF16], epilogue=twice)
    names_mix = ["w_in", "pool_w", "w_out"]
    pool_shards = jnp.moveaxis(g_pool_w.reshape(PG, N_CHIPS, PGD // N_CHIPS, PGD), 1, 0).reshape(N_CHIPS, -1, PGD)
    out_shards = lambda top, bot: jnp.stack([top.reshape(2, DS // 2, D), bot.reshape(2, DS // 2, D)]).reshape(N_CHIPS, -1, D)
    started_mix = start_reduce(
        "mix", names_mix, [g_w_main, pool_shards, out_shards(g_w_out_top, g_w_out_bot)],
        [g_w_main16, pool_shards.astype(BF16), out_shards(g_w_out_top16, g_w_out_bot16)],
        [(win.ws, win.ww), None, None])
    dhn1a, = _mm("proj_main_dx", "nt", [(dproj, w_main)], L, D, DMAIN, tm_mm, 512, tk_main, [F32],
                 epilogue=add_row, row_extras=[pin_row(started_mix, D)])
    dhn1b, = _mm("proj_dt_dx", "nt", [(ddt_pad, w_dt)], L, D, LANES, tm_mm, 512, LANES, [F32])
    g_w_dt, = _mm("proj_dt_dw", "tn", [(hn1, ddt_pad)], D, LANES, L, 512, LANES, tk_tok, [F32])
    grad_x, g_attn_norm = _rms_bwd("rms1_bwd", xl, attn_norm_w, [dhn1a, dhn1b], dh1, tm_row, False)

    def finish_reduce(tag, names, started, after):
        partials, landed, _ = _scatter_wait("scatter_wait_" + tag, started, after)
        landed = [lax.dynamic_update_slice(l, lax.dynamic_index_in_dim(p, chip, 0), (chip, 0, 0))
                  for l, p in zip(landed, partials)]
        return [_chip_sum("chip_sum_" + n, l, pos) for n, l in zip(names, landed)]

    halves = finish_reduce("ffn", names_ffn, started_ffn, grad_x) + finish_reduce("mix", names_mix, started_mix, grad_x)
    red = dict(zip(names_ffn + names_mix, _join_halves(halves)))

    small_w = [attn_norm_w, conv_b, dt_bias, a_log, d_skip, ssd_norm_w, pool_scale, ffn_norm_w, final_norm_w]
    small_m = [m_attn_norm_w, m_conv_b, m_dt_bias, m_a_log, m_d_skip, m_ssd_norm_w, m_pool_scale, m_ffn_norm_w, m_final_norm_w]
    small_v = [v_attn_norm_w, v_conv_b, v_dt_bias, v_a_log, v_d_skip, v_ssd_norm_w, v_pool_scale, v_ffn_norm_w, v_final_norm_w]
    small_g = [g_attn_norm, g_conv_b, g_dt_bias.reshape(1, H), g_a_log.reshape(1, H), g_d_skip.reshape(1, H),
               g_ssd_norm, g_pool_scale, g_ffn_norm, g_final.reshape(D)]
    extra_shapes = [(CONV_WIDTH, DCONV), (D, H), (1, LANES)]
    zeros_like_extra = [jnp.zeros(s, F32) for s in extra_shapes]
    g_blk = _pack_rows(small_g + [g_conv_w, g_w_dt[:, :H], loss_blk], LANES)
    rows = g_blk.shape[0]
    small_all = _all_gather_small("gather_small_grads", g_blk)
    s_g, s_d, s_m, s_v = _small_sum_adam(small_all, _pack_rows(small_w + zeros_like_extra, LANES),
                                         _pack_rows(small_m + zeros_like_extra, LANES),
                                         _pack_rows(small_v + zeros_like_extra, LANES), rows)
    shapes = [w.shape for w in small_w] + extra_shapes
    sg_list = _unpack_rows(s_g, shapes, LANES)
    sd_list = _unpack_rows(s_d, shapes, LANES)[:len(small_w)]
    sm_list = _unpack_rows(s_m, shapes, LANES)[:len(small_w)]
    sv_list = _unpack_rows(s_v, shapes, LANES)[:len(small_w)]
    loss = sg_list[-1][0, 0]
    grad_conv_w = lax.dynamic_slice(sg_list[-3], (0, chip * (DCONV // N_CHIPS)), (CONV_WIDTH, DCONV // N_CHIPS))
    grad_w_in = lax.switch(chip, [functools.partial(win.from_window, k) for k in range(N_CHIPS)], red["w_in"], sg_list[-2])

    def adam_nd(name, w, g, m, v):
        shp = w.shape
        to2 = lambda a: a.reshape(-1, shp[-1])
        return tuple(o.reshape(shp) for o in _adam(name, to2(w), to2(g), to2(m), to2(v)))

    sharded = {
        "w_in": (w_in, grad_w_in[None], m_w_in, v_w_in),
        "conv_w": (conv_w, grad_conv_w[None], m_conv_w, v_conv_w),
        "pool_w": (pool_w, red["pool_w"].reshape(pool_w.shape), m_pool_w, v_pool_w),
        "w_out": (w_out, red["w_out"][None], m_w_out, v_w_out),
        "w_gate": (w_gate, red["w_gate"][None], m_w_gate, v_w_gate),
        "w_up": (w_up, red["w_up"][None], m_w_up, v_w_up),
        "w_down": (w_down, red["w_down"][None], m_w_down, v_w_down),
    }
    upd = {n: adam_nd("adam_" + n, *a) for n, a in sharded.items()}
    small_names = ["attn_norm_w", "conv_b", "dt_bias", "a_log", "d_skip", "ssd_norm_w", "pool_scale", "ffn_norm_w",
                   "final_norm_w"]
    for i, n in enumerate(small_names):
        upd[n] = (sg_list[i], sd_list[i], sm_list[i], sv_list[i])

    order = ["attn_norm_w", "w_in", "conv_w", "conv_b", "dt_bias", "a_log", "d_skip", "ssd_norm_w", "pool_w",
             "pool_scale", "w_out", "ffn_norm_w", "w_gate", "w_up", "w_down", "final_norm_w"]
    outs = [loss, grad_x[None]]
    for j in range(4):
        outs += [upd[n][j] for n in order]
    return tuple(outs)
```

```python
import functools
import math

import jax
import jax.numpy as jnp
from jax import lax
from jax.experimental import pallas as pl
from jax.experimental.pallas import tpu as pltpu

F32 = jnp.float32
BF16 = jnp.bfloat16

NORM_EPS = 1e-5
HEAD_DIM = 64
SSD_GROUPS = 4
CONV_WIDTH = 4
CHUNK = 256
POOL_WINDOWS = (2, 4, 8, 16)
ADAM_LR = 0.001
ADAM_B1 = 0.9
ADAM_B2 = 0.999
ADAM_EPS = 1e-08
ADAM_WD = 0.01
ADAM_STEP = 10

N_CHIPS = 4
N_DEV = 8
LANES = 128
HALO = 16
FLAT_W = 512
VMEM_LIMIT = 52 * 1024 * 1024
MESH = pl.DeviceIdType.MESH

NN = (((1,), (0,)), ((), ()))
NT = (((1,), (1,)), ((), ()))
TN = (((0,), (0,)), ((), ()))


def _tile(n, cap, mult):
    best = None
    for t in range(mult, min(n, cap) + 1, mult):
        if n % t == 0:
            best = t
    return best if best is not None else n


def _params(sem):
    return pltpu.CompilerParams(dimension_semantics=sem, vmem_limit_bytes=VMEM_LIMIT)


def _dot(a, b, dims):
    return lax.dot_general(a, b, dims, preferred_element_type=F32)


def _sigmoid(x):
    return 1.0 / (1.0 + jnp.exp(-x))


def _silu(x):
    return x * _sigmoid(x)


def _softplus(x):
    return jnp.maximum(x, 0.0) + jnp.log(1.0 + jnp.exp(-jnp.abs(x)))


def _mm(name, mode, pairs, M, N, K, tm, tn, tk, out_dtypes, epilogue=None, extras=(), row_extras=(),
        separate=False):
    tm, tn, tk = min(tm, M), min(tn, N), min(tk, K)
    assert M % tm == 0 and N % tn == 0 and K % tk == 0, (name, M, N, K, tm, tn, tk)
    nk = K // tk
    npairs = len(pairs)
    nacc = npairs if separate else 1
    if mode == "nn":
        a_spec = pl.BlockSpec((tm, tk), lambda i, j, k: (i, k))
        b_spec = pl.BlockSpec((tk, tn), lambda i, j, k: (k, j))
        dims = NN
    elif mode == "nt":
        a_spec = pl.BlockSpec((tm, tk), lambda i, j, k: (i, k))
        b_spec = pl.BlockSpec((tn, tk), lambda i, j, k: (j, k))
        dims = NT
    else:
        a_spec = pl.BlockSpec((tk, tm), lambda i, j, k: (k, i))
        b_spec = pl.BlockSpec((tk, tn), lambda i, j, k: (k, j))
        dims = TN
    o_spec = pl.BlockSpec((tm, tn), lambda i, j, k: (i, j))
    r_spec = pl.BlockSpec((1, tn), lambda i, j, k: (0, j))
    if epilogue is None:
        epilogue = lambda accs, ex, rex: accs
    n_ex, n_rex, n_out = len(extras), len(row_extras), len(out_dtypes)

    def body(*refs):
        ab = refs[:2 * npairs]
        ex = refs[2 * npairs:2 * npairs + n_ex]
        rex = refs[2 * npairs + n_ex:2 * npairs + n_ex + n_rex]
        outs = refs[2 * npairs + n_ex + n_rex:2 * npairs + n_ex + n_rex + n_out]
        accs = refs[2 * npairs + n_ex + n_rex + n_out:]

        def products():
            res = [None] * nacc
            for p in range(npairs):
                d = _dot(ab[2 * p][...], ab[2 * p + 1][...], dims)
                q = p if separate else 0
                res[q] = d if res[q] is None else res[q] + d
            return res

        def finish(vals):
            res = epilogue(vals, [e[...] for e in ex], [r[...] for r in rex])
            for o, v in zip(outs, res):
                o[...] = v.astype(o.dtype)

        if nk == 1:
            finish(products())
        else:
            k = pl.program_id(2)

            @pl.when(k == 0)
            def _():
                for q in range(nacc):
                    accs[q][...] = jnp.zeros_like(accs[q])

            for p in range(npairs):
                accs[p if separate else 0][...] += _dot(ab[2 * p][...], ab[2 * p + 1][...], dims)

            @pl.when(k == nk - 1)
            def _():
                finish([a[...] for a in accs])

    in_specs = [a_spec, b_spec] * npairs + [o_spec] * n_ex + [r_spec] * n_rex
    args = [t for p in pairs for t in p] + list(extras) + list(row_extras)
    outs = pl.pallas_call(
        body,
        grid=(M // tm, N // tn, nk),
        in_specs=in_specs,
        out_specs=[o_spec] * n_out,
        out_shape=[jax.ShapeDtypeStruct((M, N), d) for d in out_dtypes],
        scratch_shapes=[pltpu.VMEM((tm, tn), F32) for _ in range(nacc if nk > 1 else 0)],
        compiler_params=_params(("parallel", "parallel", "arbitrary")),
        name=name,
    )(*args)
    return outs


def _rms(xf, w):
    y = xf * lax.rsqrt(jnp.mean(xf * xf, axis=-1, keepdims=True) + NORM_EPS)
    return y * w


def _rms_fwd(name, x, w, tm):
    L, D = x.shape

    def body(x_ref, w_ref, o_ref):
        o_ref[...] = _rms(x_ref[...], w_ref[...]).astype(BF16)

    return pl.pallas_call(
        body, grid=(L // tm,),
        in_specs=[pl.BlockSpec((tm, D), lambda i: (i, 0)), pl.BlockSpec((1, D), lambda i: (0, 0))],
        out_specs=pl.BlockSpec((tm, D), lambda i: (i, 0)),
        out_shape=jax.ShapeDtypeStruct((L, D), BF16),
        compiler_params=_params(("parallel",)), name=name)(x, w)


def _rms_bwd(name, x, w, dparts, dres, tm, with_bf16):
    L, D = x.shape
    nparts = len(dparts)

    def body(*refs):
        x_ref, w_ref = refs[:2]
        p_refs = refs[2:2 + nparts]
        r_ref = refs[2 + nparts]
        outs = refs[3 + nparts:]
        dhn = p_refs[0][...]
        for p in p_refs[1:]:
            dhn = dhn + p[...]
        _, vjp = jax.vjp(_rms, x_ref[...], w_ref[...])
        dx, dw = vjp(dhn)
        dx = dx + r_ref[...]
        outs[0][...] = dx
        gw_ref = outs[1]

        @pl.when(pl.program_id(0) == 0)
        def _():
            gw_ref[...] = jnp.zeros_like(gw_ref)

        gw_ref[...] += dw
        if with_bf16:
            outs[2][...] = dx.astype(BF16)

    row = pl.BlockSpec((tm, D), lambda i: (i, 0))
    vec = pl.BlockSpec((1, D), lambda i: (0, 0))
    out_shape = [jax.ShapeDtypeStruct((L, D), F32), jax.ShapeDtypeStruct((1, D), F32)]
    out_specs = [row, vec]
    if with_bf16:
        out_shape.append(jax.ShapeDtypeStruct((L, D), BF16))
        out_specs.append(row)
    return pl.pallas_call(
        body, grid=(L // tm,),
        in_specs=[row, vec] + [row] * nparts + [row],
        out_specs=out_specs, out_shape=out_shape,
        compiler_params=_params(("arbitrary",)), name=name)(x, w, *dparts, dres)


def _final_loss(h2, wf, target, tm):
    L, D = h2.shape

    def body(h_ref, w_ref, t_ref, dh_ref, dhb_ref, loss_ref, gw_ref):
        t = t_ref[...]

        def f(h, w):
            err = jnp.square(_rms(h, w) - t)
            return 0.5 * jnp.sum(jnp.mean(err, axis=-1))

        val, vjp = jax.vjp(f, h_ref[...], w_ref[...])
        dh, dw = vjp(jnp.ones((), F32))
        dh_ref[...] = dh
        dhb_ref[...] = dh.astype(BF16)

        @pl.when(pl.program_id(0) == 0)
        def _():
            gw_ref[...] = jnp.zeros_like(gw_ref)
            loss_ref[...] = jnp.zeros_like(loss_ref)

        gw_ref[...] += dw
        loss_ref[...] += jnp.full(loss_ref.shape, val, F32)

    row = pl.BlockSpec((tm, D), lambda i: (i, 0))
    vec = pl.BlockSpec((1, D), lambda i: (0, 0))
    lspec = pl.BlockSpec((1, LANES), lambda i: (0, 0))
    return pl.pallas_call(
        body, grid=(L // tm,),
        in_specs=[row, vec, row],
        out_specs=[row, row, lspec, vec],
        out_shape=[jax.ShapeDtypeStruct((L, D), F32), jax.ShapeDtypeStruct((L, D), BF16),
                   jax.ShapeDtypeStruct((1, LANES), F32), jax.ShapeDtypeStruct((1, D), F32)],
        compiler_params=_params(("arbitrary",)), name="final_loss")(h2, wf, target)


def _gated(y, z, w):
    g = y * _silu(z)
    g = g * lax.rsqrt(jnp.mean(g * g, axis=-1, keepdims=True) + NORM_EPS)
    return g * w


def _gated_fwd(y, proj, w, DS, mixed_cols, tm):
    L = y.shape[0]
    GW = DS // SSD_GROUPS

    def body(y_ref, z_ref, w_ref, o_ref):
        o_ref[...] = _gated(y_ref[...], z_ref[...], w_ref[...]).astype(BF16)

    blk = pl.BlockSpec((tm, GW), lambda i, g: (i, g))
    return pl.pallas_call(
        body, grid=(L // tm, SSD_GROUPS),
        in_specs=[blk, blk, pl.BlockSpec((1, GW), lambda i, g: (0, g))],
        out_specs=blk, out_shape=jax.ShapeDtypeStruct((L, mixed_cols), BF16),
        compiler_params=_params(("parallel", "parallel")), name="gated_fwd")(y, proj, w)


def _gated_bwd(y, proj, w, dout, DS, dproj_cols, tm):
    L = y.shape[0]
    GW = DS // SSD_GROUPS

    def body(y_ref, z_ref, w_ref, d_ref, dy_ref, dz_ref, gw_ref):
        _, vjp = jax.vjp(_gated, y_ref[...], z_ref[...], w_ref[...])
        dy, dz, dw = vjp(d_ref[...])
        dy_ref[...] = dy
        dz_ref[...] = dz.astype(BF16)

        @pl.when(pl.program_id(1) == 0)
        def _():
            gw_ref[...] = jnp.zeros_like(gw_ref)

        gw_ref[...] += dw

    blk = pl.BlockSpec((tm, GW), lambda g, i: (i, g))
    vec = pl.BlockSpec((1, GW), lambda g, i: (0, g))
    return pl.pallas_call(
        body, grid=(SSD_GROUPS, L // tm),
        in_specs=[blk, blk, vec, blk],
        out_specs=[blk, blk, vec],
        out_shape=[jax.ShapeDtypeStruct((L, DS), F32), jax.ShapeDtypeStruct((L, dproj_cols), BF16),
                   jax.ShapeDtypeStruct((1, DS), F32)],
        compiler_params=_params(("parallel", "arbitrary")), name="gated_bwd")(y, proj, w, dout)


def _halo_prev(tm, cw, col0):
    return pl.BlockSpec((HALO, cw), lambda i, j: (jnp.maximum(i * (tm // HALO) - 1, 0), col0 + j))


def _halo_next(tm, cw, col0, L):
    return pl.BlockSpec((HALO, cw), lambda i, j: (jnp.minimum((i + 1) * (tm // HALO), L // HALO - 1), col0 + j))


def _conv_fwd(proj, conv_w, conv_b, DS, DCONV, tm, cw):
    L = proj.shape[0]
    col0 = DS // cw
    K = CONV_WIDTH

    def body(x_ref, p_ref, w_ref, b_ref, o_ref, ext):
        i = pl.program_id(0)
        ext[0:HALO, :] = jnp.where(i == 0, 0.0, p_ref[...])
        ext[HALO:, :] = x_ref[...]
        acc = jnp.broadcast_to(b_ref[...], (tm, cw))
        for k in range(K):
            acc = acc + w_ref[k:k + 1, :] * ext[pl.ds(HALO - (K - 1) + k, tm), :]
        o_ref[...] = _silu(acc)

    return pl.pallas_call(
        body, grid=(L // tm, DCONV // cw),
        in_specs=[pl.BlockSpec((tm, cw), lambda i, j: (i, col0 + j)), _halo_prev(tm, cw, col0),
                  pl.BlockSpec((K, cw), lambda i, j: (0, j)), pl.BlockSpec((1, cw), lambda i, j: (0, j))],
        out_specs=pl.BlockSpec((tm, cw), lambda i, j: (i, j)),
        out_shape=jax.ShapeDtypeStruct((L, DCONV), F32),
        scratch_shapes=[pltpu.VMEM((tm + HALO, cw), F32)],
        compiler_params=_params(("parallel", "parallel")), name="conv_fwd")(proj, proj, conv_w, conv_b)


def _conv_bwd(name, proj, dact, conv_w, conv_b, dproj, DS, first, tm, cw):
    L = proj.shape[0]
    ncols = dact.shape[1]
    col0 = (DS + first) // cw
    wcol0 = first // cw
    K = CONV_WIDTH
    nrt = L // tm

    def body(x_ref, p_ref, n_ref, d_ref, dn_ref, w_ref, b_ref, alias_ref, dx_ref, dw_ref, db_ref, ext, dext):
        i = pl.program_id(1)
        last = i == nrt - 1
        ext[0:HALO, :] = jnp.where(i == 0, 0.0, p_ref[...])
        ext[HALO:HALO + tm, :] = x_ref[...]
        ext[HALO + tm:, :] = n_ref[...]
        dfull = jnp.concatenate([d_ref[...], jnp.where(last, 0.0, dn_ref[...])], axis=0)
        acc = jnp.broadcast_to(b_ref[...], (tm + HALO, cw))
        for k in range(K):
            acc = acc + w_ref[k:k + 1, :] * ext[pl.ds(HALO - (K - 1) + k, tm + HALO), :]
        sg = _sigmoid(acc)
        dconv = dfull * (sg * (1.0 + acc * (1.0 - sg)))
        dext[...] = dconv

        @pl.when(i == 0)
        def _():
            dw_ref[...] = jnp.zeros_like(dw_ref)
            db_ref[...] = jnp.zeros_like(db_ref)

        dx = jnp.zeros((tm, cw), F32)
        for k in range(K):
            dx = dx + w_ref[k:k + 1, :] * dext[pl.ds(K - 1 - k, tm), :]
        dx_ref[...] = dx.astype(BF16)
        dtile = dext[pl.ds(0, tm), :]
        db_ref[...] += jnp.sum(dtile, axis=0, keepdims=True)
        for k in range(K):
            dw_ref[k:k + 1, :] += jnp.sum(dtile * ext[pl.ds(HALO - (K - 1) + k, tm), :], axis=0, keepdims=True)

    prev = pl.BlockSpec((HALO, cw), lambda j, i: (jnp.maximum(i * (tm // HALO) - 1, 0), col0 + j))
    nxt = pl.BlockSpec((HALO, cw), lambda j, i: (jnp.minimum((i + 1) * (tm // HALO), L // HALO - 1), col0 + j))
    dnxt = pl.BlockSpec((HALO, cw), lambda j, i: (jnp.minimum((i + 1) * (tm // HALO), L // HALO - 1), j))
    return pl.pallas_call(
        body, grid=(ncols // cw, nrt),
        in_specs=[pl.BlockSpec((tm, cw), lambda j, i: (i, col0 + j)), prev, nxt,
                  pl.BlockSpec((tm, cw), lambda j, i: (i, j)), dnxt,
                  pl.BlockSpec((K, cw), lambda j, i: (0, wcol0 + j)), pl.BlockSpec((1, cw), lambda j, i: (0, wcol0 + j)),
                  _ANY],
        out_specs=[pl.BlockSpec((tm, cw), lambda j, i: (i, col0 + j)),
                   pl.BlockSpec((K, cw), lambda j, i: (0, j)), pl.BlockSpec((1, cw), lambda j, i: (0, j))],
        out_shape=[jax.ShapeDtypeStruct(dproj.shape, BF16), jax.ShapeDtypeStruct((K, ncols), F32),
                   jax.ShapeDtypeStruct((1, ncols), F32)],
        input_output_aliases={7: 0},
        scratch_shapes=[pltpu.VMEM((tm + 2 * HALO, cw), F32), pltpu.VMEM((tm + HALO, cw), F32)],
        compiler_params=_params(("parallel", "arbitrary")), name=name,
    )(proj, proj, proj, dact, dact, conv_w, conv_b, dproj)


def _pool_fwd(proj, pool_w, pool_scale, mixed, ucol, ycol, DP, tm):
    L = proj.shape[0]
    PG = len(POOL_WINDOWS)
    PGD = DP // PG
    col0 = ucol // PGD
    ycol0 = ycol // PGD

    def body(u_ref, p_ref, w_ref, s_ref, alias_ref, pooled_ref, y_ref, ext):
        i, g = pl.program_id(0), pl.program_id(1)
        ext[0:HALO, :] = jnp.where(i == 0, 0.0, p_ref[...])
        ext[HALO:, :] = u_ref[...]
        t = i * tm + lax.broadcasted_iota(jnp.int32, (tm, 1), 0)
        for gi, win in enumerate(POOL_WINDOWS):
            @pl.when(g == gi)
            def _():
                acc = ext[pl.ds(HALO, tm), :]
                for j in range(1, win):
                    acc = acc + ext[pl.ds(HALO - j, tm), :]
                count = jnp.minimum(t + 1, win).astype(F32)
                pooled = (acc / count - u_ref[...]).astype(BF16)
                pooled_ref[...] = pooled
                y_ref[...] = (_dot(pooled, w_ref[...], NN) * s_ref[...]).astype(BF16)

    blk = pl.BlockSpec((tm, PGD), lambda i, g: (i, g))
    return pl.pallas_call(
        body, grid=(L // tm, PG),
        in_specs=[pl.BlockSpec((tm, PGD), lambda i, g: (i, col0 + g)), _halo_prev(tm, PGD, col0),
                  pl.BlockSpec((None, PGD, PGD), lambda i, g: (g, 0, 0)), pl.BlockSpec((1, PGD), lambda i, g: (0, g)),
                  _ANY],
        out_specs=[blk, pl.BlockSpec((tm, PGD), lambda i, g: (i, ycol0 + g))],
        out_shape=[jax.ShapeDtypeStruct((L, DP), BF16), jax.ShapeDtypeStruct(mixed.shape, BF16)],
        input_output_aliases={4: 1},
        scratch_shapes=[pltpu.VMEM((tm + HALO, PGD), F32)],
        compiler_params=_params(("parallel", "parallel")), name="pool_fwd")(proj, proj, pool_w, pool_scale, mixed)


def _pool_bwd(dy, pooled, pool_w, pool_scale, dproj, ucol, tm):
    L, DP = dy.shape
    PG = len(POOL_WINDOWS)
    PGD = DP // PG
    nrt = L // tm
    col0 = ucol // PGD

    def body(d_ref, dn_ref, p_ref, w_ref, s_ref, alias_ref, du_ref, dw_ref, ds_ref, qext):
        g, i = pl.program_id(0), pl.program_id(1)
        last = i == nrt - 1
        dfull = jnp.concatenate([d_ref[...], jnp.where(last, 0.0, dn_ref[...])], axis=0)
        dyp = (dfull * s_ref[...]).astype(BF16)
        dpooled = _dot(dyp, w_ref[...], NT)
        t = i * tm + lax.broadcasted_iota(jnp.int32, (tm + HALO, 1), 0)
        for gi, win in enumerate(POOL_WINDOWS):
            @pl.when(g == gi)
            def _():
                qext[...] = dpooled / jnp.minimum(t + 1, win).astype(F32)
                acc = qext[pl.ds(0, tm), :]
                for j in range(1, win):
                    acc = acc + qext[pl.ds(j, tm), :]
                du_ref[...] = (acc - dpooled[0:tm, :]).astype(BF16)

        @pl.when(i == 0)
        def _():
            dw_ref[...] = jnp.zeros_like(dw_ref)
            ds_ref[...] = jnp.zeros_like(ds_ref)

        pooled_t = p_ref[...]
        dw_ref[...] += _dot(pooled_t, dyp[0:tm, :], TN)
        ypre = _dot(pooled_t, w_ref[...], NN)
        ds_ref[...] += jnp.sum(d_ref[...] * ypre, axis=0, keepdims=True)

    blk = pl.BlockSpec((tm, PGD), lambda g, i: (i, g))
    nxt = pl.BlockSpec((HALO, PGD), lambda g, i: (jnp.minimum((i + 1) * (tm // HALO), L // HALO - 1), g))
    wspec = pl.BlockSpec((None, PGD, PGD), lambda g, i: (g, 0, 0))
    vec = pl.BlockSpec((1, PGD), lambda g, i: (0, g))
    return pl.pallas_call(
        body, grid=(PG, nrt),
        in_specs=[blk, nxt, blk, wspec, vec, _ANY],
        out_specs=[pl.BlockSpec((tm, PGD), lambda g, i: (i, col0 + g)), wspec, vec],
        out_shape=[jax.ShapeDtypeStruct(dproj.shape, BF16), jax.ShapeDtypeStruct((PG, PGD, PGD), F32),
                   jax.ShapeDtypeStruct((1, DP), F32)],
        input_output_aliases={5: 0},
        scratch_shapes=[pltpu.VMEM((tm + HALO, PGD), F32)],
        compiler_params=_params(("parallel", "arbitrary")), name="pool_bwd",
    )(dy, dy, pooled, pool_w, pool_scale, dproj)


def _ssd_common(dtc_raw, dtr_raw, bc, br, ac, ar):
    ch = CHUNK
    row = lax.broadcasted_iota(jnp.int32, (ch, ch), 0)
    col = lax.broadcasted_iota(jnp.int32, (ch, ch), 1)
    lower = row >= col
    dtc = _softplus(dtc_raw + bc)
    dtr = _softplus(dtr_raw + br)
    a_c = -jnp.exp(ac)
    a_r = -jnp.exp(ar)
    hi = lax.Precision.HIGHEST
    acol = jnp.dot(lower.astype(F32), dtc * a_c, preferred_element_type=F32, precision=hi)
    arow = jnp.dot(dtr * a_r, (row <= col).astype(F32), preferred_element_type=F32, precision=hi)
    return lower, row <= col, dtc, a_c, acol, arow, dtr


def _spread_heads(cols, passes=3):
    R = cols.shape[1]
    shape = (R, R * LANES)
    spread = (lax.broadcasted_iota(jnp.int32, shape, 0) == lax.broadcasted_iota(jnp.int32, shape, 1) // LANES).astype(BF16)
    out, rest = None, cols
    for _ in range(passes):
        part = rest.astype(BF16)
        rest = rest - part.astype(F32)
        term = _dot(part, spread, NN)
        out = term if out is None else out + term
    return out


def _ssd_fwd(xbc, dtc_raw, dtr_raw, bias_c, bias_r, alog_c, alog_r, dskip_c, DS, N):
    L = xbc.shape[0]
    G, P, ch = SSD_GROUPS, HEAD_DIM, CHUNK
    R = dtc_raw.shape[2]
    GW = R * P
    nc = L // ch

    def body(xs_ref, b_ref, c_ref, dtc_ref, dtr_ref, bc_ref, br_ref, ac_ref, ar_ref, dk_ref,
             y_ref, st_ref, h_ref):
        @pl.when(pl.program_id(1) == 0)
        def _():
            h_ref[...] = jnp.zeros_like(h_ref)

        lower, _, _, _, acol_all, arow_all, dtr = _ssd_common(
            dtc_ref[...], dtr_ref[...], bc_ref[...], br_ref[...], ac_ref[...], ar_ref[...])
        bm = b_ref[...]
        cb16 = c_ref[...].astype(BF16)
        b16 = bm.astype(BF16)
        bt = bm.T
        cb = _dot(cb16, b16, NT)
        dk = dk_ref[...]
        st_ref[...] = h_ref[...]
        for r in range(R):
            acol = acol_all[:, r:r + 1]
            arow = arow_all[r:r + 1, :]
            alast = acol_all[ch - 1:ch, r:r + 1]
            dt_row = dtr[r:r + 1, :]
            decay = jnp.exp(jnp.where(lower, acol - arow, -1e30))
            x_h = xs_ref[:, pl.ds(r * P, P)]
            x16 = x_h.astype(BF16)
            m16 = (cb * decay * dt_row).astype(BF16)
            h_prev = h_ref[r]
            y = _dot(m16, x16, NN)
            y = y + jnp.exp(acol) * _dot(cb16, h_prev.astype(BF16), NN)
            y = y + dk[:, r:r + 1] * x_h
            y_ref[:, pl.ds(r * P, P)] = y
            to_end_dt = jnp.exp(alast - arow) * dt_row
            h_ref[r] = jnp.exp(alast) * h_prev + _dot((bt * to_end_dt).astype(BF16), x16, NN)

    nb = DS // N
    return pl.pallas_call(
        body, grid=(G, nc),
        in_specs=[pl.BlockSpec((ch, GW), lambda g, c: (c, g)),
                  pl.BlockSpec((ch, N), lambda g, c: (c, nb + g)),
                  pl.BlockSpec((ch, N), lambda g, c: (c, nb + G + g)),
                  pl.BlockSpec((None, ch, R), lambda g, c: (g, c, 0)),
                  pl.BlockSpec((None, R, ch), lambda g, c: (g, 0, c)),
                  pl.BlockSpec((None, 1, R), lambda g, c: (g, 0, 0)),
                  pl.BlockSpec((None, R, 1), lambda g, c: (g, 0, 0)),
                  pl.BlockSpec((None, 1, R), lambda g, c: (g, 0, 0)),
                  pl.BlockSpec((None, R, 1), lambda g, c: (g, 0, 0)),
                  pl.BlockSpec((None, 1, R), lambda g, c: (g, 0, 0))],
        out_specs=[pl.BlockSpec((ch, GW), lambda g, c: (c, g)),
                   pl.BlockSpec((None, R, N, P), lambda g, c: (c, g, 0, 0))],
        out_shape=[jax.ShapeDtypeStruct((L, DS), F32), jax.ShapeDtypeStruct((nc, G * R, N, P), F32)],
        scratch_shapes=[pltpu.VMEM((R, N, P), F32)],
        compiler_params=_params(("parallel", "arbitrary")), name="ssd_fwd",
    )(xbc, xbc, xbc, dtc_raw, dtr_raw, bias_c, bias_r, alog_c, alog_r, dskip_c)


def _ssd_bwd(xbc, dtc_raw, dtr_raw, bias_c, bias_r, alog_c, alog_r, dskip_c, dy, states, DS, N):
    L = xbc.shape[0]
    G, P, ch = SSD_GROUPS, HEAD_DIM, CHUNK
    R = dtc_raw.shape[2]
    GW = R * P
    nc = L // ch
    assert N == LANES and P <= LANES and ch % LANES == 0

    def body(xs_ref, b_ref, c_ref, dtc_ref, dtr_ref, bc_ref, br_ref, ac_ref, ar_ref, dk_ref,
             dy_ref, stp_ref,
             dxs_ref, db_ref, dc_ref, ddt_ref, dal_ref, ddk_ref, dbias_ref, dh_ref):
        @pl.when(pl.program_id(1) == 0)
        def _():
            dh_ref[...] = jnp.zeros_like(dh_ref)
            dal_ref[...] = jnp.zeros_like(dal_ref)
            ddk_ref[...] = jnp.zeros_like(ddk_ref)
            dbias_ref[...] = jnp.zeros_like(dbias_ref)

        lower, upper, dtc, a_c, acol_all, arow_all, _ = _ssd_common(
            dtc_ref[...], dtr_ref[...], bc_ref[...], br_ref[...], ac_ref[...], ar_ref[...])
        bm = b_ref[...]
        cm = c_ref[...]
        b16 = bm.astype(BF16)
        c16 = cm.astype(BF16)
        ct16 = cm.T.astype(BF16)
        cb = _dot(c16, b16, NT)
        cbt = _dot(b16, c16, NT)
        dk = dk_ref[...]
        lane_r = lax.broadcasted_iota(jnp.int32, (ch, R), 1)
        lane_1 = lax.broadcasted_iota(jnp.int32, (1, R), 1)
        dc = jnp.zeros((ch, N), F32)
        db = jnp.zeros((ch, N), F32)
        da_all = jnp.zeros((R, ch), F32)
        q_all = jnp.zeros((R, ch), F32)
        sxd_all = jnp.zeros((ch, R), F32)
        const = jnp.zeros((1, R), F32)
        ddk = jnp.zeros((1, R), F32)
        sub_r = lax.broadcasted_iota(jnp.int32, (R, ch), 0)
        ct = cm.T
        bt = bm.T
        dcb = jnp.zeros((ch, ch), F32)
        acol_lanes = _spread_heads(acol_all)
        dt_lanes = _spread_heads(dtc, passes=2)
        for r in range(R):
            a128 = acol_lanes[:, r * LANES:(r + 1) * LANES]
            arow = arow_all[r:r + 1, :]
            alast = acol_all[ch - 1:ch, r:r + 1]
            seg = jnp.tile(a128, (1, ch // LANES)) - arow
            decay = jnp.exp(jnp.where(lower, seg, -1e30))
            decay_t = jnp.exp(jnp.where(upper, -seg, -1e30))
            x_h = xs_ref[:, pl.ds(r * P, P)]
            dy_h = dy_ref[:, pl.ds(r * P, P)]
            dt_h = dt_lanes[:, r * LANES:r * LANES + P]
            dk_h = dk[:, r:r + 1]
            xdt = x_h * dt_h
            xdt16 = xdt.astype(BF16)
            dy16 = dy_h.astype(BF16)
            h_prev = stp_ref[r]
            h16 = h_prev.astype(BF16)
            dh_next = dh_ref[r]
            dhn16 = dh_next.astype(BF16)
            to_end_n = jnp.exp(alast - a128)
            e_a_n = jnp.exp(a128)
            to_end, e_a = to_end_n[:, :P], e_a_n[:, :P]
            mt = cbt * decay_t
            pm = _dot(dy16, xdt16, NT) * decay
            wt = _dot(xdt16, dy16, NT) * mt
            dxdt = _dot(mt.astype(BF16), dy16, NN) + to_end * _dot(b16, dhn16, NN)
            dcb = dcb + pm
            dc = dc + e_a_n * _dot(dy16, h16, NT)
            db = db + to_end_n * _dot(xdt16, dhn16, NT)
            dh_ref[r] = jnp.exp(alast) * dh_next + _dot(ct16, (dy_h * e_a).astype(BF16), NN)
            da = (jnp.sum(wt, axis=0, keepdims=True) - jnp.sum(pm * cb, axis=0, keepdims=True)
                  + jnp.exp(arow) * jnp.sum(ct * _dot(h16, dy16, NT), axis=0, keepdims=True))
            q = jnp.exp(alast - arow) * jnp.sum(bt * _dot(dhn16, xdt16, NT), axis=0, keepdims=True)
            da_all = da_all + jnp.where(sub_r == r, da, 0.0)
            q_all = q_all + jnp.where(sub_r == r, q, 0.0)
            sxd_all = sxd_all + jnp.where(lane_r == r, jnp.sum(dxdt * x_h, axis=1, keepdims=True), 0.0)
            const = const + jnp.where(lane_1 == r, jnp.exp(alast) * jnp.sum(dh_next * h_prev), 0.0)
            ddk = ddk + jnp.where(lane_1 == r, jnp.sum(dy_h * x_h), 0.0)
            dxs_ref[:, pl.ds(r * P, P)] = dxdt * dt_h + dk_h * dy_h
        dcb16 = dcb.astype(BF16)
        dc_ref[...] = dc + _dot(dcb16, b16, NN)
        db_ref[...] = db + _dot(dcb16, c16, TN)
        hi = lax.Precision.HIGHEST
        strict_lower = jnp.logical_and(lower, jnp.logical_not(upper))
        dda = (lax.dot_general(upper.astype(F32), da_all, NT, preferred_element_type=F32, precision=hi)
               + lax.dot_general(strict_lower.astype(F32), q_all, NT, preferred_element_type=F32, precision=hi)
               + const)
        ddt = dda * a_c + sxd_all
        dal_ref[...] += jnp.sum(dda * dtc, axis=0, keepdims=True) * a_c
        ddk_ref[...] += ddk
        ddt_raw = ddt * _sigmoid(dtc_ref[...] + bc_ref[...])
        ddt_ref[...] = ddt_raw
        dbias_ref[...] += jnp.sum(ddt_raw, axis=0, keepdims=True)

    nb = DS // N
    rc = lambda c: nc - 1 - c
    vec_c = pl.BlockSpec((None, 1, R), lambda g, c: (g, 0, 0))
    vec_r = pl.BlockSpec((None, R, 1), lambda g, c: (g, 0, 0))
    big = pl.BlockSpec((ch, GW), lambda g, c: (rc(c), g))
    return pl.pallas_call(
        body, grid=(G, nc),
        in_specs=[big,
                  pl.BlockSpec((ch, N), lambda g, c: (rc(c), nb + g)),
                  pl.BlockSpec((ch, N), lambda g, c: (rc(c), nb + G + g)),
                  pl.BlockSpec((None, ch, R), lambda g, c: (g, rc(c), 0)),
                  pl.BlockSpec((None, R, ch), lambda g, c: (g, 0, rc(c))),
                  vec_c, vec_r, vec_c, vec_r, vec_c,
                  big,
                  pl.BlockSpec((None, R, N, P), lambda g, c: (rc(c), g, 0, 0))],
        out_specs=[big,
                   pl.BlockSpec((ch, N), lambda g, c: (rc(c), g)),
                   pl.BlockSpec((ch, N), lambda g, c: (rc(c), g)),
                   pl.BlockSpec((None, ch, R), lambda g, c: (g, rc(c), 0)),
                   vec_c, vec_c, vec_c],
        out_shape=[jax.ShapeDtypeStruct((L, DS), F32), jax.ShapeDtypeStruct((L, G * N), F32),
                   jax.ShapeDtypeStruct((L, G * N), F32), jax.ShapeDtypeStruct((G, L, R), F32),
                   jax.ShapeDtypeStruct((G, 1, R), F32), jax.ShapeDtypeStruct((G, 1, R), F32),
                   jax.ShapeDtypeStruct((G, 1, R), F32)],
        scratch_shapes=[pltpu.VMEM((R, N, P), F32)],
        compiler_params=_params(("parallel", "arbitrary")), name="ssd_bwd",
    )(xbc, xbc, xbc, dtc_raw, dtr_raw, bias_c, bias_r, alog_c, alog_r, dskip_c, dy, states)


def _adam_math(w, g, m, v):
    m = ADAM_B1 * m + (1.0 - ADAM_B1) * g
    v = ADAM_B2 * v + (1.0 - ADAM_B2) * jnp.square(g)
    m_hat = m / (1.0 - ADAM_B1 ** ADAM_STEP)
    v_hat = v / (1.0 - ADAM_B2 ** ADAM_STEP)
    delta = -ADAM_LR * (m_hat / (jnp.sqrt(v_hat) + ADAM_EPS) + ADAM_WD * w)
    return delta, m, v


def _adam(name, w, g, m, v):
    rows, cols = w.shape
    tr = _tile(rows, max(8, (1 << 19) // cols // 8 * 8), 8)

    def body(w_ref, g_ref, m_ref, v_ref, go_ref, d_ref, mo_ref, vo_ref):
        g = g_ref[...]
        d, m2, v2 = _adam_math(w_ref[...], g, m_ref[...], v_ref[...])
        go_ref[...] = g
        d_ref[...] = d
        mo_ref[...] = m2
        vo_ref[...] = v2

    blk = pl.BlockSpec((tr, cols), lambda i: (i, 0))
    return pl.pallas_call(
        body, grid=(rows // tr,), in_specs=[blk] * 4, out_specs=[blk] * 4,
        out_shape=[jax.ShapeDtypeStruct((rows, cols), F32)] * 4,
        compiler_params=_params(("parallel",)), name=name)(w, g, m, v)


def _small_sum_adam(gathered, w, m, v, rows):
    def body(ga_ref, w_ref, m_ref, v_ref, g_ref, d_ref, mo_ref, vo_ref):
        g = ga_ref[0:rows, :]
        for d in range(1, N_DEV):
            g = g + ga_ref[d * rows:(d + 1) * rows, :]
        g_ref[...] = g
        dl, m2, v2 = _adam_math(w_ref[...], g, m_ref[...], v_ref[...])
        d_ref[...] = dl
        mo_ref[...] = m2
        vo_ref[...] = v2

    return pl.pallas_call(
        body, out_shape=[jax.ShapeDtypeStruct((rows, LANES), F32)] * 4,
        compiler_params=pltpu.CompilerParams(vmem_limit_bytes=VMEM_LIMIT), name="small_sum_adam",
    )(gathered, w, m, v)


def _row_tile(rh, cols):
    return _tile(rh, max(16, (1 << 19) // cols // 16 * 16), 16)


def _shard_dims(g, window=None):
    if g.ndim == 3:
        return g.shape[1], g.shape[2]
    return g.shape[0], (window[1] if window else g.shape[1] // N_CHIPS)


def _pair_sum(name, g, recv, pos, window=None):
    r, c = _shard_dims(g, window)
    rh = r // 2
    cb = LANES if window else c
    tr = _row_tile(rh, cb)
    nrt = rh // tr

    def body(pos_ref, a_ref, b_ref, o_ref):
        o_ref[...] = (a_ref[...] + b_ref[...].astype(F32)).astype(BF16)

    if g.ndim == 3:
        own = pl.BlockSpec((None, tr, c), lambda k, i, j, p: (k, p[1] * nrt + i, 0))
    elif window:
        pos = jnp.concatenate([pos, jnp.asarray([s // LANES for s in window[0]], jnp.int32)])
        own = pl.BlockSpec((tr, cb), lambda k, i, j, p: (p[1] * nrt + i, p[2 + k] + j))
    else:
        own = pl.BlockSpec((tr, c), lambda k, i, j, p: (p[1] * nrt + i, k))
    part = pl.BlockSpec((None, tr, cb), lambda k, i, j, p: (k, i, j))
    return pl.pallas_call(
        body,
        grid_spec=pltpu.PrefetchScalarGridSpec(
            num_scalar_prefetch=1, grid=(N_CHIPS, nrt, c // cb), in_specs=[own, part], out_specs=part),
        out_shape=jax.ShapeDtypeStruct((N_CHIPS, rh, c), BF16),
        compiler_params=_params(("parallel", "parallel", "parallel")), name=name)(pos, g, recv)


def _chip_sum(name, parts, pos):
    _, rh, c = parts.shape
    tr = _row_tile(rh, c)
    nrt = rh // tr

    def body(pos_ref, p_ref, o_ref):
        s = p_ref[0].astype(F32)
        for k in range(1, N_CHIPS):
            s = s + p_ref[k].astype(F32)
        o_ref[...] = s

    return pl.pallas_call(
        body,
        grid_spec=pltpu.PrefetchScalarGridSpec(
            num_scalar_prefetch=1, grid=(nrt,),
            in_specs=[pl.BlockSpec((N_CHIPS, tr, c), lambda i, p: (0, i, 0))],
            out_specs=pl.BlockSpec((tr, c), lambda i, p: (p[1] * nrt + i, 0))),
        out_shape=jax.ShapeDtypeStruct((2 * rh, c), F32),
        compiler_params=_params(("parallel",)), name=name)(pos, parts)


_HBM = pl.BlockSpec(memory_space=pltpu.HBM)


def _chip_xy(k):
    return k // 2, k % 2


def _half_rows(ref, hc, rh):
    return ref.at[pl.ds(pl.multiple_of(hc * rh, 16), rh), :]


_SEM = pl.BlockSpec(memory_space=pltpu.SEMAPHORE)
_ANY = pl.BlockSpec(memory_space=pl.ANY)
_SPLIT = pltpu.CompilerParams(has_side_effects=pltpu.SideEffectType.DATAFLOW_SIDE_EFFECTING)


def _in_hbm(a):
    return pltpu.with_memory_space_constraint(a, pltpu.HBM)


def _push_start(name, srcs, land_shapes, copies_of):
    n = len(srcs)

    def body(*refs):
        s_refs, l_refs = refs[:n], refs[n:2 * n]
        send_sems, recv_sems = refs[2 * n], refs[2 * n + 1]
        token = refs[-1]
        x, y, c = lax.axis_index("x"), lax.axis_index("y"), lax.axis_index("c")
        me = 2 * x + y
        for i in range(n):
            for k in range(N_CHIPS):
                @pl.when(k != me)
                def _():
                    src, dst, dev = copies_of(i, k, s_refs[i], l_refs[i], me, x, y, c)
                    pltpu.make_async_remote_copy(
                        src_ref=src, dst_ref=dst, send_sem=send_sems.at[N_CHIPS * i + k],
                        recv_sem=recv_sems.at[N_CHIPS * i + me], device_id=dev, device_id_type=MESH).start()
        token[...] = jnp.zeros_like(token)

    lands = [lax.empty(s, d) for s, d in land_shapes]
    outs = pl.pallas_call(
        body, name=name,
        out_shape=[pltpu.SemaphoreType.DMA((N_CHIPS * n,)), pltpu.SemaphoreType.DMA((N_CHIPS * n,))]
        + [pltpu.HBM(s.shape, s.dtype) for s in srcs] + [pltpu.HBM(s, d) for s, d in land_shapes]
        + [jax.ShapeDtypeStruct((8, LANES), F32)],
        in_specs=[_HBM] * (2 * n),
        out_specs=[_SEM, _SEM] + [_HBM] * (2 * n) + [pl.BlockSpec(memory_space=pltpu.VMEM)],
        input_output_aliases={j: 2 + j for j in range(2 * n)},
        compiler_params=_SPLIT,
    )(*[_in_hbm(s) for s in srcs], *[_in_hbm(l) for l in lands])
    return outs[0], outs[1], outs[2:2 + n], outs[2 + n:2 + 2 * n], outs[-1]


def _push_wait(name, started, after, landed_of):
    send_sems, recv_sems, srcs, lands, _ = started
    n = len(srcs)
    after = list(after) if isinstance(after, (list, tuple)) else [after]

    def body(*refs):
        s_refs, l_refs = refs[:n], refs[n:2 * n]
        send, recv = refs[2 * n], refs[2 * n + 1]
        token = refs[-1]
        token[...] = jnp.zeros_like(token)
        x, y, c = lax.axis_index("x"), lax.axis_index("y"), lax.axis_index("c")
        me = 2 * x + y
        for i in range(n):
            for k in range(N_CHIPS):
                @pl.when(k != me)
                def _():
                    src, dst = landed_of(i, k, s_refs[i], l_refs[i], me, c)
                    cp = pltpu.make_async_remote_copy(
                        src_ref=src, dst_ref=dst, send_sem=send.at[N_CHIPS * i + k], recv_sem=recv.at[N_CHIPS * i + k],
                        device_id=(x, y, c), device_id_type=MESH)
                    cp.wait_send()
                    cp.wait_recv()

    outs = pl.pallas_call(
        body, name=name,
        out_shape=[pltpu.HBM(s.shape, s.dtype) for s in srcs] + [pltpu.HBM(l.shape, l.dtype) for l in lands]
        + [jax.ShapeDtypeStruct((8, LANES), F32)],
        in_specs=[_HBM] * (2 * n) + [_SEM, _SEM] + [_ANY] * len(after),
        out_specs=[_HBM] * (2 * n) + [pl.BlockSpec(memory_space=pltpu.VMEM)],
        input_output_aliases={j: j for j in range(2 * n)},
        compiler_params=_SPLIT,
    )(*srcs, *lands, send_sems, recv_sems, *after)
    return outs[:n], outs[n:2 * n], outs[-1]


def _gather_start(name, shards):
    def copies_of(i, k, src, land, me, x, y, c):
        rh = shards[i].shape[0] // 2
        kx, ky = _chip_xy(k)
        return _half_rows(src, c, rh), _half_rows(land.at[me], c, rh), (kx, ky, c)

    return _push_start(name, shards, [((N_CHIPS,) + s.shape, s.dtype) for s in shards], copies_of)


def _gather_wait(name, started, after):
    shapes = [s.shape for s in started[2]]

    def landed_of(i, k, src, land, me, c):
        rh = shapes[i][0] // 2
        return _half_rows(src, c, rh), _half_rows(land.at[k], c, rh)

    return _push_wait(name, started, after, landed_of)


def _forward_halves(name, bufs):
    n = len(bufs)

    def body(*refs):
        i_refs, o_refs, send_sems, recv_sems = refs[:n], refs[n:2 * n], refs[2 * n], refs[2 * n + 1]
        x, y, c = lax.axis_index("x"), lax.axis_index("y"), lax.axis_index("c")
        me = 2 * x + y

        def fwd(i, k, hc):
            rh = bufs[i].shape[1] // 2
            return pltpu.make_async_remote_copy(
                src_ref=_half_rows(i_refs[i].at[k], hc, rh), dst_ref=_half_rows(o_refs[i].at[k], hc, rh),
                send_sem=send_sems.at[i, k], recv_sem=recv_sems.at[i, k],
                device_id=(x, y, 1 - c), device_id_type=MESH)

        for i in range(n):
            for k in range(N_CHIPS):
                @pl.when(k != me)
                def _():
                    fwd(i, k, c).start()
        for i in range(n):
            for k in range(N_CHIPS):
                @pl.when(k != me)
                def _():
                    fwd(i, k, 1 - c).wait_recv()
        for i in range(n):
            for k in range(N_CHIPS):
                @pl.when(k != me)
                def _():
                    fwd(i, k, c).wait_send()

    return pl.pallas_call(
        body, in_specs=[_HBM] * n, out_specs=[_HBM] * n,
        out_shape=[jax.ShapeDtypeStruct(b.shape, b.dtype) for b in bufs],
        input_output_aliases={i: i for i in range(n)},
        scratch_shapes=[pltpu.SemaphoreType.DMA((n, N_CHIPS))] * 2,
        name=name)(*bufs)


def _swap_halves(name, grads, windows):
    n = len(grads)
    dims = [_shard_dims(g, w) for g, w in zip(grads, windows)]

    def body(*refs):
        g_refs, o_refs, send_sems, recv_sems = refs[:n], refs[n:2 * n], refs[2 * n], refs[2 * n + 1]
        x, y, c = lax.axis_index("x"), lax.axis_index("y"), lax.axis_index("c")
        copies = []
        for i in range(n):
            r, cw = dims[i]
            for k in range(N_CHIPS):
                if grads[i].ndim == 3:
                    shard = g_refs[i].at[k]
                else:
                    shard = g_refs[i].at[:, pl.ds(windows[i][0][k] if windows[i] else k * cw, cw)]
                copies.append(pltpu.make_async_remote_copy(
                    src_ref=_half_rows(shard, 1 - c, r // 2), dst_ref=o_refs[i].at[k],
                    send_sem=send_sems.at[i, k], recv_sem=recv_sems.at[i, k],
                    device_id=(x, y, 1 - c), device_id_type=MESH))
        for cp in copies:
            cp.start()
        for cp in copies:
            cp.wait()

    return pl.pallas_call(
        body, in_specs=[_HBM] * n, out_specs=[_HBM] * n,
        out_shape=[jax.ShapeDtypeStruct((N_CHIPS, r // 2, cw), g.dtype) for (r, cw), g in zip(dims, grads)],
        scratch_shapes=[pltpu.SemaphoreType.DMA((n, N_CHIPS))] * 2,
        name=name)(*grads)


def _scatter_start(name, parts):
    def copies_of(i, k, src, land, me, x, y, c):
        kx, ky = _chip_xy(k)
        return src.at[k], land.at[me], (kx, ky, c)

    return _push_start(name, parts, [(p.shape, p.dtype) for p in parts], copies_of)


def _scatter_wait(name, started, after):
    return _push_wait(name, started, after, lambda i, k, src, land, me, c: (src.at[k], land.at[k]))


def _join_halves(bufs):
    n = len(bufs)

    def body(*refs):
        i_refs, o_refs, send_sems, recv_sems = refs[:n], refs[n:2 * n], refs[2 * n], refs[2 * n + 1]
        x, y, c = lax.axis_index("x"), lax.axis_index("y"), lax.axis_index("c")
        copies = []
        for i in range(n):
            rh = bufs[i].shape[0] // 2
            copies.append(pltpu.make_async_remote_copy(
                src_ref=_half_rows(i_refs[i], c, rh), dst_ref=_half_rows(o_refs[i], c, rh),
                send_sem=send_sems.at[i], recv_sem=recv_sems.at[i],
                device_id=(x, y, 1 - c), device_id_type=MESH))
        for cp in copies:
            cp.start()
        for i in range(n):
            rh = bufs[i].shape[0] // 2
            pltpu.make_async_remote_copy(
                src_ref=_half_rows(i_refs[i], c, rh), dst_ref=_half_rows(o_refs[i], 1 - c, rh),
                send_sem=send_sems.at[i], recv_sem=recv_sems.at[i],
                device_id=(x, y, 1 - c), device_id_type=MESH).wait_recv()
        for cp in copies:
            cp.wait_send()

    return pl.pallas_call(
        body, in_specs=[_HBM] * n, out_specs=[_HBM] * n,
        out_shape=[jax.ShapeDtypeStruct(b.shape, F32) for b in bufs],
        input_output_aliases={i: i for i in range(n)},
        scratch_shapes=[pltpu.SemaphoreType.DMA((n,))] * 2,
        name="join_halves")(*bufs)


def _all_gather_small(name, blk):
    m_per, n = blk.shape

    def body(x_ref, out_ref, send_sems, recv_sems, local_sem):
        x, y, c = lax.axis_index("x"), lax.axis_index("y"), lax.axis_index("c")
        me, sibling = (x, y, c), (x, y, 1 - c)
        chips = [(1 - x, y), (x, 1 - y), (1 - x, 1 - y)]

        def rows(px, py, pc):
            return out_ref.at[pl.ds((4 * px + 2 * py + pc) * m_per, m_per), :]

        def copy(k, block, to, src=None):
            return pltpu.make_async_remote_copy(
                src_ref=rows(*block) if src is None else src, dst_ref=rows(*block),
                send_sem=send_sems.at[k], recv_sem=recv_sems.at[k],
                device_id=to, device_id_type=MESH)

        mine = pltpu.make_async_copy(x_ref, rows(*me), local_sem)
        mine.start()
        first = [copy(0, me, sibling, src=x_ref)]
        first += [copy(1 + j, me, (*chip, c), src=x_ref) for j, chip in enumerate(chips)]
        for cp in first:
            cp.start()
        passed = [copy(4 + j, (*chip, c), sibling) for j, chip in enumerate(chips)]
        for j, chip in enumerate(chips):
            copy(1 + j, (*chip, c), me).wait_recv()
            passed[j].start()
        copy(0, sibling, me).wait_recv()
        for j, chip in enumerate(chips):
            copy(4 + j, (*chip, 1 - c), me).wait_recv()
        for cp in first + passed:
            cp.wait_send()
        mine.wait()

    return pl.pallas_call(
        body, out_shape=jax.ShapeDtypeStruct((N_DEV * m_per, n), blk.dtype),
        in_specs=[pl.BlockSpec(memory_space=pltpu.VMEM)],
        out_specs=pl.BlockSpec(memory_space=pltpu.VMEM),
        scratch_shapes=[pltpu.SemaphoreType.DMA((7,)), pltpu.SemaphoreType.DMA((7,)), pltpu.SemaphoreType.DMA],
        name=name)(blk)


def _pack_rows(vecs, width):
    parts = []
    for v in vecs:
        f = v.reshape(-1)
        pad = (-f.shape[0]) % (8 * width)
        parts.append(jnp.pad(f, (0, pad)) if pad else f)
    return jnp.concatenate(parts).reshape(-1, width)


def _unpack_rows(packed, shapes, width):
    flat = packed.reshape(-1)
    out, off = [], 0
    for s in shapes:
        n = math.prod(s)
        out.append(flat[off:off + n].reshape(s))
        off += n + ((-n) % (8 * width))
    return out


class _WinPlan:
    def __init__(self, ncol, dt0, h, dmain):
        self.ncol, self.h = ncol, h
        self.dt_shard = dt0 // ncol
        assert (dt0 + h - 1) // ncol == self.dt_shard and dmain % LANES == 0
        self.dt_local = dt0 - self.dt_shard * ncol
        to_main = lambda g: g if g <= dt0 else g - h
        self.lo = [to_main(ncol * k) for k in range(N_CHIPS)]
        self.hi = [to_main(ncol * (k + 1)) for k in range(N_CHIPS)]
        down = lambda v: v // LANES * LANES
        self.ww = max(-(-(hi - down(lo)) // LANES) * LANES for lo, hi in zip(self.lo, self.hi))
        self.ws = [min(down(lo), dmain - self.ww) for lo in self.lo]
        self.dmain = dmain

    def to_window(self, k, shard):
        if k == self.dt_shard:
            shard = jnp.concatenate([shard[:, :self.dt_local], shard[:, self.dt_local + self.h:]], axis=1)
        left = self.lo[k] - self.ws[k]
        return jnp.pad(shard, ((0, 0), (left, self.ww - left - shard.shape[1])))

    def from_window(self, k, window, dt_cols):
        left = self.lo[k] - self.ws[k]
        body = window[:, left:left + self.hi[k] - self.lo[k]]
        if k == self.dt_shard:
            body = jnp.concatenate([body[:, :self.dt_local], dt_cols, body[:, self.dt_local:]], axis=1)
        return body

    def merge(self, windows):
        cuts = sorted({0, self.dmain} | set(self.ws) | {w + self.ww for w in self.ws})
        segs = []
        for a, b in zip(cuts[:-1], cuts[1:]):
            parts = [windows[k][:, a - self.ws[k]:b - self.ws[k]] for k in range(N_CHIPS)
                     if self.ws[k] <= a and b <= self.ws[k] + self.ww]
            segs.append(functools.reduce(jnp.add, parts))
        return jnp.concatenate(segs, axis=1)


def kernel(x, attn_norm_w, w_in, conv_w, conv_b, dt_bias, a_log, d_skip, ssd_norm_w, pool_w, pool_scale, w_out, ffn_norm_w, w_gate, w_up, w_down, final_norm_w, loss_target, m_attn_norm_w, m_w_in, m_conv_w, m_conv_b, m_dt_bias, m_a_log, m_d_skip, m_ssd_norm_w, m_pool_w, m_pool_scale, m_w_out, m_ffn_norm_w, m_w_gate, m_w_up, m_w_down, m_final_norm_w, v_attn_norm_w, v_w_in, v_conv_w, v_conv_b, v_dt_bias, v_a_log, v_d_skip, v_ssd_norm_w, v_pool_w, v_pool_scale, v_w_out, v_ffn_norm_w, v_w_gate, v_w_up, v_w_down, v_final_norm_w):
    G, P, PG = SSD_GROUPS, HEAD_DIM, len(POOL_WINDOWS)
    _, L, D = x.shape
    H = a_log.shape[1]
    R = H // G
    DS = H * P
    DCONV = conv_b.shape[1]
    N = (DCONV - DS) // (2 * G)
    DP = pool_scale.shape[1]
    PGD = DP // PG
    DIN = N_CHIPS * w_in.shape[2]
    DFF = N_CHIPS * w_gate.shape[2]
    DMAIN = DS + DCONV + DP
    assert DIN == DMAIN + H and DS == DP and H <= LANES

    cx, cy, cc = lax.axis_index("x"), lax.axis_index("y"), lax.axis_index("c")
    chip = 2 * cx + cy

    win = _WinPlan(DIN // N_CHIPS, DS + DCONV, H, DMAIN)
    my_window = lax.switch(chip, [functools.partial(win.to_window, k) for k in range(N_CHIPS)], w_in[0].astype(BF16))
    started_in = _gather_start("gather_start_in", [my_window])

    def forward_gathered(tag, shards, landed):
        landed = _forward_halves("gather_forward_" + tag, landed)
        return [lax.dynamic_update_slice(g, s[None], (chip, 0, 0)) for g, s in zip(landed, shards)]

    def cols(p):
        return jnp.moveaxis(p, 0, -2).reshape(p.shape[1:-1] + (N_CHIPS * p.shape[-1],))

    ncw = CONV_WIDTH * DCONV // N_CHIPS
    dt_here = jnp.where(chip == win.dt_shard, w_in[0][:, win.dt_local:win.dt_local + H], 0.0)
    start_blk = _pack_rows([conv_w[0], dt_here], LANES)
    start_all = _all_gather_small("gather_conv_w", start_blk).reshape(N_CHIPS, 2, -1)[:, 0]
    conv_w_f = cols(start_all[:, :ncw].reshape(N_CHIPS, CONV_WIDTH, DCONV // N_CHIPS))
    dt_off = ncw + (-ncw) % (8 * LANES)
    w_dt = jnp.pad(start_all[win.dt_shard, dt_off:dt_off + D * H].reshape(D, H), ((0, 0), (0, LANES - H))).astype(BF16)

    xl, tgt = x[0], loss_target[0]
    tm_row = _tile(L, 256, HALO)
    tm_mm = _tile(L, 1024, 16)
    hn1 = _rms_fwd("rms1_fwd", xl, attn_norm_w, tm_row)
    rest16 = [w_out[0].astype(BF16), w_gate[0].astype(BF16), w_up[0].astype(BF16), w_down[0].astype(BF16)]
    shards_in, landed_in, landed_token = _gather_wait("gather_wait_in", started_in, [hn1, conv_w_f, w_dt] + rest16)
    shards_rest = [(pool_w[0].reshape(PG * PGD // N_CHIPS, PGD) + landed_token[0, 0]).astype(BF16)] + rest16
    started_rest = _gather_start("gather_start_rest", shards_rest)
    pin_row = lambda started, n: jnp.zeros((1, n), F32) + started[4][0, 0]
    add_row = lambda accs, ex, rex: [accs[0] + rex[0]]
    w_main = win.merge(forward_gathered("in", shards_in, landed_in)[0])
    proj, = _mm("proj_main", "nn", [(hn1, w_main)], L, DMAIN, D, tm_mm, 512, D, [F32],
                epilogue=add_row, row_extras=[pin_row(started_rest, DMAIN)])
    dt_raw, = _mm("proj_dt", "nn", [(hn1, w_dt)], L, LANES, D, tm_mm, LANES, D, [F32])

    cwid = _tile(math.gcd(DS, DCONV), 512, LANES)
    tm_conv = _tile(L, 1024, HALO)
    xbc = _conv_fwd(proj, conv_w_f, conv_b, DS, DCONV, tm_conv, cwid)

    dt_g = dt_raw[:, :H].reshape(L, G, R)
    dtc_raw = jnp.transpose(dt_g, (1, 0, 2))
    dtr_raw = jnp.transpose(dt_g, (1, 2, 0))
    as_c = lambda v: v.reshape(G, 1, R)
    as_r = lambda v: v.reshape(G, R, 1)
    ssd_args = (xbc, dtc_raw, dtr_raw, as_c(dt_bias), as_r(dt_bias), as_c(a_log), as_r(a_log), as_c(d_skip))
    y_ssd_raw, states = _ssd_fwd(*ssd_args, DS, N)
    mixed = _gated_fwd(y_ssd_raw, proj, ssd_norm_w, DS, DS + DP, tm_conv)
    gathered = forward_gathered("rest", *_gather_wait("gather_wait_rest", started_rest, mixed)[:2])
    pool_w_f = jnp.moveaxis(gathered[0].reshape(N_CHIPS, PG, PGD // N_CHIPS, PGD), 0, 1).reshape(PG, PGD, PGD)
    w_out_f = gathered[1].reshape(2 * DS, D)
    w_gate_f, w_up_f = cols(gathered[2]), cols(gathered[3])
    w_down_f = gathered[4].reshape(DFF, D)
    w_out_top, w_out_bot = w_out_f[:DS], w_out_f[DS:]
    pooled, mixed = _pool_fwd(proj, pool_w_f, pool_scale, mixed, DS + DCONV, DS, DP, tm_conv)

    add_res = lambda accs, ex, rex: [accs[0] + ex[0]]
    h1, = _mm("out_proj", "nn", [(mixed, w_out_f)], L, D, DS + DP, tm_mm, 512, DS + DP, [F32],
              epilogue=add_res, extras=[xl])
    hn2 = _rms_fwd("rms2_fwd", h1, ffn_norm_w, tm_row)

    def glu(accs, ex, rex):
        return [accs[0], accs[1], (_silu(accs[0]) * accs[1])]

    tn_ff = _tile(DFF, 512, LANES)
    gate, up, act = _mm("ffn_in", "nn", [(hn2, w_gate_f), (hn2, w_up_f)], L, DFF, D, tm_mm, tn_ff, D,
                        [F32, F32, BF16], epilogue=glu, separate=True)
    tk_ff = _tile(DFF, DFF // 2, LANES)
    h2, = _mm("ffn_out", "nn", [(act, w_down_f)], L, D, DFF, tm_mm, 512, tk_ff, [F32], epilogue=add_res, extras=[h1])
    dh2, dh2_16, loss_blk, g_final = _final_loss(h2, final_norm_w.reshape(1, D), tgt, tm_row)

    def dglu(accs, ex, rex):
        gt, u = ex
        sg = _sigmoid(gt)
        return [accs[0] * u * (sg * (1.0 + gt * (1.0 - sg))), accs[0] * (gt * sg)]

    dgate, dup = _mm("ffn_out_dx", "nt", [(dh2_16, w_down_f)], L, DFF, D, tm_mm, tn_ff, D, [BF16, BF16],
                     epilogue=dglu, extras=[gate, up])
    tk_tok = _tile(L, 2048, 16)
    twice = lambda accs, ex, rex: list(accs) + list(accs)
    g_w_down, g_w_down16 = _mm("ffn_out_dw", "tn", [(act, dh2_16)], DFF, D, L, _tile(DFF, 1536, LANES), 1024, tk_tok,
                               [F32, BF16], epilogue=twice)
    g_w_gate, g_w_up, g_w_gate16, g_w_up16 = _mm("ffn_in_dw", "tn", [(hn2, dgate), (hn2, dup)], D, DFF, L, 1024, tn_ff,
                                                 tk_tok, [F32, F32, BF16, BF16], epilogue=twice, separate=True)

    pos = jnp.stack([chip, cc]).astype(jnp.int32)

    def start_reduce(tag, names, full_grads, full_grads16, windows):
        from_sibling = _swap_halves("swap_halves_" + tag, full_grads16, windows)
        partials = [_pair_sum("pair_sum_" + n, g, r, pos, w)
                    for n, g, r, w in zip(names, full_grads, from_sibling, windows)]
        return _scatter_start("scatter_start_" + tag, partials)

    names_ffn = ["w_gate", "w_up", "w_down"]
    started_ffn = start_reduce("ffn", names_ffn, [g_w_gate, g_w_up, g_w_down.reshape(N_CHIPS, -1, D)],
                               [g_w_gate16, g_w_up16, g_w_down16.reshape(N_CHIPS, -1, D)], [None] * 3)
    dhn2, = _mm("ffn_in_dx", "nt", [(dgate, w_gate_f), (dup, w_up_f)], L, D, DFF, tm_mm, 512,
                _tile(DFF, DFF // 4, LANES), [F32], epilogue=add_row, row_extras=[pin_row(started_ffn, D)])
    dh1, g_ffn_norm, dh1_16 = _rms_bwd("rms2_bwd", h1, ffn_norm_w, [dhn2], dh2, tm_row, True)

    dy_ssd, dy_pool = _mm("out_proj_dx", "nt", [(dh1_16, w_out_top), (dh1_16, w_out_bot)], L, DS, D, tm_mm, 512, D,
                          [F32, F32], separate=True)
    g_w_out, g_w_out16 = _mm("out_proj_dw", "tn", [(mixed, dh1_16)], DS + DP, D, L, 1024, 1024, tk_tok, [F32, BF16],
                             epilogue=twice)
    dy_raw, dproj, g_ssd_norm = _gated_bwd(y_ssd_raw, proj, ssd_norm_w, dy_ssd, DS, DMAIN, tm_conv)
    dxs, db, dc, ddt_raw, g_a_log, g_d_skip, g_dt_bias = _ssd_bwd(*ssd_args, dy_raw, states, DS, N)
    g_conv_w, g_conv_b = [], []
    for tag, dact, first in (("xs", dxs, 0), ("b", db, DS), ("c", dc, DS + G * N)):
        dproj, gw, gb = _conv_bwd("conv_bwd_" + tag, proj, dact, conv_w_f, conv_b, dproj, DS, first, tm_conv, cwid)
        g_conv_w.append(gw)
        g_conv_b.append(gb)
    g_conv_w, g_conv_b = jnp.concatenate(g_conv_w, axis=1), jnp.concatenate(g_conv_b, axis=1)
    dproj, g_pool_w, g_pool_scale = _pool_bwd(dy_pool, pooled, pool_w_f, pool_scale, dproj, DS + DCONV, tm_conv)
    ddt_pad = jnp.pad(jnp.transpose(ddt_raw, (1, 0, 2)).reshape(L, H), ((0, 0), (0, LANES - H))).astype(BF16)

    tk_main = _tile(DMAIN, DMAIN // 2, LANES)
    g_w_main, g_w_main16 = _mm("proj_main_dw", "tn", [(hn1, dproj)], D, DMAIN, L, 1024, _tile(DMAIN, 1024, LANES),
                               tk_tok, [F32, BF16], epilogue=twice)
    names_mix = ["w_in", "pool_w", "w_out"]
    pool_shards = jnp.moveaxis(g_pool_w.reshape(PG, N_CHIPS, PGD // N_CHIPS, PGD), 1, 0).reshape(N_CHIPS, -1, PGD)
    started_mix = start_reduce(
        "mix", names_mix, [g_w_main, pool_shards, g_w_out.reshape(N_CHIPS, -1, D)],
        [g_w_main16, pool_shards.astype(BF16), g_w_out16.reshape(N_CHIPS, -1, D)],
        [(win.ws, win.ww), None, None])
    dhn1a, = _mm("proj_main_dx", "nt", [(dproj, w_main)], L, D, DMAIN, tm_mm, 512, tk_main, [F32],
                 epilogue=add_row, row_extras=[pin_row(started_mix, D)])
    dhn1b, = _mm("proj_dt_dx", "nt", [(ddt_pad, w_dt)], L, D, LANES, tm_mm, 512, LANES, [F32])
    g_w_dt, = _mm("proj_dt_dw", "tn", [(hn1, ddt_pad)], D, LANES, L, 512, LANES, tk_tok, [F32])
    grad_x, g_attn_norm = _rms_bwd("rms1_bwd", xl, attn_norm_w, [dhn1a, dhn1b], dh1, tm_row, False)

    def finish_reduce(tag, names, started, after):
        partials, landed, _ = _scatter_wait("scatter_wait_" + tag, started, after)
        landed = [lax.dynamic_update_slice(l, lax.dynamic_index_in_dim(p, chip, 0), (chip, 0, 0))
                  for l, p in zip(landed, partials)]
        return [_chip_sum("chip_sum_" + n, l, pos) for n, l in zip(names, landed)]

    halves = finish_reduce("ffn", names_ffn, started_ffn, grad_x) + finish_reduce("mix", names_mix, started_mix, grad_x)
    red = dict(zip(names_ffn + names_mix, _join_halves(halves)))

    small_w = [attn_norm_w, conv_b, dt_bias, a_log, d_skip, ssd_norm_w, pool_scale, ffn_norm_w, final_norm_w]
    small_m = [m_attn_norm_w, m_conv_b, m_dt_bias, m_a_log, m_d_skip, m_ssd_norm_w, m_pool_scale, m_ffn_norm_w, m_final_norm_w]
    small_v = [v_attn_norm_w, v_conv_b, v_dt_bias, v_a_log, v_d_skip, v_ssd_norm_w, v_pool_scale, v_ffn_norm_w, v_final_norm_w]
    small_g = [g_attn_norm, g_conv_b, g_dt_bias.reshape(1, H), g_a_log.reshape(1, H), g_d_skip.reshape(1, H),
               g_ssd_norm, g_pool_scale, g_ffn_norm, g_final.reshape(D)]
    extra_shapes = [(CONV_WIDTH, DCONV), (D, H), (1, LANES)]
    zeros_like_extra = [jnp.zeros(s, F32) for s in extra_shapes]
    g_blk = _pack_rows(small_g + [g_conv_w, g_w_dt[:, :H], loss_blk], LANES)
    rows = g_blk.shape[0]
    small_all = _all_gather_small("gather_small_grads", g_blk)
    s_g, s_d, s_m, s_v = _small_sum_adam(small_all, _pack_rows(small_w + zeros_like_extra, LANES),
                                         _pack_rows(small_m + zeros_like_extra, LANES),
                                         _pack_rows(small_v + zeros_like_extra, LANES), rows)
    shapes = [w.shape for w in small_w] + extra_shapes
    sg_list = _unpack_rows(s_g, shapes, LANES)
    sd_list = _unpack_rows(s_d, shapes, LANES)[:len(small_w)]
    sm_list = _unpack_rows(s_m, shapes, LANES)[:len(small_w)]
    sv_list = _unpack_rows(s_v, shapes, LANES)[:len(small_w)]
    loss = sg_list[-1][0, 0]
    grad_conv_w = lax.dynamic_slice(sg_list[-3], (0, chip * (DCONV // N_CHIPS)), (CONV_WIDTH, DCONV // N_CHIPS))
    grad_w_in = lax.switch(chip, [functools.partial(win.from_window, k) for k in range(N_CHIPS)], red["w_in"], sg_list[-2])

    def adam_nd(name, w, g, m, v):
        shp = w.shape
        to2 = lambda a: a.reshape(-1, shp[-1])
        return tuple(o.reshape(shp) for o in _adam(name, to2(w), to2(g), to2(m), to2(v)))

    sharded = {
        "w_in": (w_in, grad_w_in[None], m_w_in, v_w_in),
        "conv_w": (conv_w, grad_conv_w[None], m_conv_w, v_conv_w),
        "pool_w": (pool_w, red["pool_w"].reshape(pool_w.shape), m_pool_w, v_pool_w),
        "w_out": (w_out, red["w_out"][None], m_w_out, v_w_out),
        "w_gate": (w_gate, red["w_gate"][None], m_w_gate, v_w_gate),
        "w_up": (w_up, red["w_up"][None], m_w_up, v_w_up),
        "w_down": (w_down, red["w_down"][None], m_w_down, v_w_down),
    }
    upd = {n: adam_nd("adam_" + n, *a) for n, a in sharded.items()}
    small_names = ["attn_norm_w", "conv_b", "dt_bias", "a_log", "d_skip", "ssd_norm_w", "pool_scale", "ffn_norm_w",
                   "final_norm_w"]
    for i, n in enumerate(small_names):
        upd[n] = (sg_list[i], sd_list[i], sm_list[i], sv_list[i])

    order = ["attn_norm_w", "w_in", "conv_w", "conv_b", "dt_bias", "a_log", "d_skip", "ssd_norm_w", "pool_w",
             "pool_scale", "w_out", "ffn_norm_w", "w_gate", "w_up", "w_down", "final_norm_w"]
    outs = [loss, grad_x[None]]
    for j in range(4):
        outs += [upd[n][j] for n in order]
    return tuple(outs)
```

```python
import functools
import math

import jax
import jax.numpy as jnp
from jax import lax
from jax.experimental import pallas as pl
from jax.experimental.pallas import tpu as pltpu

F32 = jnp.float32
BF16 = jnp.bfloat16

NORM_EPS = 1e-5
HEAD_DIM = 64
SSD_GROUPS = 4
CONV_WIDTH = 4
CHUNK = 256
POOL_WINDOWS = (2, 4, 8, 16)
ADAM_LR = 0.001
ADAM_B1 = 0.9
ADAM_B2 = 0.999
ADAM_EPS = 1e-08
ADAM_WD = 0.01
ADAM_STEP = 10

N_CHIPS = 4
N_DEV = 8
LANES = 128
HALO = 16
VMEM_LIMIT = 52 * 1024 * 1024
MESH = pl.DeviceIdType.MESH

NN = (((1,), (0,)), ((), ()))
NT = (((1,), (1,)), ((), ()))
TN = (((0,), (0,)), ((), ()))


def _tile(n, cap, mult):
    best = None
    for t in range(mult, min(n, cap) + 1, mult):
        if n % t == 0:
            best = t
    return best if best is not None else n


def _params(sem):
    return pltpu.CompilerParams(dimension_semantics=sem, vmem_limit_bytes=VMEM_LIMIT)


def _dot(a, b, dims):
    return lax.dot_general(a, b, dims, preferred_element_type=F32)


def _sigmoid(x):
    return 1.0 / (1.0 + jnp.exp(-x))


def _silu(x):
    return x * _sigmoid(x)


def _softplus(x):
    return jnp.maximum(x, 0.0) + jnp.log(1.0 + jnp.exp(-jnp.abs(x)))


def _mm(name, mode, pairs, M, N, K, tm, tn, tk, out_dtypes, epilogue=None, extras=(), row_extras=(),
        separate=False):
    tm, tn, tk = min(tm, M), min(tn, N), min(tk, K)
    assert M % tm == 0 and N % tn == 0 and K % tk == 0, (name, M, N, K, tm, tn, tk)
    nk = K // tk
    npairs = len(pairs)
    nacc = npairs if separate else 1
    if mode == "nn":
        a_spec = pl.BlockSpec((tm, tk), lambda i, j, k: (i, k))
        b_spec = pl.BlockSpec((tk, tn), lambda i, j, k: (k, j))
        dims = NN
    elif mode == "nt":
        a_spec = pl.BlockSpec((tm, tk), lambda i, j, k: (i, k))
        b_spec = pl.BlockSpec((tn, tk), lambda i, j, k: (j, k))
        dims = NT
    else:
        a_spec = pl.BlockSpec((tk, tm), lambda i, j, k: (k, i))
        b_spec = pl.BlockSpec((tk, tn), lambda i, j, k: (k, j))
        dims = TN
    o_spec = pl.BlockSpec((tm, tn), lambda i, j, k: (i, j))
    r_spec = pl.BlockSpec((1, tn), lambda i, j, k: (0, j))
    if epilogue is None:
        epilogue = lambda accs, ex, rex: accs
    n_ex, n_rex, n_out = len(extras), len(row_extras), len(out_dtypes)

    def body(*refs):
        ab = refs[:2 * npairs]
        ex = refs[2 * npairs:2 * npairs + n_ex]
        rex = refs[2 * npairs + n_ex:2 * npairs + n_ex + n_rex]
        outs = refs[2 * npairs + n_ex + n_rex:2 * npairs + n_ex + n_rex + n_out]
        accs = refs[2 * npairs + n_ex + n_rex + n_out:]

        def products():
            res = [None] * nacc
            for p in range(npairs):
                d = _dot(ab[2 * p][...], ab[2 * p + 1][...], dims)
                q = p if separate else 0
                res[q] = d if res[q] is None else res[q] + d
            return res

        def finish(vals):
            res = epilogue(vals, [e[...] for e in ex], [r[...] for r in rex])
            for o, v in zip(outs, res):
                o[...] = v.astype(o.dtype)

        if nk == 1:
            finish(products())
        else:
            k = pl.program_id(2)

            @pl.when(k == 0)
            def _():
                for q in range(nacc):
                    accs[q][...] = jnp.zeros_like(accs[q])

            for p in range(npairs):
                accs[p if separate else 0][...] += _dot(ab[2 * p][...], ab[2 * p + 1][...], dims)

            @pl.when(k == nk - 1)
            def _():
                finish([a[...] for a in accs])

    in_specs = [a_spec, b_spec] * npairs + [o_spec] * n_ex + [r_spec] * n_rex
    args = [t for p in pairs for t in p] + list(extras) + list(row_extras)
    outs = pl.pallas_call(
        body,
        grid=(M // tm, N // tn, nk),
        in_specs=in_specs,
        out_specs=[o_spec] * n_out,
        out_shape=[jax.ShapeDtypeStruct((M, N), d) for d in out_dtypes],
        scratch_shapes=[pltpu.VMEM((tm, tn), F32) for _ in range(nacc if nk > 1 else 0)],
        compiler_params=_params(("parallel", "parallel", "arbitrary")),
        name=name,
    )(*args)
    return outs


def _rms(xf, w):
    y = xf * lax.rsqrt(jnp.mean(xf * xf, axis=-1, keepdims=True) + NORM_EPS)
    return y * w


def _rms_fwd(name, x, w, tm):
    L, D = x.shape

    def body(x_ref, w_ref, o_ref):
        o_ref[...] = _rms(x_ref[...], w_ref[...]).astype(BF16)

    return pl.pallas_call(
        body, grid=(L // tm,),
        in_specs=[pl.BlockSpec((tm, D), lambda i: (i, 0)), pl.BlockSpec((1, D), lambda i: (0, 0))],
        out_specs=pl.BlockSpec((tm, D), lambda i: (i, 0)),
        out_shape=jax.ShapeDtypeStruct((L, D), BF16),
        compiler_params=_params(("parallel",)), name=name)(x, w)


def _rms_bwd(name, x, w, dparts, dres, tm, with_bf16):
    L, D = x.shape
    nparts = len(dparts)

    def body(*refs):
        x_ref, w_ref = refs[:2]
        p_refs = refs[2:2 + nparts]
        r_ref = refs[2 + nparts]
        outs = refs[3 + nparts:]
        dhn = p_refs[0][...]
        for p in p_refs[1:]:
            dhn = dhn + p[...]
        _, vjp = jax.vjp(_rms, x_ref[...], w_ref[...])
        dx, dw = vjp(dhn)
        dx = dx + r_ref[...]
        outs[0][...] = dx
        gw_ref = outs[1]

        @pl.when(pl.program_id(0) == 0)
        def _():
            gw_ref[...] = jnp.zeros_like(gw_ref)

        gw_ref[...] += dw
        if with_bf16:
            outs[2][...] = dx.astype(BF16)

    row = pl.BlockSpec((tm, D), lambda i: (i, 0))
    vec = pl.BlockSpec((1, D), lambda i: (0, 0))
    out_shape = [jax.ShapeDtypeStruct((L, D), F32), jax.ShapeDtypeStruct((1, D), F32)]
    out_specs = [row, vec]
    if with_bf16:
        out_shape.append(jax.ShapeDtypeStruct((L, D), BF16))
        out_specs.append(row)
    return pl.pallas_call(
        body, grid=(L // tm,),
        in_specs=[row, vec] + [row] * nparts + [row],
        out_specs=out_specs, out_shape=out_shape,
        compiler_params=_params(("arbitrary",)), name=name)(x, w, *dparts, dres)


def _final_loss(h2, wf, target, tm):
    L, D = h2.shape

    def body(h_ref, w_ref, t_ref, dh_ref, dhb_ref, loss_ref, gw_ref):
        t = t_ref[...]

        def f(h, w):
            err = jnp.square(_rms(h, w) - t)
            return 0.5 * jnp.sum(jnp.mean(err, axis=-1))

        val, vjp = jax.vjp(f, h_ref[...], w_ref[...])
        dh, dw = vjp(jnp.ones((), F32))
        dh_ref[...] = dh
        dhb_ref[...] = dh.astype(BF16)

        @pl.when(pl.program_id(0) == 0)
        def _():
            gw_ref[...] = jnp.zeros_like(gw_ref)
            loss_ref[...] = jnp.zeros_like(loss_ref)

        gw_ref[...] += dw
        loss_ref[...] += jnp.full(loss_ref.shape, val, F32)

    row = pl.BlockSpec((tm, D), lambda i: (i, 0))
    vec = pl.BlockSpec((1, D), lambda i: (0, 0))
    lspec = pl.BlockSpec((1, LANES), lambda i: (0, 0))
    return pl.pallas_call(
        body, grid=(L // tm,),
        in_specs=[row, vec, row],
        out_specs=[row, row, lspec, vec],
        out_shape=[jax.ShapeDtypeStruct((L, D), F32), jax.ShapeDtypeStruct((L, D), BF16),
                   jax.ShapeDtypeStruct((1, LANES), F32), jax.ShapeDtypeStruct((1, D), F32)],
        compiler_params=_params(("arbitrary",)), name="final_loss")(h2, wf, target)


def _gated(y, z, w):
    g = y * _silu(z)
    g = g * lax.rsqrt(jnp.mean(g * g, axis=-1, keepdims=True) + NORM_EPS)
    return g * w


def _gated_fwd(y, proj, w, DS, mixed_cols, tm):
    L = y.shape[0]
    GW = DS // SSD_GROUPS

    def body(y_ref, z_ref, w_ref, o_ref):
        o_ref[...] = _gated(y_ref[...], z_ref[...], w_ref[...]).astype(BF16)

    blk = pl.BlockSpec((tm, GW), lambda i, g: (i, g))
    return pl.pallas_call(
        body, grid=(L // tm, SSD_GROUPS),
        in_specs=[blk, blk, pl.BlockSpec((1, GW), lambda i, g: (0, g))],
        out_specs=blk, out_shape=jax.ShapeDtypeStruct((L, mixed_cols), BF16),
        compiler_params=_params(("parallel", "parallel")), name="gated_fwd")(y, proj, w)


def _gated_bwd(y, proj, w, dout, DS, dproj_cols, tm):
    L = y.shape[0]
    GW = DS // SSD_GROUPS

    def body(y_ref, z_ref, w_ref, d_ref, dy_ref, dz_ref, gw_ref):
        _, vjp = jax.vjp(_gated, y_ref[...], z_ref[...], w_ref[...])
        dy, dz, dw = vjp(d_ref[...])
        dy_ref[...] = dy
        dz_ref[...] = dz.astype(BF16)

        @pl.when(pl.program_id(1) == 0)
        def _():
            gw_ref[...] = jnp.zeros_like(gw_ref)

        gw_ref[...] += dw

    blk = pl.BlockSpec((tm, GW), lambda g, i: (i, g))
    vec = pl.BlockSpec((1, GW), lambda g, i: (0, g))
    return pl.pallas_call(
        body, grid=(SSD_GROUPS, L // tm),
        in_specs=[blk, blk, vec, blk],
        out_specs=[blk, blk, vec],
        out_shape=[jax.ShapeDtypeStruct((L, DS), F32), jax.ShapeDtypeStruct((L, dproj_cols), BF16),
                   jax.ShapeDtypeStruct((1, DS), F32)],
        compiler_params=_params(("parallel", "arbitrary")), name="gated_bwd")(y, proj, w, dout)


def _halo_prev(tm, cw, col0):
    return pl.BlockSpec((HALO, cw), lambda i, j: (jnp.maximum(i * (tm // HALO) - 1, 0), col0 + j))


def _conv_fwd(proj, conv_w, conv_b, DS, DCONV, tm, cw):
    L = proj.shape[0]
    col0 = DS // cw
    K = CONV_WIDTH

    def body(x_ref, p_ref, w_ref, b_ref, o_ref, ext):
        i = pl.program_id(0)
        ext[0:HALO, :] = jnp.where(i == 0, 0.0, p_ref[...])
        ext[HALO:, :] = x_ref[...]
        acc = jnp.broadcast_to(b_ref[...], (tm, cw))
        for k in range(K):
            acc = acc + w_ref[k:k + 1, :] * ext[pl.ds(HALO - (K - 1) + k, tm), :]
        o_ref[...] = _silu(acc)

    return pl.pallas_call(
        body, grid=(L // tm, DCONV // cw),
        in_specs=[pl.BlockSpec((tm, cw), lambda i, j: (i, col0 + j)), _halo_prev(tm, cw, col0),
                  pl.BlockSpec((K, cw), lambda i, j: (0, j)), pl.BlockSpec((1, cw), lambda i, j: (0, j))],
        out_specs=pl.BlockSpec((tm, cw), lambda i, j: (i, j)),
        out_shape=jax.ShapeDtypeStruct((L, DCONV), F32),
        scratch_shapes=[pltpu.VMEM((tm + HALO, cw), F32)],
        compiler_params=_params(("parallel", "parallel")), name="conv_fwd")(proj, proj, conv_w, conv_b)


def _conv_bwd(name, proj, dact, conv_w, conv_b, dproj, DS, first, tm, cw):
    L = proj.shape[0]
    ncols = dact.shape[1]
    col0 = (DS + first) // cw
    wcol0 = first // cw
    K = CONV_WIDTH
    nrt = L // tm

    def body(x_ref, p_ref, n_ref, d_ref, dn_ref, w_ref, b_ref, alias_ref, dx_ref, dw_ref, db_ref, ext, dext):
        i = pl.program_id(1)
        last = i == nrt - 1
        ext[0:HALO, :] = jnp.where(i == 0, 0.0, p_ref[...])
        ext[HALO:HALO + tm, :] = x_ref[...]
        ext[HALO + tm:, :] = n_ref[...]
        dfull = jnp.concatenate([d_ref[...], jnp.where(last, 0.0, dn_ref[...])], axis=0)
        acc = jnp.broadcast_to(b_ref[...], (tm + HALO, cw))
        for k in range(K):
            acc = acc + w_ref[k:k + 1, :] * ext[pl.ds(HALO - (K - 1) + k, tm + HALO), :]
        sg = _sigmoid(acc)
        dconv = dfull * (sg * (1.0 + acc * (1.0 - sg)))
        dext[...] = dconv

        @pl.when(i == 0)
        def _():
            dw_ref[...] = jnp.zeros_like(dw_ref)
            db_ref[...] = jnp.zeros_like(db_ref)

        dx = jnp.zeros((tm, cw), F32)
        for k in range(K):
            dx = dx + w_ref[k:k + 1, :] * dext[pl.ds(K - 1 - k, tm), :]
        dx_ref[...] = dx.astype(BF16)
        dtile = dext[pl.ds(0, tm), :]
        db_ref[...] += jnp.sum(dtile, axis=0, keepdims=True)
        for k in range(K):
            dw_ref[k:k + 1, :] += jnp.sum(dtile * ext[pl.ds(HALO - (K - 1) + k, tm), :], axis=0, keepdims=True)

    prev = pl.BlockSpec((HALO, cw), lambda j, i: (jnp.maximum(i * (tm // HALO) - 1, 0), col0 + j))
    nxt = pl.BlockSpec((HALO, cw), lambda j, i: (jnp.minimum((i + 1) * (tm // HALO), L // HALO - 1), col0 + j))
    dnxt = pl.BlockSpec((HALO, cw), lambda j, i: (jnp.minimum((i + 1) * (tm // HALO), L // HALO - 1), j))
    return pl.pallas_call(
        body, grid=(ncols // cw, nrt),
        in_specs=[pl.BlockSpec((tm, cw), lambda j, i: (i, col0 + j)), prev, nxt,
                  pl.BlockSpec((tm, cw), lambda j, i: (i, j)), dnxt,
                  pl.BlockSpec((K, cw), lambda j, i: (0, wcol0 + j)), pl.BlockSpec((1, cw), lambda j, i: (0, wcol0 + j)),
                  _ANY],
        out_specs=[pl.BlockSpec((tm, cw), lambda j, i: (i, col0 + j)),
                   pl.BlockSpec((K, cw), lambda j, i: (0, j)), pl.BlockSpec((1, cw), lambda j, i: (0, j))],
        out_shape=[jax.ShapeDtypeStruct(dproj.shape, BF16), jax.ShapeDtypeStruct((K, ncols), F32),
                   jax.ShapeDtypeStruct((1, ncols), F32)],
        input_output_aliases={7: 0},
        scratch_shapes=[pltpu.VMEM((tm + 2 * HALO, cw), F32), pltpu.VMEM((tm + HALO, cw), F32)],
        compiler_params=_params(("parallel", "arbitrary")), name=name,
    )(proj, proj, proj, dact, dact, conv_w, conv_b, dproj)


def _pool_fwd(proj, pool_w, pool_scale, mixed, ucol, ycol, DP, tm):
    L = proj.shape[0]
    PG = len(POOL_WINDOWS)
    PGD = DP // PG
    col0 = ucol // PGD
    ycol0 = ycol // PGD

    def body(u_ref, p_ref, w_ref, s_ref, alias_ref, pooled_ref, y_ref, ext):
        i, g = pl.program_id(0), pl.program_id(1)
        ext[0:HALO, :] = jnp.where(i == 0, 0.0, p_ref[...])
        ext[HALO:, :] = u_ref[...]
        t = i * tm + lax.broadcasted_iota(jnp.int32, (tm, 1), 0)
        for gi, win in enumerate(POOL_WINDOWS):
            @pl.when(g == gi)
            def _():
                acc = ext[pl.ds(HALO, tm), :]
                for j in range(1, win):
                    acc = acc + ext[pl.ds(HALO - j, tm), :]
                count = jnp.minimum(t + 1, win).astype(F32)
                pooled = (acc / count - u_ref[...]).astype(BF16)
                pooled_ref[...] = pooled
                y_ref[...] = (_dot(pooled, w_ref[...], NN) * s_ref[...]).astype(BF16)

    blk = pl.BlockSpec((tm, PGD), lambda i, g: (i, g))
    return pl.pallas_call(
        body, grid=(L // tm, PG),
        in_specs=[pl.BlockSpec((tm, PGD), lambda i, g: (i, col0 + g)), _halo_prev(tm, PGD, col0),
                  pl.BlockSpec((None, PGD, PGD), lambda i, g: (g, 0, 0)), pl.BlockSpec((1, PGD), lambda i, g: (0, g)),
                  _ANY],
        out_specs=[blk, pl.BlockSpec((tm, PGD), lambda i, g: (i, ycol0 + g))],
        out_shape=[jax.ShapeDtypeStruct((L, DP), BF16), jax.ShapeDtypeStruct(mixed.shape, BF16)],
        input_output_aliases={4: 1},
        scratch_shapes=[pltpu.VMEM((tm + HALO, PGD), F32)],
        compiler_params=_params(("parallel", "parallel")), name="pool_fwd")(proj, proj, pool_w, pool_scale, mixed)


def _pool_bwd(dy, pooled, pool_w, pool_scale, dproj, ucol, tm):
    L, DP = dy.shape
    PG = len(POOL_WINDOWS)
    PGD = DP // PG
    nrt = L // tm
    col0 = ucol // PGD

    def body(d_ref, dn_ref, p_ref, w_ref, s_ref, alias_ref, du_ref, dw_ref, ds_ref, qext):
        g, i = pl.program_id(0), pl.program_id(1)
        last = i == nrt - 1
        dfull = jnp.concatenate([d_ref[...], jnp.where(last, 0.0, dn_ref[...])], axis=0)
        dyp = (dfull * s_ref[...]).astype(BF16)
        dpooled = _dot(dyp, w_ref[...], NT)
        t = i * tm + lax.broadcasted_iota(jnp.int32, (tm + HALO, 1), 0)
        for gi, win in enumerate(POOL_WINDOWS):
            @pl.when(g == gi)
            def _():
                qext[...] = dpooled / jnp.minimum(t + 1, win).astype(F32)
                acc = qext[pl.ds(0, tm), :]
                for j in range(1, win):
                    acc = acc + qext[pl.ds(j, tm), :]
                du_ref[...] = (acc - dpooled[0:tm, :]).astype(BF16)

        @pl.when(i == 0)
        def _():
            dw_ref[...] = jnp.zeros_like(dw_ref)
            ds_ref[...] = jnp.zeros_like(ds_ref)

        pooled_t = p_ref[...]
        dw_ref[...] += _dot(pooled_t, dyp[0:tm, :], TN)
        ypre = _dot(pooled_t, w_ref[...], NN)
        ds_ref[...] += jnp.sum(d_ref[...] * ypre, axis=0, keepdims=True)

    blk = pl.BlockSpec((tm, PGD), lambda g, i: (i, g))
    nxt = pl.BlockSpec((HALO, PGD), lambda g, i: (jnp.minimum((i + 1) * (tm // HALO), L // HALO - 1), g))
    wspec = pl.BlockSpec((None, PGD, PGD), lambda g, i: (g, 0, 0))
    vec = pl.BlockSpec((1, PGD), lambda g, i: (0, g))
    return pl.pallas_call(
        body, grid=(PG, nrt),
        in_specs=[blk, nxt, blk, wspec, vec, _ANY],
        out_specs=[pl.BlockSpec((tm, PGD), lambda g, i: (i, col0 + g)), wspec, vec],
        out_shape=[jax.ShapeDtypeStruct(dproj.shape, BF16), jax.ShapeDtypeStruct((PG, PGD, PGD), F32),
                   jax.ShapeDtypeStruct((1, DP), F32)],
        input_output_aliases={5: 0},
        scratch_shapes=[pltpu.VMEM((tm + HALO, PGD), F32)],
        compiler_params=_params(("parallel", "arbitrary")), name="pool_bwd",
    )(dy, dy, pooled, pool_w, pool_scale, dproj)


def _ssd_common(dtc_raw, dtr_raw, bc, br, ac, ar):
    ch = CHUNK
    row = lax.broadcasted_iota(jnp.int32, (ch, ch), 0)
    col = lax.broadcasted_iota(jnp.int32, (ch, ch), 1)
    lower = row >= col
    dtc = _softplus(dtc_raw + bc)
    dtr = _softplus(dtr_raw + br)
    a_c = -jnp.exp(ac)
    a_r = -jnp.exp(ar)
    hi = lax.Precision.HIGHEST
    acol = jnp.dot(lower.astype(F32), dtc * a_c, preferred_element_type=F32, precision=hi)
    arow = jnp.dot(dtr * a_r, (row <= col).astype(F32), preferred_element_type=F32, precision=hi)
    return lower, row <= col, dtc, a_c, acol, arow, dtr


def _spread_heads(cols, passes=3):
    R = cols.shape[1]
    shape = (R, R * LANES)
    spread = (lax.broadcasted_iota(jnp.int32, shape, 0) == lax.broadcasted_iota(jnp.int32, shape, 1) // LANES).astype(BF16)
    out, rest = None, cols
    for _ in range(passes):
        part = rest.astype(BF16)
        rest = rest - part.astype(F32)
        term = _dot(part, spread, NN)
        out = term if out is None else out + term
    return out


def _ssd_fwd(xbc, dtc_raw, dtr_raw, bias_c, bias_r, alog_c, alog_r, dskip_c, DS, N):
    L = xbc.shape[0]
    G, P, ch = SSD_GROUPS, HEAD_DIM, CHUNK
    R = dtc_raw.shape[2]
    GW = R * P
    nc = L // ch

    def body(xs_ref, b_ref, c_ref, dtc_ref, dtr_ref, bc_ref, br_ref, ac_ref, ar_ref, dk_ref,
             y_ref, st_ref, h_ref):
        @pl.when(pl.program_id(1) == 0)
        def _():
            h_ref[...] = jnp.zeros_like(h_ref)

        lower, _, _, _, acol_all, arow_all, dtr = _ssd_common(
            dtc_ref[...], dtr_ref[...], bc_ref[...], br_ref[...], ac_ref[...], ar_ref[...])
        bm = b_ref[...]
        cb16 = c_ref[...].astype(BF16)
        b16 = bm.astype(BF16)
        bt = bm.T
        cb = _dot(cb16, b16, NT)
        dk = dk_ref[...]
        st_ref[...] = h_ref[...]
        for r in range(R):
            acol = acol_all[:, r:r + 1]
            arow = arow_all[r:r + 1, :]
            alast = acol_all[ch - 1:ch, r:r + 1]
            dt_row = dtr[r:r + 1, :]
            decay = jnp.exp(jnp.where(lower, acol - arow, -1e30))
            x_h = xs_ref[:, pl.ds(r * P, P)]
            x16 = x_h.astype(BF16)
            m16 = (cb * decay * dt_row).astype(BF16)
            h_prev = h_ref[r]
            y = _dot(m16, x16, NN)
            y = y + jnp.exp(acol) * _dot(cb16, h_prev.astype(BF16), NN)
            y = y + dk[:, r:r + 1] * x_h
            y_ref[:, pl.ds(r * P, P)] = y
            to_end_dt = jnp.exp(alast - arow) * dt_row
            h_ref[r] = jnp.exp(alast) * h_prev + _dot((bt * to_end_dt).astype(BF16), x16, NN)

    nb = DS // N
    return pl.pallas_call(
        body, grid=(G, nc),
        in_specs=[pl.BlockSpec((ch, GW), lambda g, c: (c, g)),
                  pl.BlockSpec((ch, N), lambda g, c: (c, nb + g)),
                  pl.BlockSpec((ch, N), lambda g, c: (c, nb + G + g)),
                  pl.BlockSpec((None, ch, R), lambda g, c: (g, c, 0)),
                  pl.BlockSpec((None, R, ch), lambda g, c: (g, 0, c)),
                  pl.BlockSpec((None, 1, R), lambda g, c: (g, 0, 0)),
                  pl.BlockSpec((None, R, 1), lambda g, c: (g, 0, 0)),
                  pl.BlockSpec((None, 1, R), lambda g, c: (g, 0, 0)),
                  pl.BlockSpec((None, R, 1), lambda g, c: (g, 0, 0)),
                  pl.BlockSpec((None, 1, R), lambda g, c: (g, 0, 0))],
        out_specs=[pl.BlockSpec((ch, GW), lambda g, c: (c, g)),
                   pl.BlockSpec((None, R, N, P), lambda g, c: (c, g, 0, 0))],
        out_shape=[jax.ShapeDtypeStruct((L, DS), F32), jax.ShapeDtypeStruct((nc, G * R, N, P), F32)],
        scratch_shapes=[pltpu.VMEM((R, N, P), F32)],
        compiler_params=_params(("parallel", "arbitrary")), name="ssd_fwd",
    )(xbc, xbc, xbc, dtc_raw, dtr_raw, bias_c, bias_r, alog_c, alog_r, dskip_c)


def _ssd_bwd(xbc, dtc_raw, dtr_raw, bias_c, bias_r, alog_c, alog_r, dskip_c, dy, states, DS, N):
    L = xbc.shape[0]
    G, P, ch = SSD_GROUPS, HEAD_DIM, CHUNK
    R = dtc_raw.shape[2]
    GW = R * P
    nc = L // ch
    assert N == LANES and P <= LANES and ch % LANES == 0

    def body(xs_ref, b_ref, c_ref, dtc_ref, dtr_ref, bc_ref, br_ref, ac_ref, ar_ref, dk_ref,
             dy_ref, stp_ref,
             dxs_ref, db_ref, dc_ref, ddt_ref, dal_ref, ddk_ref, dbias_ref, dh_ref):
        @pl.when(pl.program_id(1) == 0)
        def _():
            dh_ref[...] = jnp.zeros_like(dh_ref)
            dal_ref[...] = jnp.zeros_like(dal_ref)
            ddk_ref[...] = jnp.zeros_like(ddk_ref)
            dbias_ref[...] = jnp.zeros_like(dbias_ref)

        lower, upper, dtc, a_c, acol_all, arow_all, _ = _ssd_common(
            dtc_ref[...], dtr_ref[...], bc_ref[...], br_ref[...], ac_ref[...], ar_ref[...])
        bm = b_ref[...]
        cm = c_ref[...]
        b16 = bm.astype(BF16)
        c16 = cm.astype(BF16)
        ct16 = cm.T.astype(BF16)
        cb = _dot(c16, b16, NT)
        cbt = _dot(b16, c16, NT)
        dk = dk_ref[...]
        lane_r = lax.broadcasted_iota(jnp.int32, (ch, R), 1)
        lane_1 = lax.broadcasted_iota(jnp.int32, (1, R), 1)
        dc = jnp.zeros((ch, N), F32)
        db = jnp.zeros((ch, N), F32)
        da_all = jnp.zeros((R, ch), F32)
        q_all = jnp.zeros((R, ch), F32)
        sxd_all = jnp.zeros((ch, R), F32)
        const = jnp.zeros((1, R), F32)
        ddk = jnp.zeros((1, R), F32)
        sub_r = lax.broadcasted_iota(jnp.int32, (R, ch), 0)
        ct = cm.T
        bt = bm.T
        dcb = jnp.zeros((ch, ch), F32)
        acol_lanes = _spread_heads(acol_all)
        dt_lanes = _spread_heads(dtc, passes=2)
        for r in range(R):
            a128 = acol_lanes[:, r * LANES:(r + 1) * LANES]
            arow = arow_all[r:r + 1, :]
            alast = acol_all[ch - 1:ch, r:r + 1]
            seg = jnp.tile(a128, (1, ch // LANES)) - arow
            decay = jnp.exp(jnp.where(lower, seg, -1e30))
            decay_t = jnp.exp(jnp.where(upper, -seg, -1e30))
            x_h = xs_ref[:, pl.ds(r * P, P)]
            dy_h = dy_ref[:, pl.ds(r * P, P)]
            dt_h = dt_lanes[:, r * LANES:r * LANES + P]
            dk_h = dk[:, r:r + 1]
            xdt = x_h * dt_h
            xdt16 = xdt.astype(BF16)
            dy16 = dy_h.astype(BF16)
            h_prev = stp_ref[r]
            h16 = h_prev.astype(BF16)
            dh_next = dh_ref[r]
            dhn16 = dh_next.astype(BF16)
            to_end_n = jnp.exp(alast - a128)
            e_a_n = jnp.exp(a128)
            to_end, e_a = to_end_n[:, :P], e_a_n[:, :P]
            mt = cbt * decay_t
            pm = _dot(dy16, xdt16, NT) * decay
            wt = _dot(xdt16, dy16, NT) * mt
            dxdt = _dot(mt.astype(BF16), dy16, NN) + to_end * _dot(b16, dhn16, NN)
            dcb = dcb + pm
            dc = dc + e_a_n * _dot(dy16, h16, NT)
            db = db + to_end_n * _dot(xdt16, dhn16, NT)
            dh_ref[r] = jnp.exp(alast) * dh_next + _dot(ct16, (dy_h * e_a).astype(BF16), NN)
            da = (jnp.sum(wt, axis=0, keepdims=True) - jnp.sum(pm * cb, axis=0, keepdims=True)
                  + jnp.exp(arow) * jnp.sum(ct * _dot(h16, dy16, NT), axis=0, keepdims=True))
            q = jnp.exp(alast - arow) * jnp.sum(bt * _dot(dhn16, xdt16, NT), axis=0, keepdims=True)
            da_all = da_all + jnp.where(sub_r == r, da, 0.0)
            q_all = q_all + jnp.where(sub_r == r, q, 0.0)
            sxd_all = sxd_all + jnp.where(lane_r == r, jnp.sum(dxdt * x_h, axis=1, keepdims=True), 0.0)
            const = const + jnp.where(lane_1 == r, jnp.exp(alast) * jnp.sum(dh_next * h_prev), 0.0)
            ddk = ddk + jnp.where(lane_1 == r, jnp.sum(dy_h * x_h), 0.0)
            dxs_ref[:, pl.ds(r * P, P)] = dxdt * dt_h + dk_h * dy_h
        dcb16 = dcb.astype(BF16)
        dc_ref[...] = dc + _dot(dcb16, b16, NN)
        db_ref[...] = db + _dot(dcb16, c16, TN)
        hi = lax.Precision.HIGHEST
        strict_lower = jnp.logical_and(lower, jnp.logical_not(upper))
        dda = (lax.dot_general(upper.astype(F32), da_all, NT, preferred_element_type=F32, precision=hi)
               + lax.dot_general(strict_lower.astype(F32), q_all, NT, preferred_element_type=F32, precision=hi)
               + const)
        ddt = dda * a_c + sxd_all
        dal_ref[...] += jnp.sum(dda * dtc, axis=0, keepdims=True) * a_c
        ddk_ref[...] += ddk
        ddt_raw = ddt * _sigmoid(dtc_ref[...] + bc_ref[...])
        ddt_ref[...] = ddt_raw
        dbias_ref[...] += jnp.sum(ddt_raw, axis=0, keepdims=True)

    nb = DS // N
    rc = lambda c: nc - 1 - c
    vec_c = pl.BlockSpec((None, 1, R), lambda g, c: (g, 0, 0))
    vec_r = pl.BlockSpec((None, R, 1), lambda g, c: (g, 0, 0))
    big = pl.BlockSpec((ch, GW), lambda g, c: (rc(c), g))
    return pl.pallas_call(
        body, grid=(G, nc),
        in_specs=[big,
                  pl.BlockSpec((ch, N), lambda g, c: (rc(c), nb + g)),
                  pl.BlockSpec((ch, N), lambda g, c: (rc(c), nb + G + g)),
                  pl.BlockSpec((None, ch, R), lambda g, c: (g, rc(c), 0)),
                  pl.BlockSpec((None, R, ch), lambda g, c: (g, 0, rc(c))),
                  vec_c, vec_r, vec_c, vec_r, vec_c,
                  big,
                  pl.BlockSpec((None, R, N, P), lambda g, c: (rc(c), g, 0, 0))],
        out_specs=[big,
                   pl.BlockSpec((ch, N), lambda g, c: (rc(c), g)),
                   pl.BlockSpec((ch, N), lambda g, c: (rc(c), g)),
                   pl.BlockSpec((None, ch, R), lambda g, c: (g, rc(c), 0)),
                   vec_c, vec_c, vec_c],
        out_shape=[jax.ShapeDtypeStruct((L, DS), F32), jax.ShapeDtypeStruct((L, G * N), F32),
                   jax.ShapeDtypeStruct((L, G * N), F32), jax.ShapeDtypeStruct((G, L, R), F32),
                   jax.ShapeDtypeStruct((G, 1, R), F32), jax.ShapeDtypeStruct((G, 1, R), F32),
                   jax.ShapeDtypeStruct((G, 1, R), F32)],
        scratch_shapes=[pltpu.VMEM((R, N, P), F32)],
        compiler_params=_params(("parallel", "arbitrary")), name="ssd_bwd",
    )(xbc, xbc, xbc, dtc_raw, dtr_raw, bias_c, bias_r, alog_c, alog_r, dskip_c, dy, states)


def _adam_math(w, g, m, v):
    m = ADAM_B1 * m + (1.0 - ADAM_B1) * g
    v = ADAM_B2 * v + (1.0 - ADAM_B2) * jnp.square(g)
    m_hat = m / (1.0 - ADAM_B1 ** ADAM_STEP)
    v_hat = v / (1.0 - ADAM_B2 ** ADAM_STEP)
    delta = -ADAM_LR * (m_hat / (jnp.sqrt(v_hat) + ADAM_EPS) + ADAM_WD * w)
    return delta, m, v


def _adam(name, w, g, m, v):
    rows, cols = w.shape
    tr = _tile(rows, max(8, (1 << 19) // cols // 8 * 8), 8)

    def body(w_ref, g_ref, m_ref, v_ref, go_ref, d_ref, mo_ref, vo_ref):
        g = g_ref[...]
        d, m2, v2 = _adam_math(w_ref[...], g, m_ref[...], v_ref[...])
        go_ref[...] = g
        d_ref[...] = d
        mo_ref[...] = m2
        vo_ref[...] = v2

    blk = pl.BlockSpec((tr, cols), lambda i: (i, 0))
    return pl.pallas_call(
        body, grid=(rows // tr,), in_specs=[blk] * 4, out_specs=[blk] * 4,
        out_shape=[jax.ShapeDtypeStruct((rows, cols), F32)] * 4,
        compiler_params=_params(("parallel",)), name=name)(w, g, m, v)


def _small_sum_adam(gathered, w, m, v, rows):
    def body(ga_ref, w_ref, m_ref, v_ref, g_ref, d_ref, mo_ref, vo_ref):
        g = ga_ref[0:rows, :]
        for d in range(1, N_DEV):
            g = g + ga_ref[d * rows:(d + 1) * rows, :]
        g_ref[...] = g
        dl, m2, v2 = _adam_math(w_ref[...], g, m_ref[...], v_ref[...])
        d_ref[...] = dl
        mo_ref[...] = m2
        vo_ref[...] = v2

    return pl.pallas_call(
        body, out_shape=[jax.ShapeDtypeStruct((rows, LANES), F32)] * 4,
        compiler_params=pltpu.CompilerParams(vmem_limit_bytes=VMEM_LIMIT), name="small_sum_adam",
    )(gathered, w, m, v)


def _row_tile(rh, cols):
    return _tile(rh, max(16, (1 << 19) // cols // 16 * 16), 16)


def _shard_dims(g, window=None):
    if g.ndim == 3:
        return g.shape[1], g.shape[2]
    return g.shape[0], (window[1] if window else g.shape[1] // N_CHIPS)


def _pair_sum(name, g, recv, pos, window=None):
    r, c = _shard_dims(g, window)
    rh = r // 2
    cb = LANES if window else c
    tr = _row_tile(rh, cb)
    nrt = rh // tr

    def body(pos_ref, a_ref, b_ref, o_ref):
        o_ref[...] = (a_ref[...] + b_ref[...].astype(F32)).astype(BF16)

    if g.ndim == 3:
        own = pl.BlockSpec((None, tr, c), lambda k, i, j, p: (k, p[1] * nrt + i, 0))
    elif window:
        pos = jnp.concatenate([pos, jnp.asarray([s // LANES for s in window[0]], jnp.int32)])
        own = pl.BlockSpec((tr, cb), lambda k, i, j, p: (p[1] * nrt + i, p[2 + k] + j))
    else:
        own = pl.BlockSpec((tr, c), lambda k, i, j, p: (p[1] * nrt + i, k))
    part = pl.BlockSpec((None, tr, cb), lambda k, i, j, p: (k, i, j))
    return pl.pallas_call(
        body,
        grid_spec=pltpu.PrefetchScalarGridSpec(
            num_scalar_prefetch=1, grid=(N_CHIPS, nrt, c // cb), in_specs=[own, part], out_specs=part),
        out_shape=jax.ShapeDtypeStruct((N_CHIPS, rh, c), BF16),
        compiler_params=_params(("parallel", "parallel", "parallel")), name=name)(pos, g, recv)


def _chip_sum(name, parts, pos):
    _, rh, c = parts.shape
    tr = _row_tile(rh, c)
    nrt = rh // tr

    def body(pos_ref, p_ref, o_ref):
        s = p_ref[0].astype(F32)
        for k in range(1, N_CHIPS):
            s = s + p_ref[k].astype(F32)
        o_ref[...] = s

    return pl.pallas_call(
        body,
        grid_spec=pltpu.PrefetchScalarGridSpec(
            num_scalar_prefetch=1, grid=(nrt,),
            in_specs=[pl.BlockSpec((N_CHIPS, tr, c), lambda i, p: (0, i, 0))],
            out_specs=pl.BlockSpec((tr, c), lambda i, p: (p[1] * nrt + i, 0))),
        out_shape=jax.ShapeDtypeStruct((2 * rh, c), F32),
        compiler_params=_params(("parallel",)), name=name)(pos, parts)


_HBM = pl.BlockSpec(memory_space=pltpu.HBM)


def _chip_xy(k):
    return k // 2, k % 2


def _half_rows(ref, hc, rh):
    return ref.at[pl.ds(pl.multiple_of(hc * rh, 16), rh), :]


_SEM = pl.BlockSpec(memory_space=pltpu.SEMAPHORE)
_ANY = pl.BlockSpec(memory_space=pl.ANY)
_SPLIT = pltpu.CompilerParams(has_side_effects=pltpu.SideEffectType.DATAFLOW_SIDE_EFFECTING)


def _in_hbm(a):
    return pltpu.with_memory_space_constraint(a, pltpu.HBM)


def _push_start(name, srcs, land_shapes, copies_of):
    n = len(srcs)

    def body(*refs):
        s_refs, l_refs = refs[:n], refs[n:2 * n]
        send_sems, recv_sems = refs[2 * n], refs[2 * n + 1]
        token = refs[-1]
        x, y, c = lax.axis_index("x"), lax.axis_index("y"), lax.axis_index("c")
        me = 2 * x + y
        for i in range(n):
            for k in range(N_CHIPS):
                @pl.when(k != me)
                def _():
                    src, dst, dev = copies_of(i, k, s_refs[i], l_refs[i], me, x, y, c)
                    pltpu.make_async_remote_copy(
                        src_ref=src, dst_ref=dst, send_sem=send_sems.at[N_CHIPS * i + k],
                        recv_sem=recv_sems.at[N_CHIPS * i + me], device_id=dev, device_id_type=MESH).start()
        token[...] = jnp.zeros_like(token)

    lands = [lax.empty(s, d) for s, d in land_shapes]
    outs = pl.pallas_call(
        body, name=name,
        out_shape=[pltpu.SemaphoreType.DMA((N_CHIPS * n,)), pltpu.SemaphoreType.DMA((N_CHIPS * n,))]
        + [pltpu.HBM(s.shape, s.dtype) for s in srcs] + [pltpu.HBM(s, d) for s, d in land_shapes]
        + [jax.ShapeDtypeStruct((8, LANES), F32)],
        in_specs=[_HBM] * (2 * n),
        out_specs=[_SEM, _SEM] + [_HBM] * (2 * n) + [pl.BlockSpec(memory_space=pltpu.VMEM)],
        input_output_aliases={j: 2 + j for j in range(2 * n)},
        compiler_params=_SPLIT,
    )(*[_in_hbm(s) for s in srcs], *[_in_hbm(l) for l in lands])
    return outs[0], outs[1], outs[2:2 + n], outs[2 + n:2 + 2 * n], outs[-1]


def _push_wait(name, started, after, landed_of):
    send_sems, recv_sems, srcs, lands, _ = started
    n = len(srcs)
    after = list(after) if isinstance(after, (list, tuple)) else [after]

    def body(*refs):
        s_refs, l_refs = refs[:n], refs[n:2 * n]
        send, recv = refs[2 * n], refs[2 * n + 1]
        token = refs[-1]
        token[...] = jnp.zeros_like(token)
        x, y, c = lax.axis_index("x"), lax.axis_index("y"), lax.axis_index("c")
        me = 2 * x + y
        for i in range(n):
            for k in range(N_CHIPS):
                @pl.when(k != me)
                def _():
                    src, dst = landed_of(i, k, s_refs[i], l_refs[i], me, c)
                    cp = pltpu.make_async_remote_copy(
                        src_ref=src, dst_ref=dst, send_sem=send.at[N_CHIPS * i + k], recv_sem=recv.at[N_CHIPS * i + k],
                        device_id=(x, y, c), device_id_type=MESH)
                    cp.wait_send()
                    cp.wait_recv()

    outs = pl.pallas_call(
        body, name=name,
        out_shape=[pltpu.HBM(s.shape, s.dtype) for s in srcs] + [pltpu.HBM(l.shape, l.dtype) for l in lands]
        + [jax.ShapeDtypeStruct((8, LANES), F32)],
        in_specs=[_HBM] * (2 * n) + [_SEM, _SEM] + [_ANY] * len(after),
        out_specs=[_HBM] * (2 * n) + [pl.BlockSpec(memory_space=pltpu.VMEM)],
        input_output_aliases={j: j for j in range(2 * n)},
        compiler_params=_SPLIT,
    )(*srcs, *lands, send_sems, recv_sems, *after)
    return outs[:n], outs[n:2 * n], outs[-1]


def _gather_start(name, shards):
    def copies_of(i, k, src, land, me, x, y, c):
        rh = shards[i].shape[0] // 2
        kx, ky = _chip_xy(k)
        return _half_rows(src, c, rh), _half_rows(land.at[me], c, rh), (kx, ky, c)

    return _push_start(name, shards, [((N_CHIPS,) + s.shape, s.dtype) for s in shards], copies_of)


def _gather_wait(name, started, after):
    shapes = [s.shape for s in started[2]]

    def landed_of(i, k, src, land, me, c):
        rh = shapes[i][0] // 2
        return _half_rows(src, c, rh), _half_rows(land.at[k], c, rh)

    return _push_wait(name, started, after, landed_of)


def _forward_halves(name, bufs):
    n = len(bufs)

    def body(*refs):
        i_refs, o_refs, send_sems, recv_sems = refs[:n], refs[n:2 * n], refs[2 * n], refs[2 * n + 1]
        x, y, c = lax.axis_index("x"), lax.axis_index("y"), lax.axis_index("c")
        me = 2 * x + y

        def fwd(i, k, hc):
            rh = bufs[i].shape[1] // 2
            return pltpu.make_async_remote_copy(
                src_ref=_half_rows(i_refs[i].at[k], hc, rh), dst_ref=_half_rows(o_refs[i].at[k], hc, rh),
                send_sem=send_sems.at[i, k], recv_sem=recv_sems.at[i, k],
                device_id=(x, y, 1 - c), device_id_type=MESH)

        for i in range(n):
            for k in range(N_CHIPS):
                @pl.when(k != me)
                def _():
                    fwd(i, k, c).start()
        for i in range(n):
            for k in range(N_CHIPS):
                @pl.when(k != me)
                def _():
                    fwd(i, k, 1 - c).wait_recv()
        for i in range(n):
            for k in range(N_CHIPS):
                @pl.when(k != me)
                def _():
                    fwd(i, k, c).wait_send()

    return pl.pallas_call(
        body, in_specs=[_HBM] * n, out_specs=[_HBM] * n,
        out_shape=[jax.ShapeDtypeStruct(b.shape, b.dtype) for b in bufs],
        input_output_aliases={i: i for i in range(n)},
        scratch_shapes=[pltpu.SemaphoreType.DMA((n, N_CHIPS))] * 2,
        name=name)(*bufs)


def _swap_halves(name, grads, windows):
    n = len(grads)
    dims = [_shard_dims(g, w) for g, w in zip(grads, windows)]

    def body(*refs):
        g_refs, o_refs, send_sems, recv_sems = refs[:n], refs[n:2 * n], refs[2 * n], refs[2 * n + 1]
        x, y, c = lax.axis_index("x"), lax.axis_index("y"), lax.axis_index("c")
        copies = []
        for i in range(n):
            r, cw = dims[i]
            for k in range(N_CHIPS):
                if grads[i].ndim == 3:
                    shard = g_refs[i].at[k]
                else:
                    shard = g_refs[i].at[:, pl.ds(windows[i][0][k] if windows[i] else k * cw, cw)]
                copies.append(pltpu.make_async_remote_copy(
                    src_ref=_half_rows(shard, 1 - c, r // 2), dst_ref=o_refs[i].at[k],
                    send_sem=send_sems.at[i, k], recv_sem=recv_sems.at[i, k],
                    device_id=(x, y, 1 - c), device_id_type=MESH))
        for cp in copies:
            cp.start()
        for cp in copies:
            cp.wait()

    return pl.pallas_call(
        body, in_specs=[_HBM] * n, out_specs=[_HBM] * n,
        out_shape=[jax.ShapeDtypeStruct((N_CHIPS, r // 2, cw), g.dtype) for (r, cw), g in zip(dims, grads)],
        scratch_shapes=[pltpu.SemaphoreType.DMA((n, N_CHIPS))] * 2,
        name=name)(*grads)


def _scatter_start(name, parts):
    def copies_of(i, k, src, land, me, x, y, c):
        kx, ky = _chip_xy(k)
        return src.at[k], land.at[me], (kx, ky, c)

    return _push_start(name, parts, [(p.shape, p.dtype) for p in parts], copies_of)


def _scatter_wait(name, started, after):
    return _push_wait(name, started, after, lambda i, k, src, land, me, c: (src.at[k], land.at[k]))


def _join_halves(bufs):
    n = len(bufs)

    def body(*refs):
        i_refs, o_refs, send_sems, recv_sems = refs[:n], refs[n:2 * n], refs[2 * n], refs[2 * n + 1]
        x, y, c = lax.axis_index("x"), lax.axis_index("y"), lax.axis_index("c")
        copies = []
        for i in range(n):
            rh = bufs[i].shape[0] // 2
            copies.append(pltpu.make_async_remote_copy(
                src_ref=_half_rows(i_refs[i], c, rh), dst_ref=_half_rows(o_refs[i], c, rh),
                send_sem=send_sems.at[i], recv_sem=recv_sems.at[i],
                device_id=(x, y, 1 - c), device_id_type=MESH))
        for cp in copies:
            cp.start()
        for i in range(n):
            rh = bufs[i].shape[0] // 2
            pltpu.make_async_remote_copy(
                src_ref=_half_rows(i_refs[i], c, rh), dst_ref=_half_rows(o_refs[i], 1 - c, rh),
                send_sem=send_sems.at[i], recv_sem=recv_sems.at[i],
                device_id=(x, y, 1 - c), device_id_type=MESH).wait_recv()
        for cp in copies:
            cp.wait_send()

    return pl.pallas_call(
        body, in_specs=[_HBM] * n, out_specs=[_HBM] * n,
        out_shape=[jax.ShapeDtypeStruct(b.shape, F32) for b in bufs],
        input_output_aliases={i: i for i in range(n)},
        scratch_shapes=[pltpu.SemaphoreType.DMA((n,))] * 2,
        name="join_halves")(*bufs)


def _all_gather_small(name, blk):
    m_per, n = blk.shape

    def body(x_ref, out_ref, send_sems, recv_sems, local_sem):
        x, y, c = lax.axis_index("x"), lax.axis_index("y"), lax.axis_index("c")
        me, sibling = (x, y, c), (x, y, 1 - c)
        chips = [(1 - x, y), (x, 1 - y), (1 - x, 1 - y)]

        def rows(px, py, pc):
            return out_ref.at[pl.ds((4 * px + 2 * py + pc) * m_per, m_per), :]

        def copy(k, block, to, src=None):
            return pltpu.make_async_remote_copy(
                src_ref=rows(*block) if src is None else src, dst_ref=rows(*block),
                send_sem=send_sems.at[k], recv_sem=recv_sems.at[k],
                device_id=to, device_id_type=MESH)

        mine = pltpu.make_async_copy(x_ref, rows(*me), local_sem)
        mine.start()
        first = [copy(0, me, sibling, src=x_ref)]
        first += [copy(1 + j, me, (*chip, c), src=x_ref) for j, chip in enumerate(chips)]
        for cp in first:
            cp.start()
        passed = [copy(4 + j, (*chip, c), sibling) for j, chip in enumerate(chips)]
        for j, chip in enumerate(chips):
            copy(1 + j, (*chip, c), me).wait_recv()
            passed[j].start()
        copy(0, sibling, me).wait_recv()
        for j, chip in enumerate(chips):
            copy(4 + j, (*chip, 1 - c), me).wait_recv()
        for cp in first + passed:
            cp.wait_send()
        mine.wait()

    return pl.pallas_call(
        body, out_shape=jax.ShapeDtypeStruct((N_DEV * m_per, n), blk.dtype),
        in_specs=[pl.BlockSpec(memory_space=pltpu.VMEM)],
        out_specs=pl.BlockSpec(memory_space=pltpu.VMEM),
        scratch_shapes=[pltpu.SemaphoreType.DMA((7,)), pltpu.SemaphoreType.DMA((7,)), pltpu.SemaphoreType.DMA],
        name=name)(blk)


def _pack_rows(vecs, width):
    parts = []
    for v in vecs:
        f = v.reshape(-1)
        pad = (-f.shape[0]) % (8 * width)
        parts.append(jnp.pad(f, (0, pad)) if pad else f)
    return jnp.concatenate(parts).reshape(-1, width)


def _unpack_rows(packed, shapes, width):
    flat = packed.reshape(-1)
    out, off = [], 0
    for s in shapes:
        n = math.prod(s)
        out.append(flat[off:off + n].reshape(s))
        off += n + ((-n) % (8 * width))
    return out


class _WinPlan:
    def __init__(self, ncol, dt0, h, dmain):
        self.ncol, self.h = ncol, h
        self.dt_shard = dt0 // ncol
        assert (dt0 + h - 1) // ncol == self.dt_shard and dmain % LANES == 0
        self.dt_local = dt0 - self.dt_shard * ncol
        to_main = lambda g: g if g <= dt0 else g - h
        self.lo = [to_main(ncol * k) for k in range(N_CHIPS)]
        self.hi = [to_main(ncol * (k + 1)) for k in range(N_CHIPS)]
        down = lambda v: v // LANES * LANES
        self.ww = max(-(-(hi - down(lo)) // LANES) * LANES for lo, hi in zip(self.lo, self.hi))
        self.ws = [min(down(lo), dmain - self.ww) for lo in self.lo]
        self.dmain = dmain

    def to_window(self, k, shard):
        if k == self.dt_shard:
            shard = jnp.concatenate([shard[:, :self.dt_local], shard[:, self.dt_local + self.h:]], axis=1)
        left = self.lo[k] - self.ws[k]
        return jnp.pad(shard, ((0, 0), (left, self.ww - left - shard.shape[1])))

    def from_window(self, k, window, dt_cols):
        left = self.lo[k] - self.ws[k]
        body = window[:, left:left + self.hi[k] - self.lo[k]]
        if k == self.dt_shard:
            body = jnp.concatenate([body[:, :self.dt_local], dt_cols, body[:, self.dt_local:]], axis=1)
        return body

    def merge(self, windows):
        cuts = sorted({0, self.dmain} | set(self.ws) | {w + self.ww for w in self.ws})
        segs = []
        for a, b in zip(cuts[:-1], cuts[1:]):
            parts = [windows[k][:, a - self.ws[k]:b - self.ws[k]] for k in range(N_CHIPS)
                     if self.ws[k] <= a and b <= self.ws[k] + self.ww]
            segs.append(functools.reduce(jnp.add, parts))
        return jnp.concatenate(segs, axis=1)


def kernel(x, attn_norm_w, w_in, conv_w, conv_b, dt_bias, a_log, d_skip, ssd_norm_w, pool_w, pool_scale, w_out, ffn_norm_w, w_gate, w_up, w_down, final_norm_w, loss_target, m_attn_norm_w, m_w_in, m_conv_w, m_conv_b, m_dt_bias, m_a_log, m_d_skip, m_ssd_norm_w, m_pool_w, m_pool_scale, m_w_out, m_ffn_norm_w, m_w_gate, m_w_up, m_w_down, m_final_norm_w, v_attn_norm_w, v_w_in, v_conv_w, v_conv_b, v_dt_bias, v_a_log, v_d_skip, v_ssd_norm_w, v_pool_w, v_pool_scale, v_w_out, v_ffn_norm_w, v_w_gate, v_w_up, v_w_down, v_final_norm_w):
    G, P, PG = SSD_GROUPS, HEAD_DIM, len(POOL_WINDOWS)
    _, L, D = x.shape
    H = a_log.shape[1]
    R = H // G
    DS = H * P
    DCONV = conv_b.shape[1]
    N = (DCONV - DS) // (2 * G)
    DP = pool_scale.shape[1]
    PGD = DP // PG
    DIN = N_CHIPS * w_in.shape[2]
    DFF = N_CHIPS * w_gate.shape[2]
    DMAIN = DS + DCONV + DP
    assert DIN == DMAIN + H and DS == DP and H <= LANES

    cx, cy, cc = lax.axis_index("x"), lax.axis_index("y"), lax.axis_index("c")
    chip = 2 * cx + cy

    win = _WinPlan(DIN // N_CHIPS, DS + DCONV, H, DMAIN)
    my_window = lax.switch(chip, [functools.partial(win.to_window, k) for k in range(N_CHIPS)], w_in[0].astype(BF16))
    started_in = _gather_start("gather_start_in", [my_window])

    def forward_gathered(tag, shards, landed):
        landed = _forward_halves("gather_forward_" + tag, landed)
        return [lax.dynamic_update_slice(g, s[None], (chip, 0, 0)) for g, s in zip(landed, shards)]

    def cols(p):
        return jnp.moveaxis(p, 0, -2).reshape(p.shape[1:-1] + (N_CHIPS * p.shape[-1],))

    ncw = CONV_WIDTH * DCONV // N_CHIPS
    dt_here = jnp.where(chip == win.dt_shard, w_in[0][:, win.dt_local:win.dt_local + H], 0.0)
    start_blk = _pack_rows([conv_w[0], dt_here], LANES)
    start_all = _all_gather_small("gather_conv_w", start_blk).reshape(N_CHIPS, 2, -1)[:, 0]
    conv_w_f = cols(start_all[:, :ncw].reshape(N_CHIPS, CONV_WIDTH, DCONV // N_CHIPS))
    dt_off = ncw + (-ncw) % (8 * LANES)
    w_dt = jnp.pad(start_all[win.dt_shard, dt_off:dt_off + D * H].reshape(D, H), ((0, 0), (0, LANES - H))).astype(BF16)

    xl, tgt = x[0], loss_target[0]
    tm_row = _tile(L, 256, HALO)
    tm_mm = _tile(L, 1024, 16)
    hn1 = _rms_fwd("rms1_fwd", xl, attn_norm_w, tm_row)
    rest16 = [w_out[0].astype(BF16), w_gate[0].astype(BF16), w_up[0].astype(BF16), w_down[0].astype(BF16)]
    shards_in, landed_in, landed_token = _gather_wait("gather_wait_in", started_in, [hn1, conv_w_f, w_dt] + rest16)
    shards_rest = [(pool_w[0].reshape(PG * PGD // N_CHIPS, PGD) + landed_token[0, 0]).astype(BF16)] + rest16
    started_rest = _gather_start("gather_start_rest", shards_rest)
    pin_row = lambda started, n: jnp.zeros((1, n), F32) + started[4][0, 0]
    add_row = lambda accs, ex, rex: [accs[0] + rex[0]]
    w_main = win.merge(forward_gathered("in", shards_in, landed_in)[0])
    proj, = _mm("proj_main", "nn", [(hn1, w_main)], L, DMAIN, D, tm_mm, _tile(DMAIN, 1024, LANES), D, [F32],
                epilogue=add_row, row_extras=[pin_row(started_rest, DMAIN)])
    dt_raw, = _mm("proj_dt", "nn", [(hn1, w_dt)], L, LANES, D, tm_mm, LANES, D, [F32])

    cwid = _tile(math.gcd(DS, DCONV), 512, LANES)
    tm_conv = _tile(L, 1024, HALO)
    xbc = _conv_fwd(proj, conv_w_f, conv_b, DS, DCONV, tm_conv, cwid)

    dt_g = dt_raw[:, :H].reshape(L, G, R)
    dtc_raw = jnp.transpose(dt_g, (1, 0, 2))
    dtr_raw = jnp.transpose(dt_g, (1, 2, 0))
    as_c = lambda v: v.reshape(G, 1, R)
    as_r = lambda v: v.reshape(G, R, 1)
    ssd_args = (xbc, dtc_raw, dtr_raw, as_c(dt_bias), as_r(dt_bias), as_c(a_log), as_r(a_log), as_c(d_skip))
    y_ssd_raw, states = _ssd_fwd(*ssd_args, DS, N)
    mixed = _gated_fwd(y_ssd_raw, proj, ssd_norm_w, DS, DS + DP, tm_conv)
    gathered = forward_gathered("rest", *_gather_wait("gather_wait_rest", started_rest, mixed)[:2])
    pool_w_f = jnp.moveaxis(gathered[0].reshape(N_CHIPS, PG, PGD // N_CHIPS, PGD), 0, 1).reshape(PG, PGD, PGD)
    w_out_f = gathered[1].reshape(2 * DS, D)
    w_gate_f, w_up_f = cols(gathered[2]), cols(gathered[3])
    w_down_f = gathered[4].reshape(DFF, D)
    w_out_top, w_out_bot = w_out_f[:DS], w_out_f[DS:]
    pooled, mixed = _pool_fwd(proj, pool_w_f, pool_scale, mixed, DS + DCONV, DS, DP, tm_conv)

    add_res = lambda accs, ex, rex: [accs[0] + ex[0]]
    h1, = _mm("out_proj", "nn", [(mixed, w_out_f)], L, D, DS + DP, tm_mm, 512, DS + DP, [F32],
              epilogue=add_res, extras=[xl])
    hn2 = _rms_fwd("rms2_fwd", h1, ffn_norm_w, tm_row)

    def glu(accs, ex, rex):
        return [accs[0], accs[1], (_silu(accs[0]) * accs[1])]

    tn_ff = _tile(DFF, 512, LANES)
    gate, up, act = _mm("ffn_in", "nn", [(hn2, w_gate_f), (hn2, w_up_f)], L, DFF, D, tm_mm, tn_ff, D,
                        [F32, F32, BF16], epilogue=glu, separate=True)
    tk_ff = _tile(DFF, DFF // 2, LANES)
    h2, = _mm("ffn_out", "nn", [(act, w_down_f)], L, D, DFF, tm_mm, 512, tk_ff, [F32], epilogue=add_res, extras=[h1])
    dh2, dh2_16, loss_blk, g_final = _final_loss(h2, final_norm_w.reshape(1, D), tgt, tm_row)

    def dglu(accs, ex, rex):
        gt, u = ex
        sg = _sigmoid(gt)
        return [accs[0] * u * (sg * (1.0 + gt * (1.0 - sg))), accs[0] * (gt * sg)]

    dgate, dup = _mm("ffn_out_dx", "nt", [(dh2_16, w_down_f)], L, DFF, D, tm_mm, tn_ff, D, [BF16, BF16],
                     epilogue=dglu, extras=[gate, up])
    tk_tok = _tile(L, 2048, 16)
    twice = lambda accs, ex, rex: list(accs) + list(accs)
    g_w_down, g_w_down16 = _mm("ffn_out_dw", "tn", [(act, dh2_16)], DFF, D, L, _tile(DFF, 1536, LANES), 1024, tk_tok,
                               [F32, BF16], epilogue=twice)
    g_w_gate, g_w_up, g_w_gate16, g_w_up16 = _mm("ffn_in_dw", "tn", [(hn2, dgate), (hn2, dup)], D, DFF, L, 1024, tn_ff,
                                                 tk_tok, [F32, F32, BF16, BF16], epilogue=twice, separate=True)

    pos = jnp.stack([chip, cc]).astype(jnp.int32)

    def start_reduce(tag, names, full_grads, full_grads16, windows):
        from_sibling = _swap_halves("swap_halves_" + tag, full_grads16, windows)
        partials = [_pair_sum("pair_sum_" + n, g, r, pos, w)
                    for n, g, r, w in zip(names, full_grads, from_sibling, windows)]
        return _scatter_start("scatter_start_" + tag, partials)

    names_ffn = ["w_gate", "w_up", "w_down"]
    started_ffn = start_reduce("ffn", names_ffn, [g_w_gate, g_w_up, g_w_down.reshape(N_CHIPS, -1, D)],
                               [g_w_gate16, g_w_up16, g_w_down16.reshape(N_CHIPS, -1, D)], [None] * 3)
    dhn2, = _mm("ffn_in_dx", "nt", [(dgate, w_gate_f), (dup, w_up_f)], L, D, DFF, tm_mm, 512,
                _tile(DFF, DFF // 4, LANES), [F32], epilogue=add_row, row_extras=[pin_row(started_ffn, D)])
    dh1, g_ffn_norm, dh1_16 = _rms_bwd("rms2_bwd", h1, ffn_norm_w, [dhn2], dh2, tm_row, True)

    dy_ssd, dy_pool = _mm("out_proj_dx", "nt", [(dh1_16, w_out_top), (dh1_16, w_out_bot)], L, DS, D, tm_mm, 512, D,
                          [F32, F32], separate=True)
    g_w_out, g_w_out16 = _mm("out_proj_dw", "tn", [(mixed, dh1_16)], DS + DP, D, L, 1024, 1024, tk_tok, [F32, BF16],
                             epilogue=twice)
    dy_raw, dproj, g_ssd_norm = _gated_bwd(y_ssd_raw, proj, ssd_norm_w, dy_ssd, DS, DMAIN, tm_conv)
    dxs, db, dc, ddt_raw, g_a_log, g_d_skip, g_dt_bias = _ssd_bwd(*ssd_args, dy_raw, states, DS, N)
    g_conv_w, g_conv_b = [], []
    for tag, dact, first in (("xs", dxs, 0), ("b", db, DS), ("c", dc, DS + G * N)):
        dproj, gw, gb = _conv_bwd("conv_bwd_" + tag, proj, dact, conv_w_f, conv_b, dproj, DS, first, tm_conv, cwid)
        g_conv_w.append(gw)
        g_conv_b.append(gb)
    g_conv_w, g_conv_b = jnp.concatenate(g_conv_w, axis=1), jnp.concatenate(g_conv_b, axis=1)
    dproj, g_pool_w, g_pool_scale = _pool_bwd(dy_pool, pooled, pool_w_f, pool_scale, dproj, DS + DCONV, tm_conv)
    ddt_pad = jnp.pad(jnp.transpose(ddt_raw, (1, 0, 2)).reshape(L, H), ((0, 0), (0, LANES - H))).astype(BF16)

    tk_main = _tile(DMAIN, DMAIN // 2, LANES)
    g_w_main, g_w_main16 = _mm("proj_main_dw", "tn", [(hn1, dproj)], D, DMAIN, L, 1024, _tile(DMAIN, 1024, LANES),
                               tk_tok, [F32, BF16], epilogue=twice)
    names_mix = ["w_in", "pool_w", "w_out"]
    pool_shards = jnp.moveaxis(g_pool_w.reshape(PG, N_CHIPS, PGD // N_CHIPS, PGD), 1, 0).reshape(N_CHIPS, -1, PGD)
    started_mix = start_reduce(
        "mix", names_mix, [g_w_main, pool_shards, g_w_out.reshape(N_CHIPS, -1, D)],
        [g_w_main16, pool_shards.astype(BF16), g_w_out16.reshape(N_CHIPS, -1, D)],
        [(win.ws, win.ww), None, None])
    dhn1a, = _mm("proj_main_dx", "nt", [(dproj, w_main)], L, D, DMAIN, tm_mm, 512, tk_main, [F32],
                 epilogue=add_row, row_extras=[pin_row(started_mix, D)])
    dhn1b, = _mm("proj_dt_dx", "nt", [(ddt_pad, w_dt)], L, D, LANES, tm_mm, 512, LANES, [F32])
    g_w_dt, = _mm("proj_dt_dw", "tn", [(hn1, ddt_pad)], D, LANES, L, 512, LANES, tk_tok, [F32])
    grad_x, g_attn_norm = _rms_bwd("rms1_bwd", xl, attn_norm_w, [dhn1a, dhn1b], dh1, tm_row, False)

    def finish_reduce(tag, names, started, after):
        partials, landed, _ = _scatter_wait("scatter_wait_" + tag, started, after)
        landed = [lax.dynamic_update_slice(l, lax.dynamic_index_in_dim(p, chip, 0), (chip, 0, 0))
                  for l, p in zip(landed, partials)]
        return [_chip_sum("chip_sum_" + n, l, pos) for n, l in zip(names, landed)]

    halves = finish_reduce("ffn", names_ffn, started_ffn, grad_x) + finish_reduce("mix", names_mix, started_mix, grad_x)
    red = dict(zip(names_ffn + names_mix, _join_halves(halves)))

    small_w = [attn_norm_w, conv_b, dt_bias, a_log, d_skip, ssd_norm_w, pool_scale, ffn_norm_w, final_norm_w]
    small_m = [m_attn_norm_w, m_conv_b, m_dt_bias, m_a_log, m_d_skip, m_ssd_norm_w, m_pool_scale, m_ffn_norm_w, m_final_norm_w]
    small_v = [v_attn_norm_w, v_conv_b, v_dt_bias, v_a_log, v_d_skip, v_ssd_norm_w, v_pool_scale, v_ffn_norm_w, v_final_norm_w]
    small_g = [g_attn_norm, g_conv_b, g_dt_bias.reshape(1, H), g_a_log.reshape(1, H), g_d_skip.reshape(1, H),
               g_ssd_norm, g_pool_scale, g_ffn_norm, g_final.reshape(D)]
    extra_shapes = [(CONV_WIDTH, DCONV), (D, H), (1, LANES)]
    zeros_like_extra = [jnp.zeros(s, F32) for s in extra_shapes]
    g_blk = _pack_rows(small_g + [g_conv_w, g_w_dt[:, :H], loss_blk], LANES)
    rows = g_blk.shape[0]
    small_all = _all_gather_small("gather_small_grads", g_blk)
    s_g, s_d, s_m, s_v = _small_sum_adam(small_all, _pack_rows(small_w + zeros_like_extra, LANES),
                                         _pack_rows(small_m + zeros_like_extra, LANES),
                                         _pack_rows(small_v + zeros_like_extra, LANES), rows)
    shapes = [w.shape for w in small_w] + extra_shapes
    sg_list = _unpack_rows(s_g, shapes, LANES)
    sd_list = _unpack_rows(s_d, shapes, LANES)[:len(small_w)]
    sm_list = _unpack_rows(s_m, shapes, LANES)[:len(small_w)]
    sv_list = _unpack_rows(s_v, shapes, LANES)[:len(small_w)]
    loss = sg_list[-1][0, 0]
    grad_conv_w = lax.dynamic_slice(sg_list[-3], (0, chip * (DCONV // N_CHIPS)), (CONV_WIDTH, DCONV // N_CHIPS))
    grad_w_in = lax.switch(chip, [functools.partial(win.from_window, k) for k in range(N_CHIPS)], red["w_in"], sg_list[-2])

    def adam_nd(name, w, g, m, v):
        shp = w.shape
        to2 = lambda a: a.reshape(-1, shp[-1])
        return tuple(o.reshape(shp) for o in _adam(name, to2(w), to2(g), to2(m), to2(v)))

    sharded = {
        "w_in": (w_in, grad_w_in[None], m_w_in, v_w_in),
        "conv_w": (conv_w, grad_conv_w[None], m_conv_w, v_conv_w),
        "pool_w": (pool_w, red["pool_w"].reshape(pool_w.shape), m_pool_w, v_pool_w),
        "w_out": (w_out, red["w_out"][None], m_w_out, v_w_out),
        "w_gate": (w_gate, red["w_gate"][None], m_w_gate, v_w_gate),
        "w_up": (w_up, red["w_up"][None], m_w_up, v_w_up),
        "w_down": (w_down, red["w_down"][None], m_w_down, v_w_down),
    }
    upd = {n: adam_nd("adam_" + n, *a) for n, a in sharded.items()}
    small_names = ["attn_norm_w", "conv_b", "dt_bias", "a_log", "d_skip", "ssd_norm_w", "pool_scale", "ffn_norm_w",
                   "final_norm_w"]
    for i, n in enumerate(small_names):
        upd[n] = (sg_list[i], sd_list[i], sm_list[i], sv_list[i])

    order = ["attn_norm_w", "w_in", "conv_w", "conv_b", "dt_bias", "a_log", "d_skip", "ssd_norm_w", "pool_w",
             "pool_scale", "w_out", "ffn_norm_w", "w_gate", "w_up", "w_down", "final_norm_w"]
    outs = [loss, grad_x[None]]
    for j in range(4):
        outs += [upd[n][j] for n in order]
    return tuple(outs)
```
